```python
import jax, jax.numpy as jnp
from jax import lax
import numpy as np


D_MODEL = 2048
BATCH = 2
SEQ = 8192
DEPTH = 4
DEC_BATCH = 32
DEC_SEQ = 64
PAST_LEN = 1024

CHUNK = 64
HEAD_DIM = 128
BRANCH_WIDTH = D_MODEL // 2
N_BRANCH = 3
D_RNN = BRANCH_WIDTH
LRU_BLOCKS = 8
LRU_BLOCK = D_RNN // LRU_BLOCKS
CONV_W = 4
LRU_C = 8.0
SWA_HEADS = BRANCH_WIDTH // HEAD_DIM
SWA_KV_HEADS = 2
SWA_GROUP = SWA_HEADS // SWA_KV_HEADS
SWA_WINDOW = 128
SWA_PREV = SWA_WINDOW // CHUNK
CB_HEADS = BRANCH_WIDTH // HEAD_DIM
CB_PREV = 8
CB_REACH = CB_PREV * CHUNK
REL_CLIP = 128
D_FF = 4 * D_MODEL
EPS = 1e-6
NEG = -1e30
ATTN_SCALE = HEAD_DIM ** -0.5
IN_SPLITS = (D_RNN, D_RNN,
             SWA_HEADS * HEAD_DIM, SWA_KV_HEADS * HEAD_DIM, SWA_KV_HEADS * HEAD_DIM,
             CB_HEADS * HEAD_DIM, CB_HEADS * HEAD_DIM, CB_HEADS * HEAD_DIM,
             N_BRANCH * D_MODEL)
D_IN = sum(IN_SPLITS)

kernel_name = "hybrid_streaming_encoder_step"


def rms_norm(x, g):
    xf = x.astype(jnp.float32)
    y = xf * lax.rsqrt(jnp.mean(xf * xf, axis=-1, keepdims=True) + EPS)
    return (y * g.astype(jnp.float32)).astype(x.dtype)


def split_cols(z):
    outs, start = [], 0
    for w in IN_SPLITS:
        outs.append(z[..., start:start + w])
        start += w
    return outs


def causal_conv(u, buf, w, b):
    t = u.shape[1]
    up = jnp.concatenate([buf.astype(u.dtype), u], axis=1)
    y = b + up[:, 0:t] * w[0]
    for k in range(1, CONV_W):
        y = y + up[:, k:k + t] * w[k]
    return y, up[:, up.shape[1] - (CONV_W - 1):]


def rg_lru(u, h0, wa, ba, wx, bx, lam):
    bsz, t, _ = u.shape
    uf = u.astype(jnp.float32)
    ub = uf.reshape(bsz, t, LRU_BLOCKS, LRU_BLOCK)
    r = jax.nn.sigmoid(jnp.einsum('btnc,ncd->btnd', ub, wa.astype(jnp.float32)).reshape(bsz, t, D_RNN) + ba.astype(jnp.float32))
    i = jax.nn.sigmoid(jnp.einsum('btnc,ncd->btnd', ub, wx.astype(jnp.float32)).reshape(bsz, t, D_RNN) + bx.astype(jnp.float32))
    log_a = -LRU_C * r * jax.nn.softplus(-lam.astype(jnp.float32))
    a = jnp.exp(log_a)
    b = jnp.sqrt(-jnp.expm1(2.0 * log_a)) * (i * uf)
    b = b.at[:, 0].add(a[:, 0] * h0.astype(jnp.float32))

    def combine(e1, e2):
        a1, b1 = e1
        a2, b2 = e2
        return a1 * a2, a2 * b1 + b2

    _, h = lax.associative_scan(combine, (a, b), axis=1)
    return h, h[:, -1]


def band_chunks(xc, n_prev):
    bsz, nc = xc.shape[:2]
    pad = jnp.zeros((bsz, n_prev) + xc.shape[2:], xc.dtype)
    xp = jnp.concatenate([pad, xc], axis=1)
    band = jnp.stack([xp[:, o:o + nc] for o in range(n_prev + 1)], axis=2)
    return band.reshape((bsz, nc, (n_prev + 1) * xc.shape[2]) + xc.shape[3:])


def band_valid(nc, n_prev):
    j = jnp.arange((n_prev + 1) * CHUNK)
    return (jnp.arange(nc)[:, None] - n_prev + j[None, :] // CHUNK) >= 0


def softmax_with_sink(s, sink):
    m = jnp.maximum(jnp.max(s, axis=-1, keepdims=True), sink)
    e = jnp.exp(s - m)
    return e / (jnp.sum(e, axis=-1, keepdims=True) + jnp.exp(sink - m))


def rel_bias(table, n_q, n_k, offset):
    d = offset + jnp.arange(n_q)[:, None] - jnp.arange(n_k)[None, :]
    idx = jnp.clip(d, -REL_CLIP, REL_CLIP) + REL_CLIP
    return table.astype(jnp.float32)[:, idx]


def swa_prompt(q, k, v, sinks):
    bsz, s_len = q.shape[:2]
    nc = s_len // CHUNK
    qc = q.reshape(bsz, nc, CHUNK, SWA_KV_HEADS, SWA_GROUP, HEAD_DIM)
    kb = band_chunks(k.reshape(bsz, nc, CHUNK, SWA_KV_HEADS, HEAD_DIM), SWA_PREV)
    vb = band_chunks(v.reshape(bsz, nc, CHUNK, SWA_KV_HEADS, HEAD_DIM), SWA_PREV)
    s = jnp.einsum('bnikgd,bnjkd->bnkgij', qc, kb, preferred_element_type=jnp.float32) * ATTN_SCALE
    s = jnp.where(band_valid(nc, SWA_PREV)[None, :, None, None, None, :], s, NEG)
    p = softmax_with_sink(s, sinks.astype(jnp.float32).reshape(SWA_KV_HEADS, SWA_GROUP, 1, 1))
    o = jnp.einsum('bnkgij,bnjkd->bnikgd', p.astype(v.dtype), vb)
    return o.reshape(bsz, s_len, SWA_HEADS * HEAD_DIM)


def swa_step(q, k, v, sinks):
    bsz, t = q.shape[:2]
    s = jnp.einsum('bikgd,bjkd->bkgij', q, k, preferred_element_type=jnp.float32) * ATTN_SCALE
    p = softmax_with_sink(s, sinks.astype(jnp.float32).reshape(SWA_KV_HEADS, SWA_GROUP, 1, 1))
    o = jnp.einsum('bkgij,bjkd->bikgd', p.astype(v.dtype), v)
    return o.reshape(bsz, t, SWA_HEADS * HEAD_DIM)


def chunk_prompt(q, k, v, table):
    bsz, s_len = q.shape[:2]
    nc = s_len // CHUNK
    qc = q.reshape(bsz, nc, CHUNK, CB_HEADS, HEAD_DIM)
    kb = band_chunks(k.reshape(bsz, nc, CHUNK, CB_HEADS, HEAD_DIM), CB_PREV)
    vb = band_chunks(v.reshape(bsz, nc, CHUNK, CB_HEADS, HEAD_DIM), CB_PREV)
    n_k = (CB_PREV + 1) * CHUNK
    s = jnp.einsum('bnihd,bnjhd->bnhij', qc, kb, preferred_element_type=jnp.float32) * ATTN_SCALE
    s = s + rel_bias(table, CHUNK, n_k, CB_REACH)[None, None]
    s = jnp.where(band_valid(nc, CB_PREV)[None, :, None, None, :], s, NEG)
    p = jax.nn.softmax(s, axis=-1)
    o = jnp.einsum('bnhij,bnjhd->bnihd', p.astype(v.dtype), vb)
    return o.reshape(bsz, s_len, CB_HEADS * HEAD_DIM)


def chunk_step(q, k, v, table, n_past):
    bsz, t = q.shape[:2]
    s = jnp.einsum('bihd,bjhd->bhij', q, k, preferred_element_type=jnp.float32) * ATTN_SCALE
    s = s + rel_bias(table, t, k.shape[1], n_past)[None]
    p = jax.nn.softmax(s, axis=-1)
    o = jnp.einsum('bhij,bjhd->bihd', p.astype(v.dtype), v)
    return o.reshape(bsz, t, CB_HEADS * HEAD_DIM)


def trunk_layer(x, conv_buf, h0, caches, norm1, w_in, conv_w, conv_b, wa, ba, wx, bx, lam,
                sinks, table, w_branch, w_out, norm2, w_up, w_down):
    bsz, t, _ = x.shape
    xn = rms_norm(x, norm1)
    z = xn @ w_in
    a_x, a_gate, qb, kb, vb, qc, kc, vc, gates = split_cols(z)
    u, new_buf = causal_conv(a_x, conv_buf, conv_w, conv_b)
    h, h_last = rg_lru(u, h0, wa, ba, wx, bx, lam)
    out_a = h.astype(x.dtype) * jax.nn.gelu(a_gate)
    qb = qb.reshape(bsz, t, SWA_KV_HEADS, SWA_GROUP, HEAD_DIM)
    kb = kb.reshape(bsz, t, SWA_KV_HEADS, HEAD_DIM)
    vb = vb.reshape(bsz, t, SWA_KV_HEADS, HEAD_DIM)
    qc = qc.reshape(bsz, t, CB_HEADS, HEAD_DIM)
    kc = kc.reshape(bsz, t, CB_HEADS, HEAD_DIM)
    vc = vc.reshape(bsz, t, CB_HEADS, HEAD_DIM)
    if caches is None:
        out_b = swa_prompt(qb, kb, vb, sinks)
        out_c = chunk_prompt(qc, kc, vc, table)
        new_kv = (kb[:, -SWA_WINDOW:], vb[:, -SWA_WINDOW:], kc[:, -CB_REACH:], vc[:, -CB_REACH:])
    else:
        ck_b, cv_b, ck_c, cv_c = caches
        out_b = swa_step(qb, jnp.concatenate([ck_b.astype(kb.dtype), kb], axis=1),
                         jnp.concatenate([cv_b.astype(vb.dtype), vb], axis=1), sinks)
        out_c = chunk_step(qc, jnp.concatenate([ck_c.astype(kc.dtype), kc], axis=1),
                           jnp.concatenate([cv_c.astype(vc.dtype), vc], axis=1), table, ck_c.shape[1])
        new_kv = (kb, vb, kc, vc)
    branches = jnp.stack([out_a, out_b, out_c], axis=2)
    proj = jnp.einsum('btrw,rwd->btrd', branches, w_branch)
    g = jax.nn.sigmoid(gates.reshape(bsz, t, N_BRANCH, D_MODEL))
    mixed = jnp.einsum('btrd,btrd->btd', g, proj)
    x = x + mixed @ w_out
    hn = rms_norm(x, norm2)
    x = x + jnp.square(jax.nn.relu(hn @ w_up)) @ w_down
    return x, (new_buf, h_last.astype(x.dtype)) + new_kv


def setup_inputs(seed: int = 0) -> dict:
    key = jax.random.key(seed)
    ks = jax.random.split(key, 32)
    f32 = jnp.float32

    def nrm(k, shape, s):
        return jax.random.normal(k, shape, f32) * s

    p_swa = min(SWA_WINDOW, PAST_LEN)
    p_cb = min(CB_REACH, PAST_LEN)
    u = jax.random.uniform(ks[16], (DEPTH, D_RNN), f32, 0.9, 0.999)
    sa = u ** (1.0 / LRU_C)
    lam = jnp.log(sa) - jnp.log1p(-sa)
    return {
        'x_prompt': nrm(ks[0], (BATCH, SEQ, D_MODEL), 1.0),
        'x_sample': nrm(ks[1], (DEC_BATCH, DEC_SEQ, D_MODEL), 1.0),
        'state_conv': nrm(ks[2], (DEPTH, DEC_BATCH, CONV_W - 1, D_RNN), 1.0),
        'state_lru': nrm(ks[3], (DEPTH, DEC_BATCH, D_RNN), 0.5),
        'cache_swa_k': nrm(ks[4], (DEPTH, DEC_BATCH, p_swa, SWA_KV_HEADS, HEAD_DIM), 1.0),
        'cache_swa_v': nrm(ks[5], (DEPTH, DEC_BATCH, p_swa, SWA_KV_HEADS, HEAD_DIM), 1.0),
        'cache_cb_k': nrm(ks[6], (DEPTH, DEC_BATCH, p_cb, CB_HEADS, HEAD_DIM), 1.0),
        'cache_cb_v': nrm(ks[7], (DEPTH, DEC_BATCH, p_cb, CB_HEADS, HEAD_DIM), 1.0),
        'norm1_g': 1.0 + nrm(ks[8], (DEPTH, D_MODEL), 0.05),
        'w_in': nrm(ks[9], (DEPTH, D_MODEL, D_IN), D_MODEL ** -0.5),
        'conv_w': nrm(ks[10], (DEPTH, CONV_W, D_RNN), CONV_W ** -0.5),
        'conv_b': nrm(ks[11], (DEPTH, D_RNN), 0.01),
        'lru_wa': nrm(ks[12], (DEPTH, LRU_BLOCKS, LRU_BLOCK, LRU_BLOCK), LRU_BLOCK ** -0.5),
        'lru_ba': nrm(ks[13], (DEPTH, D_RNN), 0.01),
        'lru_wx': nrm(ks[14], (DEPTH, LRU_BLOCKS, LRU_BLOCK, LRU_BLOCK), LRU_BLOCK ** -0.5),
        'lru_bx': nrm(ks[15], (DEPTH, D_RNN), 0.01),
        'lru_lambda': lam,
        'attn_sinks': nrm(ks[17], (DEPTH, SWA_HEADS), 1.0),
        'rel_bias_table': nrm(ks[18], (DEPTH, CB_HEADS, 2 * REL_CLIP + 1), 0.5),
        'w_branch': nrm(ks[19], (DEPTH, N_BRANCH, BRANCH_WIDTH, D_MODEL), BRANCH_WIDTH ** -0.5),
        'w_out': nrm(ks[20], (DEPTH, D_MODEL, D_MODEL), D_MODEL ** -0.5),
        'norm2_g': 1.0 + nrm(ks[21], (DEPTH, D_MODEL), 0.05),
        'w_up': nrm(ks[22], (DEPTH, D_MODEL, D_FF), D_MODEL ** -0.5),
        'w_down': nrm(ks[23], (DEPTH, D_FF, D_MODEL), D_FF ** -0.5),
        'final_g': 1.0 + nrm(ks[24], (D_MODEL,), 0.05),
    }


def reference(x_prompt, x_sample, state_conv, state_lru, cache_swa_k, cache_swa_v, cache_cb_k,
              cache_cb_v, norm1_g, w_in, conv_w, conv_b, lru_wa, lru_ba, lru_wx, lru_bx,
              lru_lambda, attn_sinks, rel_bias_table, w_branch, w_out, norm2_g, w_up, w_down,
              final_g):
    yp, ys = x_prompt, x_sample
    p_conv, p_lru, p_swa_k, p_swa_v, p_cb_k, p_cb_v = [], [], [], [], [], []
    s_conv, s_lru, s_swa_k, s_swa_v, s_cb_k, s_cb_v = [], [], [], [], [], []
    nb = yp.shape[0]
    for l in range(DEPTH):
        lw = (norm1_g[l], w_in[l], conv_w[l], conv_b[l], lru_wa[l], lru_ba[l], lru_wx[l],
              lru_bx[l], lru_lambda[l], attn_sinks[l], rel_bias_table[l], w_branch[l], w_out[l],
              norm2_g[l], w_up[l], w_down[l])
        yp, st_p = trunk_layer(yp, jnp.zeros((nb, CONV_W - 1, D_RNN), yp.dtype),
                               jnp.zeros((nb, D_RNN), jnp.float32), None, *lw)
        ys, st_s = trunk_layer(ys, state_conv[l], state_lru[l],
                               (cache_swa_k[l], cache_swa_v[l], cache_cb_k[l], cache_cb_v[l]), *lw)
        p_conv.append(st_p[0]); p_lru.append(st_p[1]); p_swa_k.append(st_p[2])
        p_swa_v.append(st_p[3]); p_cb_k.append(st_p[4]); p_cb_v.append(st_p[5])
        s_conv.append(st_s[0]); s_lru.append(st_s[1]); s_swa_k.append(st_s[2])
        s_swa_v.append(st_s[3]); s_cb_k.append(st_s[4]); s_cb_v.append(st_s[5])
    y_prompt = rms_norm(yp, final_g)
    y_sample = rms_norm(ys, final_g)
    return (y_prompt, y_sample,
            jnp.stack(p_conv), jnp.stack(p_lru), jnp.stack(p_swa_k), jnp.stack(p_swa_v),
            jnp.stack(p_cb_k), jnp.stack(p_cb_v),
            jnp.stack(s_conv), jnp.stack(s_lru), jnp.stack(s_swa_k), jnp.stack(s_swa_v),
            jnp.stack(s_cb_k), jnp.stack(s_cb_v))
```

```python
import functools

import jax
import jax.numpy as jnp
from jax import lax
from jax.experimental import pallas as pl
from jax.experimental.pallas import tpu as pltpu

F32 = jnp.float32
BF16 = jnp.bfloat16

D_MODEL = 2048
CHUNK = 64
HEAD_DIM = 128
BRANCH_WIDTH = D_MODEL // 2
N_BRANCH = 3
D_RNN = BRANCH_WIDTH
LRU_BLOCKS = 8
LRU_BLOCK = D_RNN // LRU_BLOCKS
CONV_W = 4
LRU_C = 8.0
SWA_HEADS = BRANCH_WIDTH // HEAD_DIM
SWA_KV_HEADS = 2
SWA_GROUP = SWA_HEADS // SWA_KV_HEADS
SWA_WINDOW = 128
SWA_PREV = SWA_WINDOW // CHUNK
CB_HEADS = BRANCH_WIDTH // HEAD_DIM
CB_PREV = 8
CB_REACH = CB_PREV * CHUNK
REL_CLIP = 128
D_FF = 4 * D_MODEL
EPS = 1e-6
NEG = -1e30
ATTN_SCALE = HEAD_DIM ** -0.5

_O_AX, _O_AG, _O_QB, _O_KB, _O_VB, _O_QC, _O_KC, _O_VC, _O_GATES = (
    0, 1024, 2048, 3072, 3328, 3584, 4608, 5632, 6656)
D_IN = _O_GATES + N_BRANCH * D_MODEL
D_KV = 2 * SWA_KV_HEADS * HEAD_DIM + 2 * CB_HEADS * HEAD_DIM
D_Z = D_IN - D_KV
KV_KB, KV_VB, KV_KC, KV_VC = 0, 256, 512, 1536
Z_AX, Z_AG, Z_QB, Z_QC, Z_GATES = 0, 1024, 2048, 3072, 4096

VMEM_LIMIT_BYTES = 56 * 1024 * 1024

Q_BLOCK = 4 * CHUNK
CB_BAND = (CB_PREV + Q_BLOCK // CHUNK) * CHUNK


def _params(semantics):
    return pltpu.CompilerParams(dimension_semantics=semantics, vmem_limit_bytes=VMEM_LIMIT_BYTES)


IN_TN = 1280
IN_KV_TILES = D_KV // IN_TN


def _in_proj_kernel(x_ref, g_ref, w_ref, zkv_ref, z_ref, xn_ref):
    j = pl.program_id(1)

    @pl.when(j == 0)
    def _():
        xf = x_ref[...]
        y = xf * lax.rsqrt(jnp.mean(xf * xf, axis=-1, keepdims=True) + EPS)
        xn_ref[...] = (y * g_ref[...]).astype(BF16)

    acc = jnp.dot(xn_ref[...], w_ref[...], preferred_element_type=F32)

    @pl.when(j < IN_KV_TILES)
    def _():
        zkv_ref[...] = acc

    @pl.when(j >= IN_KV_TILES)
    def _():
        z_ref[...] = acc.astype(BF16)


def _in_proj(x, g, w, tm):
    m = x.shape[0]
    return pl.pallas_call(
        _in_proj_kernel,
        grid=(m // tm, D_IN // IN_TN),
        in_specs=[
            pl.BlockSpec((tm, D_MODEL), lambda i, j: (i, 0)),
            pl.BlockSpec((1, D_MODEL), lambda i, j: (0, 0)),
            pl.BlockSpec((D_MODEL, IN_TN), lambda i, j: (0, j)),
        ],
        out_specs=[
            pl.BlockSpec((tm, IN_TN), lambda i, j: (i, jnp.minimum(j, IN_KV_TILES - 1))),
            pl.BlockSpec((tm, IN_TN), lambda i, j: (i, jnp.maximum(j - IN_KV_TILES, 0))),
        ],
        out_shape=[jax.ShapeDtypeStruct((m, D_KV), F32), jax.ShapeDtypeStruct((m, D_Z), BF16)],
        scratch_shapes=[pltpu.VMEM((tm, D_MODEL), BF16)],
        compiler_params=_params(("parallel", "arbitrary")),
        name="in_proj",
    )(x, g, w)


_XOFF = 8


def _lru_kernel(ax_ref, ag_ref, cbuf_ref, h0_ref, cw_ref, cb_ref, wa_ref, ba_ref, wx_ref, bx_ref,
                lam_ref, out_ref, convo_ref, ho_ref, xbuf, a_s, b_s, h_s):
    t = pl.program_id(1)
    nt = pl.num_programs(1)
    tt = ax_ref.shape[0]
    lo = _XOFF - (CONV_W - 1)

    @pl.when(t == 0)
    def _():
        xbuf[lo:_XOFF, :] = cbuf_ref[...]
        h_s[...] = h0_ref[...]

    xbuf[_XOFF:_XOFF + tt, :] = ax_ref[...].astype(F32)
    u = cb_ref[...] + xbuf[lo:lo + tt, :] * cw_ref[0:1, :]
    for k in range(1, CONV_W):
        u = u + xbuf[lo + k:lo + k + tt, :] * cw_ref[k:k + 1, :]
    tail = xbuf[lo + tt:_XOFF + tt, :]
    xbuf[lo:_XOFF, :] = tail

    ub = u.astype(BF16)
    r_parts, i_parts = [], []
    for n in range(LRU_BLOCKS):
        un = ub[:, n * LRU_BLOCK:(n + 1) * LRU_BLOCK]
        r_parts.append(jnp.dot(un, wa_ref[n], preferred_element_type=F32))
        i_parts.append(jnp.dot(un, wx_ref[n], preferred_element_type=F32))
    r = jax.nn.sigmoid(jnp.concatenate(r_parts, axis=1) + ba_ref[...])
    i = jax.nn.sigmoid(jnp.concatenate(i_parts, axis=1) + bx_ref[...])
    log_a = -LRU_C * r * jax.nn.softplus(-lam_ref[...])
    a_s[...] = jnp.exp(log_a)
    b_s[...] = jnp.sqrt(1.0 - jnp.exp(2.0 * log_a)) * (i * u)

    row = lax.broadcasted_iota(jnp.int32, (8, D_RNN), 0)

    def body(g, h):
        r0 = pl.multiple_of(g * 8, 8)
        a = a_s[pl.ds(r0, 8), :]
        b = b_s[pl.ds(r0, 8), :]
        for s in (1, 2, 4):
            a_sh = pltpu.roll(a, s, 0)
            b_sh = pltpu.roll(b, s, 0)
            m = row >= s
            b = jnp.where(m, a * b_sh + b, b)
            a = jnp.where(m, a * a_sh, a)
        hblk = a * h + b
        b_s[pl.ds(r0, 8), :] = hblk
        return hblk[7:8, :]

    h = lax.fori_loop(0, tt // 8, body, h_s[...])
    h_s[...] = h
    out_ref[...] = (b_s[...] * jax.nn.gelu(ag_ref[...].astype(F32))).astype(BF16)

    @pl.when(t == nt - 1)
    def _():
        convo_ref[...] = tail
        ho_ref[...] = h


def _lru(z, conv_buf, h0, cw, cb, wa, ba, wx, bx, lam, n_seq, t_len, tt):
    nt = t_len // tt
    row = lambda b, t: b * nt + t
    vec = lambda: pl.BlockSpec((1, D_RNN), lambda b, t: (0, 0))
    blk = lambda: pl.BlockSpec((LRU_BLOCKS, LRU_BLOCK, LRU_BLOCK), lambda b, t: (0, 0, 0))
    return pl.pallas_call(
        _lru_kernel,
        grid=(n_seq, nt),
        in_specs=[
            pl.BlockSpec((tt, D_RNN), lambda b, t: (row(b, t), Z_AX // D_RNN)),
            pl.BlockSpec((tt, D_RNN), lambda b, t: (row(b, t), Z_AG // D_RNN)),
            pl.BlockSpec((None, CONV_W - 1, D_RNN), lambda b, t: (b, 0, 0)),
            pl.BlockSpec((None, 1, D_RNN), lambda b, t: (b, 0, 0)),
            pl.BlockSpec((CONV_W, D_RNN), lambda b, t: (0, 0)),
            vec(), blk(), vec(), blk(), vec(), vec(),
        ],
        out_specs=[
            pl.BlockSpec((tt, D_RNN), lambda b, t: (row(b, t), 0)),
            pl.BlockSpec((None, CONV_W - 1, D_RNN), lambda b, t: (b, 0, 0)),
            pl.BlockSpec((None, 1, D_RNN), lambda b, t: (b, 0, 0)),
        ],
        out_shape=[
            jax.ShapeDtypeStruct((n_seq * t_len, D_RNN), BF16),
            jax.ShapeDtypeStruct((n_seq, CONV_W - 1, D_RNN), F32),
            jax.ShapeDtypeStruct((n_seq, 1, D_RNN), F32),
        ],
        scratch_shapes=[
            pltpu.VMEM((_XOFF + tt, D_RNN), F32),
            pltpu.VMEM((tt, D_RNN), F32),
            pltpu.VMEM((tt, D_RNN), F32),
            pltpu.VMEM((1, D_RNN), F32),
        ],
        compiler_params=_params(("parallel", "arbitrary")),
        name="lru",
    )(z, z, conv_buf, h0, cw, cb, wa, ba, wx, bx, lam)


def _dot_nt(a, b):
    return lax.dot_general(a, b, (((1,), (1,)), ((), ())), preferred_element_type=F32)


def _swa_chunk(q, kband, vband, sinks, valid):
    qst = jnp.concatenate([q[:, g * HEAD_DIM:(g + 1) * HEAD_DIM] for g in range(SWA_GROUP)], axis=0)
    s = _dot_nt(qst, kband) * ATTN_SCALE
    if valid is not None:
        s = jnp.where(valid, s, NEG)
    ps = []
    for g in range(SWA_GROUP):
        sg = s[g * CHUNK:(g + 1) * CHUNK, :]
        m = jnp.maximum(jnp.max(sg, axis=-1, keepdims=True), sinks[g])
        e = jnp.exp(sg - m)
        ps.append(e / (jnp.sum(e, axis=-1, keepdims=True) + jnp.exp(sinks[g] - m)))
    p = jnp.concatenate(ps, axis=0).astype(BF16)
    o = jnp.dot(p, vband, preferred_element_type=F32)
    return jnp.concatenate([o[g * CHUNK:(g + 1) * CHUNK, :] for g in range(SWA_GROUP)], axis=1)


def _cb_block(q, kband, vband, bias):
    s = _dot_nt(q, kband) * ATTN_SCALE + bias
    m = jnp.max(s, axis=-1, keepdims=True)
    e = jnp.exp(s - m)
    p = (e / jnp.sum(e, axis=-1, keepdims=True)).astype(BF16)
    return jnp.dot(p, vband, preferred_element_type=F32)


def _cast_rows(dst, src, n_rows, step):
    def body(i, c):
        r0 = pl.multiple_of(i * step, step)
        dst[pl.ds(r0, step), :] = src[pl.ds(r0, step), :].astype(BF16)
        return c
    lax.fori_loop(0, n_rows // step, body, 0)


SWA_BAND = (SWA_PREV + 1) * CHUNK


def _swa_prompt_kernel(sink_ref, q_ref, k_ref, v_ref, o_ref, kb_s, vb_s, *, layer):
    kh = pl.program_id(1)
    qi = pl.program_id(2)
    s_len = k_ref.shape[0]

    @pl.when(qi == 0)
    def _():
        _cast_rows(kb_s, k_ref, s_len, 512)
        _cast_rows(vb_s, v_ref, s_len, 512)

    sinks = [sink_ref[layer, kh * SWA_GROUP + g] for g in range(SWA_GROUP)]
    jchunk = lax.broadcasted_iota(jnp.int32, (1, SWA_BAND), 1) // CHUNK
    for c in range(Q_BLOCK // CHUNK):
        cg = qi * (Q_BLOCK // CHUNK) + c
        sc = jnp.maximum(cg - SWA_PREV, 0)
        s0 = pl.multiple_of(sc * CHUNK, CHUNK)
        valid = (jchunk + sc) <= cg
        o = _swa_chunk(q_ref[c * CHUNK:(c + 1) * CHUNK, :], kb_s[pl.ds(s0, SWA_BAND), :],
                       vb_s[pl.ds(s0, SWA_BAND), :], sinks, valid)
        o_ref[c * CHUNK:(c + 1) * CHUNK, :] = o.astype(BF16)


def _swa_prompt(z, zkv, sinks, layer, n_seq, s_len):
    nq = s_len // Q_BLOCK
    gw = SWA_GROUP * HEAD_DIM
    return pl.pallas_call(
        functools.partial(_swa_prompt_kernel, layer=layer),
        grid=(n_seq, SWA_KV_HEADS, nq),
        in_specs=[
            pl.BlockSpec(memory_space=pltpu.SMEM),
            pl.BlockSpec((Q_BLOCK, gw), lambda b, k, q: (b * nq + q, Z_QB // gw + k)),
            pl.BlockSpec((s_len, HEAD_DIM), lambda b, k, q: (b, KV_KB // HEAD_DIM + k)),
            pl.BlockSpec((s_len, HEAD_DIM), lambda b, k, q: (b, KV_VB // HEAD_DIM + k)),
        ],
        out_specs=pl.BlockSpec((Q_BLOCK, gw), lambda b, k, q: (b * nq + q, k)),
        out_shape=jax.ShapeDtypeStruct((n_seq * s_len, BRANCH_WIDTH), BF16),
        scratch_shapes=[pltpu.VMEM((s_len, HEAD_DIM), BF16), pltpu.VMEM((s_len, HEAD_DIM), BF16)],
        compiler_params=_params(("parallel", "parallel", "arbitrary")),
        name="swa_prompt",
    )(sinks, z, zkv, zkv)


def _swa_step_kernel(sink_ref, q_ref, k_ref, v_ref, ck_ref, cv_ref, o_ref, kb_s, vb_s, *, layer):
    kh = pl.program_id(1)
    n_past = ck_ref.shape[0]
    kb_s[0:n_past, :] = ck_ref[...].astype(BF16)
    vb_s[0:n_past, :] = cv_ref[...].astype(BF16)
    kb_s[n_past:, :] = k_ref[...].astype(BF16)
    vb_s[n_past:, :] = v_ref[...].astype(BF16)
    sinks = [sink_ref[layer, kh * SWA_GROUP + g] for g in range(SWA_GROUP)]
    o_ref[...] = _swa_chunk(q_ref[...], kb_s[...], vb_s[...], sinks, None).astype(BF16)


def _swa_step(z, zkv, cache_k, cache_v, sinks, layer, n_seq):
    n_past = cache_k.shape[2]
    gw = SWA_GROUP * HEAD_DIM
    cache = lambda: pl.BlockSpec((None, None, n_past, HEAD_DIM), lambda b, k: (layer, b, 0, k))
    return pl.pallas_call(
        functools.partial(_swa_step_kernel, layer=layer),
        grid=(n_seq, SWA_KV_HEADS),
        in_specs=[
            pl.BlockSpec(memory_space=pltpu.SMEM),
            pl.BlockSpec((CHUNK, gw), lambda b, k: (b, Z_QB // gw + k)),
            pl.BlockSpec((CHUNK, HEAD_DIM), lambda b, k: (b, KV_KB // HEAD_DIM + k)),
            pl.BlockSpec((CHUNK, HEAD_DIM), lambda b, k: (b, KV_VB // HEAD_DIM + k)),
            cache(), cache(),
        ],
        out_specs=pl.BlockSpec((CHUNK, gw), lambda b, k: (b, k)),
        out_shape=jax.ShapeDtypeStruct((n_seq * CHUNK, BRANCH_WIDTH), BF16),
        scratch_shapes=[pltpu.VMEM((n_past + CHUNK, HEAD_DIM), BF16),
                        pltpu.VMEM((n_past + CHUNK, HEAD_DIM), BF16)],
        compiler_params=_params(("parallel", "parallel")),
        name="swa_step",
    )(sinks, z, zkv, zkv, cache_k, cache_v)


def _cb_prompt_kernel(q_ref, k_ref, v_ref, bias_ref, o_ref, kb_s, vb_s):
    qi = pl.program_id(2)
    s_len = k_ref.shape[0]

    @pl.when(qi == 0)
    def _():
        _cast_rows(kb_s, k_ref, s_len, 512)
        _cast_rows(vb_s, v_ref, s_len, 512)

    sc = jnp.maximum(qi * (Q_BLOCK // CHUNK) - CB_PREV, 0)
    s0 = pl.multiple_of(sc * CHUNK, CHUNK)
    o = _cb_block(q_ref[...], kb_s[pl.ds(s0, CB_BAND), :], vb_s[pl.ds(s0, CB_BAND), :], bias_ref[...])
    o_ref[...] = o.astype(BF16)


def _cb_prompt(z, zkv, bias, n_seq, s_len):
    nq = s_len // Q_BLOCK
    n_var = bias.shape[0]
    return pl.pallas_call(
        _cb_prompt_kernel,
        grid=(n_seq, CB_HEADS, nq),
        in_specs=[
            pl.BlockSpec((Q_BLOCK, HEAD_DIM), lambda b, h, q: (b * nq + q, Z_QC // HEAD_DIM + h)),
            pl.BlockSpec((s_len, HEAD_DIM), lambda b, h, q: (b, KV_KC // HEAD_DIM + h)),
            pl.BlockSpec((s_len, HEAD_DIM), lambda b, h, q: (b, KV_VC // HEAD_DIM + h)),
            pl.BlockSpec((None, None, Q_BLOCK, CB_BAND),
                         lambda b, h, q: (jnp.minimum(q, n_var - 1), h, 0, 0)),
        ],
        out_specs=pl.BlockSpec((Q_BLOCK, HEAD_DIM), lambda b, h, q: (b * nq + q, h)),
        out_shape=jax.ShapeDtypeStruct((n_seq * s_len, BRANCH_WIDTH), BF16),
        scratch_shapes=[pltpu.VMEM((s_len, HEAD_DIM), BF16), pltpu.VMEM((s_len, HEAD_DIM), BF16)],
        compiler_params=_params(("parallel", "parallel", "arbitrary")),
        name="cb_prompt",
    )(z, zkv, zkv, bias)


def _cb_step_kernel(q_ref, k_ref, v_ref, ck_ref, cv_ref, bias_ref, o_ref, kb_s, vb_s):
    n_past = ck_ref.shape[0]
    n_k = n_past + CHUNK
    kb_s[0:n_past, :] = ck_ref[...].astype(BF16)
    vb_s[0:n_past, :] = cv_ref[...].astype(BF16)
    kb_s[n_past:, :] = k_ref[...].astype(BF16)
    vb_s[n_past:, :] = v_ref[...].astype(BF16)
    o_ref[...] = _cb_block(q_ref[...], kb_s[...], vb_s[...], bias_ref[:, 0:n_k]).astype(BF16)


def _cb_step(z, zkv, cache_k, cache_v, bias, layer, n_seq):
    n_past = cache_k.shape[2]
    cache = lambda: pl.BlockSpec((None, None, n_past, HEAD_DIM), lambda b, h: (layer, b, 0, h))
    return pl.pallas_call(
        _cb_step_kernel,
        grid=(n_seq, CB_HEADS),
        in_specs=[
            pl.BlockSpec((CHUNK, HEAD_DIM), lambda b, h: (b, Z_QC // HEAD_DIM + h)),
            pl.BlockSpec((CHUNK, HEAD_DIM), lambda b, h: (b, KV_KC // HEAD_DIM + h)),
            pl.BlockSpec((CHUNK, HEAD_DIM), lambda b, h: (b, KV_VC // HEAD_DIM + h)),
            cache(), cache(),
            pl.BlockSpec((None, CHUNK, bias.shape[2]), lambda b, h: (h, 0, 0)),
        ],
        out_specs=pl.BlockSpec((CHUNK, HEAD_DIM), lambda b, h: (b, h)),
        out_shape=jax.ShapeDtypeStruct((n_seq * CHUNK, BRANCH_WIDTH), BF16),
        scratch_shapes=[pltpu.VMEM((n_past + CHUNK, HEAD_DIM), BF16),
                        pltpu.VMEM((n_past + CHUNK, HEAD_DIM), BF16)],
        compiler_params=_params(("parallel", "parallel")),
        name="cb_step",
    )(z, zkv, zkv, cache_k, cache_v, bias)


def _merge_kernel(x_ref, a_ref, b_ref, c_ref, ga_ref, gb_ref, gc_ref, wb_ref, wo_ref, o_ref):
    mixed = None
    for r, (br, gr) in enumerate(((a_ref, ga_ref), (b_ref, gb_ref), (c_ref, gc_ref))):
        proj = jnp.dot(br[...], wb_ref[r], preferred_element_type=F32)
        gate = jax.nn.sigmoid(gr[...].astype(F32))
        mixed = gate * proj if mixed is None else mixed + gate * proj
    o_ref[...] = x_ref[...] + jnp.dot(mixed.astype(BF16), wo_ref[...], preferred_element_type=F32)


def _merge(x, out_a, out_b, out_c, z, w_branch, w_out, tm):
    m = x.shape[0]
    branch = lambda: pl.BlockSpec((tm, BRANCH_WIDTH), lambda i: (i, 0))
    gate = lambda r: pl.BlockSpec((tm, D_MODEL), lambda i: (i, Z_GATES // D_MODEL + r))
    return pl.pallas_call(
        _merge_kernel,
        grid=(m // tm,),
        in_specs=[
            pl.BlockSpec((tm, D_MODEL), lambda i: (i, 0)),
            branch(), branch(), branch(),
            gate(0), gate(1), gate(2),
            pl.BlockSpec((N_BRANCH, BRANCH_WIDTH, D_MODEL), lambda i: (0, 0, 0),
                         pipeline_mode=pl.Buffered(1)),
            pl.BlockSpec((D_MODEL, D_MODEL), lambda i: (0, 0), pipeline_mode=pl.Buffered(1)),
        ],
        out_specs=pl.BlockSpec((tm, D_MODEL), lambda i: (i, 0)),
        out_shape=jax.ShapeDtypeStruct((m, D_MODEL), F32),
        compiler_params=_params(("parallel",)),
        name="merge",
    )(x, out_a, out_b, out_c, z, z, z, w_branch, w_out)


def _mlp_kernel(x_ref, g_ref, wu_ref, wd_ref, o_ref, hn_ref):
    j = pl.program_id(1)

    @pl.when(j == 0)
    def _():
        xf = x_ref[...]
        y = xf * lax.rsqrt(jnp.mean(xf * xf, axis=-1, keepdims=True) + EPS)
        hn_ref[...] = (y * g_ref[...]).astype(BF16)
        o_ref[...] = xf

    h = jnp.dot(hn_ref[...], wu_ref[...], preferred_element_type=F32)
    h = jnp.square(jnp.maximum(h, 0.0)).astype(BF16)
    o_ref[...] += jnp.dot(h, wd_ref[...], preferred_element_type=F32)


def _mlp(x, g, w_up, w_down, tm, tf):
    m = x.shape[0]
    return pl.pallas_call(
        _mlp_kernel,
        grid=(m // tm, D_FF // tf),
        in_specs=[
            pl.BlockSpec((tm, D_MODEL), lambda i, j: (i, 0)),
            pl.BlockSpec((1, D_MODEL), lambda i, j: (0, 0)),
            pl.BlockSpec((D_MODEL, tf), lambda i, j: (0, j)),
            pl.BlockSpec((tf, D_MODEL), lambda i, j: (j, 0)),
        ],
        out_specs=pl.BlockSpec((tm, D_MODEL), lambda i, j: (i, 0)),
        out_shape=jax.ShapeDtypeStruct((m, D_MODEL), F32),
        scratch_shapes=[pltpu.VMEM((tm, D_MODEL), BF16)],
        compiler_params=_params(("parallel", "arbitrary")),
        name="mlp",
    )(x, g, w_up, w_down)


def _norm_kernel(x_ref, g_ref, o_ref):
    xf = x_ref[...]
    y = xf * lax.rsqrt(jnp.mean(xf * xf, axis=-1, keepdims=True) + EPS)
    o_ref[...] = y * g_ref[...]


def _final_norm(x, g, tm):
    m = x.shape[0]
    return pl.pallas_call(
        _norm_kernel,
        grid=(m // tm,),
        in_specs=[pl.BlockSpec((tm, D_MODEL), lambda i: (i, 0)),
                  pl.BlockSpec((1, D_MODEL), lambda i: (0, 0))],
        out_specs=pl.BlockSpec((tm, D_MODEL), lambda i: (i, 0)),
        out_shape=jax.ShapeDtypeStruct((m, D_MODEL), F32),
        compiler_params=_params(("parallel",)),
        name="final_norm",
    )(x, g)


def _permute_w_in(w_in):
    seg = lambda o, w: w_in[..., o:o + w]
    return jnp.concatenate([
        seg(_O_KB, 256), seg(_O_VB, 256), seg(_O_KC, 1024), seg(_O_VC, 1024),
        seg(_O_AX, 1024), seg(_O_AG, 1024), seg(_O_QB, 1024), seg(_O_QC, 1024),
        seg(_O_GATES, N_BRANCH * D_MODEL)], axis=-1).astype(BF16)


def _cb_bias_tables(table):
    nqc = Q_BLOCK // CHUNK
    i = jnp.arange(Q_BLOCK)[:, None]
    j = jnp.arange(CB_BAND)[None, :]
    variants = []
    for v in range(3):
        c0 = v * nqc
        sc = max(c0 - CB_PREV, 0)
        d = (c0 * CHUNK + i) - (sc * CHUNK + j)
        idx = jnp.clip(d, -REL_CLIP, REL_CLIP) + REL_CLIP
        qc = c0 + i // CHUNK
        kc = sc + j // CHUNK
        valid = (kc <= qc) & (kc >= qc - CB_PREV)
        variants.append(jnp.where(valid[None, None], table.astype(F32)[:, :, idx], NEG))
    return jnp.stack(variants, axis=1)


def _pick_tm(m, want):
    tm = min(want, m)
    while m % tm:
        tm //= 2
    return tm


def _layer(x, n_seq, t_len, conv_buf, h0, caches, layer, p):
    m = x.shape[0]
    zkv, z = _in_proj(x, p["norm1"][layer], p["w_in"][layer], _pick_tm(m, 512))
    out_a, conv_o, h_o = _lru(z, conv_buf, h0, p["conv_w"][layer], p["conv_b"][layer],
                              p["wa"][layer], p["ba"][layer], p["wx"][layer], p["bx"][layer],
                              p["lam"][layer], n_seq, t_len, _pick_tm(t_len, 256))
    if caches is None:
        out_b = _swa_prompt(z, zkv, p["sinks"], layer, n_seq, t_len)
        out_c = _cb_prompt(z, zkv, p["cb_bias"][layer], n_seq, t_len)
    else:
        ck_b, cv_b, ck_c, cv_c = caches
        out_b = _swa_step(z, zkv, ck_b, cv_b, p["sinks"], layer, n_seq)
        out_c = _cb_step(z, zkv, ck_c, cv_c, p["cb_bias"][layer, 2], layer, n_seq)
    x = _merge(x, out_a, out_b, out_c, z, p["w_branch"][layer], p["w_out"][layer], _pick_tm(m, 256))
    x = _mlp(x, p["norm2"][layer], p["w_up"][layer], p["w_down"][layer], _pick_tm(m, 512), 1024)
    return x, zkv, conv_o, h_o[:, 0]


def kernel(x_prompt, x_sample, state_conv, state_lru, cache_swa_k, cache_swa_v, cache_cb_k, cache_cb_v, norm1_g, w_in, conv_w, conv_b, lru_wa, lru_ba, lru_wx, lru_bx, lru_lambda, attn_sinks, rel_bias_table, w_branch, w_out, norm2_g, w_up, w_down, final_g):
    depth = w_in.shape[0]
    nb, s_len, _ = x_prompt.shape
    db, d_len, _ = x_sample.shape
    assert d_len == CHUNK and s_len % Q_BLOCK == 0 and s_len >= CB_BAND
    assert cache_swa_k.shape[2] == SWA_WINDOW and cache_cb_k.shape[2] == CB_REACH

    row = lambda v: v.reshape(depth, 1, -1)
    p = {
        "norm1": row(norm1_g), "norm2": row(norm2_g),
        "w_in": _permute_w_in(w_in),
        "conv_w": conv_w, "conv_b": row(conv_b),
        "wa": lru_wa.astype(BF16), "ba": row(lru_ba), "wx": lru_wx.astype(BF16), "bx": row(lru_bx),
        "lam": row(lru_lambda), "sinks": attn_sinks,
        "cb_bias": _cb_bias_tables(rel_bias_table),
        "w_branch": w_branch.astype(BF16), "w_out": w_out.astype(BF16),
        "w_up": w_up.astype(BF16), "w_down": w_down.astype(BF16),
    }
    ck_b = cache_swa_k.reshape(depth, db, SWA_WINDOW, SWA_KV_HEADS * HEAD_DIM)
    cv_b = cache_swa_v.reshape(depth, db, SWA_WINDOW, SWA_KV_HEADS * HEAD_DIM)
    ck_c = cache_cb_k.reshape(depth, db, CB_REACH, CB_HEADS * HEAD_DIM)
    cv_c = cache_cb_v.reshape(depth, db, CB_REACH, CB_HEADS * HEAD_DIM)

    xp = x_prompt.reshape(nb * s_len, D_MODEL)
    xs = x_sample.reshape(db * d_len, D_MODEL)
    zero_conv = jnp.zeros((nb, CONV_W - 1, D_RNN), F32)
    zero_h = jnp.zeros((nb, 1, D_RNN), F32)
    outs = [[] for _ in range(12)]
    n_swa = min(SWA_WINDOW, s_len)
    n_cb = min(CB_REACH, s_len)
    for l in range(depth):
        xp, zkv_p, conv_p, h_p = _layer(xp, nb, s_len, zero_conv, zero_h, None, l, p)
        xs, zkv_s, conv_s, h_s = _layer(xs, db, d_len, state_conv[l], state_lru[l].reshape(db, 1, D_RNN),
                                        (ck_b, cv_b, ck_c, cv_c), l, p)
        kvp = zkv_p.reshape(nb, s_len, D_KV)
        kvs = zkv_s.reshape(db, d_len, D_KV)
        outs[0].append(conv_p)
        outs[1].append(h_p)
        outs[2].append(kvp[:, s_len - n_swa:, KV_KB:KV_VB].reshape(nb, n_swa, SWA_KV_HEADS, HEAD_DIM))
        outs[3].append(kvp[:, s_len - n_swa:, KV_VB:KV_KC].reshape(nb, n_swa, SWA_KV_HEADS, HEAD_DIM))
        outs[4].append(kvp[:, s_len - n_cb:, KV_KC:KV_VC].reshape(nb, n_cb, CB_HEADS, HEAD_DIM))
        outs[5].append(kvp[:, s_len - n_cb:, KV_VC:D_KV].reshape(nb, n_cb, CB_HEADS, HEAD_DIM))
        outs[6].append(conv_s)
        outs[7].append(h_s)
        outs[8].append(kvs[:, :, KV_KB:KV_VB].reshape(db, d_len, SWA_KV_HEADS, HEAD_DIM))
        outs[9].append(kvs[:, :, KV_VB:KV_KC].reshape(db, d_len, SWA_KV_HEADS, HEAD_DIM))
        outs[10].append(kvs[:, :, KV_KC:KV_VC].reshape(db, d_len, CB_HEADS, HEAD_DIM))
        outs[11].append(kvs[:, :, KV_VC:D_KV].reshape(db, d_len, CB_HEADS, HEAD_DIM))

    fg = final_g.reshape(1, D_MODEL)
    y_prompt = _final_norm(xp, fg, _pick_tm(xp.shape[0], 512)).reshape(nb, s_len, D_MODEL)
    y_sample = _final_norm(xs, fg, _pick_tm(xs.shape[0], 512)).reshape(db, d_len, D_MODEL)
    return (y_prompt, y_sample) + tuple(jnp.stack(o) for o in outs)
```

```python
import functools

import jax
import jax.numpy as jnp
from jax import lax
from jax.experimental import pallas as pl
from jax.experimental.pallas import tpu as pltpu

F32 = jnp.float32
BF16 = jnp.bfloat16

D_MODEL = 2048
CHUNK = 64
HEAD_DIM = 128
BRANCH_WIDTH = D_MODEL // 2
N_BRANCH = 3
D_RNN = BRANCH_WIDTH
LRU_BLOCKS = 8
LRU_BLOCK = D_RNN // LRU_BLOCKS
CONV_W = 4
LRU_C = 8.0
SWA_HEADS = BRANCH_WIDTH // HEAD_DIM
SWA_KV_HEADS = 2
SWA_GROUP = SWA_HEADS // SWA_KV_HEADS
SWA_WINDOW = 128
SWA_PREV = SWA_WINDOW // CHUNK
CB_HEADS = BRANCH_WIDTH // HEAD_DIM
CB_PREV = 8
CB_REACH = CB_PREV * CHUNK
REL_CLIP = 128
D_FF = 4 * D_MODEL
EPS = 1e-6
NEG = -1e30
ATTN_SCALE = HEAD_DIM ** -0.5
LOG2E = 1.4426950408889634

_O_AX, _O_AG, _O_QB, _O_KB, _O_VB, _O_QC, _O_KC, _O_VC, _O_GATES = (
    0, 1024, 2048, 3072, 3328, 3584, 4608, 5632, 6656)
D_IN = _O_GATES + N_BRANCH * D_MODEL
SWA_KVW = SWA_KV_HEADS * HEAD_DIM
CB_KVW = CB_HEADS * HEAD_DIM
D_KV = 2 * SWA_KVW + 2 * CB_KVW
D_Z = D_IN - D_KV
KV_KC, KV_VC, KV_KB, KV_VB = 0, 1024, 2048, 2304
Z_AX, Z_AG, Z_QB, Z_QC, Z_GATES = 0, 1024, 2048, 3072, 4096

VMEM_LIMIT_BYTES = 56 * 1024 * 1024

Q_BLOCK = 4 * CHUNK
Q_CHUNKS = Q_BLOCK // CHUNK
CB_BAND = (CB_PREV + Q_CHUNKS) * CHUNK
CB_QSUB = 2
CB_RLEN = 1024
assert CB_RLEN >= Q_BLOCK + CB_BAND - 1
CB_VARIANTS = 3


def _params(semantics):
    return pltpu.CompilerParams(dimension_semantics=semantics, vmem_limit_bytes=VMEM_LIMIT_BYTES)


IN_TN = 1280
IN_KV_TILES = D_KV // IN_TN


def _in_proj_kernel(x_ref, g_ref, w_ref, zkv_ref, z_ref, xn_ref):
    j = pl.program_id(1)

    @pl.when(j == 0)
    def _():
        xf = x_ref[...]
        y = xf * lax.rsqrt(jnp.mean(xf * xf, axis=-1, keepdims=True) + EPS)
        xn_ref[...] = (y * g_ref[...]).astype(BF16)

    acc = jnp.dot(xn_ref[...], w_ref[...], preferred_element_type=F32)

    @pl.when(j < IN_KV_TILES)
    def _():
        zkv_ref[...] = acc

    @pl.when(j >= IN_KV_TILES)
    def _():
        z_ref[...] = acc.astype(BF16)


def _in_proj(x, g, w, tm):
    m = x.shape[0]
    return pl.pallas_call(
        _in_proj_kernel,
        grid=(m // tm, D_IN // IN_TN),
        in_specs=[
            pl.BlockSpec((tm, D_MODEL), lambda i, j: (i, 0)),
            pl.BlockSpec((1, D_MODEL), lambda i, j: (0, 0)),
            pl.BlockSpec((D_MODEL, IN_TN), lambda i, j: (0, j)),
        ],
        out_specs=[
            pl.BlockSpec((tm, IN_TN), lambda i, j: (i, jnp.minimum(j, IN_KV_TILES - 1))),
            pl.BlockSpec((tm, IN_TN), lambda i, j: (i, jnp.maximum(j - IN_KV_TILES, 0))),
        ],
        out_shape=[jax.ShapeDtypeStruct((m, D_KV), F32), jax.ShapeDtypeStruct((m, D_Z), BF16)],
        scratch_shapes=[pltpu.VMEM((tm, D_MODEL), BF16)],
        compiler_params=_params(("parallel", "arbitrary")),
        name="in_proj",
    )(x, g, w)


_XOFF = 8


def _lru_kernel(ax_ref, ag_ref, cbuf_ref, h0_ref, cw_ref, cb_ref, wa_ref, ba_ref, wx_ref, bx_ref,
                lam_ref, out_ref, convo_ref, ho_ref, xbuf, a_s, b_s, h_s):
    t = pl.program_id(1)
    nt = pl.num_programs(1)
    tt = ax_ref.shape[0]
    lo = _XOFF - (CONV_W - 1)

    @pl.when(t == 0)
    def _():
        xbuf[lo:_XOFF, :] = cbuf_ref[...]
        h_s[...] = h0_ref[...]

    xbuf[_XOFF:_XOFF + tt, :] = ax_ref[...].astype(F32)
    u = cb_ref[...] + xbuf[lo:lo + tt, :] * cw_ref[0:1, :]
    for k in range(1, CONV_W):
        u = u + xbuf[lo + k:lo + k + tt, :] * cw_ref[k:k + 1, :]
    tail = xbuf[lo + tt:_XOFF + tt, :]
    xbuf[lo:_XOFF, :] = tail

    ub = u.astype(BF16)
    r_parts, i_parts = [], []
    for n in range(LRU_BLOCKS):
        un = ub[:, n * LRU_BLOCK:(n + 1) * LRU_BLOCK]
        r_parts.append(jnp.dot(un, wa_ref[n], preferred_element_type=F32))
        i_parts.append(jnp.dot(un, wx_ref[n], preferred_element_type=F32))
    r = jax.nn.sigmoid(jnp.concatenate(r_parts, axis=1) + ba_ref[...])
    i = jax.nn.sigmoid(jnp.concatenate(i_parts, axis=1) + bx_ref[...])
    log_a = -LRU_C * r * jax.nn.softplus(-lam_ref[...])
    a_s[...] = jnp.exp(log_a)
    b_s[...] = jnp.sqrt(1.0 - jnp.exp(2.0 * log_a)) * (i * u)

    row = lax.broadcasted_iota(jnp.int32, (8, D_RNN), 0)

    def body(g, h):
        r0 = pl.multiple_of(g * 8, 8)
        a = a_s[pl.ds(r0, 8), :]
        b = b_s[pl.ds(r0, 8), :]
        for s in (1, 2, 4):
            a_sh = pltpu.roll(a, s, 0)
            b_sh = pltpu.roll(b, s, 0)
            m = row >= s
            b = jnp.where(m, a * b_sh + b, b)
            a = jnp.where(m, a * a_sh, a)
        hblk = a * h + b
        b_s[pl.ds(r0, 8), :] = hblk
        return hblk[7:8, :]

    h = lax.fori_loop(0, tt // 8, body, h_s[...])
    h_s[...] = h
    out_ref[...] = (b_s[...] * jax.nn.gelu(ag_ref[...].astype(F32))).astype(BF16)

    @pl.when(t == nt - 1)
    def _():
        convo_ref[...] = tail
        ho_ref[...] = h


def _lru(z, conv_buf, h0, cw, cb, wa, ba, wx, bx, lam, n_seq, t_len, tt):
    nt = t_len // tt
    row = lambda b, t: b * nt + t
    vec = lambda: pl.BlockSpec((1, D_RNN), lambda b, t: (0, 0))
    blk = lambda: pl.BlockSpec((LRU_BLOCKS, LRU_BLOCK, LRU_BLOCK), lambda b, t: (0, 0, 0))
    return pl.pallas_call(
        _lru_kernel,
        grid=(n_seq, nt),
        in_specs=[
            pl.BlockSpec((tt, D_RNN), lambda b, t: (row(b, t), Z_AX // D_RNN)),
            pl.BlockSpec((tt, D_RNN), lambda b, t: (row(b, t), Z_AG // D_RNN)),
            pl.BlockSpec((None, CONV_W - 1, D_RNN), lambda b, t: (b, 0, 0)),
            pl.BlockSpec((None, 1, D_RNN), lambda b, t: (b, 0, 0)),
            pl.BlockSpec((CONV_W, D_RNN), lambda b, t: (0, 0)),
            vec(), blk(), vec(), blk(), vec(), vec(),
        ],
        out_specs=[
            pl.BlockSpec((tt, D_RNN), lambda b, t: (row(b, t), 0)),
            pl.BlockSpec((None, CONV_W - 1, D_RNN), lambda b, t: (b, 0, 0)),
            pl.BlockSpec((None, 1, D_RNN), lambda b, t: (b, 0, 0)),
        ],
        out_shape=[
            jax.ShapeDtypeStruct((n_seq * t_len, D_RNN), BF16),
            jax.ShapeDtypeStruct((n_seq, CONV_W - 1, D_RNN), F32),
            jax.ShapeDtypeStruct((n_seq, 1, D_RNN), F32),
        ],
        scratch_shapes=[
            pltpu.VMEM((_XOFF + tt, D_RNN), F32),
            pltpu.VMEM((tt, D_RNN), F32),
            pltpu.VMEM((tt, D_RNN), F32),
            pltpu.VMEM((1, D_RNN), F32),
        ],
        compiler_params=_params(("parallel", "arbitrary")),
        name="lru",
    )(z, z, conv_buf, h0, cw, cb, wa, ba, wx, bx, lam)


def _dot_nt(a, b):
    return lax.dot_general(a, b, (((1,), (1,)), ((), ())), preferred_element_type=F32)


def _swa_chunk(q, kband, vband, sinks, valid):
    qst = jnp.concatenate([q[:, g * HEAD_DIM:(g + 1) * HEAD_DIM] for g in range(SWA_GROUP)], axis=0)
    s = _dot_nt(qst, kband) * (ATTN_SCALE * LOG2E)
    if valid is not None:
        s = jnp.where(valid, s, NEG)
    es, inv = [], []
    for g in range(SWA_GROUP):
        sg = s[g * CHUNK:(g + 1) * CHUNK, :]
        sink2 = sinks[g] * LOG2E
        m = jnp.maximum(jnp.max(sg, axis=-1, keepdims=True), sink2)
        e = jnp.exp2(sg - m)
        es.append(e.astype(BF16))
        inv.append(1.0 / (jnp.sum(e, axis=-1, keepdims=True) + jnp.exp2(sink2 - m)))
    o = jnp.dot(jnp.concatenate(es, axis=0), vband, preferred_element_type=F32)
    return jnp.concatenate([o[g * CHUNK:(g + 1) * CHUNK, :] * inv[g] for g in range(SWA_GROUP)], axis=1)


def _cb_attend(q, kband, vband, bias2):
    s = _dot_nt(q, kband) * (ATTN_SCALE * LOG2E) + bias2
    m = jnp.max(s, axis=-1, keepdims=True)
    e = jnp.exp2(s - m)
    inv = 1.0 / jnp.sum(e, axis=-1, keepdims=True)
    return jnp.dot(e.astype(BF16), vband, preferred_element_type=F32) * inv


def _cb_bias_block(r, variant, n_rows, n_cols):
    t = pltpu.roll(jnp.broadcast_to(r, (n_rows, CB_RLEN)), 0, 1, stride=1, stride_axis=0)[:, :n_cols]
    c0 = variant * Q_CHUNKS
    sc = max(c0 - CB_PREV, 0)
    qc = c0 + lax.broadcasted_iota(jnp.int32, (n_rows, n_cols), 0) // CHUNK
    kc = sc + lax.broadcasted_iota(jnp.int32, (n_rows, n_cols), 1) // CHUNK
    return jnp.where(kc <= qc, jnp.where(kc >= qc - CB_PREV, t * LOG2E, NEG), NEG)


def _cast_rows(dst, src, n_rows, step):
    def body(i, c):
        r0 = pl.multiple_of(i * step, step)
        dst[pl.ds(r0, step), :] = src[pl.ds(r0, step), :].astype(BF16)
        return c
    lax.fori_loop(0, n_rows // step, body, 0)


SWA_BAND = (SWA_PREV + 1) * CHUNK


def _swa_prompt_kernel(sink_ref, q_ref, k_ref, v_ref, o_ref, kb_s, vb_s, *, layer):
    kh = pl.program_id(1)
    qi = pl.program_id(2)
    s_len = k_ref.shape[0]

    @pl.when(qi == 0)
    def _():
        _cast_rows(kb_s, k_ref, s_len, 512)
        _cast_rows(vb_s, v_ref, s_len, 512)

    sinks = [sink_ref[layer, kh * SWA_GROUP + g] for g in range(SWA_GROUP)]
    jchunk = lax.broadcasted_iota(jnp.int32, (1, SWA_BAND), 1) // CHUNK
    for c in range(Q_CHUNKS):
        cg = qi * Q_CHUNKS + c
        sc = jnp.maximum(cg - SWA_PREV, 0)
        s0 = pl.multiple_of(sc * CHUNK, CHUNK)
        valid = (jchunk + sc) <= cg
        o = _swa_chunk(q_ref[c * CHUNK:(c + 1) * CHUNK, :], kb_s[pl.ds(s0, SWA_BAND), :],
                       vb_s[pl.ds(s0, SWA_BAND), :], sinks, valid)
        o_ref[c * CHUNK:(c + 1) * CHUNK, :] = o.astype(BF16)


def _swa_prompt(z, zkv, sinks, layer, n_seq, s_len):
    nq = s_len // Q_BLOCK
    gw = SWA_GROUP * HEAD_DIM
    return pl.pallas_call(
        functools.partial(_swa_prompt_kernel, layer=layer),
        grid=(n_seq, SWA_KV_HEADS, nq),
        in_specs=[
            pl.BlockSpec(memory_space=pltpu.SMEM),
            pl.BlockSpec((Q_BLOCK, gw), lambda b, k, q: (b * nq + q, Z_QB // gw + k)),
            pl.BlockSpec((s_len, HEAD_DIM), lambda b, k, q: (b, KV_KB // HEAD_DIM + k)),
            pl.BlockSpec((s_len, HEAD_DIM), lambda b, k, q: (b, KV_VB // HEAD_DIM + k)),
        ],
        out_specs=pl.BlockSpec((Q_BLOCK, gw), lambda b, k, q: (b * nq + q, k)),
        out_shape=jax.ShapeDtypeStruct((n_seq * s_len, BRANCH_WIDTH), BF16),
        scratch_shapes=[pltpu.VMEM((s_len, HEAD_DIM), BF16), pltpu.VMEM((s_len, HEAD_DIM), BF16)],
        compiler_params=_params(("parallel", "parallel", "arbitrary")),
        name="swa_prompt",
    )(sinks, z, zkv, zkv)


def _swa_step_kernel(sink_ref, q_ref, k_ref, v_ref, ck_ref, cv_ref, o_ref, ko_ref, vo_ref, *, layer):
    n_past = ck_ref.shape[0] // SWA_KV_HEADS
    gw = SWA_GROUP * HEAD_DIM
    for kh in range(SWA_KV_HEADS):
        cs = slice(kh * HEAD_DIM, (kh + 1) * HEAD_DIM)
        kn = k_ref[:, cs]
        vn = v_ref[:, cs]
        kfull = jnp.concatenate(
            [ck_ref[pl.ds(kh, n_past, stride=SWA_KV_HEADS), :].astype(BF16), kn.astype(BF16)], axis=0)
        vfull = jnp.concatenate(
            [cv_ref[pl.ds(kh, n_past, stride=SWA_KV_HEADS), :].astype(BF16), vn.astype(BF16)], axis=0)
        sinks = [sink_ref[layer, kh * SWA_GROUP + g] for g in range(SWA_GROUP)]
        o = _swa_chunk(q_ref[:, kh * gw:(kh + 1) * gw], kfull, vfull, sinks, None)
        o_ref[:, kh * gw:(kh + 1) * gw] = o.astype(BF16)
        ko_ref[pl.ds(kh, CHUNK, stride=SWA_KV_HEADS), :] = kn
        vo_ref[pl.ds(kh, CHUNK, stride=SWA_KV_HEADS), :] = vn


def _swa_step(z, zkv, cache_k, cache_v, sinks, layer, n_seq):
    rows_past = cache_k.shape[2]
    rows_new = CHUNK * SWA_KV_HEADS
    cache = lambda: pl.BlockSpec((None, None, rows_past, HEAD_DIM), lambda b: (layer, b, 0, 0))
    new = lambda: pl.BlockSpec((rows_new, HEAD_DIM), lambda b: (b, 0))
    return pl.pallas_call(
        functools.partial(_swa_step_kernel, layer=layer),
        grid=(n_seq,),
        in_specs=[
            pl.BlockSpec(memory_space=pltpu.SMEM),
            pl.BlockSpec((CHUNK, BRANCH_WIDTH), lambda b: (b, Z_QB // BRANCH_WIDTH)),
            pl.BlockSpec((CHUNK, SWA_KVW), lambda b: (b, KV_KB // SWA_KVW)),
            pl.BlockSpec((CHUNK, SWA_KVW), lambda b: (b, KV_VB // SWA_KVW)),
            cache(), cache(),
        ],
        out_specs=[pl.BlockSpec((CHUNK, BRANCH_WIDTH), lambda b: (b, 0)), new(), new()],
        out_shape=[jax.ShapeDtypeStruct((n_seq * CHUNK, BRANCH_WIDTH), BF16),
                   jax.ShapeDtypeStruct((n_seq * rows_new, HEAD_DIM), F32),
                   jax.ShapeDtypeStruct((n_seq * rows_new, HEAD_DIM), F32)],
        compiler_params=_params(("parallel",)),
        name="swa_step",
    )(sinks, z, zkv, zkv, cache_k, cache_v)


def _cb_prompt_kernel(q_ref, k_ref, v_ref, r_ref, o_ref, kb_s, vb_s, bias_s):
    qi = pl.program_id(2)
    s_len = k_ref.shape[0]

    @pl.when(qi == 0)
    def _():
        _cast_rows(kb_s, k_ref, s_len, 512)
        _cast_rows(vb_s, v_ref, s_len, 512)
        for v in range(CB_VARIANTS):
            bias_s[v] = _cb_bias_block(r_ref[v], v, Q_BLOCK, CB_BAND)

    for sub in range(CB_QSUB):
        blk = qi * CB_QSUB + sub
        sc = jnp.maximum(blk * Q_CHUNKS - CB_PREV, 0)
        s0 = pl.multiple_of(sc * CHUNK, CHUNK)
        rows = slice(sub * Q_BLOCK, (sub + 1) * Q_BLOCK)
        o = _cb_attend(q_ref[rows, :], kb_s[pl.ds(s0, CB_BAND), :], vb_s[pl.ds(s0, CB_BAND), :],
                       bias_s[jnp.minimum(blk, CB_VARIANTS - 1)])
        o_ref[rows, :] = o.astype(BF16)


def _cb_prompt(z, zkv, rows, layer, n_seq, s_len):
    tq = Q_BLOCK * CB_QSUB
    nq = s_len // tq
    return pl.pallas_call(
        _cb_prompt_kernel,
        grid=(n_seq, CB_HEADS, nq),
        in_specs=[
            pl.BlockSpec((tq, HEAD_DIM), lambda b, h, q: (b * nq + q, Z_QC // HEAD_DIM + h)),
            pl.BlockSpec((s_len, HEAD_DIM), lambda b, h, q: (b, KV_KC // HEAD_DIM + h)),
            pl.BlockSpec((s_len, HEAD_DIM), lambda b, h, q: (b, KV_VC // HEAD_DIM + h)),
            pl.BlockSpec((None, CB_VARIANTS, None, 1, CB_RLEN), lambda b, h, q: (layer, 0, h, 0, 0)),
        ],
        out_specs=pl.BlockSpec((tq, HEAD_DIM), lambda b, h, q: (b * nq + q, h)),
        out_shape=jax.ShapeDtypeStruct((n_seq * s_len, BRANCH_WIDTH), BF16),
        scratch_shapes=[pltpu.VMEM((s_len, HEAD_DIM), BF16), pltpu.VMEM((s_len, HEAD_DIM), BF16),
                        pltpu.VMEM((CB_VARIANTS, Q_BLOCK, CB_BAND), F32)],
        compiler_params=_params(("parallel", "parallel", "arbitrary")),
        name="cb_prompt",
    )(z, zkv, zkv, rows)


def _cb_step_kernel(q_ref, k_ref, v_ref, ck_ref, cv_ref, r_ref, o_ref, ko_ref, vo_ref, bias_s):
    n_past = ck_ref.shape[0] // CB_HEADS
    n_k = n_past + CHUNK

    @pl.when(pl.program_id(0) == 0)
    def _():
        for h in range(CB_HEADS):
            bias_s[h] = _cb_bias_block(r_ref[h], CB_VARIANTS - 1, CHUNK, n_k)

    for h in range(CB_HEADS):
        cs = slice(h * HEAD_DIM, (h + 1) * HEAD_DIM)
        kn = k_ref[:, cs]
        vn = v_ref[:, cs]
        kfull = jnp.concatenate(
            [ck_ref[pl.ds(h, n_past, stride=CB_HEADS), :].astype(BF16), kn.astype(BF16)], axis=0)
        vfull = jnp.concatenate(
            [cv_ref[pl.ds(h, n_past, stride=CB_HEADS), :].astype(BF16), vn.astype(BF16)], axis=0)
        o_ref[:, cs] = _cb_attend(q_ref[:, cs], kfull, vfull, bias_s[h]).astype(BF16)
        ko_ref[pl.ds(h, CHUNK, stride=CB_HEADS), :] = kn
        vo_ref[pl.ds(h, CHUNK, stride=CB_HEADS), :] = vn


def _cb_step(z, zkv, cache_k, cache_v, rows, layer, n_seq):
    rows_past = cache_k.shape[2]
    n_k = rows_past // CB_HEADS + CHUNK
    rows_new = CHUNK * CB_HEADS
    cache = lambda: pl.BlockSpec((None, None, rows_past, HEAD_DIM), lambda b: (layer, b, 0, 0))
    new = lambda: pl.BlockSpec((rows_new, HEAD_DIM), lambda b: (b, 0))
    return pl.pallas_call(
        _cb_step_kernel,
        grid=(n_seq,),
        in_specs=[
            pl.BlockSpec((CHUNK, BRANCH_WIDTH), lambda b: (b, Z_QC // BRANCH_WIDTH)),
            pl.BlockSpec((CHUNK, CB_KVW), lambda b: (b, KV_KC // CB_KVW)),
            pl.BlockSpec((CHUNK, CB_KVW), lambda b: (b, KV_VC // CB_KVW)),
            cache(), cache(),
            pl.BlockSpec((None, None, CB_HEADS, 1, CB_RLEN),
                         lambda b: (layer, CB_VARIANTS - 1, 0, 0, 0)),
        ],
        out_specs=[pl.BlockSpec((CHUNK, BRANCH_WIDTH), lambda b: (b, 0)), new(), new()],
        out_shape=[jax.ShapeDtypeStruct((n_seq * CHUNK, BRANCH_WIDTH), BF16),
                   jax.ShapeDtypeStruct((n_seq * rows_new, HEAD_DIM), F32),
                   jax.ShapeDtypeStruct((n_seq * rows_new, HEAD_DIM), F32)],
        scratch_shapes=[pltpu.VMEM((CB_HEADS, CHUNK, n_k), F32)],
        compiler_params=_params(("arbitrary",)),
        name="cb_step",
    )(z, zkv, zkv, cache_k, cache_v, rows)


def _merge_kernel(x_ref, a_ref, b_ref, c_ref, ga_ref, gb_ref, gc_ref, wb_ref, wo_ref, o_ref):
    mixed = None
    for r, (br, gr) in enumerate(((a_ref, ga_ref), (b_ref, gb_ref), (c_ref, gc_ref))):
        proj = jnp.dot(br[...], wb_ref[r], preferred_element_type=F32)
        gate = jax.nn.sigmoid(gr[...].astype(F32))
        mixed = gate * proj if mixed is None else mixed + gate * proj
    o_ref[...] = x_ref[...] + jnp.dot(mixed.astype(BF16), wo_ref[...], preferred_element_type=F32)


def _merge(x, out_a, out_b, out_c, z, w_branch, w_out, tm):
    m = x.shape[0]
    branch = lambda: pl.BlockSpec((tm, BRANCH_WIDTH), lambda i: (i, 0))
    gate = lambda r: pl.BlockSpec((tm, D_MODEL), lambda i: (i, Z_GATES // D_MODEL + r))
    return pl.pallas_call(
        _merge_kernel,
        grid=(m // tm,),
        in_specs=[
            pl.BlockSpec((tm, D_MODEL), lambda i: (i, 0)),
            branch(), branch(), branch(),
            gate(0), gate(1), gate(2),
            pl.BlockSpec((N_BRANCH, BRANCH_WIDTH, D_MODEL), lambda i: (0, 0, 0),
                         pipeline_mode=pl.Buffered(1)),
            pl.BlockSpec((D_MODEL, D_MODEL), lambda i: (0, 0), pipeline_mode=pl.Buffered(1)),
        ],
        out_specs=pl.BlockSpec((tm, D_MODEL), lambda i: (i, 0)),
        out_shape=jax.ShapeDtypeStruct((m, D_MODEL), F32),
        compiler_params=_params(("parallel",)),
        name="merge",
    )(x, out_a, out_b, out_c, z, z, z, w_branch, w_out)


def _mlp_kernel(x_ref, g_ref, wu_ref, wd_ref, o_ref, hn_ref):
    j = pl.program_id(1)

    @pl.when(j == 0)
    def _():
        xf = x_ref[...]
        y = xf * lax.rsqrt(jnp.mean(xf * xf, axis=-1, keepdims=True) + EPS)
        hn_ref[...] = (y * g_ref[...]).astype(BF16)
        o_ref[...] = xf

    h = jnp.dot(hn_ref[...], wu_ref[...], preferred_element_type=F32)
    h = jnp.square(jnp.maximum(h, 0.0)).astype(BF16)
    o_ref[...] += jnp.dot(h, wd_ref[...], preferred_element_type=F32)


def _mlp(x, g, w_up, w_down, tm, tf):
    m = x.shape[0]
    return pl.pallas_call(
        _mlp_kernel,
        grid=(m // tm, D_FF // tf),
        in_specs=[
            pl.BlockSpec((tm, D_MODEL), lambda i, j: (i, 0)),
            pl.BlockSpec((1, D_MODEL), lambda i, j: (0, 0)),
            pl.BlockSpec((D_MODEL, tf), lambda i, j: (0, j)),
            pl.BlockSpec((tf, D_MODEL), lambda i, j: (j, 0)),
        ],
        out_specs=pl.BlockSpec((tm, D_MODEL), lambda i, j: (i, 0)),
        out_shape=jax.ShapeDtypeStruct((m, D_MODEL), F32),
        scratch_shapes=[pltpu.VMEM((tm, D_MODEL), BF16)],
        compiler_params=_params(("parallel", "arbitrary")),
        name="mlp",
    )(x, g, w_up, w_down)


def _norm_kernel(x_ref, g_ref, o_ref):
    xf = x_ref[...]
    y = xf * lax.rsqrt(jnp.mean(xf * xf, axis=-1, keepdims=True) + EPS)
    o_ref[...] = y * g_ref[...]


def _final_norm(x, g, tm):
    m = x.shape[0]
    return pl.pallas_call(
        _norm_kernel,
        grid=(m // tm,),
        in_specs=[pl.BlockSpec((tm, D_MODEL), lambda i: (i, 0)),
                  pl.BlockSpec((1, D_MODEL), lambda i: (0, 0))],
        out_specs=pl.BlockSpec((tm, D_MODEL), lambda i: (i, 0)),
        out_shape=jax.ShapeDtypeStruct((m, D_MODEL), F32),
        compiler_params=_params(("parallel",)),
        name="final_norm",
    )(x, g)


def _permute_w_in(w_in):
    seg = lambda o, w: w_in[..., o:o + w]
    return jnp.concatenate([
        seg(_O_KC, CB_KVW), seg(_O_VC, CB_KVW), seg(_O_KB, SWA_KVW), seg(_O_VB, SWA_KVW),
        seg(_O_AX, D_RNN), seg(_O_AG, D_RNN), seg(_O_QB, BRANCH_WIDTH), seg(_O_QC, BRANCH_WIDTH),
        seg(_O_GATES, N_BRANCH * D_MODEL)], axis=-1).astype(BF16)


def _cb_bias_rows(table):
    m = jnp.arange(CB_RLEN)
    rel = jnp.where(m < CB_BAND, m, m - CB_RLEN)
    out = []
    for v in range(CB_VARIANTS):
        c0 = v * Q_CHUNKS
        sc = max(c0 - CB_PREV, 0)
        d = (c0 - sc) * CHUNK - rel
        idx = jnp.clip(d, -REL_CLIP, REL_CLIP) + REL_CLIP
        out.append(table.astype(F32)[:, :, idx])
    return jnp.stack(out, axis=1)[:, :, :, None, :]


def _pick_tm(m, want):
    tm = min(want, m)
    while m % tm:
        tm //= 2
    return tm


def _layer(x, n_seq, t_len, conv_buf, h0, caches, layer, p):
    m = x.shape[0]
    zkv, z = _in_proj(x, p["norm1"][layer], p["w_in"][layer], _pick_tm(m, 1024))
    out_a, conv_o, h_o = _lru(z, conv_buf, h0, p["conv_w"][layer], p["conv_b"][layer],
                              p["wa"][layer], p["ba"][layer], p["wx"][layer], p["bx"][layer],
                              p["lam"][layer], n_seq, t_len, _pick_tm(t_len, 256))
    if caches is None:
        out_b = _swa_prompt(z, zkv, p["sinks"], layer, n_seq, t_len)
        out_c = _cb_prompt(z, zkv, p["cb_rows"], layer, n_seq, t_len)
        kv3 = zkv.reshape(n_seq, t_len, D_KV)
        n_swa = min(SWA_WINDOW, t_len)
        n_cb = min(CB_REACH, t_len)
        kv = (kv3[:, t_len - n_swa:, KV_KB:KV_KB + SWA_KVW], kv3[:, t_len - n_swa:, KV_VB:KV_VB + SWA_KVW],
              kv3[:, t_len - n_cb:, KV_KC:KV_KC + CB_KVW], kv3[:, t_len - n_cb:, KV_VC:KV_VC + CB_KVW])
    else:
        ck_b, cv_b, ck_c, cv_c = caches
        out_b, kb, vb = _swa_step(z, zkv, ck_b, cv_b, p["sinks"], layer, n_seq)
        out_c, kc, vc = _cb_step(z, zkv, ck_c, cv_c, p["cb_rows"], layer, n_seq)
        kv = (kb, vb, kc, vc)
    x = _merge(x, out_a, out_b, out_c, z, p["w_branch"][layer], p["w_out"][layer], _pick_tm(m, 256))
    x = _mlp(x, p["norm2"][layer], p["w_up"][layer], p["w_down"][layer], _pick_tm(m, 512), 1024)
    return x, conv_o, h_o[:, 0], kv


def kernel(x_prompt, x_sample, state_conv, state_lru, cache_swa_k, cache_swa_v, cache_cb_k, cache_cb_v, norm1_g, w_in, conv_w, conv_b, lru_wa, lru_ba, lru_wx, lru_bx, lru_lambda, attn_sinks, rel_bias_table, w_branch, w_out, norm2_g, w_up, w_down, final_g):
    depth = w_in.shape[0]
    nb, s_len, _ = x_prompt.shape
    db, d_len, _ = x_sample.shape
    assert d_len == CHUNK and s_len % (Q_BLOCK * CB_QSUB) == 0 and s_len >= CB_BAND
    assert cache_swa_k.shape[2] == SWA_WINDOW and cache_cb_k.shape[2] == CB_REACH

    row = lambda v: v.reshape(depth, 1, -1)
    p = {
        "norm1": row(norm1_g), "norm2": row(norm2_g),
        "w_in": _permute_w_in(w_in),
        "conv_w": conv_w, "conv_b": row(conv_b),
        "wa": lru_wa.astype(BF16), "ba": row(lru_ba), "wx": lru_wx.astype(BF16), "bx": row(lru_bx),
        "lam": row(lru_lambda), "sinks": attn_sinks,
        "cb_rows": _cb_bias_rows(rel_bias_table),
        "w_branch": w_branch.astype(BF16), "w_out": w_out.astype(BF16),
        "w_up": w_up.astype(BF16), "w_down": w_down.astype(BF16),
    }
    caches = (cache_swa_k.reshape(depth, db, SWA_WINDOW * SWA_KV_HEADS, HEAD_DIM),
              cache_swa_v.reshape(depth, db, SWA_WINDOW * SWA_KV_HEADS, HEAD_DIM),
              cache_cb_k.reshape(depth, db, CB_REACH * CB_HEADS, HEAD_DIM),
              cache_cb_v.reshape(depth, db, CB_REACH * CB_HEADS, HEAD_DIM))

    xp = x_prompt.reshape(nb * s_len, D_MODEL)
    xs = x_sample.reshape(db * d_len, D_MODEL)
    zero_conv = jnp.zeros((nb, CONV_W - 1, D_RNN), F32)
    zero_h = jnp.zeros((nb, 1, D_RNN), F32)
    heads = (SWA_KV_HEADS, SWA_KV_HEADS, CB_HEADS, CB_HEADS)
    outs = [[] for _ in range(12)]
    for l in range(depth):
        xp, conv_p, h_p, kv_p = _layer(xp, nb, s_len, zero_conv, zero_h, None, l, p)
        xs, conv_s, h_s, kv_s = _layer(xs, db, d_len, state_conv[l], state_lru[l].reshape(db, 1, D_RNN),
                                       caches, l, p)
        outs[0].append(conv_p)
        outs[1].append(h_p)
        outs[6].append(conv_s)
        outs[7].append(h_s)
        for n in range(4):
            outs[2 + n].append(kv_p[n].reshape(nb, -1, heads[n], HEAD_DIM))
            outs[8 + n].append(kv_s[n].reshape(db, d_len, heads[n], HEAD_DIM))

    fg = final_g.reshape(1, D_MODEL)
    y_prompt = _final_norm(xp, fg, _pick_tm(xp.shape[0], 512)).reshape(nb, s_len, D_MODEL)
    y_sample = _final_norm(xs, fg, _pick_tm(xs.shape[0], 512)).reshape(db, d_len, D_MODEL)
    return (y_prompt, y_sample) + tuple(jnp.stack(o) for o in outs)
```

```python
import functools

import jax
import jax.numpy as jnp
from jax import lax
from jax.experimental import pallas as pl
from jax.experimental.pallas import tpu as pltpu

F32 = jnp.float32
BF16 = jnp.bfloat16

D_MODEL = 2048
CHUNK = 64
HEAD_DIM = 128
BRANCH_WIDTH = D_MODEL // 2
N_BRANCH = 3
D_RNN = BRANCH_WIDTH
LRU_BLOCKS = 8
LRU_BLOCK = D_RNN // LRU_BLOCKS
CONV_W = 4
LRU_C = 8.0
SWA_HEADS = BRANCH_WIDTH // HEAD_DIM
SWA_KV_HEADS = 2
SWA_GROUP = SWA_HEADS // SWA_KV_HEADS
SWA_WINDOW = 128
SWA_PREV = SWA_WINDOW // CHUNK
CB_HEADS = BRANCH_WIDTH // HEAD_DIM
CB_PREV = 8
CB_REACH = CB_PREV * CHUNK
REL_CLIP = 128
D_FF = 4 * D_MODEL
EPS = 1e-6
NEG = -1e30
ATTN_SCALE = HEAD_DIM ** -0.5
LOG2E = 1.4426950408889634

_O_AX, _O_AG, _O_QB, _O_KB, _O_VB, _O_QC, _O_KC, _O_VC, _O_GATES = (
    0, 1024, 2048, 3072, 3328, 3584, 4608, 5632, 6656)
D_IN = _O_GATES + N_BRANCH * D_MODEL
SWA_KVW = SWA_KV_HEADS * HEAD_DIM
CB_KVW = CB_HEADS * HEAD_DIM
D_KV = 2 * SWA_KVW + 2 * CB_KVW
D_Z = D_IN - D_KV
KV_KC, KV_VC, KV_KB, KV_VB = 0, 1024, 2048, 2304
Z_AX, Z_AG, Z_QB, Z_QC, Z_GATES = 0, 1024, 2048, 3072, 4096

VMEM_LIMIT_BYTES = 56 * 1024 * 1024

Q_BLOCK = 4 * CHUNK
Q_CHUNKS = Q_BLOCK // CHUNK
CB_BAND = (CB_PREV + Q_CHUNKS) * CHUNK
CB_QSUB = 4
CB_RLEN = 1024
assert CB_RLEN >= Q_BLOCK + CB_BAND - 1
CB_VARIANTS = 3


def _params(semantics):
    return pltpu.CompilerParams(dimension_semantics=semantics, vmem_limit_bytes=VMEM_LIMIT_BYTES)


def _rms(xf, g):
    return xf * lax.rsqrt(jnp.mean(xf * xf, axis=-1, keepdims=True) + EPS) * g


IN_TN = 1280
IN_KV_TILES = D_KV // IN_TN


def _in_proj_kernel(x_ref, g_ref, w_ref, zkv_ref, z_ref, xn_ref):
    j = pl.program_id(1)

    @pl.when(j == 0)
    def _():
        xn_ref[...] = _rms(x_ref[...], g_ref[...]).astype(BF16)

    acc = jnp.dot(xn_ref[...], w_ref[...], preferred_element_type=F32)

    @pl.when(j < IN_KV_TILES)
    def _():
        zkv_ref[...] = acc

    @pl.when(j >= IN_KV_TILES)
    def _():
        z_ref[...] = acc.astype(BF16)


def _in_proj(x, g, w, layer, tm):
    m = x.shape[0]
    return pl.pallas_call(
        _in_proj_kernel,
        grid=(m // tm, D_IN // IN_TN),
        in_specs=[
            pl.BlockSpec((tm, D_MODEL), lambda i, j: (i, 0)),
            pl.BlockSpec((1, D_MODEL), lambda i, j: (0, 0)),
            pl.BlockSpec((None, D_MODEL, IN_TN), lambda i, j: (layer, 0, j)),
        ],
        out_specs=[
            pl.BlockSpec((tm, IN_TN), lambda i, j: (i, jnp.minimum(j, IN_KV_TILES - 1))),
            pl.BlockSpec((tm, IN_TN), lambda i, j: (i, jnp.maximum(j - IN_KV_TILES, 0))),
        ],
        out_shape=[jax.ShapeDtypeStruct((m, D_KV), F32), jax.ShapeDtypeStruct((m, D_Z), BF16)],
        scratch_shapes=[pltpu.VMEM((tm, D_MODEL), BF16)],
        compiler_params=_params(("parallel", "arbitrary")),
        name="in_proj",
    )(x, g, w)


_XPAD = 8


def _lru_kernel(ax_ref, ag_ref, cbuf_ref, h0_ref, cw_ref, cb_ref, wa_ref, ba_ref, wx_ref, bx_ref,
                lam_ref, out_ref, convo_ref, ho_ref, xbuf, a_s, b_s, h_s):
    t = pl.program_id(1)
    nt = pl.num_programs(1)
    tt = ax_ref.shape[0]

    @pl.when(t == 0)
    def _():
        xbuf[...] = jnp.zeros_like(xbuf)
        xbuf[_XPAD - (CONV_W - 1):, :] = cbuf_ref[...]
        h_s[...] = h0_ref[...]

    x = ax_ref[...].astype(F32)
    xe = jnp.concatenate([xbuf[...], x], axis=0)
    acc = xe * cw_ref[0:1, :]
    for k in range(1, CONV_W):
        acc = xe * cw_ref[k:k + 1, :] + pltpu.roll(acc, 1, 0)
    u = cb_ref[...] + acc[_XPAD:, :]
    tail = x[tt - (CONV_W - 1):, :]
    xbuf[...] = x[tt - _XPAD:, :]

    ub = u.astype(BF16)
    r_parts, i_parts = [], []
    for n in range(LRU_BLOCKS):
        un = ub[:, n * LRU_BLOCK:(n + 1) * LRU_BLOCK]
        r_parts.append(jnp.dot(un, wa_ref[n], preferred_element_type=F32))
        i_parts.append(jnp.dot(un, wx_ref[n], preferred_element_type=F32))
    r = jax.nn.sigmoid(jnp.concatenate(r_parts, axis=1) + ba_ref[...])
    i = jax.nn.sigmoid(jnp.concatenate(i_parts, axis=1) + bx_ref[...])
    log_a = -LRU_C * r * jax.nn.softplus(-lam_ref[...])
    a = jnp.exp(log_a)
    a_s[...] = a
    b_s[...] = jnp.sqrt(1.0 - a * a) * (i * u)

    row = lax.broadcasted_iota(jnp.int32, (8, D_RNN), 0)

    def body(g, h):
        r0 = pl.multiple_of(g * 8, 8)
        a = a_s[pl.ds(r0, 8), :]
        b = b_s[pl.ds(r0, 8), :]
        for s in (1, 2, 4):
            a_sh = pltpu.roll(a, s, 0)
            b_sh = pltpu.roll(b, s, 0)
            m = row >= s
            b = jnp.where(m, a * b_sh + b, b)
            a = jnp.where(m, a * a_sh, a)
        hblk = a * h + b
        b_s[pl.ds(r0, 8), :] = hblk
        return hblk[7:8, :]

    h = lax.fori_loop(0, tt // 8, body, h_s[...])
    h_s[...] = h
    out_ref[...] = (b_s[...] * jax.nn.gelu(ag_ref[...].astype(F32))).astype(BF16)

    @pl.when(t == nt - 1)
    def _():
        convo_ref[...] = tail
        ho_ref[...] = h


def _lru(z, conv_buf, h0, cw, cb, wa, ba, wx, bx, lam, n_seq, t_len, tt):
    nt = t_len // tt
    row = lambda b, t: b * nt + t
    vec = lambda: pl.BlockSpec((1, D_RNN), lambda b, t: (0, 0))
    blk = lambda: pl.BlockSpec((LRU_BLOCKS, LRU_BLOCK, LRU_BLOCK), lambda b, t: (0, 0, 0))
    return pl.pallas_call(
        _lru_kernel,
        grid=(n_seq, nt),
        in_specs=[
            pl.BlockSpec((tt, D_RNN), lambda b, t: (row(b, t), Z_AX // D_RNN)),
            pl.BlockSpec((tt, D_RNN), lambda b, t: (row(b, t), Z_AG // D_RNN)),
            pl.BlockSpec((None, CONV_W - 1, D_RNN), lambda b, t: (b, 0, 0)),
            pl.BlockSpec((None, 1, D_RNN), lambda b, t: (b, 0, 0)),
            pl.BlockSpec((CONV_W, D_RNN), lambda b, t: (0, 0)),
            vec(), blk(), vec(), blk(), vec(), vec(),
        ],
        out_specs=[
            pl.BlockSpec((tt, D_RNN), lambda b, t: (row(b, t), 0)),
            pl.BlockSpec((None, CONV_W - 1, D_RNN), lambda b, t: (b, 0, 0)),
            pl.BlockSpec((None, 1, D_RNN), lambda b, t: (b, 0, 0)),
        ],
        out_shape=[
            jax.ShapeDtypeStruct((n_seq * t_len, D_RNN), BF16),
            jax.ShapeDtypeStruct((n_seq, CONV_W - 1, D_RNN), F32),
            jax.ShapeDtypeStruct((n_seq, 1, D_RNN), F32),
        ],
        scratch_shapes=[
            pltpu.VMEM((_XPAD, D_RNN), F32),
            pltpu.VMEM((tt, D_RNN), F32),
            pltpu.VMEM((tt, D_RNN), F32),
            pltpu.VMEM((1, D_RNN), F32),
        ],
        compiler_params=_params(("parallel", "arbitrary")),
        name="lru",
    )(z, z, conv_buf, h0, cw, cb, wa, ba, wx, bx, lam)


def _dot_nt(a, b):
    return lax.dot_general(a, b, (((1,), (1,)), ((), ())), preferred_element_type=F32)


def _with_ones(v):
    return jnp.concatenate([v, jnp.ones_like(v)], axis=1)


def _swa_chunk(q, kband, vext, sinks, valid):
    qst = jnp.concatenate([q[:, g * HEAD_DIM:(g + 1) * HEAD_DIM] for g in range(SWA_GROUP)], axis=0)
    s = _dot_nt(qst, kband) * (ATTN_SCALE * LOG2E)
    if valid is not None:
        s = jnp.where(valid, s, NEG)
    es, sink_e = [], []
    for g in range(SWA_GROUP):
        sg = s[g * CHUNK:(g + 1) * CHUNK, :]
        sink2 = sinks[g] * LOG2E
        m = jnp.maximum(jnp.max(sg, axis=-1, keepdims=True), sink2)
        es.append(jnp.exp2(sg - m).astype(BF16))
        sink_e.append(jnp.exp2(sink2 - m))
    r = jnp.dot(jnp.concatenate(es, axis=0), vext, preferred_element_type=F32)
    outs = []
    for g in range(SWA_GROUP):
        rg = r[g * CHUNK:(g + 1) * CHUNK, :]
        outs.append(rg[:, :HEAD_DIM] * (1.0 / (rg[:, HEAD_DIM:] + sink_e[g])))
    return jnp.concatenate(outs, axis=1)


def _cb_attend(q, kband, vext, bias2):
    s = _dot_nt(q, kband) * (ATTN_SCALE * LOG2E) + bias2
    m = jnp.max(s, axis=-1, keepdims=True)
    r = jnp.dot(jnp.exp2(s - m).astype(BF16), vext, preferred_element_type=F32)
    return r[:, :HEAD_DIM] * (1.0 / r[:, HEAD_DIM:])


def _cb_bias_block(r, variant, n_rows, n_cols):
    t = pltpu.roll(jnp.broadcast_to(r, (n_rows, CB_RLEN)), 0, 1, stride=1, stride_axis=0)[:, :n_cols]
    c0 = variant * Q_CHUNKS
    sc = max(c0 - CB_PREV, 0)
    qc = c0 + lax.broadcasted_iota(jnp.int32, (n_rows, n_cols), 0) // CHUNK
    kc = sc + lax.broadcasted_iota(jnp.int32, (n_rows, n_cols), 1) // CHUNK
    return jnp.where(kc <= qc, jnp.where(kc >= qc - CB_PREV, t * LOG2E, NEG), NEG)


def _cast_rows(dst, src, n_rows, step, ones=False):
    def body(i, c):
        r0 = pl.multiple_of(i * step, step)
        v = src[pl.ds(r0, step), :].astype(BF16)
        dst[pl.ds(r0, step), :] = _with_ones(v) if ones else v
        return c
    lax.fori_loop(0, n_rows // step, body, 0)


SWA_BAND = (SWA_PREV + 1) * CHUNK


def _swa_prompt_kernel(sink_ref, q_ref, k_ref, v_ref, o_ref, kb_s, vb_s, *, layer):
    kh = pl.program_id(1)
    qi = pl.program_id(2)
    s_len = k_ref.shape[0]

    @pl.when(qi == 0)
    def _():
        _cast_rows(kb_s, k_ref, s_len, 512)
        _cast_rows(vb_s, v_ref, s_len, 512, ones=True)

    sinks = [sink_ref[layer, kh * SWA_GROUP + g] for g in range(SWA_GROUP)]
    jchunk = lax.broadcasted_iota(jnp.int32, (1, SWA_BAND), 1) // CHUNK
    for c in range(Q_CHUNKS):
        cg = qi * Q_CHUNKS + c
        sc = jnp.maximum(cg - SWA_PREV, 0)
        s0 = pl.multiple_of(sc * CHUNK, CHUNK)
        valid = (jchunk + sc) <= cg
        o = _swa_chunk(q_ref[c * CHUNK:(c + 1) * CHUNK, :], kb_s[pl.ds(s0, SWA_BAND), :],
                       vb_s[pl.ds(s0, SWA_BAND), :], sinks, valid)
        o_ref[c * CHUNK:(c + 1) * CHUNK, :] = o.astype(BF16)


def _swa_prompt(z, zkv, sinks, layer, n_seq, s_len):
    nq = s_len // Q_BLOCK
    gw = SWA_GROUP * HEAD_DIM
    return pl.pallas_call(
        functools.partial(_swa_prompt_kernel, layer=layer),
        grid=(n_seq, SWA_KV_HEADS, nq),
        in_specs=[
            pl.BlockSpec(memory_space=pltpu.SMEM),
            pl.BlockSpec((Q_BLOCK, gw), lambda b, k, q: (b * nq + q, Z_QB // gw + k)),
            pl.BlockSpec((s_len, HEAD_DIM), lambda b, k, q: (b, KV_KB // HEAD_DIM + k)),
            pl.BlockSpec((s_len, HEAD_DIM), lambda b, k, q: (b, KV_VB // HEAD_DIM + k)),
        ],
        out_specs=pl.BlockSpec((Q_BLOCK, gw), lambda b, k, q: (b * nq + q, k)),
        out_shape=jax.ShapeDtypeStruct((n_seq * s_len, BRANCH_WIDTH), BF16),
        scratch_shapes=[pltpu.VMEM((s_len, HEAD_DIM), BF16), pltpu.VMEM((s_len, 2 * HEAD_DIM), BF16)],
        compiler_params=_params(("parallel", "parallel", "arbitrary")),
        name="swa_prompt",
    )(sinks, z, zkv, zkv)


def _swa_step_kernel(sink_ref, q_ref, k_ref, v_ref, ck_ref, cv_ref, o_ref, ko_ref, vo_ref, *, layer):
    n_past = ck_ref.shape[0] // SWA_KV_HEADS
    gw = SWA_GROUP * HEAD_DIM
    for kh in range(SWA_KV_HEADS):
        cs = slice(kh * HEAD_DIM, (kh + 1) * HEAD_DIM)
        kn = k_ref[:, cs]
        vn = v_ref[:, cs]
        kfull = jnp.concatenate(
            [ck_ref[pl.ds(kh, n_past, stride=SWA_KV_HEADS), :].astype(BF16), kn.astype(BF16)], axis=0)
        vfull = jnp.concatenate(
            [cv_ref[pl.ds(kh, n_past, stride=SWA_KV_HEADS), :].astype(BF16), vn.astype(BF16)], axis=0)
        sinks = [sink_ref[layer, kh * SWA_GROUP + g] for g in range(SWA_GROUP)]
        o = _swa_chunk(q_ref[:, kh * gw:(kh + 1) * gw], kfull, _with_ones(vfull), sinks, None)
        o_ref[:, kh * gw:(kh + 1) * gw] = o.astype(BF16)
        ko_ref[pl.ds(kh, CHUNK, stride=SWA_KV_HEADS), :] = kn
        vo_ref[pl.ds(kh, CHUNK, stride=SWA_KV_HEADS), :] = vn


def _swa_step(z, zkv, cache_k, cache_v, sinks, layer, n_seq):
    rows_past = cache_k.shape[2]
    rows_new = CHUNK * SWA_KV_HEADS
    cache = lambda: pl.BlockSpec((None, None, rows_past, HEAD_DIM), lambda b: (layer, b, 0, 0))
    new = lambda: pl.BlockSpec((rows_new, HEAD_DIM), lambda b: (b, 0))
    return pl.pallas_call(
        functools.partial(_swa_step_kernel, layer=layer),
        grid=(n_seq,),
        in_specs=[
            pl.BlockSpec(memory_space=pltpu.SMEM),
            pl.BlockSpec((CHUNK, BRANCH_WIDTH), lambda b: (b, Z_QB // BRANCH_WIDTH)),
            pl.BlockSpec((CHUNK, SWA_KVW), lambda b: (b, KV_KB // SWA_KVW)),
            pl.BlockSpec((CHUNK, SWA_KVW), lambda b: (b, KV_VB // SWA_KVW)),
            cache(), cache(),
        ],
        out_specs=[pl.BlockSpec((CHUNK, BRANCH_WIDTH), lambda b: (b, 0)), new(), new()],
        out_shape=[jax.ShapeDtypeStruct((n_seq * CHUNK, BRANCH_WIDTH), BF16),
                   jax.ShapeDtypeStruct((n_seq * rows_new, HEAD_DIM), F32),
                   jax.ShapeDtypeStruct((n_seq * rows_new, HEAD_DIM), F32)],
        compiler_params=_params(("parallel",)),
        name="swa_step",
    )(sinks, z, zkv, zkv, cache_k, cache_v)


def _cb_prompt_kernel(q_ref, k_ref, v_ref, r_ref, o_ref, kb_s, vb_s, bias_s):
    qi = pl.program_id(2)
    s_len = k_ref.shape[0]

    @pl.when(qi == 0)
    def _():
        _cast_rows(kb_s, k_ref, s_len, 512)
        _cast_rows(vb_s, v_ref, s_len, 512, ones=True)
        for v in range(CB_VARIANTS):
            bias_s[v] = _cb_bias_block(r_ref[v], v, Q_BLOCK, CB_BAND)

    for sub in range(CB_QSUB):
        blk = qi * CB_QSUB + sub
        sc = jnp.maximum(blk * Q_CHUNKS - CB_PREV, 0)
        s0 = pl.multiple_of(sc * CHUNK, CHUNK)
        rows = slice(sub * Q_BLOCK, (sub + 1) * Q_BLOCK)
        o = _cb_attend(q_ref[rows, :], kb_s[pl.ds(s0, CB_BAND), :], vb_s[pl.ds(s0, CB_BAND), :],
                       bias_s[jnp.minimum(blk, CB_VARIANTS - 1)])
        o_ref[rows, :] = o.astype(BF16)


def _cb_prompt(z, zkv, rows, layer, n_seq, s_len):
    tq = Q_BLOCK * CB_QSUB
    nq = s_len // tq
    return pl.pallas_call(
        _cb_prompt_kernel,
        grid=(n_seq, CB_HEADS, nq),
        in_specs=[
            pl.BlockSpec((tq, HEAD_DIM), lambda b, h, q: (b * nq + q, Z_QC // HEAD_DIM + h)),
            pl.BlockSpec((s_len, HEAD_DIM), lambda b, h, q: (b, KV_KC // HEAD_DIM + h)),
            pl.BlockSpec((s_len, HEAD_DIM), lambda b, h, q: (b, KV_VC // HEAD_DIM + h)),
            pl.BlockSpec((None, CB_VARIANTS, None, 1, CB_RLEN), lambda b, h, q: (layer, 0, h, 0, 0)),
        ],
        out_specs=pl.BlockSpec((tq, HEAD_DIM), lambda b, h, q: (b * nq + q, h)),
        out_shape=jax.ShapeDtypeStruct((n_seq * s_len, BRANCH_WIDTH), BF16),
        scratch_shapes=[pltpu.VMEM((s_len, HEAD_DIM), BF16), pltpu.VMEM((s_len, 2 * HEAD_DIM), BF16),
                        pltpu.VMEM((CB_VARIANTS, Q_BLOCK, CB_BAND), F32)],
        compiler_params=_params(("parallel", "parallel", "arbitrary")),
        name="cb_prompt",
    )(z, zkv, zkv, rows)


def _cb_step_kernel(q_ref, k_ref, v_ref, ck_ref, cv_ref, r_ref, o_ref, ko_ref, vo_ref, bias_s):
    n_past = ck_ref.shape[0] // CB_HEADS
    n_k = n_past + CHUNK

    @pl.when(pl.program_id(0) == 0)
    def _():
        for h in range(CB_HEADS):
            bias_s[h] = _cb_bias_block(r_ref[h], CB_VARIANTS - 1, CHUNK, n_k)

    for h in range(CB_HEADS):
        cs = slice(h * HEAD_DIM, (h + 1) * HEAD_DIM)
        kn = k_ref[:, cs]
        vn = v_ref[:, cs]
        kfull = jnp.concatenate(
            [ck_ref[pl.ds(h, n_past, stride=CB_HEADS), :].astype(BF16), kn.astype(BF16)], axis=0)
        vfull = jnp.concatenate(
            [cv_ref[pl.ds(h, n_past, stride=CB_HEADS), :].astype(BF16), vn.astype(BF16)], axis=0)
        o_ref[:, cs] = _cb_attend(q_ref[:, cs], kfull, _with_ones(vfull), bias_s[h]).astype(BF16)
        ko_ref[pl.ds(h, CHUNK, stride=CB_HEADS), :] = kn
        vo_ref[pl.ds(h, CHUNK, stride=CB_HEADS), :] = vn


def _cb_step(z, zkv, cache_k, cache_v, rows, layer, n_seq):
    rows_past = cache_k.shape[2]
    n_k = rows_past // CB_HEADS + CHUNK
    rows_new = CHUNK * CB_HEADS
    cache = lambda: pl.BlockSpec((None, None, rows_past, HEAD_DIM), lambda b: (layer, b, 0, 0))
    new = lambda: pl.BlockSpec((rows_new, HEAD_DIM), lambda b: (b, 0))
    return pl.pallas_call(
        _cb_step_kernel,
        grid=(n_seq,),
        in_specs=[
            pl.BlockSpec((CHUNK, BRANCH_WIDTH), lambda b: (b, Z_QC // BRANCH_WIDTH)),
            pl.BlockSpec((CHUNK, CB_KVW), lambda b: (b, KV_KC // CB_KVW)),
            pl.BlockSpec((CHUNK, CB_KVW), lambda b: (b, KV_VC // CB_KVW)),
            cache(), cache(),
            pl.BlockSpec((None, None, CB_HEADS, 1, CB_RLEN),
                         lambda b: (layer, CB_VARIANTS - 1, 0, 0, 0)),
        ],
        out_specs=[pl.BlockSpec((CHUNK, BRANCH_WIDTH), lambda b: (b, 0)), new(), new()],
        out_shape=[jax.ShapeDtypeStruct((n_seq * CHUNK, BRANCH_WIDTH), BF16),
                   jax.ShapeDtypeStruct((n_seq * rows_new, HEAD_DIM), F32),
                   jax.ShapeDtypeStruct((n_seq * rows_new, HEAD_DIM), F32)],
        scratch_shapes=[pltpu.VMEM((CB_HEADS, CHUNK, n_k), F32)],
        compiler_params=_params(("arbitrary",)),
        name="cb_step",
    )(z, zkv, zkv, cache_k, cache_v, rows)


def _merge_kernel(x_ref, a_ref, b_ref, c_ref, ga_ref, gb_ref, gc_ref, wb_ref, wo_ref, o_ref):
    mixed = None
    for r, (br, gr) in enumerate(((a_ref, ga_ref), (b_ref, gb_ref), (c_ref, gc_ref))):
        proj = jnp.dot(br[...], wb_ref[r], preferred_element_type=F32)
        gate = jax.nn.sigmoid(gr[...].astype(F32))
        mixed = gate * proj if mixed is None else mixed + gate * proj
    o_ref[...] = x_ref[...] + jnp.dot(mixed.astype(BF16), wo_ref[...], preferred_element_type=F32)


def _merge(x, out_a, out_b, out_c, z, w_branch, w_out, layer, tm):
    m = x.shape[0]
    branch = lambda: pl.BlockSpec((tm, BRANCH_WIDTH), lambda i: (i, 0))
    gate = lambda r: pl.BlockSpec((tm, D_MODEL), lambda i: (i, Z_GATES // D_MODEL + r))
    return pl.pallas_call(
        _merge_kernel,
        grid=(m // tm,),
        in_specs=[
            pl.BlockSpec((tm, D_MODEL), lambda i: (i, 0)),
            branch(), branch(), branch(),
            gate(0), gate(1), gate(2),
            pl.BlockSpec((None, N_BRANCH, BRANCH_WIDTH, D_MODEL), lambda i: (layer, 0, 0, 0),
                         pipeline_mode=pl.Buffered(1)),
            pl.BlockSpec((None, D_MODEL, D_MODEL), lambda i: (layer, 0, 0),
                         pipeline_mode=pl.Buffered(1)),
        ],
        out_specs=pl.BlockSpec((tm, D_MODEL), lambda i: (i, 0)),
        out_shape=jax.ShapeDtypeStruct((m, D_MODEL), F32),
        compiler_params=_params(("parallel",)),
        name="merge",
    )(x, out_a, out_b, out_c, z, z, z, w_branch, w_out)


def _mlp_kernel(x_ref, g_ref, fg_ref, wu_ref, wd_ref, o_ref, hn_ref, *, final_norm):
    j = pl.program_id(1)

    @pl.when(j == 0)
    def _():
        xf = x_ref[...]
        hn_ref[...] = _rms(xf, g_ref[...]).astype(BF16)
        o_ref[...] = xf

    h = jnp.dot(hn_ref[...], wu_ref[...], preferred_element_type=F32)
    h = jnp.square(jnp.maximum(h, 0.0)).astype(BF16)
    o_ref[...] += jnp.dot(h, wd_ref[...], preferred_element_type=F32)

    if final_norm:
        @pl.when(j == pl.num_programs(1) - 1)
        def _():
            o_ref[...] = _rms(o_ref[...], fg_ref[...])


def _mlp(x, g, final_g, w_up, w_down, layer, tm, tf, final_norm):
    m = x.shape[0]
    return pl.pallas_call(
        functools.partial(_mlp_kernel, final_norm=final_norm),
        grid=(m // tm, D_FF // tf),
        in_specs=[
            pl.BlockSpec((tm, D_MODEL), lambda i, j: (i, 0)),
            pl.BlockSpec((1, D_MODEL), lambda i, j: (0, 0)),
            pl.BlockSpec((1, D_MODEL), lambda i, j: (0, 0)),
            pl.BlockSpec((None, D_MODEL, tf), lambda i, j: (layer, 0, j)),
            pl.BlockSpec((None, tf, D_MODEL), lambda i, j: (layer, j, 0)),
        ],
        out_specs=pl.BlockSpec((tm, D_MODEL), lambda i, j: (i, 0)),
        out_shape=jax.ShapeDtypeStruct((m, D_MODEL), F32),
        scratch_shapes=[pltpu.VMEM((tm, D_MODEL), BF16)],
        compiler_params=_params(("parallel", "arbitrary")),
        name="mlp",
    )(x, g, final_g, w_up, w_down)


def _permute_w_in(w_in):
    seg = lambda o, w: w_in[..., o:o + w]
    return jnp.concatenate([
        seg(_O_KC, CB_KVW), seg(_O_VC, CB_KVW), seg(_O_KB, SWA_KVW), seg(_O_VB, SWA_KVW),
        seg(_O_AX, D_RNN), seg(_O_AG, D_RNN), seg(_O_QB, BRANCH_WIDTH), seg(_O_QC, BRANCH_WIDTH),
        seg(_O_GATES, N_BRANCH * D_MODEL)], axis=-1).astype(BF16)


def _cb_bias_rows(table):
    m = jnp.arange(CB_RLEN)
    rel = jnp.where(m < CB_BAND, m, m - CB_RLEN)
    out = []
    for v in range(CB_VARIANTS):
        c0 = v * Q_CHUNKS
        sc = max(c0 - CB_PREV, 0)
        d = (c0 - sc) * CHUNK - rel
        idx = jnp.clip(d, -REL_CLIP, REL_CLIP) + REL_CLIP
        out.append(table.astype(F32)[:, :, idx])
    return jnp.stack(out, axis=1)[:, :, :, None, :]


def _pick_tm(m, want):
    tm = min(want, m)
    while m % tm:
        tm //= 2
    return tm


def _layer(x, n_seq, t_len, conv_buf, h0, caches, layer, p):
    m = x.shape[0]
    zkv, z = _in_proj(x, p["norm1"][layer], p["w_in"], layer, _pick_tm(m, 1024))
    out_a, conv_o, h_o = _lru(z, conv_buf, h0, p["conv_w"][layer], p["conv_b"][layer],
                              p["wa"][layer], p["ba"][layer], p["wx"][layer], p["bx"][layer],
                              p["lam"][layer], n_seq, t_len, _pick_tm(t_len, 256))
    if caches is None:
        out_b = _swa_prompt(z, zkv, p["sinks"], layer, n_seq, t_len)
        out_c = _cb_prompt(z, zkv, p["cb_rows"], layer, n_seq, t_len)
        kv3 = zkv.reshape(n_seq, t_len, D_KV)
        n_swa = min(SWA_WINDOW, t_len)
        n_cb = min(CB_REACH, t_len)
        kv = (kv3[:, t_len - n_swa:, KV_KB:KV_KB + SWA_KVW], kv3[:, t_len - n_swa:, KV_VB:KV_VB + SWA_KVW],
              kv3[:, t_len - n_cb:, KV_KC:KV_KC + CB_KVW], kv3[:, t_len - n_cb:, KV_VC:KV_VC + CB_KVW])
    else:
        ck_b, cv_b, ck_c, cv_c = caches
        out_b, kb, vb = _swa_step(z, zkv, ck_b, cv_b, p["sinks"], layer, n_seq)
        out_c, kc, vc = _cb_step(z, zkv, ck_c, cv_c, p["cb_rows"], layer, n_seq)
        kv = (kb, vb, kc, vc)
    x = _merge(x, out_a, out_b, out_c, z, p["w_branch"], p["w_out"], layer, _pick_tm(m, 256))
    x = _mlp(x, p["norm2"][layer], p["final_g"], p["w_up"], p["w_down"], layer, _pick_tm(m, 512), 1024,
             final_norm=layer == p["w_up"].shape[0] - 1)
    return x, conv_o, h_o[:, 0], kv


def kernel(x_prompt, x_sample, state_conv, state_lru, cache_swa_k, cache_swa_v, cache_cb_k, cache_cb_v, norm1_g, w_in, conv_w, conv_b, lru_wa, lru_ba, lru_wx, lru_bx, lru_lambda, attn_sinks, rel_bias_table, w_branch, w_out, norm2_g, w_up, w_down, final_g):
    depth = w_in.shape[0]
    nb, s_len, _ = x_prompt.shape
    db, d_len, _ = x_sample.shape
    assert d_len == CHUNK and s_len % (Q_BLOCK * CB_QSUB) == 0 and s_len >= CB_BAND
    assert cache_swa_k.shape[2] == SWA_WINDOW and cache_cb_k.shape[2] == CB_REACH

    row = lambda v: v.reshape(depth, 1, -1)
    p = {
        "norm1": row(norm1_g), "norm2": row(norm2_g), "final_g": final_g.reshape(1, D_MODEL),
        "w_in": _permute_w_in(w_in),
        "conv_w": conv_w, "conv_b": row(conv_b),
        "wa": lru_wa.astype(BF16), "ba": row(lru_ba), "wx": lru_wx.astype(BF16), "bx": row(lru_bx),
        "lam": row(lru_lambda), "sinks": attn_sinks,
        "cb_rows": _cb_bias_rows(rel_bias_table),
        "w_branch": w_branch.astype(BF16), "w_out": w_out.astype(BF16),
        "w_up": w_up.astype(BF16), "w_down": w_down.astype(BF16),
    }
    caches = (cache_swa_k.reshape(depth, db, SWA_WINDOW * SWA_KV_HEADS, HEAD_DIM),
              cache_swa_v.reshape(depth, db, SWA_WINDOW * SWA_KV_HEADS, HEAD_DIM),
              cache_cb_k.reshape(depth, db, CB_REACH * CB_HEADS, HEAD_DIM),
              cache_cb_v.reshape(depth, db, CB_REACH * CB_HEADS, HEAD_DIM))

    xp = x_prompt.reshape(nb * s_len, D_MODEL)
    xs = x_sample.reshape(db * d_len, D_MODEL)
    zero_conv = jnp.zeros((nb, CONV_W - 1, D_RNN), F32)
    zero_h = jnp.zeros((nb, 1, D_RNN), F32)
    heads = (SWA_KV_HEADS, SWA_KV_HEADS, CB_HEADS, CB_HEADS)
    outs = [[] for _ in range(12)]
    for l in range(depth):
        xp, conv_p, h_p, kv_p = _layer(xp, nb, s_len, zero_conv, zero_h, None, l, p)
        xs, conv_s, h_s, kv_s = _layer(xs, db, d_len, state_conv[l], state_lru[l].reshape(db, 1, D_RNN),
                                       caches, l, p)
        outs[0].append(conv_p)
        outs[1].append(h_p)
        outs[6].append(conv_s)
        outs[7].append(h_s)
        for n in range(4):
            outs[2 + n].append(kv_p[n].reshape(nb, -1, heads[n], HEAD_DIM))
            outs[8 + n].append(kv_s[n].reshape(db, d_len, heads[n], HEAD_DIM))

    y_prompt = xp.reshape(nb, s_len, D_MODEL)
    y_sample = xs.reshape(db, d_len, D_MODEL)
    return (y_prompt, y_sample) + tuple(jnp.stack(o) for o in outs)
```

```python
import functools

import jax
import jax.numpy as jnp
from jax import lax
from jax.experimental import pallas as pl
from jax.experimental.pallas import tpu as pltpu

F32 = jnp.float32
BF16 = jnp.bfloat16

D_MODEL = 2048
CHUNK = 64
HEAD_DIM = 128
BRANCH_WIDTH = D_MODEL // 2
N_BRANCH = 3
D_RNN = BRANCH_WIDTH
LRU_BLOCKS = 8
LRU_BLOCK = D_RNN // LRU_BLOCKS
CONV_W = 4
LRU_C = 8.0
SWA_HEADS = BRANCH_WIDTH // HEAD_DIM
SWA_KV_HEADS = 2
SWA_GROUP = SWA_HEADS // SWA_KV_HEADS
SWA_WINDOW = 128
SWA_PREV = SWA_WINDOW // CHUNK
CB_HEADS = BRANCH_WIDTH // HEAD_DIM
CB_PREV = 8
CB_REACH = CB_PREV * CHUNK
REL_CLIP = 128
D_FF = 4 * D_MODEL
EPS = 1e-6
NEG = -1e30
ATTN_SCALE = HEAD_DIM ** -0.5
LOG2E = 1.4426950408889634

_O_AX, _O_AG, _O_QB, _O_KB, _O_VB, _O_QC, _O_KC, _O_VC, _O_GATES = (
    0, 1024, 2048, 3072, 3328, 3584, 4608, 5632, 6656)
D_IN = _O_GATES + N_BRANCH * D_MODEL
SWA_KVW = SWA_KV_HEADS * HEAD_DIM
CB_KVW = CB_HEADS * HEAD_DIM
D_KV = 2 * SWA_KVW + 2 * CB_KVW
D_Z = D_IN - D_KV
KV_KC, KV_VC, KV_KB, KV_VB = 0, 1024, 2048, 2304
Z_AX, Z_AG, Z_QB, Z_QC, Z_GATES = 0, 1024, 2048, 3072, 4096

VMEM_LIMIT_BYTES = 56 * 1024 * 1024

Q_BLOCK = 4 * CHUNK
Q_CHUNKS = Q_BLOCK // CHUNK
CB_BAND = (CB_PREV + Q_CHUNKS) * CHUNK
CB_QSUB = 4
CB_RLEN = 1024
assert CB_RLEN >= Q_BLOCK + CB_BAND - 1
CB_VARIANTS = 3


def _params(semantics):
    return pltpu.CompilerParams(dimension_semantics=semantics, vmem_limit_bytes=VMEM_LIMIT_BYTES)


def _rms(xf, g):
    return xf * lax.rsqrt(jnp.mean(xf * xf, axis=-1, keepdims=True) + EPS) * g


IN_TN = 1280
IN_KV_TILES = D_KV // IN_TN


class _CastJobs:
    def __init__(self, weights, layer, grid, transform=None):
        self.weights, self.layer, self.transform = weights, layer, transform
        steps = grid[0] * grid[1]
        n = 1
        while n * 2 <= min(steps, MAX_CAST_STEPS):
            n *= 2
        self.n_conv, self.inner = n, grid[1]
        assert all(w.shape[1] % (n * BF16_ROWS) == 0 for w in weights)

    def _slab(self, i, j):
        return jnp.minimum(i * self.inner + j, self.n_conv - 1)

    def in_specs(self):
        return [pl.BlockSpec((None, w.shape[1] // self.n_conv, w.shape[2]),
                             lambda i, j: (self.layer, self._slab(i, j), 0)) for w in self.weights]

    def out_specs(self):
        return [pl.BlockSpec((w.shape[1] // self.n_conv, w.shape[2]),
                             lambda i, j: (self._slab(i, j), 0)) for w in self.weights]

    def out_shape(self):
        return [jax.ShapeDtypeStruct(w.shape[1:], BF16) for w in self.weights]

    def run(self, src_refs, dst_refs):
        step = pl.program_id(0) * self.inner + pl.program_id(1)

        @pl.when(step < self.n_conv)
        def _():
            for s, d in zip(src_refs, dst_refs):
                v = s[...]
                d[...] = (self.transform(v) if self.transform else v).astype(BF16)


MAX_CAST_STEPS = 128
BF16_ROWS = 16


def _in_proj_kernel(*refs, jobs):
    n_side = len(jobs.weights) if jobs else 0
    x_ref, g_ref, w_ref = refs[:3]
    side_in = refs[3:3 + n_side]
    zkv_ref, z_ref = refs[3 + n_side:5 + n_side]
    side_out = refs[5 + n_side:5 + 2 * n_side]
    xn_ref = refs[-1]
    j = pl.program_id(1)

    @pl.when(j == 0)
    def _():
        xn_ref[...] = _rms(x_ref[...], g_ref[...]).astype(BF16)

    acc = jnp.dot(xn_ref[...], w_ref[...], preferred_element_type=F32)

    @pl.when(j < IN_KV_TILES)
    def _():
        zkv_ref[...] = acc

    @pl.when(j >= IN_KV_TILES)
    def _():
        z_ref[...] = acc.astype(BF16)

    if jobs:
        jobs.run(side_in, side_out)


def _in_proj(x, g, w, layer, tm, cast=None):
    m = x.shape[0]
    grid = (m // tm, D_IN // IN_TN)
    jobs = _CastJobs(cast[0], cast[1], grid) if cast else None
    return pl.pallas_call(
        functools.partial(_in_proj_kernel, jobs=jobs),
        grid=grid,
        in_specs=[
            pl.BlockSpec((tm, D_MODEL), lambda i, j: (i, 0)),
            pl.BlockSpec((1, D_MODEL), lambda i, j: (0, 0)),
            pl.BlockSpec((None, D_MODEL, IN_TN), lambda i, j: (layer, 0, j)),
        ] + (jobs.in_specs() if jobs else []),
        out_specs=[
            pl.BlockSpec((tm, IN_TN), lambda i, j: (i, jnp.minimum(j, IN_KV_TILES - 1))),
            pl.BlockSpec((tm, IN_TN), lambda i, j: (i, jnp.maximum(j - IN_KV_TILES, 0))),
        ] + (jobs.out_specs() if jobs else []),
        out_shape=[jax.ShapeDtypeStruct((m, D_KV), F32), jax.ShapeDtypeStruct((m, D_Z), BF16)]
        + (jobs.out_shape() if jobs else []),
        scratch_shapes=[pltpu.VMEM((tm, D_MODEL), BF16)],
        compiler_params=_params(("arbitrary", "arbitrary")),
        name="in_proj",
    )(x, g, w, *(cast[0] if cast else ()))


_XPAD = 8


def _sigmoid(x):
    return 0.5 * (jnp.tanh(0.5 * x) + 1.0)


def _lru_kernel(ax_ref, ag_ref, cbuf_ref, h0_ref, cw_ref, cb_ref, wa_ref, ba_ref, wx_ref, bx_ref,
                lam_ref, out_ref, convo_ref, ho_ref, xbuf, a_s, b_s, h_s):
    t = pl.program_id(1)
    nt = pl.num_programs(1)
    tt = ax_ref.shape[0]

    @pl.when(t == 0)
    def _():
        xbuf[...] = jnp.zeros_like(xbuf)
        xbuf[_XPAD - (CONV_W - 1):, :] = cbuf_ref[...]
        h_s[...] = h0_ref[...]

    x = ax_ref[...].astype(F32)
    xe = jnp.concatenate([xbuf[...], x], axis=0)
    acc = xe * cw_ref[0:1, :]
    for k in range(1, CONV_W):
        acc = xe * cw_ref[k:k + 1, :] + pltpu.roll(acc, 1, 0)
    u = cb_ref[...] + acc[_XPAD:, :]
    tail = x[tt - (CONV_W - 1):, :]
    xbuf[...] = x[tt - _XPAD:, :]

    ub = u.astype(BF16)
    r_parts, i_parts = [], []
    for n in range(LRU_BLOCKS):
        un = ub[:, n * LRU_BLOCK:(n + 1) * LRU_BLOCK]
        r_parts.append(jnp.dot(un, wa_ref[n], preferred_element_type=F32))
        i_parts.append(jnp.dot(un, wx_ref[n], preferred_element_type=F32))
    r = _sigmoid(jnp.concatenate(r_parts, axis=1) + ba_ref[...])
    i = _sigmoid(jnp.concatenate(i_parts, axis=1) + bx_ref[...])
    log_a = -LRU_C * r * jax.nn.softplus(-lam_ref[...])
    a = jnp.exp(log_a)
    y = 1.0 - a * a
    a_s[...] = a
    b_s[...] = jnp.where(y > 0.0, y * lax.rsqrt(y), 0.0) * (i * u)

    row = lax.broadcasted_iota(jnp.int32, (8, D_RNN), 0)

    def body(g, h):
        r0 = pl.multiple_of(g * 8, 8)
        a = a_s[pl.ds(r0, 8), :]
        b = b_s[pl.ds(r0, 8), :]
        for s in (1, 2, 4):
            a_sh = pltpu.roll(a, s, 0)
            b_sh = pltpu.roll(b, s, 0)
            m = row >= s
            b = jnp.where(m, a * b_sh + b, b)
            a = jnp.where(m, a * a_sh, a)
        hblk = a * h + b
        b_s[pl.ds(r0, 8), :] = hblk
        return hblk[7:8, :]

    h = lax.fori_loop(0, tt // 8, body, h_s[...])
    h_s[...] = h
    out_ref[...] = (b_s[...] * jax.nn.gelu(ag_ref[...].astype(F32))).astype(BF16)

    @pl.when(t == nt - 1)
    def _():
        convo_ref[...] = tail
        ho_ref[...] = h


def _lru(z, conv_buf, h0, cw, cb, wa, ba, wx, bx, lam, n_seq, t_len, tt):
    nt = t_len // tt
    row = lambda b, t: b * nt + t
    vec = lambda: pl.BlockSpec((1, D_RNN), lambda b, t: (0, 0))
    blk = lambda: pl.BlockSpec((LRU_BLOCKS, LRU_BLOCK, LRU_BLOCK), lambda b, t: (0, 0, 0))
    return pl.pallas_call(
        _lru_kernel,
        grid=(n_seq, nt),
        in_specs=[
            pl.BlockSpec((tt, D_RNN), lambda b, t: (row(b, t), Z_AX // D_RNN)),
            pl.BlockSpec((tt, D_RNN), lambda b, t: (row(b, t), Z_AG // D_RNN)),
            pl.BlockSpec((None, CONV_W - 1, D_RNN), lambda b, t: (b, 0, 0)),
            pl.BlockSpec((None, 1, D_RNN), lambda b, t: (b, 0, 0)),
            pl.BlockSpec((CONV_W, D_RNN), lambda b, t: (0, 0)),
            vec(), blk(), vec(), blk(), vec(), vec(),
        ],
        out_specs=[
            pl.BlockSpec((tt, D_RNN), lambda b, t: (row(b, t), 0)),
            pl.BlockSpec((None, CONV_W - 1, D_RNN), lambda b, t: (b, 0, 0)),
            pl.BlockSpec((None, 1, D_RNN), lambda b, t: (b, 0, 0)),
        ],
        out_shape=[
            jax.ShapeDtypeStruct((n_seq * t_len, D_RNN), BF16),
            jax.ShapeDtypeStruct((n_seq, CONV_W - 1, D_RNN), F32),
            jax.ShapeDtypeStruct((n_seq, 1, D_RNN), F32),
        ],
        scratch_shapes=[
            pltpu.VMEM((_XPAD, D_RNN), F32),
            pltpu.VMEM((tt, D_RNN), F32),
            pltpu.VMEM((tt, D_RNN), F32),
            pltpu.VMEM((1, D_RNN), F32),
        ],
        compiler_params=_params(("parallel", "arbitrary")),
        name="lru",
    )(z, z, conv_buf, h0, cw, cb, wa, ba, wx, bx, lam)


def _dot_nt(a, b):
    return lax.dot_general(a, b, (((1,), (1,)), ((), ())), preferred_element_type=F32)


def _with_ones(v):
    return jnp.concatenate([v, jnp.ones_like(v)], axis=1)


def _swa_chunk(q, kband, vext, sinks, valid):
    qst = jnp.concatenate([q[:, g * HEAD_DIM:(g + 1) * HEAD_DIM] for g in range(SWA_GROUP)], axis=0)
    s = _dot_nt(qst, kband) * (ATTN_SCALE * LOG2E)
    if valid is not None:
        s = jnp.where(valid, s, NEG)
    es, sink_e = [], []
    for g in range(SWA_GROUP):
        sg = s[g * CHUNK:(g + 1) * CHUNK, :]
        sink2 = sinks[g] * LOG2E
        m = jnp.maximum(jnp.max(sg, axis=-1, keepdims=True), sink2)
        es.append(jnp.exp2(sg - m).astype(BF16))
        sink_e.append(jnp.exp2(sink2 - m))
    r = jnp.dot(jnp.concatenate(es, axis=0), vext, preferred_element_type=F32)
    outs = []
    for g in range(SWA_GROUP):
        rg = r[g * CHUNK:(g + 1) * CHUNK, :]
        outs.append(rg[:, :HEAD_DIM] * (1.0 / (rg[:, HEAD_DIM:] + sink_e[g])))
    return jnp.concatenate(outs, axis=1)


def _cb_attend(q, kband, vext, bias2):
    s = _dot_nt(q, kband) * (ATTN_SCALE * LOG2E) + bias2
    m = jnp.max(s, axis=-1, keepdims=True)
    r = jnp.dot(jnp.exp2(s - m).astype(BF16), vext, preferred_element_type=F32)
    return r[:, :HEAD_DIM] * (1.0 / r[:, HEAD_DIM:])


def _cb_bias_block(r, variant, n_rows, n_cols):
    t = pltpu.roll(jnp.broadcast_to(r, (n_rows, CB_RLEN)), 0, 1, stride=1, stride_axis=0)[:, :n_cols]
    c0 = variant * Q_CHUNKS
    sc = max(c0 - CB_PREV, 0)
    qc = c0 + lax.broadcasted_iota(jnp.int32, (n_rows, n_cols), 0) // CHUNK
    kc = sc + lax.broadcasted_iota(jnp.int32, (n_rows, n_cols), 1) // CHUNK
    return jnp.where(kc <= qc, jnp.where(kc >= qc - CB_PREV, t * LOG2E, NEG), NEG)


def _cast_rows(dst, src, n_rows, step, ones=False):
    def body(i, c):
        r0 = pl.multiple_of(i * step, step)
        v = src[pl.ds(r0, step), :].astype(BF16)
        dst[pl.ds(r0, step), :] = _with_ones(v) if ones else v
        return c
    lax.fori_loop(0, n_rows // step, body, 0)


SWA_BAND = (SWA_PREV + 1) * CHUNK
SWA_TQ = 8 * CHUNK


def _swa_prompt_kernel(sink_ref, q_ref, k_ref, v_ref, o_ref, kb_s, vb_s, *, layer):
    kh = pl.program_id(1)
    qi = pl.program_id(2)
    s_len = k_ref.shape[0]

    @pl.when(qi == 0)
    def _():
        _cast_rows(kb_s, k_ref, s_len, 512)
        _cast_rows(vb_s, v_ref, s_len, 512, ones=True)

    sinks = [sink_ref[layer, kh * SWA_GROUP + g] for g in range(SWA_GROUP)]
    jchunk = lax.broadcasted_iota(jnp.int32, (1, SWA_BAND), 1) // CHUNK
    for c in range(SWA_TQ // CHUNK):
        cg = qi * (SWA_TQ // CHUNK) + c
        sc = jnp.maximum(cg - SWA_PREV, 0)
        s0 = pl.multiple_of(sc * CHUNK, CHUNK)
        valid = (jchunk + sc) <= cg
        o = _swa_chunk(q_ref[c * CHUNK:(c + 1) * CHUNK, :], kb_s[pl.ds(s0, SWA_BAND), :],
                       vb_s[pl.ds(s0, SWA_BAND), :], sinks, valid)
        o_ref[c * CHUNK:(c + 1) * CHUNK, :] = o.astype(BF16)


def _swa_prompt(z, zkv, sinks, layer, n_seq, s_len):
    nq = s_len // SWA_TQ
    gw = SWA_GROUP * HEAD_DIM
    return pl.pallas_call(
        functools.partial(_swa_prompt_kernel, layer=layer),
        grid=(n_seq, SWA_KV_HEADS, nq),
        in_specs=[
            pl.BlockSpec(memory_space=pltpu.SMEM),
            pl.BlockSpec((SWA_TQ, gw), lambda b, k, q: (b * nq + q, Z_QB // gw + k)),
            pl.BlockSpec((s_len, HEAD_DIM), lambda b, k, q: (b, KV_KB // HEAD_DIM + k)),
            pl.BlockSpec((s_len, HEAD_DIM), lambda b, k, q: (b, KV_VB // HEAD_DIM + k)),
        ],
        out_specs=pl.BlockSpec((SWA_TQ, gw), lambda b, k, q: (b * nq + q, k)),
        out_shape=jax.ShapeDtypeStruct((n_seq * s_len, BRANCH_WIDTH), BF16),
        scratch_shapes=[pltpu.VMEM((s_len, HEAD_DIM), BF16), pltpu.VMEM((s_len, 2 * HEAD_DIM), BF16)],
        compiler_params=_params(("parallel", "parallel", "arbitrary")),
        name="swa_prompt",
    )(sinks, z, zkv, zkv)


def _swa_step_kernel(sink_ref, q_ref, k_ref, v_ref, ck_ref, cv_ref, o_ref, ko_ref, vo_ref, *, layer):
    n_past = ck_ref.shape[0] // SWA_KV_HEADS
    gw = SWA_GROUP * HEAD_DIM
    for kh in range(SWA_KV_HEADS):
        cs = slice(kh * HEAD_DIM, (kh + 1) * HEAD_DIM)
        kn = k_ref[:, cs]
        vn = v_ref[:, cs]
        kfull = jnp.concatenate(
            [ck_ref[pl.ds(kh, n_past, stride=SWA_KV_HEADS), :].astype(BF16), kn.astype(BF16)], axis=0)
        vfull = jnp.concatenate(
            [cv_ref[pl.ds(kh, n_past, stride=SWA_KV_HEADS), :].astype(BF16), vn.astype(BF16)], axis=0)
        sinks = [sink_ref[layer, kh * SWA_GROUP + g] for g in range(SWA_GROUP)]
        o = _swa_chunk(q_ref[:, kh * gw:(kh + 1) * gw], kfull, _with_ones(vfull), sinks, None)
        o_ref[:, kh * gw:(kh + 1) * gw] = o.astype(BF16)
        ko_ref[pl.ds(kh, CHUNK, stride=SWA_KV_HEADS), :] = kn
        vo_ref[pl.ds(kh, CHUNK, stride=SWA_KV_HEADS), :] = vn


def _swa_step(z, zkv, cache_k, cache_v, sinks, layer, n_seq):
    rows_past = cache_k.shape[2]
    rows_new = CHUNK * SWA_KV_HEADS
    cache = lambda: pl.BlockSpec((None, None, rows_past, HEAD_DIM), lambda b: (layer, b, 0, 0))
    new = lambda: pl.BlockSpec((rows_new, HEAD_DIM), lambda b: (b, 0))
    return pl.pallas_call(
        functools.partial(_swa_step_kernel, layer=layer),
        grid=(n_seq,),
        in_specs=[
            pl.BlockSpec(memory_space=pltpu.SMEM),
            pl.BlockSpec((CHUNK, BRANCH_WIDTH), lambda b: (b, Z_QB // BRANCH_WIDTH)),
            pl.BlockSpec((CHUNK, SWA_KVW), lambda b: (b, KV_KB // SWA_KVW)),
            pl.BlockSpec((CHUNK, SWA_KVW), lambda b: (b, KV_VB // SWA_KVW)),
            cache(), cache(),
        ],
        out_specs=[pl.BlockSpec((CHUNK, BRANCH_WIDTH), lambda b: (b, 0)), new(), new()],
        out_shape=[jax.ShapeDtypeStruct((n_seq * CHUNK, BRANCH_WIDTH), BF16),
                   jax.ShapeDtypeStruct((n_seq * rows_new, HEAD_DIM), F32),
                   jax.ShapeDtypeStruct((n_seq * rows_new, HEAD_DIM), F32)],
        compiler_params=_params(("parallel",)),
        name="swa_step",
    )(sinks, z, zkv, zkv, cache_k, cache_v)


def _cb_prompt_kernel(q_ref, k_ref, v_ref, r_ref, o_ref, kb_s, vb_s, bias_s):
    qi = pl.program_id(2)
    s_len = k_ref.shape[0]

    @pl.when(qi == 0)
    def _():
        _cast_rows(kb_s, k_ref, s_len, 512)
        _cast_rows(vb_s, v_ref, s_len, 512, ones=True)
        for v in range(CB_VARIANTS):
            bias_s[v] = _cb_bias_block(r_ref[v], v, Q_BLOCK, CB_BAND)

    for sub in range(CB_QSUB):
        blk = qi * CB_QSUB + sub
        sc = jnp.maximum(blk * Q_CHUNKS - CB_PREV, 0)
        s0 = pl.multiple_of(sc * CHUNK, CHUNK)
        rows = slice(sub * Q_BLOCK, (sub + 1) * Q_BLOCK)
        o = _cb_attend(q_ref[rows, :], kb_s[pl.ds(s0, CB_BAND), :], vb_s[pl.ds(s0, CB_BAND), :],
                       bias_s[jnp.minimum(blk, CB_VARIANTS - 1)])
        o_ref[rows, :] = o.astype(BF16)


def _cb_prompt(z, zkv, rows, layer, n_seq, s_len):
    tq = Q_BLOCK * CB_QSUB
    nq = s_len // tq
    return pl.pallas_call(
        _cb_prompt_kernel,
        grid=(n_seq, CB_HEADS, nq),
        in_specs=[
            pl.BlockSpec((tq, HEAD_DIM), lambda b, h, q: (b * nq + q, Z_QC // HEAD_DIM + h)),
            pl.BlockSpec((s_len, HEAD_DIM), lambda b, h, q: (b, KV_KC // HEAD_DIM + h)),
            pl.BlockSpec((s_len, HEAD_DIM), lambda b, h, q: (b, KV_VC // HEAD_DIM + h)),
            pl.BlockSpec((None, CB_VARIANTS, None, 1, CB_RLEN), lambda b, h, q: (layer, 0, h, 0, 0)),
        ],
        out_specs=pl.BlockSpec((tq, HEAD_DIM), lambda b, h, q: (b * nq + q, h)),
        out_shape=jax.ShapeDtypeStruct((n_seq * s_len, BRANCH_WIDTH), BF16),
        scratch_shapes=[pltpu.VMEM((s_len, HEAD_DIM), BF16), pltpu.VMEM((s_len, 2 * HEAD_DIM), BF16),
                        pltpu.VMEM((CB_VARIANTS, Q_BLOCK, CB_BAND), F32)],
        compiler_params=_params(("parallel", "parallel", "arbitrary")),
        name="cb_prompt",
    )(z, zkv, zkv, rows)


def _cb_step_kernel(q_ref, k_ref, v_ref, ck_ref, cv_ref, r_ref, o_ref, ko_ref, vo_ref, bias_s):
    n_past = ck_ref.shape[0] // CB_HEADS
    n_k = n_past + CHUNK

    @pl.when(pl.program_id(0) == 0)
    def _():
        for h in range(CB_HEADS):
            bias_s[h] = _cb_bias_block(r_ref[h], CB_VARIANTS - 1, CHUNK, n_k)

    for h in range(CB_HEADS):
        cs = slice(h * HEAD_DIM, (h + 1) * HEAD_DIM)
        kn = k_ref[:, cs]
        vn = v_ref[:, cs]
        kfull = jnp.concatenate(
            [ck_ref[pl.ds(h, n_past, stride=CB_HEADS), :].astype(BF16), kn.astype(BF16)], axis=0)
        vfull = jnp.concatenate(
            [cv_ref[pl.ds(h, n_past, stride=CB_HEADS), :].astype(BF16), vn.astype(BF16)], axis=0)
        o_ref[:, cs] = _cb_attend(q_ref[:, cs], kfull, _with_ones(vfull), bias_s[h]).astype(BF16)
        ko_ref[pl.ds(h, CHUNK, stride=CB_HEADS), :] = kn
        vo_ref[pl.ds(h, CHUNK, stride=CB_HEADS), :] = vn


def _cb_step(z, zkv, cache_k, cache_v, rows, layer, n_seq):
    rows_past = cache_k.shape[2]
    n_k = rows_past // CB_HEADS + CHUNK
    rows_new = CHUNK * CB_HEADS
    cache = lambda: pl.BlockSpec((None, None, rows_past, HEAD_DIM), lambda b: (layer, b, 0, 0))
    new = lambda: pl.BlockSpec((rows_new, HEAD_DIM), lambda b: (b, 0))
    return pl.pallas_call(
        _cb_step_kernel,
        grid=(n_seq,),
        in_specs=[
            pl.BlockSpec((CHUNK, BRANCH_WIDTH), lambda b: (b, Z_QC // BRANCH_WIDTH)),
            pl.BlockSpec((CHUNK, CB_KVW), lambda b: (b, KV_KC // CB_KVW)),
            pl.BlockSpec((CHUNK, CB_KVW), lambda b: (b, KV_VC // CB_KVW)),
            cache(), cache(),
            pl.BlockSpec((None, None, CB_HEADS, 1, CB_RLEN),
                         lambda b: (layer, CB_VARIANTS - 1, 0, 0, 0)),
        ],
        out_specs=[pl.BlockSpec((CHUNK, BRANCH_WIDTH), lambda b: (b, 0)), new(), new()],
        out_shape=[jax.ShapeDtypeStruct((n_seq * CHUNK, BRANCH_WIDTH), BF16),
                   jax.ShapeDtypeStruct((n_seq * rows_new, HEAD_DIM), F32),
                   jax.ShapeDtypeStruct((n_seq * rows_new, HEAD_DIM), F32)],
        scratch_shapes=[pltpu.VMEM((CB_HEADS, CHUNK, n_k), F32)],
        compiler_params=_params(("arbitrary",)),
        name="cb_step",
    )(z, zkv, zkv, cache_k, cache_v, rows)


def _merge_kernel(x_ref, a_ref, b_ref, c_ref, ga_ref, gb_ref, gc_ref, wb_ref, wo_ref, o_ref):
    mixed = None
    for r, (br, gr) in enumerate(((a_ref, ga_ref), (b_ref, gb_ref), (c_ref, gc_ref))):
        proj = jnp.dot(br[...], wb_ref[r], preferred_element_type=F32)
        gate = jax.nn.sigmoid(gr[...].astype(F32))
        mixed = gate * proj if mixed is None else mixed + gate * proj
    o_ref[...] = x_ref[...] + jnp.dot(mixed.astype(BF16), wo_ref[...], preferred_element_type=F32)


def _merge(x, out_a, out_b, out_c, z, w_branch, w_out, layer, tm):
    m = x.shape[0]
    branch = lambda: pl.BlockSpec((tm, BRANCH_WIDTH), lambda i: (i, 0))
    gate = lambda r: pl.BlockSpec((tm, D_MODEL), lambda i: (i, Z_GATES // D_MODEL + r))
    return pl.pallas_call(
        _merge_kernel,
        grid=(m // tm,),
        in_specs=[
            pl.BlockSpec((tm, D_MODEL), lambda i: (i, 0)),
            branch(), branch(), branch(),
            gate(0), gate(1), gate(2),
            pl.BlockSpec((None, N_BRANCH, BRANCH_WIDTH, D_MODEL), lambda i: (layer, 0, 0, 0),
                         pipeline_mode=pl.Buffered(1)),
            pl.BlockSpec((None, D_MODEL, D_MODEL), lambda i: (layer, 0, 0),
                         pipeline_mode=pl.Buffered(1)),
        ],
        out_specs=pl.BlockSpec((tm, D_MODEL), lambda i: (i, 0)),
        out_shape=jax.ShapeDtypeStruct((m, D_MODEL), F32),
        compiler_params=_params(("parallel",)),
        name="merge",
    )(x, out_a, out_b, out_c, z, z, z, w_branch, w_out)


def _mlp_kernel(*refs, final_norm, jobs):
    n_side = len(jobs.weights) if jobs else 0
    x_ref, g_ref, fg_ref, wu_ref, wd_ref = refs[:5]
    side_in = refs[5:5 + n_side]
    o_ref = refs[5 + n_side]
    side_out = refs[6 + n_side:6 + 2 * n_side]
    hn_ref = refs[-1]
    j = pl.program_id(1)

    @pl.when(j == 0)
    def _():
        xf = x_ref[...]
        hn_ref[...] = _rms(xf, g_ref[...]).astype(BF16)
        o_ref[...] = xf

    h = jnp.dot(hn_ref[...], wu_ref[...], preferred_element_type=F32)
    h = jnp.square(jnp.maximum(h, 0.0)).astype(BF16)
    o_ref[...] += jnp.dot(h, wd_ref[...], preferred_element_type=F32)

    if final_norm:
        @pl.when(j == pl.num_programs(1) - 1)
        def _():
            o_ref[...] = _rms(o_ref[...], fg_ref[...])

    if jobs:
        jobs.run(side_in, side_out)


def _mlp(x, g, final_g, w_up, w_down, layer, tm, tf, final_norm, cast=None):
    m = x.shape[0]
    grid = (m // tm, D_FF // tf)
    jobs = _CastJobs(cast[0], cast[1], grid, cast[2]) if cast else None
    return pl.pallas_call(
        functools.partial(_mlp_kernel, final_norm=final_norm, jobs=jobs),
        grid=grid,
        in_specs=[
            pl.BlockSpec((tm, D_MODEL), lambda i, j: (i, 0)),
            pl.BlockSpec((1, D_MODEL), lambda i, j: (0, 0)),
            pl.BlockSpec((1, D_MODEL), lambda i, j: (0, 0)),
            pl.BlockSpec((None, D_MODEL, tf), lambda i, j: (layer, 0, j)),
            pl.BlockSpec((None, tf, D_MODEL), lambda i, j: (layer, j, 0)),
        ] + (jobs.in_specs() if jobs else []),
        out_specs=[pl.BlockSpec((tm, D_MODEL), lambda i, j: (i, 0))] + (jobs.out_specs() if jobs else []),
        out_shape=[jax.ShapeDtypeStruct((m, D_MODEL), F32)] + (jobs.out_shape() if jobs else []),
        scratch_shapes=[pltpu.VMEM((tm, D_MODEL), BF16)],
        compiler_params=_params(("arbitrary", "arbitrary")),
        name="mlp",
    )(x, g, final_g, w_up, w_down, *(cast[0] if cast else ()))


def _permute_w_in(w_in):
    seg = lambda o, w: w_in[..., o:o + w]
    return jnp.concatenate([
        seg(_O_KC, CB_KVW), seg(_O_VC, CB_KVW), seg(_O_KB, SWA_KVW), seg(_O_VB, SWA_KVW),
        seg(_O_AX, D_RNN), seg(_O_AG, D_RNN), seg(_O_QB, BRANCH_WIDTH), seg(_O_QC, BRANCH_WIDTH),
        seg(_O_GATES, N_BRANCH * D_MODEL)], axis=-1).astype(BF16)


def _cb_bias_rows(table):
    m = jnp.arange(CB_RLEN)
    rel = jnp.where(m < CB_BAND, m, m - CB_RLEN)
    out = []
    for v in range(CB_VARIANTS):
        c0 = v * Q_CHUNKS
        sc = max(c0 - CB_PREV, 0)
        d = (c0 - sc) * CHUNK - rel
        idx = jnp.clip(d, -REL_CLIP, REL_CLIP) + REL_CLIP
        out.append(table.astype(F32)[:, :, idx])
    return jnp.stack(out, axis=1)[:, :, :, None, :]


def _pick_tm(m, want):
    tm = min(want, m)
    while m % tm:
        tm //= 2
    return tm


def _layer(x, n_seq, t_len, conv_buf, h0, caches, layer, p, w_in_l, w_mlp):
    m = x.shape[0]
    depth = p["w_up"].shape[0]
    tm_in = _pick_tm(m, 1024)
    if w_mlp is None:
        zkv, z, wu, wd = _in_proj(x, p["norm1"][layer], w_in_l, 0, tm_in,
                                  cast=((p["w_up"], p["w_down"]), layer))
        w_mlp = (wu[None], wd[None])
        cast_next = ((p["w_in"],), layer + 1, _permute_w_in) if layer + 1 < depth else None
    else:
        zkv, z = _in_proj(x, p["norm1"][layer], w_in_l, 0, tm_in)
        cast_next = None
    out_a, conv_o, h_o = _lru(z, conv_buf, h0, p["conv_w"][layer], p["conv_b"][layer],
                              p["wa"][layer], p["ba"][layer], p["wx"][layer], p["bx"][layer],
                              p["lam"][layer], n_seq, t_len, _pick_tm(t_len, 256))
    if caches is None:
        out_b = _swa_prompt(z, zkv, p["sinks"], layer, n_seq, t_len)
        out_c = _cb_prompt(z, zkv, p["cb_rows"], layer, n_seq, t_len)
        kv3 = zkv.reshape(n_seq, t_len, D_KV)
        n_swa = min(SWA_WINDOW, t_len)
        n_cb = min(CB_REACH, t_len)
        kv = (kv3[:, t_len - n_swa:, KV_KB:KV_KB + SWA_KVW], kv3[:, t_len - n_swa:, KV_VB:KV_VB + SWA_KVW],
              kv3[:, t_len - n_cb:, KV_KC:KV_KC + CB_KVW], kv3[:, t_len - n_cb:, KV_VC:KV_VC + CB_KVW])
    else:
        ck_b, cv_b, ck_c, cv_c = caches
        out_b, kb, vb = _swa_step(z, zkv, ck_b, cv_b, p["sinks"], layer, n_seq)
        out_c, kc, vc = _cb_step(z, zkv, ck_c, cv_c, p["cb_rows"], layer, n_seq)
        kv = (kb, vb, kc, vc)
    x = _merge(x, out_a, out_b, out_c, z, p["w_branch"], p["w_out"], layer, _pick_tm(m, 256))
    res = _mlp(x, p["norm2"][layer], p["final_g"], w_mlp[0], w_mlp[1], 0, _pick_tm(m, 512), 1024,
               final_norm=layer == depth - 1, cast=cast_next)
    w_in_next = res[1][None] if cast_next else None
    return res[0], conv_o, h_o[:, 0], kv, w_mlp, w_in_next


def kernel(x_prompt, x_sample, state_conv, state_lru, cache_swa_k, cache_swa_v, cache_cb_k, cache_cb_v, norm1_g, w_in, conv_w, conv_b, lru_wa, lru_ba, lru_wx, lru_bx, lru_lambda, attn_sinks, rel_bias_table, w_branch, w_out, norm2_g, w_up, w_down, final_g):
    depth = w_in.shape[0]
    nb, s_len, _ = x_prompt.shape
    db, d_len, _ = x_sample.shape
    assert d_len == CHUNK and s_len % (Q_BLOCK * CB_QSUB) == 0 and s_len >= CB_BAND
    assert cache_swa_k.shape[2] == SWA_WINDOW and cache_cb_k.shape[2] == CB_REACH

    row = lambda v: v.reshape(depth, 1, -1)
    p = {
        "norm1": row(norm1_g), "norm2": row(norm2_g), "final_g": final_g.reshape(1, D_MODEL),
        "w_in": w_in, "w_up": w_up, "w_down": w_down,
        "conv_w": conv_w, "conv_b": row(conv_b),
        "wa": lru_wa.astype(BF16), "ba": row(lru_ba), "wx": lru_wx.astype(BF16), "bx": row(lru_bx),
        "lam": row(lru_lambda), "sinks": attn_sinks,
        "cb_rows": _cb_bias_rows(rel_bias_table),
        "w_branch": w_branch.astype(BF16), "w_out": w_out.astype(BF16),
    }
    w_in_l = _permute_w_in(w_in[0])[None]
    caches = (cache_swa_k.reshape(depth, db, SWA_WINDOW * SWA_KV_HEADS, HEAD_DIM),
              cache_swa_v.reshape(depth, db, SWA_WINDOW * SWA_KV_HEADS, HEAD_DIM),
              cache_cb_k.reshape(depth, db, CB_REACH * CB_HEADS, HEAD_DIM),
              cache_cb_v.reshape(depth, db, CB_REACH * CB_HEADS, HEAD_DIM))

    xp = x_prompt.reshape(nb * s_len, D_MODEL)
    xs = x_sample.reshape(db * d_len, D_MODEL)
    zero_conv = jnp.zeros((nb, CONV_W - 1, D_RNN), F32)
    zero_h = jnp.zeros((nb, 1, D_RNN), F32)
    heads = (SWA_KV_HEADS, SWA_KV_HEADS, CB_HEADS, CB_HEADS)
    outs = [[] for _ in range(12)]
    for l in range(depth):
        xp, conv_p, h_p, kv_p, w_mlp, w_in_next = _layer(xp, nb, s_len, zero_conv, zero_h, None, l, p,
                                                         w_in_l, None)
        xs, conv_s, h_s, kv_s, _, _ = _layer(xs, db, d_len, state_conv[l],
                                             state_lru[l].reshape(db, 1, D_RNN), caches, l, p, w_in_l, w_mlp)
        w_in_l = w_in_next
        outs[0].append(conv_p)
        outs[1].append(h_p)
        outs[6].append(conv_s)
        outs[7].append(h_s)
        for n in range(4):
            outs[2 + n].append(kv_p[n].reshape(nb, -1, heads[n], HEAD_DIM))
            outs[8 + n].append(kv_s[n].reshape(db, d_len, heads[n], HEAD_DIM))

    y_prompt = xp.reshape(nb, s_len, D_MODEL)
    y_sample = xs.reshape(db, d_len, D_MODEL)
    return (y_prompt, y_sample) + tuple(jnp.stack(o) for o in outs)
```

```python
import functools

import jax
import jax.numpy as jnp
from jax import lax
from jax.experimental import pallas as pl
from jax.experimental.pallas import tpu as pltpu

F32 = jnp.float32
BF16 = jnp.bfloat16

D_MODEL = 2048
CHUNK = 64
HEAD_DIM = 128
BRANCH_WIDTH = D_MODEL // 2
N_BRANCH = 3
D_RNN = BRANCH_WIDTH
LRU_BLOCKS = 8
LRU_BLOCK = D_RNN // LRU_BLOCKS
CONV_W = 4
LRU_C = 8.0
SWA_HEADS = BRANCH_WIDTH // HEAD_DIM
SWA_KV_HEADS = 2
SWA_GROUP = SWA_HEADS // SWA_KV_HEADS
SWA_WINDOW = 128
SWA_PREV = SWA_WINDOW // CHUNK
CB_HEADS = BRANCH_WIDTH // HEAD_DIM
CB_PREV = 8
CB_REACH = CB_PREV * CHUNK
REL_CLIP = 128
D_FF = 4 * D_MODEL
EPS = 1e-6
NEG = -1e30
ATTN_SCALE = HEAD_DIM ** -0.5
LOG2E = 1.4426950408889634

_O_AX, _O_AG, _O_QB, _O_KB, _O_VB, _O_QC, _O_KC, _O_VC, _O_GATES = (
    0, 1024, 2048, 3072, 3328, 3584, 4608, 5632, 6656)
D_IN = _O_GATES + N_BRANCH * D_MODEL
SWA_KVW = SWA_KV_HEADS * HEAD_DIM
CB_KVW = CB_HEADS * HEAD_DIM
D_KV = 2 * SWA_KVW + 2 * CB_KVW
D_Z = D_IN - D_KV
KV_KC, KV_VC, KV_KB, KV_VB = 0, 1024, 2048, 2304
Z_AX, Z_AG, Z_QB, Z_QC, Z_GATES = 0, 1024, 2048, 3072, 4096

VMEM_LIMIT_BYTES = 56 * 1024 * 1024

Q_BLOCK = 4 * CHUNK
Q_CHUNKS = Q_BLOCK // CHUNK
CB_BAND = (CB_PREV + Q_CHUNKS) * CHUNK
CB_QSUB = 8
CB_RLEN = 1024
assert CB_RLEN >= Q_BLOCK + CB_BAND - 1
CB_VARIANTS = 3


def _params(semantics):
    return pltpu.CompilerParams(dimension_semantics=semantics, vmem_limit_bytes=VMEM_LIMIT_BYTES)


def _rms(xf, g):
    return xf * lax.rsqrt(jnp.mean(xf * xf, axis=-1, keepdims=True) + EPS) * g


IN_TN = 1280
IN_KV_TILES = D_KV // IN_TN


class _CastJobs:
    def __init__(self, weights, layer, grid, transform=None):
        self.weights, self.layer, self.transform = weights, layer, transform
        steps = grid[0] * grid[1]
        n = 1
        while n * 2 <= min(steps, MAX_CAST_STEPS):
            n *= 2
        self.n_conv, self.inner = n, grid[1]
        assert all(w.shape[1] % (n * BF16_ROWS) == 0 for w in weights)

    def _slab(self, i, j):
        return jnp.minimum(i * self.inner + j, self.n_conv - 1)

    def in_specs(self):
        return [pl.BlockSpec((None, w.shape[1] // self.n_conv, w.shape[2]),
                             lambda i, j: (self.layer, self._slab(i, j), 0)) for w in self.weights]

    def out_specs(self):
        return [pl.BlockSpec((w.shape[1] // self.n_conv, w.shape[2]),
                             lambda i, j: (self._slab(i, j), 0)) for w in self.weights]

    def out_shape(self):
        return [jax.ShapeDtypeStruct(w.shape[1:], BF16) for w in self.weights]

    def run(self, src_refs, dst_refs):
        step = pl.program_id(0) * self.inner + pl.program_id(1)

        @pl.when(step < self.n_conv)
        def _():
            for s, d in zip(src_refs, dst_refs):
                v = s[...]
                d[...] = (self.transform(v) if self.transform else v).astype(BF16)


MAX_CAST_STEPS = 128
BF16_ROWS = 16


def _in_proj_kernel(*refs, jobs):
    n_side = len(jobs.weights) if jobs else 0
    x_ref, g_ref, w_ref = refs[:3]
    side_in = refs[3:3 + n_side]
    zkv_ref, z_ref = refs[3 + n_side:5 + n_side]
    side_out = refs[5 + n_side:5 + 2 * n_side]
    xn_ref = refs[-1]
    j = pl.program_id(1)

    @pl.when(j == 0)
    def _():
        xn_ref[...] = _rms(x_ref[...], g_ref[...]).astype(BF16)

    @pl.when(j < IN_KV_TILES)
    def _():
        zkv_ref[...] = jnp.dot(xn_ref[...], w_ref[...], preferred_element_type=F32)

    @pl.when(j >= IN_KV_TILES)
    def _():
        z_ref[...] = jnp.dot(xn_ref[...], w_ref[...], preferred_element_type=F32).astype(BF16)

    if jobs:
        jobs.run(side_in, side_out)


def _in_proj(x, g, w, layer, tm, cast=None):
    m = x.shape[0]
    grid = (m // tm, D_IN // IN_TN)
    jobs = _CastJobs(cast[0], cast[1], grid) if cast else None
    return pl.pallas_call(
        functools.partial(_in_proj_kernel, jobs=jobs),
        grid=grid,
        in_specs=[
            pl.BlockSpec((tm, D_MODEL), lambda i, j: (i, 0)),
            pl.BlockSpec((1, D_MODEL), lambda i, j: (0, 0)),
            pl.BlockSpec((None, D_MODEL, IN_TN), lambda i, j: (layer, 0, j)),
        ] + (jobs.in_specs() if jobs else []),
        out_specs=[
            pl.BlockSpec((tm, IN_TN), lambda i, j: (i, jnp.minimum(j, IN_KV_TILES - 1))),
            pl.BlockSpec((tm, IN_TN), lambda i, j: (i, jnp.maximum(j - IN_KV_TILES, 0))),
        ] + (jobs.out_specs() if jobs else []),
        out_shape=[jax.ShapeDtypeStruct((m, D_KV), F32), jax.ShapeDtypeStruct((m, D_Z), BF16)]
        + (jobs.out_shape() if jobs else []),
        scratch_shapes=[pltpu.VMEM((tm, D_MODEL), BF16)],
        compiler_params=_params(("arbitrary", "arbitrary")),
        name="in_proj",
    )(x, g, w, *(cast[0] if cast else ()))


_XPAD = 8


def _sigmoid(x):
    return 0.5 * (jnp.tanh(0.5 * x) + 1.0)


def _lru_kernel(ax_ref, ag_ref, cbuf_ref, h0_ref, cw_ref, cb_ref, wa_ref, ba_ref, wx_ref, bx_ref,
                lam_ref, out_ref, convo_ref, ho_ref, xbuf, a_s, b_s, h_s):
    t = pl.program_id(1)
    nt = pl.num_programs(1)
    tt = ax_ref.shape[0]

    @pl.when(t == 0)
    def _():
        xbuf[...] = jnp.zeros_like(xbuf)
        xbuf[_XPAD - (CONV_W - 1):, :] = cbuf_ref[...]
        h_s[...] = h0_ref[...]

    x = ax_ref[...].astype(F32)
    xe = jnp.concatenate([xbuf[...], x], axis=0)
    acc = xe * cw_ref[0:1, :]
    for k in range(1, CONV_W):
        acc = xe * cw_ref[k:k + 1, :] + pltpu.roll(acc, 1, 0)
    u = cb_ref[...] + acc[_XPAD:, :]
    tail = x[tt - (CONV_W - 1):, :]
    xbuf[...] = x[tt - _XPAD:, :]

    ub = u.astype(BF16)
    r_parts, i_parts = [], []
    for n in range(LRU_BLOCKS):
        un = ub[:, n * LRU_BLOCK:(n + 1) * LRU_BLOCK]
        r_parts.append(jnp.dot(un, wa_ref[n], preferred_element_type=F32))
        i_parts.append(jnp.dot(un, wx_ref[n], preferred_element_type=F32))
    r = _sigmoid(jnp.concatenate(r_parts, axis=1) + ba_ref[...])
    i = _sigmoid(jnp.concatenate(i_parts, axis=1) + bx_ref[...])
    log_a = -LRU_C * r * jax.nn.softplus(-lam_ref[...])
    a = jnp.exp(log_a)
    y = 1.0 - a * a
    a_s[...] = a
    b_s[...] = jnp.where(y > 0.0, y * lax.rsqrt(y), 0.0) * (i * u)

    row = lax.broadcasted_iota(jnp.int32, (8, D_RNN), 0)

    def body(g, h):
        r0 = pl.multiple_of(g * 8, 8)
        a = a_s[pl.ds(r0, 8), :]
        b = b_s[pl.ds(r0, 8), :]
        for s in (1, 2, 4):
            a_sh = pltpu.roll(a, s, 0)
            b_sh = pltpu.roll(b, s, 0)
            m = row >= s
            b = jnp.where(m, a * b_sh + b, b)
            a = jnp.where(m, a * a_sh, a)
        hblk = a * h + b
        b_s[pl.ds(r0, 8), :] = hblk
        return hblk[7:8, :]

    h = lax.fori_loop(0, tt // 8, body, h_s[...])
    h_s[...] = h
    out_ref[...] = (b_s[...] * jax.nn.gelu(ag_ref[...].astype(F32))).astype(BF16)

    @pl.when(t == nt - 1)
    def _():
        convo_ref[...] = tail
        ho_ref[...] = h


def _lru(z, conv_buf, h0, cw, cb, wa, ba, wx, bx, lam, n_seq, t_len, tt):
    nt = t_len // tt
    row = lambda b, t: b * nt + t
    vec = lambda: pl.BlockSpec((1, D_RNN), lambda b, t: (0, 0))
    blk = lambda: pl.BlockSpec((LRU_BLOCKS, LRU_BLOCK, LRU_BLOCK), lambda b, t: (0, 0, 0))
    return pl.pallas_call(
        _lru_kernel,
        grid=(n_seq, nt),
        in_specs=[
            pl.BlockSpec((tt, D_RNN), lambda b, t: (row(b, t), Z_AX // D_RNN)),
            pl.BlockSpec((tt, D_RNN), lambda b, t: (row(b, t), Z_AG // D_RNN)),
            pl.BlockSpec((None, CONV_W - 1, D_RNN), lambda b, t: (b, 0, 0)),
            pl.BlockSpec((None, 1, D_RNN), lambda b, t: (b, 0, 0)),
            pl.BlockSpec((CONV_W, D_RNN), lambda b, t: (0, 0)),
            vec(), blk(), vec(), blk(), vec(), vec(),
        ],
        out_specs=[
            pl.BlockSpec((tt, D_RNN), lambda b, t: (row(b, t), 0)),
            pl.BlockSpec((None, CONV_W - 1, D_RNN), lambda b, t: (b, 0, 0)),
            pl.BlockSpec((None, 1, D_RNN), lambda b, t: (b, 0, 0)),
        ],
        out_shape=[
            jax.ShapeDtypeStruct((n_seq * t_len, D_RNN), BF16),
            jax.ShapeDtypeStruct((n_seq, CONV_W - 1, D_RNN), F32),
            jax.ShapeDtypeStruct((n_seq, 1, D_RNN), F32),
        ],
        scratch_shapes=[
            pltpu.VMEM((_XPAD, D_RNN), F32),
            pltpu.VMEM((tt, D_RNN), F32),
            pltpu.VMEM((tt, D_RNN), F32),
            pltpu.VMEM((1, D_RNN), F32),
        ],
        compiler_params=_params(("parallel", "arbitrary")),
        name="lru",
    )(z, z, conv_buf, h0, cw, cb, wa, ba, wx, bx, lam)


def _dot_nt(a, b):
    return lax.dot_general(a, b, (((1,), (1,)), ((), ())), preferred_element_type=F32)


def _with_ones(v):
    return jnp.concatenate([v, jnp.ones_like(v)], axis=1)


def _swa_chunk(q, kband, vext, sinks, valid):
    qst = jnp.concatenate([q[:, g * HEAD_DIM:(g + 1) * HEAD_DIM] for g in range(SWA_GROUP)], axis=0)
    s = _dot_nt(qst, kband) * (ATTN_SCALE * LOG2E)
    if valid is not None:
        s = jnp.where(valid, s, NEG)
    es, sink_e = [], []
    for g in range(SWA_GROUP):
        sg = s[g * CHUNK:(g + 1) * CHUNK, :]
        sink2 = sinks[g] * LOG2E
        m = jnp.maximum(jnp.max(sg, axis=-1, keepdims=True), sink2)
        es.append(jnp.exp2(sg - m).astype(BF16))
        sink_e.append(jnp.exp2(sink2 - m))
    r = jnp.dot(jnp.concatenate(es, axis=0), vext, preferred_element_type=F32)
    outs = []
    for g in range(SWA_GROUP):
        rg = r[g * CHUNK:(g + 1) * CHUNK, :]
        outs.append(rg[:, :HEAD_DIM] * (1.0 / (rg[:, HEAD_DIM:] + sink_e[g])))
    return jnp.concatenate(outs, axis=1)


def _cb_attend(q, kband, vext, bias2):
    s = _dot_nt(q, kband) * (ATTN_SCALE * LOG2E) + bias2
    m = jnp.max(s, axis=-1, keepdims=True)
    r = jnp.dot(jnp.exp2(s - m).astype(BF16), vext, preferred_element_type=F32)
    return r[:, :HEAD_DIM] * (1.0 / r[:, HEAD_DIM:])


def _cb_bias_block(r, variant, n_rows, n_cols):
    t = pltpu.roll(jnp.broadcast_to(r, (n_rows, CB_RLEN)), 0, 1, stride=1, stride_axis=0)[:, :n_cols]
    c0 = variant * Q_CHUNKS
    sc = max(c0 - CB_PREV, 0)
    qc = c0 + lax.broadcasted_iota(jnp.int32, (n_rows, n_cols), 0) // CHUNK
    kc = sc + lax.broadcasted_iota(jnp.int32, (n_rows, n_cols), 1) // CHUNK
    return jnp.where(kc <= qc, jnp.where(kc >= qc - CB_PREV, t * LOG2E, NEG), NEG)


def _cast_rows(dst, src, n_rows, step, ones=False):
    def body(i, c):
        r0 = pl.multiple_of(i * step, step)
        v = src[pl.ds(r0, step), :].astype(BF16)
        dst[pl.ds(r0, step), :] = _with_ones(v) if ones else v
        return c
    lax.fori_loop(0, n_rows // step, body, 0)


SWA_BAND = (SWA_PREV + 1) * CHUNK
SWA_TQ = 16 * CHUNK


def _swa_prompt_kernel(sink_ref, q_ref, k_ref, v_ref, o_ref, kb_s, vb_s, *, layer):
    kh = pl.program_id(1)
    qi = pl.program_id(2)
    s_len = k_ref.shape[0]

    @pl.when(qi == 0)
    def _():
        _cast_rows(kb_s, k_ref, s_len, 512)
        _cast_rows(vb_s, v_ref, s_len, 512, ones=True)

    sinks = [sink_ref[layer, kh * SWA_GROUP + g] for g in range(SWA_GROUP)]
    jchunk = lax.broadcasted_iota(jnp.int32, (1, SWA_BAND), 1) // CHUNK
    for c in range(SWA_TQ // CHUNK):
        cg = qi * (SWA_TQ // CHUNK) + c
        sc = jnp.maximum(cg - SWA_PREV, 0)
        s0 = pl.multiple_of(sc * CHUNK, CHUNK)
        valid = (jchunk + sc) <= cg
        o = _swa_chunk(q_ref[c * CHUNK:(c + 1) * CHUNK, :], kb_s[pl.ds(s0, SWA_BAND), :],
                       vb_s[pl.ds(s0, SWA_BAND), :], sinks, valid)
        o_ref[c * CHUNK:(c + 1) * CHUNK, :] = o.astype(BF16)


def _swa_prompt(z, zkv, sinks, layer, n_seq, s_len):
    nq = s_len // SWA_TQ
    gw = SWA_GROUP * HEAD_DIM
    return pl.pallas_call(
        functools.partial(_swa_prompt_kernel, layer=layer),
        grid=(n_seq, SWA_KV_HEADS, nq),
        in_specs=[
            pl.BlockSpec(memory_space=pltpu.SMEM),
            pl.BlockSpec((SWA_TQ, gw), lambda b, k, q: (b * nq + q, Z_QB // gw + k)),
            pl.BlockSpec((s_len, HEAD_DIM), lambda b, k, q: (b, KV_KB // HEAD_DIM + k)),
            pl.BlockSpec((s_len, HEAD_DIM), lambda b, k, q: (b, KV_VB // HEAD_DIM + k)),
        ],
        out_specs=pl.BlockSpec((SWA_TQ, gw), lambda b, k, q: (b * nq + q, k)),
        out_shape=jax.ShapeDtypeStruct((n_seq * s_len, BRANCH_WIDTH), BF16),
        scratch_shapes=[pltpu.VMEM((s_len, HEAD_DIM), BF16), pltpu.VMEM((s_len, 2 * HEAD_DIM), BF16)],
        compiler_params=_params(("parallel", "parallel", "arbitrary")),
        name="swa_prompt",
    )(sinks, z, zkv, zkv)


def _swa_step_kernel(sink_ref, q_ref, k_ref, v_ref, ck_ref, cv_ref, o_ref, ko_ref, vo_ref, *, layer):
    n_past = ck_ref.shape[0] // SWA_KV_HEADS
    gw = SWA_GROUP * HEAD_DIM
    for kh in range(SWA_KV_HEADS):
        cs = slice(kh * HEAD_DIM, (kh + 1) * HEAD_DIM)
        kn = k_ref[:, cs]
        vn = v_ref[:, cs]
        kfull = jnp.concatenate(
            [ck_ref[pl.ds(kh, n_past, stride=SWA_KV_HEADS), :].astype(BF16), kn.astype(BF16)], axis=0)
        vfull = jnp.concatenate(
            [cv_ref[pl.ds(kh, n_past, stride=SWA_KV_HEADS), :].astype(BF16), vn.astype(BF16)], axis=0)
        sinks = [sink_ref[layer, kh * SWA_GROUP + g] for g in range(SWA_GROUP)]
        o = _swa_chunk(q_ref[:, kh * gw:(kh + 1) * gw], kfull, _with_ones(vfull), sinks, None)
        o_ref[:, kh * gw:(kh + 1) * gw] = o.astype(BF16)
        ko_ref[pl.ds(kh, CHUNK, stride=SWA_KV_HEADS), :] = kn
        vo_ref[pl.ds(kh, CHUNK, stride=SWA_KV_HEADS), :] = vn


def _swa_step(z, zkv, cache_k, cache_v, sinks, layer, n_seq):
    rows_past = cache_k.shape[2]
    rows_new = CHUNK * SWA_KV_HEADS
    cache = lambda: pl.BlockSpec((None, None, rows_past, HEAD_DIM), lambda b: (layer, b, 0, 0))
    new = lambda: pl.BlockSpec((rows_new, HEAD_DIM), lambda b: (b, 0))
    return pl.pallas_call(
        functools.partial(_swa_step_kernel, layer=layer),
        grid=(n_seq,),
        in_specs=[
            pl.BlockSpec(memory_space=pltpu.SMEM),
            pl.BlockSpec((CHUNK, BRANCH_WIDTH), lambda b: (b, Z_QB // BRANCH_WIDTH)),
            pl.BlockSpec((CHUNK, SWA_KVW), lambda b: (b, KV_KB // SWA_KVW)),
            pl.BlockSpec((CHUNK, SWA_KVW), lambda b: (b, KV_VB // SWA_KVW)),
            cache(), cache(),
        ],
        out_specs=[pl.BlockSpec((CHUNK, BRANCH_WIDTH), lambda b: (b, 0)), new(), new()],
        out_shape=[jax.ShapeDtypeStruct((n_seq * CHUNK, BRANCH_WIDTH), BF16),
                   jax.ShapeDtypeStruct((n_seq * rows_new, HEAD_DIM), F32),
                   jax.ShapeDtypeStruct((n_seq * rows_new, HEAD_DIM), F32)],
        compiler_params=_params(("parallel",)),
        name="swa_step",
    )(sinks, z, zkv, zkv, cache_k, cache_v)


def _cb_prompt_kernel(q_ref, k_ref, v_ref, r_ref, o_ref, kb_s, vb_s, bias_s):
    qi = pl.program_id(2)
    s_len = k_ref.shape[0]

    @pl.when(qi == 0)
    def _():
        _cast_rows(kb_s, k_ref, s_len, 512)
        _cast_rows(vb_s, v_ref, s_len, 512, ones=True)
        for v in range(CB_VARIANTS):
            bias_s[v] = _cb_bias_block(r_ref[v], v, Q_BLOCK, CB_BAND)

    for sub in range(CB_QSUB):
        blk = qi * CB_QSUB + sub
        sc = jnp.maximum(blk * Q_CHUNKS - CB_PREV, 0)
        s0 = pl.multiple_of(sc * CHUNK, CHUNK)
        rows = slice(sub * Q_BLOCK, (sub + 1) * Q_BLOCK)
        o = _cb_attend(q_ref[rows, :], kb_s[pl.ds(s0, CB_BAND), :], vb_s[pl.ds(s0, CB_BAND), :],
                       bias_s[jnp.minimum(blk, CB_VARIANTS - 1)])
        o_ref[rows, :] = o.astype(BF16)


def _cb_prompt(z, zkv, rows, layer, n_seq, s_len):
    tq = Q_BLOCK * CB_QSUB
    nq = s_len // tq
    return pl.pallas_call(
        _cb_prompt_kernel,
        grid=(n_seq, CB_HEADS, nq),
        in_specs=[
            pl.BlockSpec((tq, HEAD_DIM), lambda b, h, q: (b * nq + q, Z_QC // HEAD_DIM + h)),
            pl.BlockSpec((s_len, HEAD_DIM), lambda b, h, q: (b, KV_KC // HEAD_DIM + h)),
            pl.BlockSpec((s_len, HEAD_DIM), lambda b, h, q: (b, KV_VC // HEAD_DIM + h)),
            pl.BlockSpec((None, CB_VARIANTS, None, 1, CB_RLEN), lambda b, h, q: (layer, 0, h, 0, 0)),
        ],
        out_specs=pl.BlockSpec((tq, HEAD_DIM), lambda b, h, q: (b * nq + q, h)),
        out_shape=jax.ShapeDtypeStruct((n_seq * s_len, BRANCH_WIDTH), BF16),
        scratch_shapes=[pltpu.VMEM((s_len, HEAD_DIM), BF16), pltpu.VMEM((s_len, 2 * HEAD_DIM), BF16),
                        pltpu.VMEM((CB_VARIANTS, Q_BLOCK, CB_BAND), F32)],
        compiler_params=_params(("parallel", "parallel", "arbitrary")),
        name="cb_prompt",
    )(z, zkv, zkv, rows)


def _cb_step_kernel(q_ref, k_ref, v_ref, ck_ref, cv_ref, r_ref, o_ref, ko_ref, vo_ref, bias_s):
    n_past = ck_ref.shape[0] // CB_HEADS
    n_k = n_past + CHUNK

    @pl.when(pl.program_id(0) == 0)
    def _():
        for h in range(CB_HEADS):
            bias_s[h] = _cb_bias_block(r_ref[h], CB_VARIANTS - 1, CHUNK, n_k)

    for h in range(CB_HEADS):
        cs = slice(h * HEAD_DIM, (h + 1) * HEAD_DIM)
        kn = k_ref[:, cs]
        vn = v_ref[:, cs]
        kfull = jnp.concatenate(
            [ck_ref[pl.ds(h, n_past, stride=CB_HEADS), :].astype(BF16), kn.astype(BF16)], axis=0)
        vfull = jnp.concatenate(
            [cv_ref[pl.ds(h, n_past, stride=CB_HEADS), :].astype(BF16), vn.astype(BF16)], axis=0)
        o_ref[:, cs] = _cb_attend(q_ref[:, cs], kfull, _with_ones(vfull), bias_s[h]).astype(BF16)
        ko_ref[pl.ds(h, CHUNK, stride=CB_HEADS), :] = kn
        vo_ref[pl.ds(h, CHUNK, stride=CB_HEADS), :] = vn


def _cb_step(z, zkv, cache_k, cache_v, rows, layer, n_seq):
    rows_past = cache_k.shape[2]
    n_k = rows_past // CB_HEADS + CHUNK
    rows_new = CHUNK * CB_HEADS
    cache = lambda: pl.BlockSpec((None, None, rows_past, HEAD_DIM), lambda b: (layer, b, 0, 0))
    new = lambda: pl.BlockSpec((rows_new, HEAD_DIM), lambda b: (b, 0))
    return pl.pallas_call(
        _cb_step_kernel,
        grid=(n_seq,),
        in_specs=[
            pl.BlockSpec((CHUNK, BRANCH_WIDTH), lambda b: (b, Z_QC // BRANCH_WIDTH)),
            pl.BlockSpec((CHUNK, CB_KVW), lambda b: (b, KV_KC // CB_KVW)),
            pl.BlockSpec((CHUNK, CB_KVW), lambda b: (b, KV_VC // CB_KVW)),
            cache(), cache(),
            pl.BlockSpec((None, None, CB_HEADS, 1, CB_RLEN),
                         lambda b: (layer, CB_VARIANTS - 1, 0, 0, 0)),
        ],
        out_specs=[pl.BlockSpec((CHUNK, BRANCH_WIDTH), lambda b: (b, 0)), new(), new()],
        out_shape=[jax.ShapeDtypeStruct((n_seq * CHUNK, BRANCH_WIDTH), BF16),
                   jax.ShapeDtypeStruct((n_seq * rows_new, HEAD_DIM), F32),
                   jax.ShapeDtypeStruct((n_seq * rows_new, HEAD_DIM), F32)],
        scratch_shapes=[pltpu.VMEM((CB_HEADS, CHUNK, n_k), F32)],
        compiler_params=_params(("arbitrary",)),
        name="cb_step",
    )(z, zkv, zkv, cache_k, cache_v, rows)


def _merge_kernel(x_ref, a_ref, b_ref, c_ref, ga_ref, gb_ref, gc_ref, wb_ref, wo_ref, o_ref):
    mixed = None
    for r, (br, gr) in enumerate(((a_ref, ga_ref), (b_ref, gb_ref), (c_ref, gc_ref))):
        proj = jnp.dot(br[...], wb_ref[r], preferred_element_type=F32)
        gate = jax.nn.sigmoid(gr[...].astype(F32))
        mixed = gate * proj if mixed is None else mixed + gate * proj
    o_ref[...] = x_ref[...] + jnp.dot(mixed.astype(BF16), wo_ref[...], preferred_element_type=F32)


def _merge(x, out_a, out_b, out_c, z, w_branch, w_out, layer, tm):
    m = x.shape[0]
    branch = lambda: pl.BlockSpec((tm, BRANCH_WIDTH), lambda i: (i, 0))
    gate = lambda r: pl.BlockSpec((tm, D_MODEL), lambda i: (i, Z_GATES // D_MODEL + r))
    return pl.pallas_call(
        _merge_kernel,
        grid=(m // tm,),
        in_specs=[
            pl.BlockSpec((tm, D_MODEL), lambda i: (i, 0)),
            branch(), branch(), branch(),
            gate(0), gate(1), gate(2),
            pl.BlockSpec((None, N_BRANCH, BRANCH_WIDTH, D_MODEL), lambda i: (layer, 0, 0, 0),
                         pipeline_mode=pl.Buffered(1)),
            pl.BlockSpec((None, D_MODEL, D_MODEL), lambda i: (layer, 0, 0),
                         pipeline_mode=pl.Buffered(1)),
        ],
        out_specs=pl.BlockSpec((tm, D_MODEL), lambda i: (i, 0)),
        out_shape=jax.ShapeDtypeStruct((m, D_MODEL), F32),
        compiler_params=_params(("parallel",)),
        name="merge",
    )(x, out_a, out_b, out_c, z, z, z, w_branch, w_out)


def _mlp_kernel(*refs, final_norm, jobs):
    n_side = len(jobs.weights) if jobs else 0
    x_ref, g_ref, fg_ref, wu_ref, wd_ref = refs[:5]
    side_in = refs[5:5 + n_side]
    o_ref = refs[5 + n_side]
    side_out = refs[6 + n_side:6 + 2 * n_side]
    hn_ref = refs[-1]
    j = pl.program_id(1)

    @pl.when(j == 0)
    def _():
        xf = x_ref[...]
        hn_ref[...] = _rms(xf, g_ref[...]).astype(BF16)
        o_ref[...] = xf

    h = jnp.dot(hn_ref[...], wu_ref[...], preferred_element_type=F32)
    h = jnp.square(jnp.maximum(h, 0.0)).astype(BF16)
    o_ref[...] += jnp.dot(h, wd_ref[...], preferred_element_type=F32)

    if final_norm:
        @pl.when(j == pl.num_programs(1) - 1)
        def _():
            o_ref[...] = _rms(o_ref[...], fg_ref[...])

    if jobs:
        jobs.run(side_in, side_out)


def _mlp(x, g, final_g, w_up, w_down, layer, tm, tf, final_norm, cast=None):
    m = x.shape[0]
    grid = (m // tm, D_FF // tf)
    jobs = _CastJobs(cast[0], cast[1], grid, cast[2]) if cast else None
    return pl.pallas_call(
        functools.partial(_mlp_kernel, final_norm=final_norm, jobs=jobs),
        grid=grid,
        in_specs=[
            pl.BlockSpec((tm, D_MODEL), lambda i, j: (i, 0)),
            pl.BlockSpec((1, D_MODEL), lambda i, j: (0, 0)),
            pl.BlockSpec((1, D_MODEL), lambda i, j: (0, 0)),
            pl.BlockSpec((None, D_MODEL, tf), lambda i, j: (layer, 0, j)),
            pl.BlockSpec((None, tf, D_MODEL), lambda i, j: (layer, j, 0)),
        ] + (jobs.in_specs() if jobs else []),
        out_specs=[pl.BlockSpec((tm, D_MODEL), lambda i, j: (i, 0))] + (jobs.out_specs() if jobs else []),
        out_shape=[jax.ShapeDtypeStruct((m, D_MODEL), F32)] + (jobs.out_shape() if jobs else []),
        scratch_shapes=[pltpu.VMEM((tm, D_MODEL), BF16)],
        compiler_params=_params(("arbitrary", "arbitrary")),
        name="mlp",
    )(x, g, final_g, w_up, w_down, *(cast[0] if cast else ()))


def _permute_w_in(w_in):
    seg = lambda o, w: w_in[..., o:o + w]
    return jnp.concatenate([
        seg(_O_KC, CB_KVW), seg(_O_VC, CB_KVW), seg(_O_KB, SWA_KVW), seg(_O_VB, SWA_KVW),
        seg(_O_AX, D_RNN), seg(_O_AG, D_RNN), seg(_O_QB, BRANCH_WIDTH), seg(_O_QC, BRANCH_WIDTH),
        seg(_O_GATES, N_BRANCH * D_MODEL)], axis=-1).astype(BF16)


def _cb_bias_rows(table):
    m = jnp.arange(CB_RLEN)
    rel = jnp.where(m < CB_BAND, m, m - CB_RLEN)
    out = []
    for v in range(CB_VARIANTS):
        c0 = v * Q_CHUNKS
        sc = max(c0 - CB_PREV, 0)
        d = (c0 - sc) * CHUNK - rel
        idx = jnp.clip(d, -REL_CLIP, REL_CLIP) + REL_CLIP
        out.append(table.astype(F32)[:, :, idx])
    return jnp.stack(out, axis=1)[:, :, :, None, :]


def _pick_tm(m, want):
    tm = min(want, m)
    while m % tm:
        tm //= 2
    return tm


def _layer(x, n_seq, t_len, conv_buf, h0, caches, layer, p, w_in_l, w_mlp):
    m = x.shape[0]
    depth = p["w_up"].shape[0]
    tm_in = _pick_tm(m, 1024)
    if w_mlp is None:
        zkv, z, wu, wd = _in_proj(x, p["norm1"][layer], w_in_l, 0, tm_in,
                                  cast=((p["w_up"], p["w_down"]), layer))
        w_mlp = (wu[None], wd[None])
        cast_next = ((p["w_in"],), layer + 1, _permute_w_in) if layer + 1 < depth else None
    else:
        zkv, z = _in_proj(x, p["norm1"][layer], w_in_l, 0, tm_in)
        cast_next = None
    out_a, conv_o, h_o = _lru(z, conv_buf, h0, p["conv_w"][layer], p["conv_b"][layer],
                              p["wa"][layer], p["ba"][layer], p["wx"][layer], p["bx"][layer],
                              p["lam"][layer], n_seq, t_len, _pick_tm(t_len, 512))
    if caches is None:
        out_b = _swa_prompt(z, zkv, p["sinks"], layer, n_seq, t_len)
        out_c = _cb_prompt(z, zkv, p["cb_rows"], layer, n_seq, t_len)
        kv3 = zkv.reshape(n_seq, t_len, D_KV)
        n_swa = min(SWA_WINDOW, t_len)
        n_cb = min(CB_REACH, t_len)
        kv = (kv3[:, t_len - n_swa:, KV_KB:KV_KB + SWA_KVW], kv3[:, t_len - n_swa:, KV_VB:KV_VB + SWA_KVW],
              kv3[:, t_len - n_cb:, KV_KC:KV_KC + CB_KVW], kv3[:, t_len - n_cb:, KV_VC:KV_VC + CB_KVW])
    else:
        ck_b, cv_b, ck_c, cv_c = caches
        out_b, kb, vb = _swa_step(z, zkv, ck_b, cv_b, p["sinks"], layer, n_seq)
        out_c, kc, vc = _cb_step(z, zkv, ck_c, cv_c, p["cb_rows"], layer, n_seq)
        kv = (kb, vb, kc, vc)
    x = _merge(x, out_a, out_b, out_c, z, p["w_branch"], p["w_out"], layer, _pick_tm(m, 256))
    res = _mlp(x, p["norm2"][layer], p["final_g"], w_mlp[0], w_mlp[1], 0, _pick_tm(m, 512), 1024,
               final_norm=layer == depth - 1, cast=cast_next)
    w_in_next = res[1][None] if cast_next else None
    return res[0], conv_o, h_o[:, 0], kv, w_mlp, w_in_next


def kernel(x_prompt, x_sample, state_conv, state_lru, cache_swa_k, cache_swa_v, cache_cb_k, cache_cb_v, norm1_g, w_in, conv_w, conv_b, lru_wa, lru_ba, lru_wx, lru_bx, lru_lambda, attn_sinks, rel_bias_table, w_branch, w_out, norm2_g, w_up, w_down, final_g):
    depth = w_in.shape[0]
    nb, s_len, _ = x_prompt.shape
    db, d_len, _ = x_sample.shape
    assert d_len == CHUNK and s_len % (Q_BLOCK * CB_QSUB) == 0 and s_len % SWA_TQ == 0
    assert cache_swa_k.shape[2] == SWA_WINDOW and cache_cb_k.shape[2] == CB_REACH

    row = lambda v: v.reshape(depth, 1, -1)
    p = {
        "norm1": row(norm1_g), "norm2": row(norm2_g), "final_g": final_g.reshape(1, D_MODEL),
        "w_in": w_in, "w_up": w_up, "w_down": w_down,
        "conv_w": conv_w, "conv_b": row(conv_b),
        "wa": lru_wa.astype(BF16), "ba": row(lru_ba), "wx": lru_wx.astype(BF16), "bx": row(lru_bx),
        "lam": row(lru_lambda), "sinks": attn_sinks,
        "cb_rows": _cb_bias_rows(rel_bias_table),
        "w_branch": w_branch.astype(BF16), "w_out": w_out.astype(BF16),
    }
    w_in_l = _permute_w_in(w_in[0])[None]
    caches = (cache_swa_k.reshape(depth, db, SWA_WINDOW * SWA_KV_HEADS, HEAD_DIM),
              cache_swa_v.reshape(depth, db, SWA_WINDOW * SWA_KV_HEADS, HEAD_DIM),
              cache_cb_k.reshape(depth, db, CB_REACH * CB_HEADS, HEAD_DIM),
              cache_cb_v.reshape(depth, db, CB_REACH * CB_HEADS, HEAD_DIM))

    xp = x_prompt.reshape(nb * s_len, D_MODEL)
    xs = x_sample.reshape(db * d_len, D_MODEL)
    zero_conv = jnp.zeros((nb, CONV_W - 1, D_RNN), F32)
    zero_h = jnp.zeros((nb, 1, D_RNN), F32)
    heads = (SWA_KV_HEADS, SWA_KV_HEADS, CB_HEADS, CB_HEADS)
    outs = [[] for _ in range(12)]
    for l in range(depth):
        xp, conv_p, h_p, kv_p, w_mlp, w_in_next = _layer(xp, nb, s_len, zero_conv, zero_h, None, l, p,
                                                         w_in_l, None)
        xs, conv_s, h_s, kv_s, _, _ = _layer(xs, db, d_len, state_conv[l],
                                             state_lru[l].reshape(db, 1, D_RNN), caches, l, p, w_in_l, w_mlp)
        w_in_l = w_in_next
        outs[0].append(conv_p)
        outs[1].append(h_p)
        outs[6].append(conv_s)
        outs[7].append(h_s)
        for n in range(4):
            outs[2 + n].append(kv_p[n].reshape(nb, -1, heads[n], HEAD_DIM))
            outs[8 + n].append(kv_s[n].reshape(db, d_len, heads[n], HEAD_DIM))

    y_prompt = xp.reshape(nb, s_len, D_MODEL)
    y_sample = xs.reshape(db, d_len, D_MODEL)
    return (y_prompt, y_sample) + tuple(jnp.stack(o) for o in outs)
```

```python
import functools

import jax
import jax.numpy as jnp
from jax import lax
from jax.experimental import pallas as pl
from jax.experimental.pallas import tpu as pltpu

F32 = jnp.float32
BF16 = jnp.bfloat16

D_MODEL = 2048
CHUNK = 64
HEAD_DIM = 128
BRANCH_WIDTH = D_MODEL // 2
N_BRANCH = 3
D_RNN = BRANCH_WIDTH
LRU_BLOCKS = 8
LRU_BLOCK = D_RNN // LRU_BLOCKS
CONV_W = 4
LRU_C = 8.0
SWA_HEADS = BRANCH_WIDTH // HEAD_DIM
SWA_KV_HEADS = 2
SWA_GROUP = SWA_HEADS // SWA_KV_HEADS
SWA_WINDOW = 128
SWA_PREV = SWA_WINDOW // CHUNK
CB_HEADS = BRANCH_WIDTH // HEAD_DIM
CB_PREV = 8
CB_REACH = CB_PREV * CHUNK
REL_CLIP = 128
D_FF = 4 * D_MODEL
EPS = 1e-6
NEG = -1e30
ATTN_SCALE = HEAD_DIM ** -0.5
LOG2E = 1.4426950408889634

_O_AX, _O_AG, _O_QB, _O_KB, _O_VB, _O_QC, _O_KC, _O_VC, _O_GATES = (
    0, 1024, 2048, 3072, 3328, 3584, 4608, 5632, 6656)
D_IN = _O_GATES + N_BRANCH * D_MODEL
SWA_KVW = SWA_KV_HEADS * HEAD_DIM
CB_KVW = CB_HEADS * HEAD_DIM
D_KV = 2 * SWA_KVW + 2 * CB_KVW
D_Z = D_IN - D_KV
KV_KC, KV_VC, KV_KB, KV_VB = 0, 1024, 2048, 2304
Z_AX, Z_AG, Z_QB, Z_QC, Z_GATES = 0, 1024, 2048, 3072, 4096

VMEM_LIMIT_BYTES = 56 * 1024 * 1024

Q_BLOCK = 4 * CHUNK
Q_CHUNKS = Q_BLOCK // CHUNK
CB_BAND = (CB_PREV + Q_CHUNKS) * CHUNK
CB_QSUB = 8
CB_RLEN = 1024
assert CB_RLEN >= Q_BLOCK + CB_BAND - 1
CB_VARIANTS = 3


def _params(semantics):
    return pltpu.CompilerParams(dimension_semantics=semantics, vmem_limit_bytes=VMEM_LIMIT_BYTES)


def _rms(xf, g):
    return xf * lax.rsqrt(jnp.mean(xf * xf, axis=-1, keepdims=True) + EPS) * g


IN_TN = 1280
IN_KV_TILES = D_KV // IN_TN


class _CastJobs:
    def __init__(self, weights, layer, grid, transform=None):
        self.weights, self.layer, self.transform = weights, layer, transform
        steps = grid[0] * grid[1]
        n = 1
        while n * 2 <= min(steps, MAX_CAST_STEPS):
            n *= 2
        self.n_conv, self.inner = n, grid[1]
        assert all(w.shape[1] % (n * BF16_ROWS) == 0 for w in weights)

    def _slab(self, i, j):
        return jnp.minimum(i * self.inner + j, self.n_conv - 1)

    def in_specs(self):
        return [pl.BlockSpec((None, w.shape[1] // self.n_conv, w.shape[2]),
                             lambda i, j: (self.layer, self._slab(i, j), 0)) for w in self.weights]

    def out_specs(self):
        return [pl.BlockSpec((w.shape[1] // self.n_conv, w.shape[2]),
                             lambda i, j: (self._slab(i, j), 0)) for w in self.weights]

    def out_shape(self):
        return [jax.ShapeDtypeStruct(w.shape[1:], BF16) for w in self.weights]

    def run(self, src_refs, dst_refs):
        step = pl.program_id(0) * self.inner + pl.program_id(1)

        @pl.when(step < self.n_conv)
        def _():
            for s, d in zip(src_refs, dst_refs):
                v = s[...]
                d[...] = (self.transform(v) if self.transform else v).astype(BF16)


MAX_CAST_STEPS = 128
BF16_ROWS = 16


def _in_proj_kernel(*refs, jobs):
    n_side = len(jobs.weights) if jobs else 0
    x_ref, g_ref, w_ref = refs[:3]
    side_in = refs[3:3 + n_side]
    zkv_ref, z_ref = refs[3 + n_side:5 + n_side]
    side_out = refs[5 + n_side:5 + 2 * n_side]
    xn_ref = refs[-1]
    j = pl.program_id(1)

    @pl.when(j == 0)
    def _():
        xn_ref[...] = _rms(x_ref[...], g_ref[...]).astype(BF16)

    @pl.when(j < IN_KV_TILES)
    def _():
        zkv_ref[...] = jnp.dot(xn_ref[...], w_ref[...], preferred_element_type=F32)

    @pl.when(j >= IN_KV_TILES)
    def _():
        z_ref[...] = jnp.dot(xn_ref[...], w_ref[...], preferred_element_type=F32).astype(BF16)

    if jobs:
        jobs.run(side_in, side_out)


def _in_proj(x, g, w, layer, tm, cast=None):
    m = x.shape[0]
    grid = (m // tm, D_IN // IN_TN)
    jobs = _CastJobs(cast[0], cast[1], grid) if cast else None
    return pl.pallas_call(
        functools.partial(_in_proj_kernel, jobs=jobs),
        grid=grid,
        in_specs=[
            pl.BlockSpec((tm, D_MODEL), lambda i, j: (i, 0)),
            pl.BlockSpec((1, D_MODEL), lambda i, j: (0, 0)),
            pl.BlockSpec((None, D_MODEL, IN_TN), lambda i, j: (layer, 0, j)),
        ] + (jobs.in_specs() if jobs else []),
        out_specs=[
            pl.BlockSpec((tm, IN_TN), lambda i, j: (i, jnp.minimum(j, IN_KV_TILES - 1))),
            pl.BlockSpec((tm, IN_TN), lambda i, j: (i, jnp.maximum(j - IN_KV_TILES, 0))),
        ] + (jobs.out_specs() if jobs else []),
        out_shape=[jax.ShapeDtypeStruct((m, D_KV), F32), jax.ShapeDtypeStruct((m, D_Z), BF16)]
        + (jobs.out_shape() if jobs else []),
        scratch_shapes=[pltpu.VMEM((tm, D_MODEL), BF16)],
        compiler_params=_params(("arbitrary", "arbitrary")),
        name="in_proj",
    )(x, g, w, *(cast[0] if cast else ()))


_XPAD = 8
LANES = 128
_SEGS = 8
_SEG_LEN = 4


def _sigmoid(x):
    return 0.5 * (jnp.tanh(0.5 * x) + 1.0)


def _lru_kernel(ax_ref, ag_ref, cbuf_ref, h0_ref, cw_ref, cb_ref, wa_ref, ba_ref, wx_ref, bx_ref,
                lam_ref, out_ref, convo_ref, ho_ref, xbuf, a_s, b_s, h_s):
    t = pl.program_id(1)
    nt = pl.num_programs(1)
    tt = ax_ref.shape[0]

    @pl.when(t == 0)
    def _():
        xbuf[...] = jnp.zeros_like(xbuf)
        xbuf[_XPAD - (CONV_W - 1):, :] = cbuf_ref[...]
        h_s[...] = h0_ref[...]

    x = ax_ref[...].astype(F32)
    xe = jnp.concatenate([xbuf[...], x], axis=0)
    acc = xe * cw_ref[0:1, :]
    for k in range(1, CONV_W):
        acc = xe * cw_ref[k:k + 1, :] + pltpu.roll(acc, 1, 0)
    u = cb_ref[...] + acc[_XPAD:, :]
    tail = x[tt - (CONV_W - 1):, :]
    xbuf[...] = x[tt - _XPAD:, :]

    ub = u.astype(BF16)
    r_parts, i_parts = [], []
    for n in range(LRU_BLOCKS):
        un = ub[:, n * LRU_BLOCK:(n + 1) * LRU_BLOCK]
        r_parts.append(jnp.dot(un, wa_ref[n], preferred_element_type=F32))
        i_parts.append(jnp.dot(un, wx_ref[n], preferred_element_type=F32))
    r = _sigmoid(jnp.concatenate(r_parts, axis=1) + ba_ref[...])
    i = _sigmoid(jnp.concatenate(i_parts, axis=1) + bx_ref[...])
    log_a = -LRU_C * r * jax.nn.softplus(-lam_ref[...])
    a = jnp.exp(log_a)
    y = 1.0 - a * a
    b = jnp.where(y > 0.0, y * lax.rsqrt(y), 0.0) * (i * u)
    n_lg = D_RNN // LANES
    for lg in range(n_lg):
        a_s[lg] = a[:, lg * LANES:(lg + 1) * LANES]
        b_s[lg] = b[:, lg * LANES:(lg + 1) * LANES]

    row = lax.broadcasted_iota(jnp.int32, (_SEGS, LANES), 0)
    sub = _SEGS * _SEG_LEN

    def body(sb, h):
        r0 = pl.multiple_of(sb * sub, sub)
        step = lambda j: pl.ds(r0 + j, _SEGS, stride=_SEG_LEN)
        h_next = []
        for lg in range(n_lg):
            a_g, b_g = a_s.at[lg], b_s.at[lg]
            h_g = h[:, lg * LANES:(lg + 1) * LANES]
            acs, bcs = [a_g[step(0), :]], [b_g[step(0), :]]
            for j in range(1, _SEG_LEN):
                aj = a_g[step(j), :]
                bcs.append(aj * bcs[-1] + b_g[step(j), :])
                acs.append(aj * acs[-1])
            at, bt = acs[-1], bcs[-1]
            for s in (1, 2, 4):
                m = row >= s
                bt_new = jnp.where(m, at * pltpu.roll(bt, s, 0) + bt, bt)
                at = jnp.where(m, at * pltpu.roll(at, s, 0), at)
                bt = bt_new
            after = at * h_g + bt
            entry = jnp.where(row >= 1, pltpu.roll(after, 1, 0), h_g)
            for j in range(_SEG_LEN):
                b_g[step(j), :] = acs[j] * entry + bcs[j]
            h_next.append(after[_SEGS - 1:_SEGS, :])
        return jnp.concatenate(h_next, axis=1)

    h = lax.fori_loop(0, tt // sub, body, h_s[...])
    h_s[...] = h
    hs = jnp.concatenate([b_s[lg] for lg in range(n_lg)], axis=1)
    out_ref[...] = (hs * jax.nn.gelu(ag_ref[...].astype(F32))).astype(BF16)

    @pl.when(t == nt - 1)
    def _():
        convo_ref[...] = tail
        ho_ref[...] = h


def _lru(z, conv_buf, h0, cw, cb, wa, ba, wx, bx, lam, n_seq, t_len, tt):
    nt = t_len // tt
    row = lambda b, t: b * nt + t
    vec = lambda: pl.BlockSpec((1, D_RNN), lambda b, t: (0, 0))
    blk = lambda: pl.BlockSpec((LRU_BLOCKS, LRU_BLOCK, LRU_BLOCK), lambda b, t: (0, 0, 0))
    return pl.pallas_call(
        _lru_kernel,
        grid=(n_seq, nt),
        in_specs=[
            pl.BlockSpec((tt, D_RNN), lambda b, t: (row(b, t), Z_AX // D_RNN)),
            pl.BlockSpec((tt, D_RNN), lambda b, t: (row(b, t), Z_AG // D_RNN)),
            pl.BlockSpec((None, CONV_W - 1, D_RNN), lambda b, t: (b, 0, 0)),
            pl.BlockSpec((None, 1, D_RNN), lambda b, t: (b, 0, 0)),
            pl.BlockSpec((CONV_W, D_RNN), lambda b, t: (0, 0)),
            vec(), blk(), vec(), blk(), vec(), vec(),
        ],
        out_specs=[
            pl.BlockSpec((tt, D_RNN), lambda b, t: (row(b, t), 0)),
            pl.BlockSpec((None, CONV_W - 1, D_RNN), lambda b, t: (b, 0, 0)),
            pl.BlockSpec((None, 1, D_RNN), lambda b, t: (b, 0, 0)),
        ],
        out_shape=[
            jax.ShapeDtypeStruct((n_seq * t_len, D_RNN), BF16),
            jax.ShapeDtypeStruct((n_seq, CONV_W - 1, D_RNN), F32),
            jax.ShapeDtypeStruct((n_seq, 1, D_RNN), F32),
        ],
        scratch_shapes=[
            pltpu.VMEM((_XPAD, D_RNN), F32),
            pltpu.VMEM((D_RNN // LANES, tt, LANES), F32),
            pltpu.VMEM((D_RNN // LANES, tt, LANES), F32),
            pltpu.VMEM((1, D_RNN), F32),
        ],
        compiler_params=_params(("parallel", "arbitrary")),
        name="lru",
    )(z, z, conv_buf, h0, cw, cb, wa, ba, wx, bx, lam)


def _dot_nt(a, b):
    return lax.dot_general(a, b, (((1,), (1,)), ((), ())), preferred_element_type=F32)


def _with_ones(v):
    return jnp.concatenate([v, jnp.ones_like(v)], axis=1)


def _swa_chunk(q, kband, vext, sinks, valid):
    qst = jnp.concatenate([q[:, g * HEAD_DIM:(g + 1) * HEAD_DIM] for g in range(SWA_GROUP)], axis=0)
    s = _dot_nt(qst, kband) * (ATTN_SCALE * LOG2E)
    if valid is not None:
        s = jnp.where(valid, s, NEG)
    es, sink_e = [], []
    for g in range(SWA_GROUP):
        sg = s[g * CHUNK:(g + 1) * CHUNK, :]
        sink2 = sinks[g] * LOG2E
        m = jnp.maximum(jnp.max(sg, axis=-1, keepdims=True), sink2)
        es.append(jnp.exp2(sg - m).astype(BF16))
        sink_e.append(jnp.exp2(sink2 - m))
    r = jnp.dot(jnp.concatenate(es, axis=0), vext, preferred_element_type=F32)
    outs = []
    for g in range(SWA_GROUP):
        rg = r[g * CHUNK:(g + 1) * CHUNK, :]
        outs.append(rg[:, :HEAD_DIM] * (1.0 / (rg[:, HEAD_DIM:] + sink_e[g])))
    return jnp.concatenate(outs, axis=1)


def _cb_attend(q, kband, vext, bias2):
    s = _dot_nt(q, kband) * (ATTN_SCALE * LOG2E) + bias2
    m = jnp.max(s, axis=-1, keepdims=True)
    r = jnp.dot(jnp.exp2(s - m).astype(BF16), vext, preferred_element_type=F32)
    return r[:, :HEAD_DIM] * (1.0 / r[:, HEAD_DIM:])


def _cb_bias_block(r, variant, n_rows, n_cols):
    t = pltpu.roll(jnp.broadcast_to(r, (n_rows, CB_RLEN)), 0, 1, stride=1, stride_axis=0)[:, :n_cols]
    c0 = variant * Q_CHUNKS
    sc = max(c0 - CB_PREV, 0)
    qc = c0 + lax.broadcasted_iota(jnp.int32, (n_rows, n_cols), 0) // CHUNK
    kc = sc + lax.broadcasted_iota(jnp.int32, (n_rows, n_cols), 1) // CHUNK
    return jnp.where(kc <= qc, jnp.where(kc >= qc - CB_PREV, t * LOG2E, NEG), NEG)


def _cast_rows(dst, src, n_rows, step, ones=False):
    def body(i, c):
        r0 = pl.multiple_of(i * step, step)
        v = src[pl.ds(r0, step), :].astype(BF16)
        dst[pl.ds(r0, step), :] = _with_ones(v) if ones else v
        return c
    lax.fori_loop(0, n_rows // step, body, 0)


SWA_BAND = (SWA_PREV + 1) * CHUNK
SWA_TQ = 16 * CHUNK


def _swa_prompt_kernel(sink_ref, q_ref, k_ref, v_ref, o_ref, kb_s, vb_s, *, layer):
    kh = pl.program_id(1)
    qi = pl.program_id(2)
    s_len = k_ref.shape[0]

    @pl.when(qi == 0)
    def _():
        _cast_rows(kb_s, k_ref, s_len, 512)
        _cast_rows(vb_s, v_ref, s_len, 512, ones=True)

    sinks = [sink_ref[layer, kh * SWA_GROUP + g] for g in range(SWA_GROUP)]
    jchunk = lax.broadcasted_iota(jnp.int32, (1, SWA_BAND), 1) // CHUNK
    for c in range(SWA_TQ // CHUNK):
        cg = qi * (SWA_TQ // CHUNK) + c
        sc = jnp.maximum(cg - SWA_PREV, 0)
        s0 = pl.multiple_of(sc * CHUNK, CHUNK)
        valid = (jchunk + sc) <= cg
        o = _swa_chunk(q_ref[c * CHUNK:(c + 1) * CHUNK, :], kb_s[pl.ds(s0, SWA_BAND), :],
                       vb_s[pl.ds(s0, SWA_BAND), :], sinks, valid)
        o_ref[c * CHUNK:(c + 1) * CHUNK, :] = o.astype(BF16)


def _swa_prompt(z, zkv, sinks, layer, n_seq, s_len):
    nq = s_len // SWA_TQ
    gw = SWA_GROUP * HEAD_DIM
    return pl.pallas_call(
        functools.partial(_swa_prompt_kernel, layer=layer),
        grid=(n_seq, SWA_KV_HEADS, nq),
        in_specs=[
            pl.BlockSpec(memory_space=pltpu.SMEM),
            pl.BlockSpec((SWA_TQ, gw), lambda b, k, q: (b * nq + q, Z_QB // gw + k)),
            pl.BlockSpec((s_len, HEAD_DIM), lambda b, k, q: (b, KV_KB // HEAD_DIM + k)),
            pl.BlockSpec((s_len, HEAD_DIM), lambda b, k, q: (b, KV_VB // HEAD_DIM + k)),
        ],
        out_specs=pl.BlockSpec((SWA_TQ, gw), lambda b, k, q: (b * nq + q, k)),
        out_shape=jax.ShapeDtypeStruct((n_seq * s_len, BRANCH_WIDTH), BF16),
        scratch_shapes=[pltpu.VMEM((s_len, HEAD_DIM), BF16), pltpu.VMEM((s_len, 2 * HEAD_DIM), BF16)],
        compiler_params=_params(("parallel", "parallel", "arbitrary")),
        name="swa_prompt",
    )(sinks, z, zkv, zkv)


def _swa_step_kernel(sink_ref, q_ref, k_ref, v_ref, ck_ref, cv_ref, o_ref, ko_ref, vo_ref, *, layer):
    n_past = ck_ref.shape[0] // SWA_KV_HEADS
    gw = SWA_GROUP * HEAD_DIM
    for kh in range(SWA_KV_HEADS):
        cs = slice(kh * HEAD_DIM, (kh + 1) * HEAD_DIM)
        kn = k_ref[:, cs]
        vn = v_ref[:, cs]
        kfull = jnp.concatenate(
            [ck_ref[pl.ds(kh, n_past, stride=SWA_KV_HEADS), :].astype(BF16), kn.astype(BF16)], axis=0)
        vfull = jnp.concatenate(
            [cv_ref[pl.ds(kh, n_past, stride=SWA_KV_HEADS), :].astype(BF16), vn.astype(BF16)], axis=0)
        sinks = [sink_ref[layer, kh * SWA_GROUP + g] for g in range(SWA_GROUP)]
        o = _swa_chunk(q_ref[:, kh * gw:(kh + 1) * gw], kfull, _with_ones(vfull), sinks, None)
        o_ref[:, kh * gw:(kh + 1) * gw] = o.astype(BF16)
        ko_ref[pl.ds(kh, CHUNK, stride=SWA_KV_HEADS), :] = kn
        vo_ref[pl.ds(kh, CHUNK, stride=SWA_KV_HEADS), :] = vn


def _swa_step(z, zkv, cache_k, cache_v, sinks, layer, n_seq):
    rows_past = cache_k.shape[2]
    rows_new = CHUNK * SWA_KV_HEADS
    cache = lambda: pl.BlockSpec((None, None, rows_past, HEAD_DIM), lambda b: (layer, b, 0, 0))
    new = lambda: pl.BlockSpec((rows_new, HEAD_DIM), lambda b: (b, 0))
    return pl.pallas_call(
        functools.partial(_swa_step_kernel, layer=layer),
        grid=(n_seq,),
        in_specs=[
            pl.BlockSpec(memory_space=pltpu.SMEM),
            pl.BlockSpec((CHUNK, BRANCH_WIDTH), lambda b: (b, Z_QB // BRANCH_WIDTH)),
            pl.BlockSpec((CHUNK, SWA_KVW), lambda b: (b, KV_KB // SWA_KVW)),
            pl.BlockSpec((CHUNK, SWA_KVW), lambda b: (b, KV_VB // SWA_KVW)),
            cache(), cache(),
        ],
        out_specs=[pl.BlockSpec((CHUNK, BRANCH_WIDTH), lambda b: (b, 0)), new(), new()],
        out_shape=[jax.ShapeDtypeStruct((n_seq * CHUNK, BRANCH_WIDTH), BF16),
                   jax.ShapeDtypeStruct((n_seq * rows_new, HEAD_DIM), F32),
                   jax.ShapeDtypeStruct((n_seq * rows_new, HEAD_DIM), F32)],
        compiler_params=_params(("parallel",)),
        name="swa_step",
    )(sinks, z, zkv, zkv, cache_k, cache_v)


def _cb_prompt_kernel(q_ref, k_ref, v_ref, r_ref, o_ref, kb_s, vb_s, bias_s):
    qi = pl.program_id(2)
    s_len = k_ref.shape[0]

    @pl.when(qi == 0)
    def _():
        _cast_rows(kb_s, k_ref, s_len, 512)
        _cast_rows(vb_s, v_ref, s_len, 512, ones=True)
        for v in range(CB_VARIANTS):
            bias_s[v] = _cb_bias_block(r_ref[v], v, Q_BLOCK, CB_BAND)

    for sub in range(CB_QSUB):
        blk = qi * CB_QSUB + sub
        sc = jnp.maximum(blk * Q_CHUNKS - CB_PREV, 0)
        s0 = pl.multiple_of(sc * CHUNK, CHUNK)
        rows = slice(sub * Q_BLOCK, (sub + 1) * Q_BLOCK)
        o = _cb_attend(q_ref[rows, :], kb_s[pl.ds(s0, CB_BAND), :], vb_s[pl.ds(s0, CB_BAND), :],
                       bias_s[jnp.minimum(blk, CB_VARIANTS - 1)])
        o_ref[rows, :] = o.astype(BF16)


def _cb_prompt(z, zkv, rows, layer, n_seq, s_len):
    tq = Q_BLOCK * CB_QSUB
    nq = s_len // tq
    return pl.pallas_call(
        _cb_prompt_kernel,
        grid=(n_seq, CB_HEADS, nq),
        in_specs=[
            pl.BlockSpec((tq, HEAD_DIM), lambda b, h, q: (b * nq + q, Z_QC // HEAD_DIM + h)),
            pl.BlockSpec((s_len, HEAD_DIM), lambda b, h, q: (b, KV_KC // HEAD_DIM + h)),
            pl.BlockSpec((s_len, HEAD_DIM), lambda b, h, q: (b, KV_VC // HEAD_DIM + h)),
            pl.BlockSpec((None, CB_VARIANTS, None, 1, CB_RLEN), lambda b, h, q: (layer, 0, h, 0, 0)),
        ],
        out_specs=pl.BlockSpec((tq, HEAD_DIM), lambda b, h, q: (b * nq + q, h)),
        out_shape=jax.ShapeDtypeStruct((n_seq * s_len, BRANCH_WIDTH), BF16),
        scratch_shapes=[pltpu.VMEM((s_len, HEAD_DIM), BF16), pltpu.VMEM((s_len, 2 * HEAD_DIM), BF16),
                        pltpu.VMEM((CB_VARIANTS, Q_BLOCK, CB_BAND), F32)],
        compiler_params=_params(("parallel", "parallel", "arbitrary")),
        name="cb_prompt",
    )(z, zkv, zkv, rows)


def _cb_step_kernel(q_ref, k_ref, v_ref, ck_ref, cv_ref, r_ref, o_ref, ko_ref, vo_ref, bias_s):
    n_past = ck_ref.shape[0] // CB_HEADS
    n_k = n_past + CHUNK

    @pl.when(pl.program_id(0) == 0)
    def _():
        for h in range(CB_HEADS):
            bias_s[h] = _cb_bias_block(r_ref[h], CB_VARIANTS - 1, CHUNK, n_k)

    for h in range(CB_HEADS):
        cs = slice(h * HEAD_DIM, (h + 1) * HEAD_DIM)
        kn = k_ref[:, cs]
        vn = v_ref[:, cs]
        kfull = jnp.concatenate(
            [ck_ref[pl.ds(h, n_past, stride=CB_HEADS), :].astype(BF16), kn.astype(BF16)], axis=0)
        vfull = jnp.concatenate(
            [cv_ref[pl.ds(h, n_past, stride=CB_HEADS), :].astype(BF16), vn.astype(BF16)], axis=0)
        o_ref[:, cs] = _cb_attend(q_ref[:, cs], kfull, _with_ones(vfull), bias_s[h]).astype(BF16)
        ko_ref[pl.ds(h, CHUNK, stride=CB_HEADS), :] = kn
        vo_ref[pl.ds(h, CHUNK, stride=CB_HEADS), :] = vn


def _cb_step(z, zkv, cache_k, cache_v, rows, layer, n_seq):
    rows_past = cache_k.shape[2]
    n_k = rows_past // CB_HEADS + CHUNK
    rows_new = CHUNK * CB_HEADS
    cache = lambda: pl.BlockSpec((None, None, rows_past, HEAD_DIM), lambda b: (layer, b, 0, 0))
    new = lambda: pl.BlockSpec((rows_new, HEAD_DIM), lambda b: (b, 0))
    return pl.pallas_call(
        _cb_step_kernel,
        grid=(n_seq,),
        in_specs=[
            pl.BlockSpec((CHUNK, BRANCH_WIDTH), lambda b: (b, Z_QC // BRANCH_WIDTH)),
            pl.BlockSpec((CHUNK, CB_KVW), lambda b: (b, KV_KC // CB_KVW)),
            pl.BlockSpec((CHUNK, CB_KVW), lambda b: (b, KV_VC // CB_KVW)),
            cache(), cache(),
            pl.BlockSpec((None, None, CB_HEADS, 1, CB_RLEN),
                         lambda b: (layer, CB_VARIANTS - 1, 0, 0, 0)),
        ],
        out_specs=[pl.BlockSpec((CHUNK, BRANCH_WIDTH), lambda b: (b, 0)), new(), new()],
        out_shape=[jax.ShapeDtypeStruct((n_seq * CHUNK, BRANCH_WIDTH), BF16),
                   jax.ShapeDtypeStruct((n_seq * rows_new, HEAD_DIM), F32),
                   jax.ShapeDtypeStruct((n_seq * rows_new, HEAD_DIM), F32)],
        scratch_shapes=[pltpu.VMEM((CB_HEADS, CHUNK, n_k), F32)],
        compiler_params=_params(("arbitrary",)),
        name="cb_step",
    )(z, zkv, zkv, cache_k, cache_v, rows)


def _merge_kernel(x_ref, a_ref, b_ref, c_ref, ga_ref, gb_ref, gc_ref, wb_ref, wo_ref, o_ref):
    mixed = None
    for r, (br, gr) in enumerate(((a_ref, ga_ref), (b_ref, gb_ref), (c_ref, gc_ref))):
        proj = jnp.dot(br[...], wb_ref[r], preferred_element_type=F32)
        gate = jax.nn.sigmoid(gr[...].astype(F32))
        mixed = gate * proj if mixed is None else mixed + gate * proj
    o_ref[...] = x_ref[...] + jnp.dot(mixed.astype(BF16), wo_ref[...], preferred_element_type=F32)


def _merge(x, out_a, out_b, out_c, z, w_branch, w_out, layer, tm):
    m = x.shape[0]
    branch = lambda: pl.BlockSpec((tm, BRANCH_WIDTH), lambda i: (i, 0))
    gate = lambda r: pl.BlockSpec((tm, D_MODEL), lambda i: (i, Z_GATES // D_MODEL + r))
    return pl.pallas_call(
        _merge_kernel,
        grid=(m // tm,),
        in_specs=[
            pl.BlockSpec((tm, D_MODEL), lambda i: (i, 0)),
            branch(), branch(), branch(),
            gate(0), gate(1), gate(2),
            pl.BlockSpec((None, N_BRANCH, BRANCH_WIDTH, D_MODEL), lambda i: (layer, 0, 0, 0),
                         pipeline_mode=pl.Buffered(1)),
            pl.BlockSpec((None, D_MODEL, D_MODEL), lambda i: (layer, 0, 0),
                         pipeline_mode=pl.Buffered(1)),
        ],
        out_specs=pl.BlockSpec((tm, D_MODEL), lambda i: (i, 0)),
        out_shape=jax.ShapeDtypeStruct((m, D_MODEL), F32),
        compiler_params=_params(("parallel",)),
        name="merge",
    )(x, out_a, out_b, out_c, z, z, z, w_branch, w_out)


def _mlp_kernel(*refs, final_norm, jobs):
    n_side = len(jobs.weights) if jobs else 0
    x_ref, g_ref, fg_ref, wu_ref, wd_ref = refs[:5]
    side_in = refs[5:5 + n_side]
    o_ref = refs[5 + n_side]
    side_out = refs[6 + n_side:6 + 2 * n_side]
    hn_ref = refs[-1]
    j = pl.program_id(1)

    @pl.when(j == 0)
    def _():
        xf = x_ref[...]
        hn_ref[...] = _rms(xf, g_ref[...]).astype(BF16)
        o_ref[...] = xf

    h = jnp.dot(hn_ref[...], wu_ref[...], preferred_element_type=F32)
    h = jnp.square(jnp.maximum(h, 0.0)).astype(BF16)
    o_ref[...] += jnp.dot(h, wd_ref[...], preferred_element_type=F32)

    if final_norm:
        @pl.when(j == pl.num_programs(1) - 1)
        def _():
            o_ref[...] = _rms(o_ref[...], fg_ref[...])

    if jobs:
        jobs.run(side_in, side_out)


def _mlp(x, g, final_g, w_up, w_down, layer, tm, tf, final_norm, cast=None):
    m = x.shape[0]
    grid = (m // tm, D_FF // tf)
    jobs = _CastJobs(cast[0], cast[1], grid, cast[2]) if cast else None
    return pl.pallas_call(
        functools.partial(_mlp_kernel, final_norm=final_norm, jobs=jobs),
        grid=grid,
        in_specs=[
            pl.BlockSpec((tm, D_MODEL), lambda i, j: (i, 0)),
            pl.BlockSpec((1, D_MODEL), lambda i, j: (0, 0)),
            pl.BlockSpec((1, D_MODEL), lambda i, j: (0, 0)),
            pl.BlockSpec((None, D_MODEL, tf), lambda i, j: (layer, 0, j)),
            pl.BlockSpec((None, tf, D_MODEL), lambda i, j: (layer, j, 0)),
        ] + (jobs.in_specs() if jobs else []),
        out_specs=[pl.BlockSpec((tm, D_MODEL), lambda i, j: (i, 0))] + (jobs.out_specs() if jobs else []),
        out_shape=[jax.ShapeDtypeStruct((m, D_MODEL), F32)] + (jobs.out_shape() if jobs else []),
        scratch_shapes=[pltpu.VMEM((tm, D_MODEL), BF16)],
        compiler_params=_params(("arbitrary", "arbitrary")),
        name="mlp",
    )(x, g, final_g, w_up, w_down, *(cast[0] if cast else ()))


def _permute_w_in(w_in):
    seg = lambda o, w: w_in[..., o:o + w]
    return jnp.concatenate([
        seg(_O_KC, CB_KVW), seg(_O_VC, CB_KVW), seg(_O_KB, SWA_KVW), seg(_O_VB, SWA_KVW),
        seg(_O_AX, D_RNN), seg(_O_AG, D_RNN), seg(_O_QB, BRANCH_WIDTH), seg(_O_QC, BRANCH_WIDTH),
        seg(_O_GATES, N_BRANCH * D_MODEL)], axis=-1).astype(BF16)


def _cb_bias_rows(table):
    m = jnp.arange(CB_RLEN)
    rel = jnp.where(m < CB_BAND, m, m - CB_RLEN)
    out = []
    for v in range(CB_VARIANTS):
        c0 = v * Q_CHUNKS
        sc = max(c0 - CB_PREV, 0)
        d = (c0 - sc) * CHUNK - rel
        idx = jnp.clip(d, -REL_CLIP, REL_CLIP) + REL_CLIP
        out.append(table.astype(F32)[:, :, idx])
    return jnp.stack(out, axis=1)[:, :, :, None, :]


def _pick_tm(m, want):
    tm = min(want, m)
    while m % tm:
        tm //= 2
    return tm


def _layer(x, n_seq, t_len, conv_buf, h0, caches, layer, p, w_in_l, w_mlp):
    m = x.shape[0]
    depth = p["w_up"].shape[0]
    tm_in = _pick_tm(m, 1024)
    if w_mlp is None:
        zkv, z, wu, wd = _in_proj(x, p["norm1"][layer], w_in_l, 0, tm_in,
                                  cast=((p["w_up"], p["w_down"]), layer))
        w_mlp = (wu[None], wd[None])
        cast_next = ((p["w_in"],), layer + 1, _permute_w_in) if layer + 1 < depth else None
    else:
        zkv, z = _in_proj(x, p["norm1"][layer], w_in_l, 0, tm_in)
        cast_next = None
    out_a, conv_o, h_o = _lru(z, conv_buf, h0, p["conv_w"][layer], p["conv_b"][layer],
                              p["wa"][layer], p["ba"][layer], p["wx"][layer], p["bx"][layer],
                              p["lam"][layer], n_seq, t_len, _pick_tm(t_len, 512))
    if caches is None:
        out_b = _swa_prompt(z, zkv, p["sinks"], layer, n_seq, t_len)
        out_c = _cb_prompt(z, zkv, p["cb_rows"], layer, n_seq, t_len)
        kv3 = zkv.reshape(n_seq, t_len, D_KV)
        n_swa = min(SWA_WINDOW, t_len)
        n_cb = min(CB_REACH, t_len)
        kv = (kv3[:, t_len - n_swa:, KV_KB:KV_KB + SWA_KVW], kv3[:, t_len - n_swa:, KV_VB:KV_VB + SWA_KVW],
              kv3[:, t_len - n_cb:, KV_KC:KV_KC + CB_KVW], kv3[:, t_len - n_cb:, KV_VC:KV_VC + CB_KVW])
    else:
        ck_b, cv_b, ck_c, cv_c = caches
        out_b, kb, vb = _swa_step(z, zkv, ck_b, cv_b, p["sinks"], layer, n_seq)
        out_c, kc, vc = _cb_step(z, zkv, ck_c, cv_c, p["cb_rows"], layer, n_seq)
        kv = (kb, vb, kc, vc)
    x = _merge(x, out_a, out_b, out_c, z, p["w_branch"], p["w_out"], layer, _pick_tm(m, 256))
    res = _mlp(x, p["norm2"][layer], p["final_g"], w_mlp[0], w_mlp[1], 0, _pick_tm(m, 512), 1024,
               final_norm=layer == depth - 1, cast=cast_next)
    w_in_next = res[1][None] if cast_next else None
    return res[0], conv_o, h_o[:, 0], kv, w_mlp, w_in_next


def kernel(x_prompt, x_sample, state_conv, state_lru, cache_swa_k, cache_swa_v, cache_cb_k, cache_cb_v, norm1_g, w_in, conv_w, conv_b, lru_wa, lru_ba, lru_wx, lru_bx, lru_lambda, attn_sinks, rel_bias_table, w_branch, w_out, norm2_g, w_up, w_down, final_g):
    depth = w_in.shape[0]
    nb, s_len, _ = x_prompt.shape
    db, d_len, _ = x_sample.shape
    assert d_len == CHUNK and s_len % (Q_BLOCK * CB_QSUB) == 0 and s_len % SWA_TQ == 0
    assert cache_swa_k.shape[2] == SWA_WINDOW and cache_cb_k.shape[2] == CB_REACH

    row = lambda v: v.reshape(depth, 1, -1)
    p = {
        "norm1": row(norm1_g), "norm2": row(norm2_g), "final_g": final_g.reshape(1, D_MODEL),
        "w_in": w_in, "w_up": w_up, "w_down": w_down,
        "conv_w": conv_w, "conv_b": row(conv_b),
        "wa": lru_wa.astype(BF16), "ba": row(lru_ba), "wx": lru_wx.astype(BF16), "bx": row(lru_bx),
        "lam": row(lru_lambda), "sinks": attn_sinks,
        "cb_rows": _cb_bias_rows(rel_bias_table),
        "w_branch": w_branch.astype(BF16), "w_out": w_out.astype(BF16),
    }
    w_in_l = _permute_w_in(w_in[0])[None]
    caches = (cache_swa_k.reshape(depth, db, SWA_WINDOW * SWA_KV_HEADS, HEAD_DIM),
              cache_swa_v.reshape(depth, db, SWA_WINDOW * SWA_KV_HEADS, HEAD_DIM),
              cache_cb_k.reshape(depth, db, CB_REACH * CB_HEADS, HEAD_DIM),
              cache_cb_v.reshape(depth, db, CB_REACH * CB_HEADS, HEAD_DIM))

    xp = x_prompt.reshape(nb * s_len, D_MODEL)
    xs = x_sample.reshape(db * d_len, D_MODEL)
    zero_conv = jnp.zeros((nb, CONV_W - 1, D_RNN), F32)
    zero_h = jnp.zeros((nb, 1, D_RNN), F32)
    heads = (SWA_KV_HEADS, SWA_KV_HEADS, CB_HEADS, CB_HEADS)
    outs = [[] for _ in range(12)]
    for l in range(depth):
        xp, conv_p, h_p, kv_p, w_mlp, w_in_next = _layer(xp, nb, s_len, zero_conv, zero_h, None, l, p,
                                                         w_in_l, None)
        xs, conv_s, h_s, kv_s, _, _ = _layer(xs, db, d_len, state_conv[l],
                                             state_lru[l].reshape(db, 1, D_RNN), caches, l, p, w_in_l, w_mlp)
        w_in_l = w_in_next
        outs[0].append(conv_p)
        outs[1].append(h_p)
        outs[6].append(conv_s)
        outs[7].append(h_s)
        for n in range(4):
            outs[2 + n].append(kv_p[n].reshape(nb, -1, heads[n], HEAD_DIM))
            outs[8 + n].append(kv_s[n].reshape(db, d_len, heads[n], HEAD_DIM))

    y_prompt = xp.reshape(nb, s_len, D_MODEL)
    y_sample = xs.reshape(db, d_len, D_MODEL)
    return (y_prompt, y_sample) + tuple(jnp.stack(o) for o in outs)
```

```python
import functools

import jax
import jax.numpy as jnp
from jax import lax
from jax.experimental import pallas as pl
from jax.experimental.pallas import tpu as pltpu

F32 = jnp.float32
BF16 = jnp.bfloat16

D_MODEL = 2048
CHUNK = 64
HEAD_DIM = 128
BRANCH_WIDTH = D_MODEL // 2
N_BRANCH = 3
D_RNN = BRANCH_WIDTH
LRU_BLOCKS = 8
LRU_BLOCK = D_RNN // LRU_BLOCKS
CONV_W = 4
LRU_C = 8.0
SWA_HEADS = BRANCH_WIDTH // HEAD_DIM
SWA_KV_HEADS = 2
SWA_GROUP = SWA_HEADS // SWA_KV_HEADS
SWA_WINDOW = 128
SWA_PREV = SWA_WINDOW // CHUNK
CB_HEADS = BRANCH_WIDTH // HEAD_DIM
CB_PREV = 8
CB_REACH = CB_PREV * CHUNK
REL_CLIP = 128
D_FF = 4 * D_MODEL
EPS = 1e-6
NEG = -1e30
ATTN_SCALE = HEAD_DIM ** -0.5
LOG2E = 1.4426950408889634

_O_AX, _O_AG, _O_QB, _O_KB, _O_VB, _O_QC, _O_KC, _O_VC, _O_GATES = (
    0, 1024, 2048, 3072, 3328, 3584, 4608, 5632, 6656)
D_IN = _O_GATES + N_BRANCH * D_MODEL
SWA_KVW = SWA_KV_HEADS * HEAD_DIM
CB_KVW = CB_HEADS * HEAD_DIM
D_KV = 2 * SWA_KVW + 2 * CB_KVW
D_Z = D_IN - D_KV
KV_KC, KV_VC, KV_KB, KV_VB = 0, 1024, 2048, 2304
Z_AX, Z_AG, Z_QB, Z_QC, Z_GATES = 0, 1024, 2048, 3072, 4096

VMEM_LIMIT_BYTES = 56 * 1024 * 1024

Q_BLOCK = 4 * CHUNK
Q_CHUNKS = Q_BLOCK // CHUNK
CB_BAND = (CB_PREV + Q_CHUNKS) * CHUNK
CB_QSUB = 8
CB_RLEN = 1024
assert CB_RLEN >= Q_BLOCK + CB_BAND - 1
CB_VARIANTS = 3


def _params(semantics, vmem_limit_bytes=VMEM_LIMIT_BYTES):
    return pltpu.CompilerParams(dimension_semantics=semantics, vmem_limit_bytes=vmem_limit_bytes)


def _rms(xf, g):
    return xf * lax.rsqrt(jnp.mean(xf * xf, axis=-1, keepdims=True) + EPS) * g


IN_TN = 1280
IN_KV_TILES = D_KV // IN_TN
IN_PROJ_VMEM_BYTES = 59 * 1024 * 1024


class _CastJobs:
    def __init__(self, weights, layer, grid, transform=None):
        self.weights, self.layer, self.transform = weights, layer, transform
        self.inner = grid[1]
        n_max = 1
        while n_max * 2 <= min(grid[0] * grid[1], MAX_CAST_STEPS):
            n_max *= 2
        self.n = []
        for w in weights:
            n = n_max
            while w.shape[1] % (n * BF16_ROWS):
                n //= 2
            self.n.append(n)

    def _spec(self, w, n, lead):
        slab = lambda i, j: jnp.minimum(i * self.inner + j, n - 1)
        if lead:
            return pl.BlockSpec((None, w.shape[1] // n, w.shape[2]), lambda i, j: (self.layer, slab(i, j), 0))
        return pl.BlockSpec((w.shape[1] // n, w.shape[2]), lambda i, j: (slab(i, j), 0))

    def in_specs(self):
        return [self._spec(w, n, True) for w, n in zip(self.weights, self.n)]

    def out_specs(self):
        return [self._spec(w, n, False) for w, n in zip(self.weights, self.n)]

    def out_shape(self):
        return [jax.ShapeDtypeStruct(w.shape[1:], BF16) for w in self.weights]

    def run(self, src_refs, dst_refs):
        step = pl.program_id(0) * self.inner + pl.program_id(1)
        for n in sorted(set(self.n)):
            @pl.when(step < n)
            def _():
                for s, d, n_w in zip(src_refs, dst_refs, self.n):
                    if n_w == n:
                        v = s[...]
                        d[...] = (self.transform(v) if self.transform else v).astype(BF16)


MAX_CAST_STEPS = 128
BF16_ROWS = 16


def _shifted_rows(n_tiles):
    return lambda i, j: (jnp.minimum(i + jnp.minimum(j, 1), n_tiles - 1), 0)


def _norm_first(x_ref, g_ref, xn_ref):
    i, j = pl.program_id(0), pl.program_id(1)

    @pl.when(jnp.logical_and(i == 0, j == 0))
    def _():
        xn_ref[0] = _rms(x_ref[...], g_ref[...]).astype(BF16)

    return lax.rem(i, 2)


def _norm_ahead(x_ref, g_ref, xn_ref, n_cols):
    i, j = pl.program_id(0), pl.program_id(1)
    tm = x_ref.shape[0]
    rows = -(-tm // (n_cols - 1))
    rows = -(-rows // BF16_ROWS) * BF16_ROWS
    r0 = pl.multiple_of(jnp.minimum(jnp.maximum(j - 1, 0) * rows, tm - rows), BF16_ROWS)
    xn_ref[1 - lax.rem(i, 2), pl.ds(r0, rows), :] = _rms(x_ref[pl.ds(r0, rows), :], g_ref[...]).astype(BF16)


def _in_proj_kernel(*refs, jobs):
    n_side = len(jobs.weights) if jobs else 0
    x_ref, g_ref, w_ref = refs[:3]
    side_in = refs[3:3 + n_side]
    zkv_ref, z_ref = refs[3 + n_side:5 + n_side]
    side_out = refs[5 + n_side:5 + 2 * n_side]
    xn_ref = refs[-1]
    j = pl.program_id(1)
    cur = _norm_first(x_ref, g_ref, xn_ref)

    @pl.when(j < IN_KV_TILES)
    def _():
        zkv_ref[...] = jnp.dot(xn_ref[cur], w_ref[...], preferred_element_type=F32)
        _norm_ahead(x_ref, g_ref, xn_ref, D_IN // IN_TN)

    @pl.when(j >= IN_KV_TILES)
    def _():
        z_ref[...] = jnp.dot(xn_ref[cur], w_ref[...], preferred_element_type=F32).astype(BF16)
        _norm_ahead(x_ref, g_ref, xn_ref, D_IN // IN_TN)

    if jobs:
        jobs.run(side_in, side_out)


def _in_proj(x, g, w, layer, tm, cast=None):
    m = x.shape[0]
    grid = (m // tm, D_IN // IN_TN)
    assert grid[0] == 1 or tm % BF16_ROWS == 0
    jobs = _CastJobs(cast[0], cast[1], grid) if cast else None
    return pl.pallas_call(
        functools.partial(_in_proj_kernel, jobs=jobs),
        grid=grid,
        in_specs=[
            pl.BlockSpec((tm, D_MODEL), _shifted_rows(grid[0])),
            pl.BlockSpec((None, 1, D_MODEL), lambda i, j: (layer, 0, 0)),
            pl.BlockSpec((None, D_MODEL, IN_TN), lambda i, j: (0, 0, j)),
        ] + (jobs.in_specs() if jobs else []),
        out_specs=[
            pl.BlockSpec((tm, IN_TN), lambda i, j: (i, jnp.minimum(j, IN_KV_TILES - 1))),
            pl.BlockSpec((tm, IN_TN), lambda i, j: (i, jnp.maximum(j - IN_KV_TILES, 0))),
        ] + (jobs.out_specs() if jobs else []),
        out_shape=[jax.ShapeDtypeStruct((m, D_KV), F32), jax.ShapeDtypeStruct((m, D_Z), BF16)]
        + (jobs.out_shape() if jobs else []),
        scratch_shapes=[pltpu.VMEM((2, tm, D_MODEL), BF16)],
        compiler_params=_params(("arbitrary", "arbitrary"), IN_PROJ_VMEM_BYTES),
        name="in_proj",
    )(x, g, w, *(cast[0] if cast else ()))


_XPAD = 8
LANES = 128
_SEGS = 8
_SEG_LEN = 4


def _sigmoid(x):
    return 0.5 * (jnp.tanh(0.5 * x) + 1.0)


def _lru_kernel(ax_ref, ag_ref, cbuf_ref, h0_ref, cw_ref, cb_ref, wa_ref, ba_ref, wx_ref, bx_ref,
                lam_ref, out_ref, convo_ref, ho_ref, xbuf, a_s, b_s, h_s):
    t = pl.program_id(1)
    nt = pl.num_programs(1)
    tt = ax_ref.shape[0]

    @pl.when(t == 0)
    def _():
        xbuf[...] = jnp.zeros_like(xbuf)
        xbuf[_XPAD - (CONV_W - 1):, :] = cbuf_ref[...]
        h_s[...] = h0_ref[...]

    x = ax_ref[...].astype(F32)
    xe = jnp.concatenate([xbuf[...], x], axis=0)
    acc = xe * cw_ref[0:1, :]
    for k in range(1, CONV_W):
        acc = xe * cw_ref[k:k + 1, :] + pltpu.roll(acc, 1, 0)
    u = cb_ref[...] + acc[_XPAD:, :]
    tail = x[tt - (CONV_W - 1):, :]
    xbuf[...] = x[tt - _XPAD:, :]

    ub = u.astype(BF16)
    r_parts, i_parts = [], []
    for n in range(LRU_BLOCKS):
        un = ub[:, n * LRU_BLOCK:(n + 1) * LRU_BLOCK]
        r_parts.append(jnp.dot(un, wa_ref[n], preferred_element_type=F32))
        i_parts.append(jnp.dot(un, wx_ref[n], preferred_element_type=F32))
    r = _sigmoid(jnp.concatenate(r_parts, axis=1) + ba_ref[...])
    i = _sigmoid(jnp.concatenate(i_parts, axis=1) + bx_ref[...])
    log_a = -LRU_C * r * jax.nn.softplus(-lam_ref[...])
    a = jnp.exp(log_a)
    y = 1.0 - a * a
    b = jnp.where(y > 0.0, y * lax.rsqrt(y), 0.0) * (i * u)
    n_lg = D_RNN // LANES
    for lg in range(n_lg):
        a_s[lg] = a[:, lg * LANES:(lg + 1) * LANES]
        b_s[lg] = b[:, lg * LANES:(lg + 1) * LANES]

    row = lax.broadcasted_iota(jnp.int32, (_SEGS, LANES), 0)
    sub = _SEGS * _SEG_LEN

    def body(sb, h):
        r0 = pl.multiple_of(sb * sub, sub)
        step = lambda j: pl.ds(r0 + j, _SEGS, stride=_SEG_LEN)
        h_next = []
        for lg in range(n_lg):
            a_g, b_g = a_s.at[lg], b_s.at[lg]
            h_g = h[:, lg * LANES:(lg + 1) * LANES]
            acs, bcs = [a_g[step(0), :]], [b_g[step(0), :]]
            for j in range(1, _SEG_LEN):
                aj = a_g[step(j), :]
                bcs.append(aj * bcs[-1] + b_g[step(j), :])
                acs.append(aj * acs[-1])
            at, bt = acs[-1], bcs[-1]
            for s in (1, 2, 4):
                m = row >= s
                bt_new = jnp.where(m, at * pltpu.roll(bt, s, 0) + bt, bt)
                at = jnp.where(m, at * pltpu.roll(at, s, 0), at)
                bt = bt_new
            after = at * h_g + bt
            entry = jnp.where(row >= 1, pltpu.roll(after, 1, 0), h_g)
            for j in range(_SEG_LEN):
                b_g[step(j), :] = acs[j] * entry + bcs[j]
            h_next.append(after[_SEGS - 1:_SEGS, :])
        return jnp.concatenate(h_next, axis=1)

    h = lax.fori_loop(0, tt // sub, body, h_s[...])
    h_s[...] = h
    hs = jnp.concatenate([b_s[lg] for lg in range(n_lg)], axis=1)
    out_ref[...] = (hs * jax.nn.gelu(ag_ref[...].astype(F32))).astype(BF16)

    @pl.when(t == nt - 1)
    def _():
        convo_ref[...] = tail
        ho_ref[...] = h


def _lru(z, conv_buf, h0, state_layer, cw, cb, wa, ba, wx, bx, lam, layer, n_seq, t_len, tt):
    nt = t_len // tt
    row = lambda b, t: b * nt + t
    vec = lambda: pl.BlockSpec((None, 1, D_RNN), lambda b, t: (layer, 0, 0))
    blk = lambda: pl.BlockSpec((None, LRU_BLOCKS, LRU_BLOCK, LRU_BLOCK), lambda b, t: (layer, 0, 0, 0))
    return pl.pallas_call(
        _lru_kernel,
        grid=(n_seq, nt),
        in_specs=[
            pl.BlockSpec((tt, D_RNN), lambda b, t: (row(b, t), Z_AX // D_RNN)),
            pl.BlockSpec((tt, D_RNN), lambda b, t: (row(b, t), Z_AG // D_RNN)),
            pl.BlockSpec((None, None, CONV_W - 1, D_RNN), lambda b, t: (state_layer, b, 0, 0)),
            pl.BlockSpec((None, None, 1, D_RNN), lambda b, t: (state_layer, b, 0, 0)),
            pl.BlockSpec((None, CONV_W, D_RNN), lambda b, t: (layer, 0, 0)),
            vec(), blk(), vec(), blk(), vec(), vec(),
        ],
        out_specs=[
            pl.BlockSpec((tt, D_RNN), lambda b, t: (row(b, t), 0)),
            pl.BlockSpec((None, CONV_W - 1, D_RNN), lambda b, t: (b, 0, 0)),
            pl.BlockSpec((None, 1, D_RNN), lambda b, t: (b, 0, 0)),
        ],
        out_shape=[
            jax.ShapeDtypeStruct((n_seq * t_len, D_RNN), BF16),
            jax.ShapeDtypeStruct((n_seq, CONV_W - 1, D_RNN), F32),
            jax.ShapeDtypeStruct((n_seq, 1, D_RNN), F32),
        ],
        scratch_shapes=[
            pltpu.VMEM((_XPAD, D_RNN), F32),
            pltpu.VMEM((D_RNN // LANES, tt, LANES), F32),
            pltpu.VMEM((D_RNN // LANES, tt, LANES), F32),
            pltpu.VMEM((1, D_RNN), F32),
        ],
        compiler_params=_params(("parallel", "arbitrary")),
        name="lru",
    )(z, z, conv_buf, h0, cw, cb, wa, ba, wx, bx, lam)


def _dot_nt(a, b):
    return lax.dot_general(a, b, (((1,), (1,)), ((), ())), preferred_element_type=F32)


def _with_ones(v):
    return jnp.concatenate([v, jnp.ones_like(v)], axis=1)


def _swa_chunk(q, kband, vext, sinks, valid):
    qst = jnp.concatenate([q[:, g * HEAD_DIM:(g + 1) * HEAD_DIM] for g in range(SWA_GROUP)], axis=0)
    s = _dot_nt(qst, kband) * (ATTN_SCALE * LOG2E)
    if valid is not None:
        s = jnp.where(valid, s, NEG)
    es, sink_e = [], []
    for g in range(SWA_GROUP):
        sg = s[g * CHUNK:(g + 1) * CHUNK, :]
        sink2 = sinks[g] * LOG2E
        m = jnp.maximum(jnp.max(sg, axis=-1, keepdims=True), sink2)
        es.append(jnp.exp2(sg - m).astype(BF16))
        sink_e.append(jnp.exp2(sink2 - m))
    r = jnp.dot(jnp.concatenate(es, axis=0), vext, preferred_element_type=F32)
    outs = []
    for g in range(SWA_GROUP):
        rg = r[g * CHUNK:(g + 1) * CHUNK, :]
        outs.append(rg[:, :HEAD_DIM] * (1.0 / (rg[:, HEAD_DIM:] + sink_e[g])))
    return jnp.concatenate(outs, axis=1)


def _cb_attend(q, kband, vext, bias2):
    s = _dot_nt(q, kband) * (ATTN_SCALE * LOG2E) + bias2
    m = jnp.max(s, axis=-1, keepdims=True)
    r = jnp.dot(jnp.exp2(s - m).astype(BF16), vext, preferred_element_type=F32)
    return r[:, :HEAD_DIM] * (1.0 / r[:, HEAD_DIM:])


def _cb_bias_block(r, variant, n_rows, n_cols):
    t = pltpu.roll(jnp.broadcast_to(r, (n_rows, CB_RLEN)), 0, 1, stride=1, stride_axis=0)[:, :n_cols]
    c0 = variant * Q_CHUNKS
    sc = max(c0 - CB_PREV, 0)
    qc = c0 + lax.broadcasted_iota(jnp.int32, (n_rows, n_cols), 0) // CHUNK
    kc = sc + lax.broadcasted_iota(jnp.int32, (n_rows, n_cols), 1) // CHUNK
    return jnp.where(kc <= qc, jnp.where(kc >= qc - CB_PREV, t * LOG2E, NEG), NEG)


def _cast_rows(dst, src, n_rows, step, ones=False):
    def body(i, c):
        r0 = pl.multiple_of(i * step, step)
        v = src[pl.ds(r0, step), :].astype(BF16)
        dst[pl.ds(r0, step), :] = _with_ones(v) if ones else v
        return c
    lax.fori_loop(0, n_rows // step, body, 0)


SWA_BAND = (SWA_PREV + 1) * CHUNK
SWA_TQ = 16 * CHUNK


def _swa_prompt_kernel(sink_ref, q_ref, k_ref, v_ref, o_ref, kb_s, vb_s, *, layer):
    kh = pl.program_id(1)
    qi = pl.program_id(2)
    s_len = k_ref.shape[0]

    @pl.when(qi == 0)
    def _():
        _cast_rows(kb_s, k_ref, s_len, 512)
        _cast_rows(vb_s, v_ref, s_len, 512, ones=True)

    sinks = [sink_ref[layer, kh * SWA_GROUP + g] for g in range(SWA_GROUP)]
    jchunk = lax.broadcasted_iota(jnp.int32, (1, SWA_BAND), 1) // CHUNK
    for c in range(SWA_TQ // CHUNK):
        cg = qi * (SWA_TQ // CHUNK) + c
        sc = jnp.maximum(cg - SWA_PREV, 0)
        s0 = pl.multiple_of(sc * CHUNK, CHUNK)
        valid = (jchunk + sc) <= cg
        o = _swa_chunk(q_ref[c * CHUNK:(c + 1) * CHUNK, :], kb_s[pl.ds(s0, SWA_BAND), :],
                       vb_s[pl.ds(s0, SWA_BAND), :], sinks, valid)
        o_ref[c * CHUNK:(c + 1) * CHUNK, :] = o.astype(BF16)


def _swa_prompt(z, zkv, sinks, layer, n_seq, s_len):
    nq = s_len // SWA_TQ
    gw = SWA_GROUP * HEAD_DIM
    return pl.pallas_call(
        functools.partial(_swa_prompt_kernel, layer=layer),
        grid=(n_seq, SWA_KV_HEADS, nq),
        in_specs=[
            pl.BlockSpec(memory_space=pltpu.SMEM),
            pl.BlockSpec((SWA_TQ, gw), lambda b, k, q: (b * nq + q, Z_QB // gw + k)),
            pl.BlockSpec((s_len, HEAD_DIM), lambda b, k, q: (b, KV_KB // HEAD_DIM + k)),
            pl.BlockSpec((s_len, HEAD_DIM), lambda b, k, q: (b, KV_VB // HEAD_DIM + k)),
        ],
        out_specs=pl.BlockSpec((SWA_TQ, gw), lambda b, k, q: (b * nq + q, k)),
        out_shape=jax.ShapeDtypeStruct((n_seq * s_len, BRANCH_WIDTH), BF16),
        scratch_shapes=[pltpu.VMEM((s_len, HEAD_DIM), BF16), pltpu.VMEM((s_len, 2 * HEAD_DIM), BF16)],
        compiler_params=_params(("parallel", "parallel", "arbitrary")),
        name="swa_prompt",
    )(sinks, z, zkv, zkv)


def _swa_step_kernel(sink_ref, q_ref, k_ref, v_ref, ck_ref, cv_ref, o_ref, ko_ref, vo_ref, *, layer):
    n_past = ck_ref.shape[0] // SWA_KV_HEADS
    gw = SWA_GROUP * HEAD_DIM
    for kh in range(SWA_KV_HEADS):
        cs = slice(kh * HEAD_DIM, (kh + 1) * HEAD_DIM)
        kn = k_ref[:, cs]
        vn = v_ref[:, cs]
        kfull = jnp.concatenate(
            [ck_ref[pl.ds(kh, n_past, stride=SWA_KV_HEADS), :].astype(BF16), kn.astype(BF16)], axis=0)
        vfull = jnp.concatenate(
            [cv_ref[pl.ds(kh, n_past, stride=SWA_KV_HEADS), :].astype(BF16), vn.astype(BF16)], axis=0)
        sinks = [sink_ref[layer, kh * SWA_GROUP + g] for g in range(SWA_GROUP)]
        o = _swa_chunk(q_ref[:, kh * gw:(kh + 1) * gw], kfull, _with_ones(vfull), sinks, None)
        o_ref[:, kh * gw:(kh + 1) * gw] = o.astype(BF16)
        ko_ref[pl.ds(kh, CHUNK, stride=SWA_KV_HEADS), :] = kn
        vo_ref[pl.ds(kh, CHUNK, stride=SWA_KV_HEADS), :] = vn


def _swa_step(z, zkv, cache_k, cache_v, sinks, layer, n_seq):
    rows_past = cache_k.shape[2]
    rows_new = CHUNK * SWA_KV_HEADS
    cache = lambda: pl.BlockSpec((None, None, rows_past, HEAD_DIM), lambda b: (layer, b, 0, 0))
    new = lambda: pl.BlockSpec((rows_new, HEAD_DIM), lambda b: (b, 0))
    return pl.pallas_call(
        functools.partial(_swa_step_kernel, layer=layer),
        grid=(n_seq,),
        in_specs=[
            pl.BlockSpec(memory_space=pltpu.SMEM),
            pl.BlockSpec((CHUNK, BRANCH_WIDTH), lambda b: (b, Z_QB // BRANCH_WIDTH)),
            pl.BlockSpec((CHUNK, SWA_KVW), lambda b: (b, KV_KB // SWA_KVW)),
            pl.BlockSpec((CHUNK, SWA_KVW), lambda b: (b, KV_VB // SWA_KVW)),
            cache(), cache(),
        ],
        out_specs=[pl.BlockSpec((CHUNK, BRANCH_WIDTH), lambda b: (b, 0)), new(), new()],
        out_shape=[jax.ShapeDtypeStruct((n_seq * CHUNK, BRANCH_WIDTH), BF16),
                   jax.ShapeDtypeStruct((n_seq * rows_new, HEAD_DIM), F32),
                   jax.ShapeDtypeStruct((n_seq * rows_new, HEAD_DIM), F32)],
        compiler_params=_params(("parallel",)),
        name="swa_step",
    )(sinks, z, zkv, zkv, cache_k, cache_v)


def _cb_prompt_kernel(q_ref, k_ref, v_ref, r_ref, o_ref, kb_s, vb_s, bias_s):
    qi = pl.program_id(2)
    s_len = k_ref.shape[0]

    @pl.when(qi == 0)
    def _():
        _cast_rows(kb_s, k_ref, s_len, 512)
        _cast_rows(vb_s, v_ref, s_len, 512, ones=True)
        for v in range(CB_VARIANTS):
            bias_s[v] = _cb_bias_block(r_ref[v], v, Q_BLOCK, CB_BAND)

    for sub in range(CB_QSUB):
        blk = qi * CB_QSUB + sub
        sc = jnp.maximum(blk * Q_CHUNKS - CB_PREV, 0)
        s0 = pl.multiple_of(sc * CHUNK, CHUNK)
        rows = slice(sub * Q_BLOCK, (sub + 1) * Q_BLOCK)
        o = _cb_attend(q_ref[rows, :], kb_s[pl.ds(s0, CB_BAND), :], vb_s[pl.ds(s0, CB_BAND), :],
                       bias_s[jnp.minimum(blk, CB_VARIANTS - 1)])
        o_ref[rows, :] = o.astype(BF16)


def _cb_prompt(z, zkv, rows, layer, n_seq, s_len):
    tq = Q_BLOCK * CB_QSUB
    nq = s_len // tq
    return pl.pallas_call(
        _cb_prompt_kernel,
        grid=(n_seq, CB_HEADS, nq),
        in_specs=[
            pl.BlockSpec((tq, HEAD_DIM), lambda b, h, q: (b * nq + q, Z_QC // HEAD_DIM + h)),
            pl.BlockSpec((s_len, HEAD_DIM), lambda b, h, q: (b, KV_KC // HEAD_DIM + h)),
            pl.BlockSpec((s_len, HEAD_DIM), lambda b, h, q: (b, KV_VC // HEAD_DIM + h)),
            pl.BlockSpec((None, CB_VARIANTS, None, 1, CB_RLEN), lambda b, h, q: (layer, 0, h, 0, 0)),
        ],
        out_specs=pl.BlockSpec((tq, HEAD_DIM), lambda b, h, q: (b * nq + q, h)),
        out_shape=jax.ShapeDtypeStruct((n_seq * s_len, BRANCH_WIDTH), BF16),
        scratch_shapes=[pltpu.VMEM((s_len, HEAD_DIM), BF16), pltpu.VMEM((s_len, 2 * HEAD_DIM), BF16),
                        pltpu.VMEM((CB_VARIANTS, Q_BLOCK, CB_BAND), F32)],
        compiler_params=_params(("parallel", "parallel", "arbitrary")),
        name="cb_prompt",
    )(z, zkv, zkv, rows)


def _cb_step_kernel(q_ref, k_ref, v_ref, ck_ref, cv_ref, r_ref, o_ref, ko_ref, vo_ref, bias_s):
    n_past = ck_ref.shape[0] // CB_HEADS
    n_k = n_past + CHUNK

    @pl.when(pl.program_id(0) == 0)
    def _():
        for h in range(CB_HEADS):
            bias_s[h] = _cb_bias_block(r_ref[h], CB_VARIANTS - 1, CHUNK, n_k)

    for h in range(CB_HEADS):
        cs = slice(h * HEAD_DIM, (h + 1) * HEAD_DIM)
        kn = k_ref[:, cs]
        vn = v_ref[:, cs]
        kfull = jnp.concatenate(
            [ck_ref[pl.ds(h, n_past, stride=CB_HEADS), :].astype(BF16), kn.astype(BF16)], axis=0)
        vfull = jnp.concatenate(
            [cv_ref[pl.ds(h, n_past, stride=CB_HEADS), :].astype(BF16), vn.astype(BF16)], axis=0)
        o_ref[:, cs] = _cb_attend(q_ref[:, cs], kfull, _with_ones(vfull), bias_s[h]).astype(BF16)
        ko_ref[pl.ds(h, CHUNK, stride=CB_HEADS), :] = kn
        vo_ref[pl.ds(h, CHUNK, stride=CB_HEADS), :] = vn


def _cb_step(z, zkv, cache_k, cache_v, rows, layer, n_seq):
    rows_past = cache_k.shape[2]
    n_k = rows_past // CB_HEADS + CHUNK
    rows_new = CHUNK * CB_HEADS
    cache = lambda: pl.BlockSpec((None, None, rows_past, HEAD_DIM), lambda b: (layer, b, 0, 0))
    new = lambda: pl.BlockSpec((rows_new, HEAD_DIM), lambda b: (b, 0))
    return pl.pallas_call(
        _cb_step_kernel,
        grid=(n_seq,),
        in_specs=[
            pl.BlockSpec((CHUNK, BRANCH_WIDTH), lambda b: (b, Z_QC // BRANCH_WIDTH)),
            pl.BlockSpec((CHUNK, CB_KVW), lambda b: (b, KV_KC // CB_KVW)),
            pl.BlockSpec((CHUNK, CB_KVW), lambda b: (b, KV_VC // CB_KVW)),
            cache(), cache(),
            pl.BlockSpec((None, None, CB_HEADS, 1, CB_RLEN),
                         lambda b: (layer, CB_VARIANTS - 1, 0, 0, 0)),
        ],
        out_specs=[pl.BlockSpec((CHUNK, BRANCH_WIDTH), lambda b: (b, 0)), new(), new()],
        out_shape=[jax.ShapeDtypeStruct((n_seq * CHUNK, BRANCH_WIDTH), BF16),
                   jax.ShapeDtypeStruct((n_seq * rows_new, HEAD_DIM), F32),
                   jax.ShapeDtypeStruct((n_seq * rows_new, HEAD_DIM), F32)],
        scratch_shapes=[pltpu.VMEM((CB_HEADS, CHUNK, n_k), F32)],
        compiler_params=_params(("arbitrary",)),
        name="cb_step",
    )(z, zkv, zkv, cache_k, cache_v, rows)


def _merge_kernel(x_ref, a_ref, b_ref, c_ref, ga_ref, gb_ref, gc_ref, wb_ref, wo_ref, o_ref):
    mixed = None
    for r, (br, gr) in enumerate(((a_ref, ga_ref), (b_ref, gb_ref), (c_ref, gc_ref))):
        proj = jnp.dot(br[...], wb_ref[r], preferred_element_type=F32)
        gate = jax.nn.sigmoid(gr[...].astype(F32))
        mixed = gate * proj if mixed is None else mixed + gate * proj
    o_ref[...] = x_ref[...] + jnp.dot(mixed.astype(BF16), wo_ref[...], preferred_element_type=F32)


def _merge(x, out_a, out_b, out_c, z, w_branch, w_out, layer, tm):
    m = x.shape[0]
    branch = lambda: pl.BlockSpec((tm, BRANCH_WIDTH), lambda i: (i, 0))
    gate = lambda r: pl.BlockSpec((tm, D_MODEL), lambda i: (i, Z_GATES // D_MODEL + r))
    return pl.pallas_call(
        _merge_kernel,
        grid=(m // tm,),
        in_specs=[
            pl.BlockSpec((tm, D_MODEL), lambda i: (i, 0)),
            branch(), branch(), branch(),
            gate(0), gate(1), gate(2),
            pl.BlockSpec((None, N_BRANCH, BRANCH_WIDTH, D_MODEL), lambda i: (layer, 0, 0, 0),
                         pipeline_mode=pl.Buffered(1)),
            pl.BlockSpec((None, D_MODEL, D_MODEL), lambda i: (layer, 0, 0),
                         pipeline_mode=pl.Buffered(1)),
        ],
        out_specs=pl.BlockSpec((tm, D_MODEL), lambda i: (i, 0)),
        out_shape=jax.ShapeDtypeStruct((m, D_MODEL), F32),
        compiler_params=_params(("parallel",)),
        name="merge",
    )(x, out_a, out_b, out_c, z, z, z, w_branch, w_out)


def _mlp_kernel(*refs, final_norm, jobs):
    n_side = len(jobs.weights) if jobs else 0
    x_ref, g_ref, fg_ref, wu_ref, wd_ref = refs[:5]
    side_in = refs[5:5 + n_side]
    o_ref = refs[5 + n_side]
    side_out = refs[6 + n_side:6 + 2 * n_side]
    hn_ref = refs[-1]
    j = pl.program_id(1)

    @pl.when(j == 0)
    def _():
        xf = x_ref[...]
        hn_ref[...] = _rms(xf, g_ref[...]).astype(BF16)
        o_ref[...] = xf

    h = jnp.dot(hn_ref[...], wu_ref[...], preferred_element_type=F32)
    h = jnp.square(jnp.maximum(h, 0.0)).astype(BF16)
    o_ref[...] += jnp.dot(h, wd_ref[...], preferred_element_type=F32)

    if final_norm:
        @pl.when(j == pl.num_programs(1) - 1)
        def _():
            o_ref[...] = _rms(o_ref[...], fg_ref[...])

    if jobs:
        jobs.run(side_in, side_out)


def _mlp(x, g, final_g, w_up, w_down, layer, tm, tf, final_norm, cast=None):
    m = x.shape[0]
    grid = (m // tm, D_FF // tf)
    jobs = _CastJobs(cast[0], cast[1], grid, cast[2]) if cast else None
    return pl.pallas_call(
        functools.partial(_mlp_kernel, final_norm=final_norm, jobs=jobs),
        grid=grid,
        in_specs=[
            pl.BlockSpec((tm, D_MODEL), lambda i, j: (i, 0)),
            pl.BlockSpec((None, 1, D_MODEL), lambda i, j: (layer, 0, 0)),
            pl.BlockSpec((1, D_MODEL), lambda i, j: (0, 0)),
            pl.BlockSpec((None, D_MODEL, tf), lambda i, j: (0, 0, j)),
            pl.BlockSpec((None, tf, D_MODEL), lambda i, j: (0, j, 0)),
        ] + (jobs.in_specs() if jobs else []),
        out_specs=[pl.BlockSpec((tm, D_MODEL), lambda i, j: (i, 0))] + (jobs.out_specs() if jobs else []),
        out_shape=[jax.ShapeDtypeStruct((m, D_MODEL), F32)] + (jobs.out_shape() if jobs else []),
        scratch_shapes=[pltpu.VMEM((tm, D_MODEL), BF16)],
        compiler_params=_params(("arbitrary", "arbitrary")),
        name="mlp",
    )(x, g, final_g, w_up, w_down, *(cast[0] if cast else ()))


def _permute_w_in(w_in):
    seg = lambda o, w: w_in[..., o:o + w]
    return jnp.concatenate([
        seg(_O_KC, CB_KVW), seg(_O_VC, CB_KVW), seg(_O_KB, SWA_KVW), seg(_O_VB, SWA_KVW),
        seg(_O_AX, D_RNN), seg(_O_AG, D_RNN), seg(_O_QB, BRANCH_WIDTH), seg(_O_QC, BRANCH_WIDTH),
        seg(_O_GATES, N_BRANCH * D_MODEL)], axis=-1).astype(BF16)


def _cb_bias_rows(table):
    m = jnp.arange(CB_RLEN)
    rel = jnp.where(m < CB_BAND, m, m - CB_RLEN)
    out = []
    for v in range(CB_VARIANTS):
        c0 = v * Q_CHUNKS
        sc = max(c0 - CB_PREV, 0)
        d = (c0 - sc) * CHUNK - rel
        idx = jnp.clip(d, -REL_CLIP, REL_CLIP) + REL_CLIP
        out.append(table.astype(F32)[:, :, idx])
    return jnp.stack(out, axis=1)[:, :, :, None, :]


def _pick_tm(m, want):
    tm = min(want, m)
    while m % tm:
        tm //= 2
    return tm


def _layer(x, n_seq, t_len, conv_buf, h0, state_layer, caches, layer, p, w_in_l, w_layer):
    m = x.shape[0]
    depth = p["w_up"].shape[0]
    tm_in = _pick_tm(m, 1024)
    if w_layer is None:
        zkv, z, wu, wd, wb, wo = _in_proj(
            x, p["norm1"], w_in_l, layer, tm_in,
            cast=((p["w_up"], p["w_down"], p["w_branch"], p["w_out"]), layer))
        w_layer = (wu[None], wd[None], wb.reshape(1, N_BRANCH, BRANCH_WIDTH, D_MODEL), wo[None])
        cast_next = ((p["w_in"],), layer + 1, _permute_w_in) if layer + 1 < depth else None
    else:
        zkv, z = _in_proj(x, p["norm1"], w_in_l, layer, tm_in)
        cast_next = None
    w_up_l, w_down_l, w_branch_l, w_out_l = w_layer
    out_a, conv_o, h_o = _lru(z, conv_buf, h0, state_layer, p["conv_w"], p["conv_b"], p["wa"], p["ba"],
                              p["wx"], p["bx"], p["lam"], layer, n_seq, t_len, _pick_tm(t_len, 512))
    if caches is None:
        out_b = _swa_prompt(z, zkv, p["sinks"], layer, n_seq, t_len)
        out_c = _cb_prompt(z, zkv, p["cb_rows"], layer, n_seq, t_len)
        kv3 = zkv.reshape(n_seq, t_len, D_KV)
        n_swa = min(SWA_WINDOW, t_len)
        n_cb = min(CB_REACH, t_len)
        kv = (kv3[:, t_len - n_swa:, KV_KB:KV_KB + SWA_KVW], kv3[:, t_len - n_swa:, KV_VB:KV_VB + SWA_KVW],
              kv3[:, t_len - n_cb:, KV_KC:KV_KC + CB_KVW], kv3[:, t_len - n_cb:, KV_VC:KV_VC + CB_KVW])
    else:
        ck_b, cv_b, ck_c, cv_c = caches
        out_b, kb, vb = _swa_step(z, zkv, ck_b, cv_b, p["sinks"], layer, n_seq)
        out_c, kc, vc = _cb_step(z, zkv, ck_c, cv_c, p["cb_rows"], layer, n_seq)
        kv = (kb, vb, kc, vc)
    x = _merge(x, out_a, out_b, out_c, z, w_branch_l, w_out_l, 0, _pick_tm(m, 256))
    res = _mlp(x, p["norm2"], p["final_g"], w_up_l, w_down_l, layer, _pick_tm(m, 512), 1024,
               final_norm=layer == depth - 1, cast=cast_next)
    w_in_next = res[1][None] if cast_next else None
    return res[0], conv_o, h_o[:, 0], kv, w_layer, w_in_next


def kernel(x_prompt, x_sample, state_conv, state_lru, cache_swa_k, cache_swa_v, cache_cb_k, cache_cb_v, norm1_g, w_in, conv_w, conv_b, lru_wa, lru_ba, lru_wx, lru_bx, lru_lambda, attn_sinks, rel_bias_table, w_branch, w_out, norm2_g, w_up, w_down, final_g):
    depth = w_in.shape[0]
    nb, s_len, _ = x_prompt.shape
    db, d_len, _ = x_sample.shape
    assert d_len == CHUNK and s_len % (Q_BLOCK * CB_QSUB) == 0 and s_len % SWA_TQ == 0
    assert cache_swa_k.shape[2] == SWA_WINDOW and cache_cb_k.shape[2] == CB_REACH

    row = lambda v: v.reshape(depth, 1, -1)
    p = {
        "norm1": row(norm1_g), "norm2": row(norm2_g), "final_g": final_g.reshape(1, D_MODEL),
        "w_in": w_in, "w_up": w_up, "w_down": w_down,
        "w_branch": w_branch.reshape(depth, N_BRANCH * BRANCH_WIDTH, D_MODEL), "w_out": w_out,
        "conv_w": conv_w, "conv_b": row(conv_b),
        "wa": lru_wa.astype(BF16), "ba": row(lru_ba), "wx": lru_wx.astype(BF16), "bx": row(lru_bx),
        "lam": row(lru_lambda), "sinks": attn_sinks,
        "cb_rows": _cb_bias_rows(rel_bias_table),
    }
    w_in_l = _permute_w_in(w_in[0])[None]
    caches = (cache_swa_k.reshape(depth, db, SWA_WINDOW * SWA_KV_HEADS, HEAD_DIM),
              cache_swa_v.reshape(depth, db, SWA_WINDOW * SWA_KV_HEADS, HEAD_DIM),
              cache_cb_k.reshape(depth, db, CB_REACH * CB_HEADS, HEAD_DIM),
              cache_cb_v.reshape(depth, db, CB_REACH * CB_HEADS, HEAD_DIM))

    xp = x_prompt.reshape(nb * s_len, D_MODEL)
    xs = x_sample.reshape(db * d_len, D_MODEL)
    zero_conv = jnp.zeros((1, nb, CONV_W - 1, D_RNN), F32)
    zero_h = jnp.zeros((1, nb, 1, D_RNN), F32)
    h0_s = state_lru.reshape(depth, db, 1, D_RNN)
    heads = (SWA_KV_HEADS, SWA_KV_HEADS, CB_HEADS, CB_HEADS)
    outs = [[] for _ in range(12)]
    for l in range(depth):
        xp, conv_p, h_p, kv_p, w_layer, w_in_next = _layer(xp, nb, s_len, zero_conv, zero_h, 0, None, l, p,
                                                           w_in_l, None)
        xs, conv_s, h_s, kv_s, _, _ = _layer(xs, db, d_len, state_conv, h0_s, l, caches, l, p,
                                             w_in_l, w_layer)
        w_in_l = w_in_next
        outs[0].append(conv_p)
        outs[1].append(h_p)
        outs[6].append(conv_s)
        outs[7].append(h_s)
        for n in range(4):
            outs[2 + n].append(kv_p[n].reshape(nb, -1, heads[n], HEAD_DIM))
            outs[8 + n].append(kv_s[n].reshape(db, d_len, heads[n], HEAD_DIM))

    y_prompt = xp.reshape(nb, s_len, D_MODEL)
    y_sample = xs.reshape(db, d_len, D_MODEL)
    return (y_prompt, y_sample) + tuple(jnp.stack(o) for o in outs)
```

```python
import functools

import jax
import jax.numpy as jnp
from jax import lax
from jax.experimental import pallas as pl
from jax.experimental.pallas import tpu as pltpu

F32 = jnp.float32
BF16 = jnp.bfloat16

D_MODEL = 2048
CHUNK = 64
HEAD_DIM = 128
BRANCH_WIDTH = D_MODEL // 2
N_BRANCH = 3
D_RNN = BRANCH_WIDTH
LRU_BLOCKS = 8
LRU_BLOCK = D_RNN // LRU_BLOCKS
CONV_W = 4
LRU_C = 8.0
SWA_HEADS = BRANCH_WIDTH // HEAD_DIM
SWA_KV_HEADS = 2
SWA_GROUP = SWA_HEADS // SWA_KV_HEADS
SWA_WINDOW = 128
SWA_PREV = SWA_WINDOW // CHUNK
CB_HEADS = BRANCH_WIDTH // HEAD_DIM
CB_PREV = 8
CB_REACH = CB_PREV * CHUNK
REL_CLIP = 128
D_FF = 4 * D_MODEL
EPS = 1e-6
NEG = -1e30
ATTN_SCALE = HEAD_DIM ** -0.5
LOG2E = 1.4426950408889634

_O_AX, _O_AG, _O_QB, _O_KB, _O_VB, _O_QC, _O_KC, _O_VC, _O_GATES = (
    0, 1024, 2048, 3072, 3328, 3584, 4608, 5632, 6656)
D_IN = _O_GATES + N_BRANCH * D_MODEL
SWA_KVW = SWA_KV_HEADS * HEAD_DIM
CB_KVW = CB_HEADS * HEAD_DIM
D_KV = 2 * SWA_KVW + 2 * CB_KVW
D_Z = D_IN - D_KV
KV_KC, KV_VC, KV_KB, KV_VB = 0, 1024, 2048, 2304
Z_AX, Z_AG, Z_QB, Z_QC, Z_GATES = 0, 1024, 2048, 3072, 4096

VMEM_LIMIT_BYTES = 56 * 1024 * 1024

Q_BLOCK = 4 * CHUNK
Q_CHUNKS = Q_BLOCK // CHUNK
CB_BAND = (CB_PREV + Q_CHUNKS) * CHUNK
CB_QSUB = 8
CB_RLEN = 1024
assert CB_RLEN >= Q_BLOCK + CB_BAND - 1
CB_VARIANTS = 3


def _params(semantics):
    return pltpu.CompilerParams(dimension_semantics=semantics, vmem_limit_bytes=VMEM_LIMIT_BYTES)


def _rms(xf, g):
    return xf * lax.rsqrt(jnp.mean(xf * xf, axis=-1, keepdims=True) + EPS) * g


IN_TN = 1280
IN_KV_TILES = D_KV // IN_TN


class _CastJobs:
    def __init__(self, weights, layer, grid, transform=None):
        self.weights, self.layer, self.transform = weights, layer, transform
        self.inner = grid[1]
        n_max = 1
        while n_max * 2 <= min(grid[0] * grid[1], MAX_CAST_STEPS):
            n_max *= 2
        self.n = []
        for w in weights:
            n = n_max
            while w.shape[1] % (n * BF16_ROWS):
                n //= 2
            self.n.append(n)

    def _spec(self, w, n, lead):
        slab = lambda i, j: jnp.minimum(i * self.inner + j, n - 1)
        if lead:
            return pl.BlockSpec((None, w.shape[1] // n, w.shape[2]), lambda i, j: (self.layer, slab(i, j), 0))
        return pl.BlockSpec((w.shape[1] // n, w.shape[2]), lambda i, j: (slab(i, j), 0))

    def in_specs(self):
        return [self._spec(w, n, True) for w, n in zip(self.weights, self.n)]

    def out_specs(self):
        return [self._spec(w, n, False) for w, n in zip(self.weights, self.n)]

    def out_shape(self):
        return [jax.ShapeDtypeStruct(w.shape[1:], BF16) for w in self.weights]

    def run(self, src_refs, dst_refs):
        step = pl.program_id(0) * self.inner + pl.program_id(1)
        for n in sorted(set(self.n)):
            @pl.when(step < n)
            def _():
                for s, d, n_w in zip(src_refs, dst_refs, self.n):
                    if n_w == n:
                        v = s[...]
                        d[...] = (self.transform(v) if self.transform else v).astype(BF16)


MAX_CAST_STEPS = 128
BF16_ROWS = 16


def _in_proj_kernel(*refs, jobs):
    n_side = len(jobs.weights) if jobs else 0
    x_ref, g_ref, w_ref = refs[:3]
    side_in = refs[3:3 + n_side]
    zkv_ref, z_ref = refs[3 + n_side:5 + n_side]
    side_out = refs[5 + n_side:5 + 2 * n_side]
    xn_ref = refs[-1]
    j = pl.program_id(1)

    @pl.when(j == 0)
    def _():
        xn_ref[...] = _rms(x_ref[...], g_ref[...]).astype(BF16)

    @pl.when(j < IN_KV_TILES)
    def _():
        zkv_ref[...] = jnp.dot(xn_ref[...], w_ref[...], preferred_element_type=F32)

    @pl.when(j >= IN_KV_TILES)
    def _():
        z_ref[...] = jnp.dot(xn_ref[...], w_ref[...], preferred_element_type=F32).astype(BF16)

    if jobs:
        jobs.run(side_in, side_out)


def _in_proj(x, g, w, layer, tm, cast=None):
    m = x.shape[0]
    grid = (m // tm, D_IN // IN_TN)
    jobs = _CastJobs(cast[0], cast[1], grid) if cast else None
    return pl.pallas_call(
        functools.partial(_in_proj_kernel, jobs=jobs),
        grid=grid,
        in_specs=[
            pl.BlockSpec((tm, D_MODEL), lambda i, j: (i, 0)),
            pl.BlockSpec((None, 1, D_MODEL), lambda i, j: (layer, 0, 0)),
            pl.BlockSpec((None, D_MODEL, IN_TN), lambda i, j: (0, 0, j)),
        ] + (jobs.in_specs() if jobs else []),
        out_specs=[
            pl.BlockSpec((tm, IN_TN), lambda i, j: (i, jnp.minimum(j, IN_KV_TILES - 1))),
            pl.BlockSpec((tm, IN_TN), lambda i, j: (i, jnp.maximum(j - IN_KV_TILES, 0))),
        ] + (jobs.out_specs() if jobs else []),
        out_shape=[jax.ShapeDtypeStruct((m, D_KV), F32), jax.ShapeDtypeStruct((m, D_Z), BF16)]
        + (jobs.out_shape() if jobs else []),
        scratch_shapes=[pltpu.VMEM((tm, D_MODEL), BF16)],
        compiler_params=_params(("arbitrary", "arbitrary")),
        name="in_proj",
    )(x, g, w, *(cast[0] if cast else ()))


_XPAD = 8
LANES = 128
_SEGS = 8
_SEG_LEN = 4


def _sigmoid(x):
    return 0.5 * (jnp.tanh(0.5 * x) + 1.0)


def _lru_kernel(ax_ref, ag_ref, cbuf_ref, h0_ref, cw_ref, cb_ref, wa_ref, ba_ref, wx_ref, bx_ref,
                lam_ref, out_ref, convo_ref, ho_ref, xbuf, a_s, b_s, h_s):
    t = pl.program_id(1)
    nt = pl.num_programs(1)
    tt = ax_ref.shape[0]

    @pl.when(t == 0)
    def _():
        xbuf[...] = jnp.zeros_like(xbuf)
        xbuf[_XPAD - (CONV_W - 1):, :] = cbuf_ref[...]
        h_s[...] = h0_ref[...]

    x = ax_ref[...].astype(F32)
    xe = jnp.concatenate([xbuf[...], x], axis=0)
    acc = xe * cw_ref[0:1, :]
    for k in range(1, CONV_W):
        acc = xe * cw_ref[k:k + 1, :] + pltpu.roll(acc, 1, 0)
    u = cb_ref[...] + acc[_XPAD:, :]
    tail = x[tt - (CONV_W - 1):, :]
    xbuf[...] = x[tt - _XPAD:, :]

    ub = u.astype(BF16)
    r_parts, i_parts = [], []
    for n in range(LRU_BLOCKS):
        un = ub[:, n * LRU_BLOCK:(n + 1) * LRU_BLOCK]
        r_parts.append(jnp.dot(un, wa_ref[n], preferred_element_type=F32))
        i_parts.append(jnp.dot(un, wx_ref[n], preferred_element_type=F32))
    r = _sigmoid(jnp.concatenate(r_parts, axis=1) + ba_ref[...])
    i = _sigmoid(jnp.concatenate(i_parts, axis=1) + bx_ref[...])
    log_a = -LRU_C * r * jax.nn.softplus(-lam_ref[...])
    a = jnp.exp(log_a)
    y = 1.0 - a * a
    b = jnp.where(y > 0.0, y * lax.rsqrt(y), 0.0) * (i * u)
    n_lg = D_RNN // LANES
    for lg in range(n_lg):
        a_s[lg] = a[:, lg * LANES:(lg + 1) * LANES]
        b_s[lg] = b[:, lg * LANES:(lg + 1) * LANES]

    row = lax.broadcasted_iota(jnp.int32, (_SEGS, LANES), 0)
    sub = _SEGS * _SEG_LEN

    def body(sb, h):
        r0 = pl.multiple_of(sb * sub, sub)
        step = lambda j: pl.ds(r0 + j, _SEGS, stride=_SEG_LEN)
        h_next = []
        for lg in range(n_lg):
            a_g, b_g = a_s.at[lg], b_s.at[lg]
            h_g = h[:, lg * LANES:(lg + 1) * LANES]
            acs, bcs = [a_g[step(0), :]], [b_g[step(0), :]]
            for j in range(1, _SEG_LEN):
                aj = a_g[step(j), :]
                bcs.append(aj * bcs[-1] + b_g[step(j), :])
                acs.append(aj * acs[-1])
            at, bt = acs[-1], bcs[-1]
            for s in (1, 2, 4):
                m = row >= s
                bt_new = jnp.where(m, at * pltpu.roll(bt, s, 0) + bt, bt)
                at = jnp.where(m, at * pltpu.roll(at, s, 0), at)
                bt = bt_new
            after = at * h_g + bt
            entry = jnp.where(row >= 1, pltpu.roll(after, 1, 0), h_g)
            for j in range(_SEG_LEN):
                b_g[step(j), :] = acs[j] * entry + bcs[j]
            h_next.append(after[_SEGS - 1:_SEGS, :])
        return jnp.concatenate(h_next, axis=1)

    h = lax.fori_loop(0, tt // sub, body, h_s[...])
    h_s[...] = h
    hs = jnp.concatenate([b_s[lg] for lg in range(n_lg)], axis=1)
    out_ref[...] = (hs * jax.nn.gelu(ag_ref[...].astype(F32))).astype(BF16)

    @pl.when(t == nt - 1)
    def _():
        convo_ref[...] = tail
        ho_ref[...] = h


def _lru(z, conv_buf, h0, state_layer, cw, cb, wa, ba, wx, bx, lam, layer, n_seq, t_len, tt):
    nt = t_len // tt
    row = lambda b, t: b * nt + t
    vec = lambda: pl.BlockSpec((None, 1, D_RNN), lambda b, t: (layer, 0, 0))
    blk = lambda: pl.BlockSpec((None, LRU_BLOCKS, LRU_BLOCK, LRU_BLOCK), lambda b, t: (layer, 0, 0, 0))
    return pl.pallas_call(
        _lru_kernel,
        grid=(n_seq, nt),
        in_specs=[
            pl.BlockSpec((tt, D_RNN), lambda b, t: (row(b, t), Z_AX // D_RNN)),
            pl.BlockSpec((tt, D_RNN), lambda b, t: (row(b, t), Z_AG // D_RNN)),
            pl.BlockSpec((None, None, CONV_W - 1, D_RNN), lambda b, t: (state_layer, b, 0, 0)),
            pl.BlockSpec((None, None, 1, D_RNN), lambda b, t: (state_layer, b, 0, 0)),
            pl.BlockSpec((None, CONV_W, D_RNN), lambda b, t: (layer, 0, 0)),
            vec(), blk(), vec(), blk(), vec(), vec(),
        ],
        out_specs=[
            pl.BlockSpec((tt, D_RNN), lambda b, t: (row(b, t), 0)),
            pl.BlockSpec((None, CONV_W - 1, D_RNN), lambda b, t: (b, 0, 0)),
            pl.BlockSpec((None, 1, D_RNN), lambda b, t: (b, 0, 0)),
        ],
        out_shape=[
            jax.ShapeDtypeStruct((n_seq * t_len, D_RNN), BF16),
            jax.ShapeDtypeStruct((n_seq, CONV_W - 1, D_RNN), F32),
            jax.ShapeDtypeStruct((n_seq, 1, D_RNN), F32),
        ],
        scratch_shapes=[
            pltpu.VMEM((_XPAD, D_RNN), F32),
            pltpu.VMEM((D_RNN // LANES, tt, LANES), F32),
            pltpu.VMEM((D_RNN // LANES, tt, LANES), F32),
            pltpu.VMEM((1, D_RNN), F32),
        ],
        compiler_params=_params(("parallel", "arbitrary")),
        name="lru",
    )(z, z, conv_buf, h0, cw, cb, wa, ba, wx, bx, lam)


def _dot_nt(a, b):
    return lax.dot_general(a, b, (((1,), (1,)), ((), ())), preferred_element_type=F32)


def _with_ones(v):
    return jnp.concatenate([v, jnp.ones_like(v)], axis=1)


def _swa_chunk(q, kband, vext, sinks, valid):
    qst = jnp.concatenate([q[:, g * HEAD_DIM:(g + 1) * HEAD_DIM] for g in range(SWA_GROUP)], axis=0)
    s = _dot_nt(qst, kband) * (ATTN_SCALE * LOG2E)
    if valid is not None:
        s = jnp.where(valid, s, NEG)
    es, sink_e = [], []
    for g in range(SWA_GROUP):
        sg = s[g * CHUNK:(g + 1) * CHUNK, :]
        sink2 = sinks[g] * LOG2E
        m = jnp.maximum(jnp.max(sg, axis=-1, keepdims=True), sink2)
        es.append(jnp.exp2(sg - m).astype(BF16))
        sink_e.append(jnp.exp2(sink2 - m))
    r = jnp.dot(jnp.concatenate(es, axis=0), vext, preferred_element_type=F32)
    outs = []
    for g in range(SWA_GROUP):
        rg = r[g * CHUNK:(g + 1) * CHUNK, :]
        outs.append(rg[:, :HEAD_DIM] * (1.0 / (rg[:, HEAD_DIM:] + sink_e[g])))
    return jnp.concatenate(outs, axis=1)


def _cb_attend(q, kband, vext, bias2):
    s = _dot_nt(q, kband) * (ATTN_SCALE * LOG2E) + bias2
    m = jnp.max(s, axis=-1, keepdims=True)
    r = jnp.dot(jnp.exp2(s - m).astype(BF16), vext, preferred_element_type=F32)
    return r[:, :HEAD_DIM] * (1.0 / r[:, HEAD_DIM:])


def _cb_bias_block(r, variant, n_rows, n_cols):
    t = pltpu.roll(jnp.broadcast_to(r, (n_rows, CB_RLEN)), 0, 1, stride=1, stride_axis=0)[:, :n_cols]
    c0 = variant * Q_CHUNKS
    sc = max(c0 - CB_PREV, 0)
    qc = c0 + lax.broadcasted_iota(jnp.int32, (n_rows, n_cols), 0) // CHUNK
    kc = sc + lax.broadcasted_iota(jnp.int32, (n_rows, n_cols), 1) // CHUNK
    return jnp.where(kc <= qc, jnp.where(kc >= qc - CB_PREV, t * LOG2E, NEG), NEG)


def _cast_rows(dst, src, n_rows, step, ones=False):
    def body(i, c):
        r0 = pl.multiple_of(i * step, step)
        v = src[pl.ds(r0, step), :].astype(BF16)
        dst[pl.ds(r0, step), :] = _with_ones(v) if ones else v
        return c
    lax.fori_loop(0, n_rows // step, body, 0)


SWA_BAND = (SWA_PREV + 1) * CHUNK
SWA_TQ = 16 * CHUNK


def _swa_prompt_kernel(sink_ref, q_ref, k_ref, v_ref, o_ref, kb_s, vb_s, *, layer):
    kh = pl.program_id(1)
    qi = pl.program_id(2)
    s_len = k_ref.shape[0]

    @pl.when(qi == 0)
    def _():
        _cast_rows(kb_s, k_ref, s_len, 512)
        _cast_rows(vb_s, v_ref, s_len, 512, ones=True)

    sinks = [sink_ref[layer, kh * SWA_GROUP + g] for g in range(SWA_GROUP)]
    jchunk = lax.broadcasted_iota(jnp.int32, (1, SWA_BAND), 1) // CHUNK
    for c in range(SWA_TQ // CHUNK):
        cg = qi * (SWA_TQ // CHUNK) + c
        sc = jnp.maximum(cg - SWA_PREV, 0)
        s0 = pl.multiple_of(sc * CHUNK, CHUNK)
        valid = (jchunk + sc) <= cg
        o = _swa_chunk(q_ref[c * CHUNK:(c + 1) * CHUNK, :], kb_s[pl.ds(s0, SWA_BAND), :],
                       vb_s[pl.ds(s0, SWA_BAND), :], sinks, valid)
        o_ref[c * CHUNK:(c + 1) * CHUNK, :] = o.astype(BF16)


def _swa_prompt(z, zkv, sinks, layer, n_seq, s_len):
    nq = s_len // SWA_TQ
    gw = SWA_GROUP * HEAD_DIM
    return pl.pallas_call(
        functools.partial(_swa_prompt_kernel, layer=layer),
        grid=(n_seq, SWA_KV_HEADS, nq),
        in_specs=[
            pl.BlockSpec(memory_space=pltpu.SMEM),
            pl.BlockSpec((SWA_TQ, gw), lambda b, k, q: (b * nq + q, Z_QB // gw + k)),
            pl.BlockSpec((s_len, HEAD_DIM), lambda b, k, q: (b, KV_KB // HEAD_DIM + k)),
            pl.BlockSpec((s_len, HEAD_DIM), lambda b, k, q: (b, KV_VB // HEAD_DIM + k)),
        ],
        out_specs=pl.BlockSpec((SWA_TQ, gw), lambda b, k, q: (b * nq + q, k)),
        out_shape=jax.ShapeDtypeStruct((n_seq * s_len, BRANCH_WIDTH), BF16),
        scratch_shapes=[pltpu.VMEM((s_len, HEAD_DIM), BF16), pltpu.VMEM((s_len, 2 * HEAD_DIM), BF16)],
        compiler_params=_params(("parallel", "parallel", "arbitrary")),
        name="swa_prompt",
    )(sinks, z, zkv, zkv)


def _swa_step_kernel(sink_ref, q_ref, k_ref, v_ref, ck_ref, cv_ref, o_ref, ko_ref, vo_ref, *, layer):
    n_past = ck_ref.shape[0] // SWA_KV_HEADS
    gw = SWA_GROUP * HEAD_DIM
    for kh in range(SWA_KV_HEADS):
        cs = slice(kh * HEAD_DIM, (kh + 1) * HEAD_DIM)
        kn = k_ref[:, cs]
        vn = v_ref[:, cs]
        kfull = jnp.concatenate(
            [ck_ref[pl.ds(kh, n_past, stride=SWA_KV_HEADS), :].astype(BF16), kn.astype(BF16)], axis=0)
        vfull = jnp.concatenate(
            [cv_ref[pl.ds(kh, n_past, stride=SWA_KV_HEADS), :].astype(BF16), vn.astype(BF16)], axis=0)
        sinks = [sink_ref[layer, kh * SWA_GROUP + g] for g in range(SWA_GROUP)]
        o = _swa_chunk(q_ref[:, kh * gw:(kh + 1) * gw], kfull, _with_ones(vfull), sinks, None)
        o_ref[:, kh * gw:(kh + 1) * gw] = o.astype(BF16)
        ko_ref[pl.ds(kh, CHUNK, stride=SWA_KV_HEADS), :] = kn
        vo_ref[pl.ds(kh, CHUNK, stride=SWA_KV_HEADS), :] = vn


def _swa_step(z, zkv, cache_k, cache_v, sinks, layer, n_seq):
    rows_past = cache_k.shape[2]
    rows_new = CHUNK * SWA_KV_HEADS
    cache = lambda: pl.BlockSpec((None, None, rows_past, HEAD_DIM), lambda b: (layer, b, 0, 0))
    new = lambda: pl.BlockSpec((rows_new, HEAD_DIM), lambda b: (b, 0))
    return pl.pallas_call(
        functools.partial(_swa_step_kernel, layer=layer),
        grid=(n_seq,),
        in_specs=[
            pl.BlockSpec(memory_space=pltpu.SMEM),
            pl.BlockSpec((CHUNK, BRANCH_WIDTH), lambda b: (b, Z_QB // BRANCH_WIDTH)),
            pl.BlockSpec((CHUNK, SWA_KVW), lambda b: (b, KV_KB // SWA_KVW)),
            pl.BlockSpec((CHUNK, SWA_KVW), lambda b: (b, KV_VB // SWA_KVW)),
            cache(), cache(),
        ],
        out_specs=[pl.BlockSpec((CHUNK, BRANCH_WIDTH), lambda b: (b, 0)), new(), new()],
        out_shape=[jax.ShapeDtypeStruct((n_seq * CHUNK, BRANCH_WIDTH), BF16),
                   jax.ShapeDtypeStruct((n_seq * rows_new, HEAD_DIM), F32),
                   jax.ShapeDtypeStruct((n_seq * rows_new, HEAD_DIM), F32)],
        compiler_params=_params(("parallel",)),
        name="swa_step",
    )(sinks, z, zkv, zkv, cache_k, cache_v)


def _cb_prompt_kernel(q_ref, k_ref, v_ref, r_ref, o_ref, kb_s, vb_s, bias_s):
    qi = pl.program_id(2)
    s_len = k_ref.shape[0]

    @pl.when(qi == 0)
    def _():
        _cast_rows(kb_s, k_ref, s_len, 512)
        _cast_rows(vb_s, v_ref, s_len, 512, ones=True)
        for v in range(CB_VARIANTS):
            bias_s[v] = _cb_bias_block(r_ref[v], v, Q_BLOCK, CB_BAND)

    for sub in range(CB_QSUB):
        blk = qi * CB_QSUB + sub
        sc = jnp.maximum(blk * Q_CHUNKS - CB_PREV, 0)
        s0 = pl.multiple_of(sc * CHUNK, CHUNK)
        rows = slice(sub * Q_BLOCK, (sub + 1) * Q_BLOCK)
        o = _cb_attend(q_ref[rows, :], kb_s[pl.ds(s0, CB_BAND), :], vb_s[pl.ds(s0, CB_BAND), :],
                       bias_s[jnp.minimum(blk, CB_VARIANTS - 1)])
        o_ref[rows, :] = o.astype(BF16)


def _cb_prompt(z, zkv, rows, layer, n_seq, s_len):
    tq = Q_BLOCK * CB_QSUB
    nq = s_len // tq
    return pl.pallas_call(
        _cb_prompt_kernel,
        grid=(n_seq, CB_HEADS, nq),
        in_specs=[
            pl.BlockSpec((tq, HEAD_DIM), lambda b, h, q: (b * nq + q, Z_QC // HEAD_DIM + h)),
            pl.BlockSpec((s_len, HEAD_DIM), lambda b, h, q: (b, KV_KC // HEAD_DIM + h)),
            pl.BlockSpec((s_len, HEAD_DIM), lambda b, h, q: (b, KV_VC // HEAD_DIM + h)),
            pl.BlockSpec((None, CB_VARIANTS, None, 1, CB_RLEN), lambda b, h, q: (layer, 0, h, 0, 0)),
        ],
        out_specs=pl.BlockSpec((tq, HEAD_DIM), lambda b, h, q: (b * nq + q, h)),
        out_shape=jax.ShapeDtypeStruct((n_seq * s_len, BRANCH_WIDTH), BF16),
        scratch_shapes=[pltpu.VMEM((s_len, HEAD_DIM), BF16), pltpu.VMEM((s_len, 2 * HEAD_DIM), BF16),
                        pltpu.VMEM((CB_VARIANTS, Q_BLOCK, CB_BAND), F32)],
        compiler_params=_params(("parallel", "parallel", "arbitrary")),
        name="cb_prompt",
    )(z, zkv, zkv, rows)


def _cb_step_kernel(q_ref, k_ref, v_ref, ck_ref, cv_ref, r_ref, o_ref, ko_ref, vo_ref, bias_s):
    n_past = ck_ref.shape[0] // CB_HEADS
    n_k = n_past + CHUNK

    @pl.when(pl.program_id(0) == 0)
    def _():
        for h in range(CB_HEADS):
            bias_s[h] = _cb_bias_block(r_ref[h], CB_VARIANTS - 1, CHUNK, n_k)

    for h in range(CB_HEADS):
        cs = slice(h * HEAD_DIM, (h + 1) * HEAD_DIM)
        kn = k_ref[:, cs]
        vn = v_ref[:, cs]
        kfull = jnp.concatenate(
            [ck_ref[pl.ds(h, n_past, stride=CB_HEADS), :].astype(BF16), kn.astype(BF16)], axis=0)
        vfull = jnp.concatenate(
            [cv_ref[pl.ds(h, n_past, stride=CB_HEADS), :].astype(BF16), vn.astype(BF16)], axis=0)
        o_ref[:, cs] = _cb_attend(q_ref[:, cs], kfull, _with_ones(vfull), bias_s[h]).astype(BF16)
        ko_ref[pl.ds(h, CHUNK, stride=CB_HEADS), :] = kn
        vo_ref[pl.ds(h, CHUNK, stride=CB_HEADS), :] = vn


def _cb_step(z, zkv, cache_k, cache_v, rows, layer, n_seq):
    rows_past = cache_k.shape[2]
    n_k = rows_past // CB_HEADS + CHUNK
    rows_new = CHUNK * CB_HEADS
    cache = lambda: pl.BlockSpec((None, None, rows_past, HEAD_DIM), lambda b: (layer, b, 0, 0))
    new = lambda: pl.BlockSpec((rows_new, HEAD_DIM), lambda b: (b, 0))
    return pl.pallas_call(
        _cb_step_kernel,
        grid=(n_seq,),
        in_specs=[
            pl.BlockSpec((CHUNK, BRANCH_WIDTH), lambda b: (b, Z_QC // BRANCH_WIDTH)),
            pl.BlockSpec((CHUNK, CB_KVW), lambda b: (b, KV_KC // CB_KVW)),
            pl.BlockSpec((CHUNK, CB_KVW), lambda b: (b, KV_VC // CB_KVW)),
            cache(), cache(),
            pl.BlockSpec((None, None, CB_HEADS, 1, CB_RLEN),
                         lambda b: (layer, CB_VARIANTS - 1, 0, 0, 0)),
        ],
        out_specs=[pl.BlockSpec((CHUNK, BRANCH_WIDTH), lambda b: (b, 0)), new(), new()],
        out_shape=[jax.ShapeDtypeStruct((n_seq * CHUNK, BRANCH_WIDTH), BF16),
                   jax.ShapeDtypeStruct((n_seq * rows_new, HEAD_DIM), F32),
                   jax.ShapeDtypeStruct((n_seq * rows_new, HEAD_DIM), F32)],
        scratch_shapes=[pltpu.VMEM((CB_HEADS, CHUNK, n_k), F32)],
        compiler_params=_params(("arbitrary",)),
        name="cb_step",
    )(z, zkv, zkv, cache_k, cache_v, rows)


def _merge_kernel(x_ref, a_ref, b_ref, c_ref, ga_ref, gb_ref, gc_ref, wb_ref, wo_ref, o_ref):
    mixed = None
    for r, (br, gr) in enumerate(((a_ref, ga_ref), (b_ref, gb_ref), (c_ref, gc_ref))):
        proj = jnp.dot(br[...], wb_ref[r], preferred_element_type=F32)
        gate = jax.nn.sigmoid(gr[...].astype(F32))
        mixed = gate * proj if mixed is None else mixed + gate * proj
    o_ref[...] = x_ref[...] + jnp.dot(mixed.astype(BF16), wo_ref[...], preferred_element_type=F32)


def _merge(x, out_a, out_b, out_c, z, w_branch, w_out, layer, tm):
    m = x.shape[0]
    branch = lambda: pl.BlockSpec((tm, BRANCH_WIDTH), lambda i: (i, 0))
    gate = lambda r: pl.BlockSpec((tm, D_MODEL), lambda i: (i, Z_GATES // D_MODEL + r))
    return pl.pallas_call(
        _merge_kernel,
        grid=(m // tm,),
        in_specs=[
            pl.BlockSpec((tm, D_MODEL), lambda i: (i, 0)),
            branch(), branch(), branch(),
            gate(0), gate(1), gate(2),
            pl.BlockSpec((None, N_BRANCH, BRANCH_WIDTH, D_MODEL), lambda i: (layer, 0, 0, 0),
                         pipeline_mode=pl.Buffered(1)),
            pl.BlockSpec((None, D_MODEL, D_MODEL), lambda i: (layer, 0, 0),
                         pipeline_mode=pl.Buffered(1)),
        ],
        out_specs=pl.BlockSpec((tm, D_MODEL), lambda i: (i, 0)),
        out_shape=jax.ShapeDtypeStruct((m, D_MODEL), F32),
        compiler_params=_params(("parallel",)),
        name="merge",
    )(x, out_a, out_b, out_c, z, z, z, w_branch, w_out)


def _mlp_kernel(*refs, final_norm, jobs):
    n_side = len(jobs.weights) if jobs else 0
    x_ref, g_ref, fg_ref, wu_ref, wd_ref = refs[:5]
    side_in = refs[5:5 + n_side]
    o_ref = refs[5 + n_side]
    side_out = refs[6 + n_side:6 + 2 * n_side]
    hn_ref = refs[-1]
    j = pl.program_id(1)

    @pl.when(j == 0)
    def _():
        xf = x_ref[...]
        hn_ref[...] = _rms(xf, g_ref[...]).astype(BF16)
        o_ref[...] = xf

    h = jnp.dot(hn_ref[...], wu_ref[...], preferred_element_type=F32)
    h = jnp.square(jnp.maximum(h, 0.0)).astype(BF16)
    o_ref[...] += jnp.dot(h, wd_ref[...], preferred_element_type=F32)

    if final_norm:
        @pl.when(j == pl.num_programs(1) - 1)
        def _():
            o_ref[...] = _rms(o_ref[...], fg_ref[...])

    if jobs:
        jobs.run(side_in, side_out)


def _mlp(x, g, final_g, w_up, w_down, layer, tm, tf, final_norm, cast=None):
    m = x.shape[0]
    grid = (m // tm, D_FF // tf)
    jobs = _CastJobs(cast[0], cast[1], grid, cast[2]) if cast else None
    return pl.pallas_call(
        functools.partial(_mlp_kernel, final_norm=final_norm, jobs=jobs),
        grid=grid,
        in_specs=[
            pl.BlockSpec((tm, D_MODEL), lambda i, j: (i, 0)),
            pl.BlockSpec((None, 1, D_MODEL), lambda i, j: (layer, 0, 0)),
            pl.BlockSpec((1, D_MODEL), lambda i, j: (0, 0)),
            pl.BlockSpec((None, D_MODEL, tf), lambda i, j: (0, 0, j)),
            pl.BlockSpec((None, tf, D_MODEL), lambda i, j: (0, j, 0)),
        ] + (jobs.in_specs() if jobs else []),
        out_specs=[pl.BlockSpec((tm, D_MODEL), lambda i, j: (i, 0))] + (jobs.out_specs() if jobs else []),
        out_shape=[jax.ShapeDtypeStruct((m, D_MODEL), F32)] + (jobs.out_shape() if jobs else []),
        scratch_shapes=[pltpu.VMEM((tm, D_MODEL), BF16)],
        compiler_params=_params(("arbitrary", "arbitrary")),
        name="mlp",
    )(x, g, final_g, w_up, w_down, *(cast[0] if cast else ()))


def _permute_w_in(w_in):
    seg = lambda o, w: w_in[..., o:o + w]
    return jnp.concatenate([
        seg(_O_KC, CB_KVW), seg(_O_VC, CB_KVW), seg(_O_KB, SWA_KVW), seg(_O_VB, SWA_KVW),
        seg(_O_AX, D_RNN), seg(_O_AG, D_RNN), seg(_O_QB, BRANCH_WIDTH), seg(_O_QC, BRANCH_WIDTH),
        seg(_O_GATES, N_BRANCH * D_MODEL)], axis=-1).astype(BF16)


def _cb_bias_rows(table):
    m = jnp.arange(CB_RLEN)
    rel = jnp.where(m < CB_BAND, m, m - CB_RLEN)
    out = []
    for v in range(CB_VARIANTS):
        c0 = v * Q_CHUNKS
        sc = max(c0 - CB_PREV, 0)
        d = (c0 - sc) * CHUNK - rel
        idx = jnp.clip(d, -REL_CLIP, REL_CLIP) + REL_CLIP
        out.append(table.astype(F32)[:, :, idx])
    return jnp.stack(out, axis=1)[:, :, :, None, :]


def _pick_tm(m, want):
    tm = min(want, m)
    while m % tm:
        tm //= 2
    return tm


def _layer(x, n_seq, t_len, conv_buf, h0, state_layer, caches, layer, p, w_in_l, w_layer):
    m = x.shape[0]
    depth = p["w_up"].shape[0]
    tm_in = _pick_tm(m, 1024)
    if w_layer is None:
        zkv, z, wu, wd, wb, wo = _in_proj(
            x, p["norm1"], w_in_l, layer, tm_in,
            cast=((p["w_up"], p["w_down"], p["w_branch"], p["w_out"]), layer))
        w_layer = (wu[None], wd[None], wb.reshape(1, N_BRANCH, BRANCH_WIDTH, D_MODEL), wo[None])
        cast_next = ((p["w_in"],), layer + 1, _permute_w_in) if layer + 1 < depth else None
    else:
        zkv, z = _in_proj(x, p["norm1"], w_in_l, layer, tm_in)
        cast_next = None
    w_up_l, w_down_l, w_branch_l, w_out_l = w_layer
    out_a, conv_o, h_o = _lru(z, conv_buf, h0, state_layer, p["conv_w"], p["conv_b"], p["wa"], p["ba"],
                              p["wx"], p["bx"], p["lam"], layer, n_seq, t_len, _pick_tm(t_len, 512))
    if caches is None:
        out_b = _swa_prompt(z, zkv, p["sinks"], layer, n_seq, t_len)
        out_c = _cb_prompt(z, zkv, p["cb_rows"], layer, n_seq, t_len)
        kv3 = zkv.reshape(n_seq, t_len, D_KV)
        n_swa = min(SWA_WINDOW, t_len)
        n_cb = min(CB_REACH, t_len)
        kv = (kv3[:, t_len - n_swa:, KV_KB:KV_KB + SWA_KVW], kv3[:, t_len - n_swa:, KV_VB:KV_VB + SWA_KVW],
              kv3[:, t_len - n_cb:, KV_KC:KV_KC + CB_KVW], kv3[:, t_len - n_cb:, KV_VC:KV_VC + CB_KVW])
    else:
        ck_b, cv_b, ck_c, cv_c = caches
        out_b, kb, vb = _swa_step(z, zkv, ck_b, cv_b, p["sinks"], layer, n_seq)
        out_c, kc, vc = _cb_step(z, zkv, ck_c, cv_c, p["cb_rows"], layer, n_seq)
        kv = (kb, vb, kc, vc)
    x = _merge(x, out_a, out_b, out_c, z, w_branch_l, w_out_l, 0, _pick_tm(m, 256))
    res = _mlp(x, p["norm2"], p["final_g"], w_up_l, w_down_l, layer, _pick_tm(m, 512), 1024,
               final_norm=layer == depth - 1, cast=cast_next)
    w_in_next = res[1][None] if cast_next else None
    return res[0], conv_o, h_o[:, 0], kv, w_layer, w_in_next


def kernel(x_prompt, x_sample, state_conv, state_lru, cache_swa_k, cache_swa_v, cache_cb_k, cache_cb_v, norm1_g, w_in, conv_w, conv_b, lru_wa, lru_ba, lru_wx, lru_bx, lru_lambda, attn_sinks, rel_bias_table, w_branch, w_out, norm2_g, w_up, w_down, final_g):
    depth = w_in.shape[0]
    nb, s_len, _ = x_prompt.shape
    db, d_len, _ = x_sample.shape
    assert d_len == CHUNK and s_len % (Q_BLOCK * CB_QSUB) == 0 and s_len % SWA_TQ == 0
    assert cache_swa_k.shape[2] == SWA_WINDOW and cache_cb_k.shape[2] == CB_REACH

    row = lambda v: v.reshape(depth, 1, -1)
    p = {
        "norm1": row(norm1_g), "norm2": row(norm2_g), "final_g": final_g.reshape(1, D_MODEL),
        "w_in": w_in, "w_up": w_up, "w_down": w_down,
        "w_branch": w_branch.reshape(depth, N_BRANCH * BRANCH_WIDTH, D_MODEL), "w_out": w_out,
        "conv_w": conv_w, "conv_b": row(conv_b),
        "wa": lru_wa.astype(BF16), "ba": row(lru_ba), "wx": lru_wx.astype(BF16), "bx": row(lru_bx),
        "lam": row(lru_lambda), "sinks": attn_sinks,
        "cb_rows": _cb_bias_rows(rel_bias_table),
    }
    w_in_l = _permute_w_in(w_in[0])[None]
    caches = (cache_swa_k.reshape(depth, db, SWA_WINDOW * SWA_KV_HEADS, HEAD_DIM),
              cache_swa_v.reshape(depth, db, SWA_WINDOW * SWA_KV_HEADS, HEAD_DIM),
              cache_cb_k.reshape(depth, db, CB_REACH * CB_HEADS, HEAD_DIM),
              cache_cb_v.reshape(depth, db, CB_REACH * CB_HEADS, HEAD_DIM))

    xp = x_prompt.reshape(nb * s_len, D_MODEL)
    xs = x_sample.reshape(db * d_len, D_MODEL)
    zero_conv = jnp.zeros((1, nb, CONV_W - 1, D_RNN), F32)
    zero_h = jnp.zeros((1, nb, 1, D_RNN), F32)
    h0_s = state_lru.reshape(depth, db, 1, D_RNN)
    heads = (SWA_KV_HEADS, SWA_KV_HEADS, CB_HEADS, CB_HEADS)
    outs = [[] for _ in range(12)]
    for l in range(depth):
        xp, conv_p, h_p, kv_p, w_layer, w_in_next = _layer(xp, nb, s_len, zero_conv, zero_h, 0, None, l, p,
                                                           w_in_l, None)
        xs, conv_s, h_s, kv_s, _, _ = _layer(xs, db, d_len, state_conv, h0_s, l, caches, l, p,
                                             w_in_l, w_layer)
        w_in_l = w_in_next
        outs[0].append(conv_p)
        outs[1].append(h_p)
        outs[6].append(conv_s)
        outs[7].append(h_s)
        for n in range(4):
            outs[2 + n].append(kv_p[n].reshape(nb, -1, heads[n], HEAD_DIM))
            outs[8 + n].append(kv_s[n].reshape(db, d_len, heads[n], HEAD_DIM))

    y_prompt = xp.reshape(nb, s_len, D_MODEL)
    y_sample = xs.reshape(db, d_len, D_MODEL)
    return (y_prompt, y_sample) + tuple(jnp.stack(o) for o in outs)
```

```python
import functools

import jax
import jax.numpy as jnp
from jax import lax
from jax.experimental import pallas as pl
from jax.experimental.pallas import tpu as pltpu

F32 = jnp.float32
BF16 = jnp.bfloat16

D_MODEL = 2048
CHUNK = 64
HEAD_DIM = 128
BRANCH_WIDTH = D_MODEL // 2
N_BRANCH = 3
D_RNN = BRANCH_WIDTH
LRU_BLOCKS = 8
LRU_BLOCK = D_RNN // LRU_BLOCKS
CONV_W = 4
LRU_C = 8.0
SWA_HEADS = BRANCH_WIDTH // HEAD_DIM
SWA_KV_HEADS = 2
SWA_GROUP = SWA_HEADS // SWA_KV_HEADS
SWA_WINDOW = 128
SWA_PREV = SWA_WINDOW // CHUNK
CB_HEADS = BRANCH_WIDTH // HEAD_DIM
CB_PREV = 8
CB_REACH = CB_PREV * CHUNK
REL_CLIP = 128
D_FF = 4 * D_MODEL
EPS = 1e-6
NEG = -1e30
ATTN_SCALE = HEAD_DIM ** -0.5
LOG2E = 1.4426950408889634

_O_AX, _O_AG, _O_QB, _O_KB, _O_VB, _O_QC, _O_KC, _O_VC, _O_GATES = (
    0, 1024, 2048, 3072, 3328, 3584, 4608, 5632, 6656)
D_IN = _O_GATES + N_BRANCH * D_MODEL
SWA_KVW = SWA_KV_HEADS * HEAD_DIM
CB_KVW = CB_HEADS * HEAD_DIM
D_KV = 2 * SWA_KVW + 2 * CB_KVW
D_Z = D_IN - D_KV
KV_KC, KV_VC, KV_KB, KV_VB = 0, 1024, 2048, 2304
Z_AX, Z_AG, Z_QB, Z_QC, Z_GATES = 0, 1024, 2048, 3072, 4096

VMEM_LIMIT_BYTES = 56 * 1024 * 1024

Q_BLOCK = 4 * CHUNK
Q_CHUNKS = Q_BLOCK // CHUNK
CB_BAND = (CB_PREV + Q_CHUNKS) * CHUNK
CB_QSUB = 8
CB_RLEN = 1024
assert CB_RLEN >= Q_BLOCK + CB_BAND - 1
CB_VARIANTS = 3


def _params(semantics):
    return pltpu.CompilerParams(dimension_semantics=semantics, vmem_limit_bytes=VMEM_LIMIT_BYTES)


def _rms(xf, g):
    return xf * lax.rsqrt(jnp.mean(xf * xf, axis=-1, keepdims=True) + EPS) * g


IN_TN = 1280
IN_KV_TILES = D_KV // IN_TN
IN_Z_TILES = D_Z // IN_TN


class _CastJobs:
    def __init__(self, weights, layer, grid, transform=None):
        self.weights, self.layer, self.transform = weights, layer, transform
        self.inner = grid[1]
        n_max = 1
        while n_max * 2 <= min(grid[0] * grid[1], MAX_CAST_STEPS):
            n_max *= 2
        self.n = []
        for w in weights:
            n = n_max
            while w.shape[1] % (n * BF16_ROWS):
                n //= 2
            self.n.append(n)

    def _spec(self, w, n, lead):
        slab = lambda i, j: jnp.minimum(i * self.inner + j, n - 1)
        if lead:
            return pl.BlockSpec((None, w.shape[1] // n, w.shape[2]), lambda i, j: (self.layer, slab(i, j), 0))
        return pl.BlockSpec((w.shape[1] // n, w.shape[2]), lambda i, j: (slab(i, j), 0))

    def in_specs(self):
        return [self._spec(w, n, True) for w, n in zip(self.weights, self.n)]

    def out_specs(self):
        return [self._spec(w, n, False) for w, n in zip(self.weights, self.n)]

    def out_shape(self):
        return [jax.ShapeDtypeStruct(w.shape[1:], BF16) for w in self.weights]

    def run(self, src_refs, dst_refs):
        step = pl.program_id(0) * self.inner + pl.program_id(1)
        for n in sorted(set(self.n)):
            @pl.when(step < n)
            def _():
                for s, d, n_w in zip(src_refs, dst_refs, self.n):
                    if n_w == n:
                        v = s[...]
                        d[...] = (self.transform(v) if self.transform else v).astype(BF16)


MAX_CAST_STEPS = 128
BF16_ROWS = 16


def _in_proj_kernel(*refs, jobs, mode):
    n_side = len(jobs.weights) if jobs else 0
    n_out = 2 if mode == "dual" else 1
    x_ref, g_ref, w_ref = refs[:3]
    side_in = refs[3:3 + n_side]
    outs = refs[3 + n_side:3 + n_side + n_out]
    side_out = refs[3 + n_side + n_out:3 + 2 * n_side + n_out]
    xn_ref = refs[-1]
    j = pl.program_id(1)

    @pl.when(j == 0)
    def _():
        xn_ref[...] = _rms(x_ref[...], g_ref[...]).astype(BF16)

    dot = lambda: jnp.dot(xn_ref[...], w_ref[...], preferred_element_type=F32)
    if mode == "bf16":
        outs[0][...] = dot().astype(BF16)
    elif mode == "kv_f32":
        outs[0][...] = dot()
    else:
        z_ref, zkv_ref = outs

        @pl.when(j < IN_Z_TILES)
        def _():
            z_ref[...] = dot().astype(BF16)

        @pl.when(j >= IN_Z_TILES)
        def _():
            zkv_ref[...] = dot()

    if jobs:
        jobs.run(side_in, side_out)


def _in_proj(x, g, w, layer, tm, mode, cast=None, row_tile=None, n_tiles=None):
    m = x.shape[0]
    col0 = 0
    if mode == "kv_f32":
        grid = (n_tiles, IN_KV_TILES)
        col0 = IN_Z_TILES
        x_map = lambda i, j: (row_tile(i), 0)
        out_specs = [pl.BlockSpec((tm, IN_TN), lambda i, j: (i, j))]
        out_shape = [jax.ShapeDtypeStruct((n_tiles * tm, D_KV), F32)]
    else:
        grid = (m // tm, D_IN // IN_TN)
        x_map = lambda i, j: (i, 0)
        if mode == "bf16":
            out_specs = [pl.BlockSpec((tm, IN_TN), lambda i, j: (i, j))]
            out_shape = [jax.ShapeDtypeStruct((m, D_IN), BF16)]
        else:
            out_specs = [pl.BlockSpec((tm, IN_TN), lambda i, j: (i, jnp.minimum(j, IN_Z_TILES - 1))),
                         pl.BlockSpec((tm, IN_TN), lambda i, j: (i, jnp.maximum(j - IN_Z_TILES, 0)))]
            out_shape = [jax.ShapeDtypeStruct((m, D_Z), BF16), jax.ShapeDtypeStruct((m, D_KV), F32)]
    jobs = _CastJobs(cast[0], cast[1], grid) if cast else None
    return pl.pallas_call(
        functools.partial(_in_proj_kernel, jobs=jobs, mode=mode),
        grid=grid,
        in_specs=[
            pl.BlockSpec((tm, D_MODEL), x_map),
            pl.BlockSpec((None, 1, D_MODEL), lambda i, j: (layer, 0, 0)),
            pl.BlockSpec((None, D_MODEL, IN_TN), lambda i, j: (0, 0, col0 + j)),
        ] + (jobs.in_specs() if jobs else []),
        out_specs=out_specs + (jobs.out_specs() if jobs else []),
        out_shape=out_shape + (jobs.out_shape() if jobs else []),
        scratch_shapes=[pltpu.VMEM((tm, D_MODEL), BF16)],
        compiler_params=_params(("arbitrary", "arbitrary")),
        name="in_proj",
    )(x, g, w, *(cast[0] if cast else ()))


_XPAD = 8
LANES = 128
_SEGS = 8
_SEG_LEN = 4


def _sigmoid(x):
    return 0.5 * (jnp.tanh(0.5 * x) + 1.0)


def _lru_kernel(ax_ref, ag_ref, cbuf_ref, h0_ref, cw_ref, cb_ref, wa_ref, ba_ref, wx_ref, bx_ref,
                lam_ref, out_ref, convo_ref, ho_ref, xbuf, a_s, b_s, h_s):
    t = pl.program_id(1)
    nt = pl.num_programs(1)
    tt = ax_ref.shape[0]

    @pl.when(t == 0)
    def _():
        xbuf[...] = jnp.zeros_like(xbuf)
        xbuf[_XPAD - (CONV_W - 1):, :] = cbuf_ref[...]
        h_s[...] = h0_ref[...]

    x = ax_ref[...].astype(F32)
    xe = jnp.concatenate([xbuf[...], x], axis=0)
    acc = xe * cw_ref[0:1, :]
    for k in range(1, CONV_W):
        acc = xe * cw_ref[k:k + 1, :] + pltpu.roll(acc, 1, 0)
    u = cb_ref[...] + acc[_XPAD:, :]
    tail = x[tt - (CONV_W - 1):, :]
    xbuf[...] = x[tt - _XPAD:, :]

    ub = u.astype(BF16)
    r_parts, i_parts = [], []
    for n in range(LRU_BLOCKS):
        un = ub[:, n * LRU_BLOCK:(n + 1) * LRU_BLOCK]
        r_parts.append(jnp.dot(un, wa_ref[n], preferred_element_type=F32))
        i_parts.append(jnp.dot(un, wx_ref[n], preferred_element_type=F32))
    r = _sigmoid(jnp.concatenate(r_parts, axis=1) + ba_ref[...])
    i = _sigmoid(jnp.concatenate(i_parts, axis=1) + bx_ref[...])
    log_a = -LRU_C * r * jax.nn.softplus(-lam_ref[...])
    a = jnp.exp(log_a)
    y = 1.0 - a * a
    b = jnp.where(y > 0.0, y * lax.rsqrt(y), 0.0) * (i * u)
    n_lg = D_RNN // LANES
    for lg in range(n_lg):
        a_s[lg] = a[:, lg * LANES:(lg + 1) * LANES]
        b_s[lg] = b[:, lg * LANES:(lg + 1) * LANES]

    row = lax.broadcasted_iota(jnp.int32, (_SEGS, LANES), 0)
    sub = _SEGS * _SEG_LEN

    def body(sb, h):
        r0 = pl.multiple_of(sb * sub, sub)
        step = lambda j: pl.ds(r0 + j, _SEGS, stride=_SEG_LEN)
        h_next = []
        for lg in range(n_lg):
            a_g, b_g = a_s.at[lg], b_s.at[lg]
            h_g = h[:, lg * LANES:(lg + 1) * LANES]
            acs, bcs = [a_g[step(0), :]], [b_g[step(0), :]]
            for j in range(1, _SEG_LEN):
                aj = a_g[step(j), :]
                bcs.append(aj * bcs[-1] + b_g[step(j), :])
                acs.append(aj * acs[-1])
            at, bt = acs[-1], bcs[-1]
            for s in (1, 2, 4):
                m = row >= s
                bt_new = jnp.where(m, at * pltpu.roll(bt, s, 0) + bt, bt)
                at = jnp.where(m, at * pltpu.roll(at, s, 0), at)
                bt = bt_new
            after = at * h_g + bt
            entry = jnp.where(row >= 1, pltpu.roll(after, 1, 0), h_g)
            for j in range(_SEG_LEN):
                b_g[step(j), :] = acs[j] * entry + bcs[j]
            h_next.append(after[_SEGS - 1:_SEGS, :])
        return jnp.concatenate(h_next, axis=1)

    h = lax.fori_loop(0, tt // sub, body, h_s[...])
    h_s[...] = h
    hs = jnp.concatenate([b_s[lg] for lg in range(n_lg)], axis=1)
    out_ref[...] = (hs * jax.nn.gelu(ag_ref[...].astype(F32))).astype(BF16)

    @pl.when(t == nt - 1)
    def _():
        convo_ref[...] = tail
        ho_ref[...] = h


def _lru(z, conv_buf, h0, state_layer, cw, cb, wa, ba, wx, bx, lam, layer, n_seq, t_len, tt):
    nt = t_len // tt
    row = lambda b, t: b * nt + t
    vec = lambda: pl.BlockSpec((None, 1, D_RNN), lambda b, t: (layer, 0, 0))
    blk = lambda: pl.BlockSpec((None, LRU_BLOCKS, LRU_BLOCK, LRU_BLOCK), lambda b, t: (layer, 0, 0, 0))
    return pl.pallas_call(
        _lru_kernel,
        grid=(n_seq, nt),
        in_specs=[
            pl.BlockSpec((tt, D_RNN), lambda b, t: (row(b, t), Z_AX // D_RNN)),
            pl.BlockSpec((tt, D_RNN), lambda b, t: (row(b, t), Z_AG // D_RNN)),
            pl.BlockSpec((None, None, CONV_W - 1, D_RNN), lambda b, t: (state_layer, b, 0, 0)),
            pl.BlockSpec((None, None, 1, D_RNN), lambda b, t: (state_layer, b, 0, 0)),
            pl.BlockSpec((None, CONV_W, D_RNN), lambda b, t: (layer, 0, 0)),
            vec(), blk(), vec(), blk(), vec(), vec(),
        ],
        out_specs=[
            pl.BlockSpec((tt, D_RNN), lambda b, t: (row(b, t), 0)),
            pl.BlockSpec((None, CONV_W - 1, D_RNN), lambda b, t: (b, 0, 0)),
            pl.BlockSpec((None, 1, D_RNN), lambda b, t: (b, 0, 0)),
        ],
        out_shape=[
            jax.ShapeDtypeStruct((n_seq * t_len, D_RNN), BF16),
            jax.ShapeDtypeStruct((n_seq, CONV_W - 1, D_RNN), F32),
            jax.ShapeDtypeStruct((n_seq, 1, D_RNN), F32),
        ],
        scratch_shapes=[
            pltpu.VMEM((_XPAD, D_RNN), F32),
            pltpu.VMEM((D_RNN // LANES, tt, LANES), F32),
            pltpu.VMEM((D_RNN // LANES, tt, LANES), F32),
            pltpu.VMEM((1, D_RNN), F32),
        ],
        compiler_params=_params(("parallel", "arbitrary")),
        name="lru",
    )(z, z, conv_buf, h0, cw, cb, wa, ba, wx, bx, lam)


def _dot_nt(a, b):
    return lax.dot_general(a, b, (((1,), (1,)), ((), ())), preferred_element_type=F32)


def _with_ones(v):
    return jnp.concatenate([v, jnp.ones_like(v)], axis=1)


def _swa_chunk(q, kband, vext, sinks, valid):
    qst = jnp.concatenate([q[:, g * HEAD_DIM:(g + 1) * HEAD_DIM] for g in range(SWA_GROUP)], axis=0)
    s = _dot_nt(qst, kband) * (ATTN_SCALE * LOG2E)
    if valid is not None:
        s = jnp.where(valid, s, NEG)
    es, sink_e = [], []
    for g in range(SWA_GROUP):
        sg = s[g * CHUNK:(g + 1) * CHUNK, :]
        sink2 = sinks[g] * LOG2E
        m = jnp.maximum(jnp.max(sg, axis=-1, keepdims=True), sink2)
        es.append(jnp.exp2(sg - m).astype(BF16))
        sink_e.append(jnp.exp2(sink2 - m))
    r = jnp.dot(jnp.concatenate(es, axis=0), vext, preferred_element_type=F32)
    outs = []
    for g in range(SWA_GROUP):
        rg = r[g * CHUNK:(g + 1) * CHUNK, :]
        outs.append(rg[:, :HEAD_DIM] * (1.0 / (rg[:, HEAD_DIM:] + sink_e[g])))
    return jnp.concatenate(outs, axis=1)


def _cb_attend(q, kband, vext, bias2):
    s = _dot_nt(q, kband) * (ATTN_SCALE * LOG2E) + bias2
    m = jnp.max(s, axis=-1, keepdims=True)
    r = jnp.dot(jnp.exp2(s - m).astype(BF16), vext, preferred_element_type=F32)
    return r[:, :HEAD_DIM] * (1.0 / r[:, HEAD_DIM:])


def _cb_bias_block(r, variant, n_rows, n_cols):
    t = pltpu.roll(jnp.broadcast_to(r, (n_rows, CB_RLEN)), 0, 1, stride=1, stride_axis=0)[:, :n_cols]
    c0 = variant * Q_CHUNKS
    sc = max(c0 - CB_PREV, 0)
    qc = c0 + lax.broadcasted_iota(jnp.int32, (n_rows, n_cols), 0) // CHUNK
    kc = sc + lax.broadcasted_iota(jnp.int32, (n_rows, n_cols), 1) // CHUNK
    return jnp.where(kc <= qc, jnp.where(kc >= qc - CB_PREV, t * LOG2E, NEG), NEG)


def _cast_rows(dst, src, n_rows, step, ones=False):
    def body(i, c):
        r0 = pl.multiple_of(i * step, step)
        v = src[pl.ds(r0, step), :].astype(BF16)
        dst[pl.ds(r0, step), :] = _with_ones(v) if ones else v
        return c
    lax.fori_loop(0, n_rows // step, body, 0)


SWA_BAND = (SWA_PREV + 1) * CHUNK
SWA_TQ = 16 * CHUNK


def _swa_prompt_kernel(sink_ref, q_ref, k_ref, v_ref, o_ref, vb_s, *, layer):
    kh = pl.program_id(1)
    qi = pl.program_id(2)
    s_len = k_ref.shape[0]

    @pl.when(qi == 0)
    def _():
        _cast_rows(vb_s, v_ref, s_len, 512, ones=True)

    sinks = [sink_ref[layer, kh * SWA_GROUP + g] for g in range(SWA_GROUP)]
    jchunk = lax.broadcasted_iota(jnp.int32, (1, SWA_BAND), 1) // CHUNK
    for c in range(SWA_TQ // CHUNK):
        cg = qi * (SWA_TQ // CHUNK) + c
        sc = jnp.maximum(cg - SWA_PREV, 0)
        s0 = pl.multiple_of(sc * CHUNK, CHUNK)
        valid = (jchunk + sc) <= cg
        o = _swa_chunk(q_ref[c * CHUNK:(c + 1) * CHUNK, :], k_ref[pl.ds(s0, SWA_BAND), :],
                       vb_s[pl.ds(s0, SWA_BAND), :], sinks, valid)
        o_ref[c * CHUNK:(c + 1) * CHUNK, :] = o.astype(BF16)


def _swa_prompt(z, sinks, layer, n_seq, s_len):
    nq = s_len // SWA_TQ
    gw = SWA_GROUP * HEAD_DIM
    return pl.pallas_call(
        functools.partial(_swa_prompt_kernel, layer=layer),
        grid=(n_seq, SWA_KV_HEADS, nq),
        in_specs=[
            pl.BlockSpec(memory_space=pltpu.SMEM),
            pl.BlockSpec((SWA_TQ, gw), lambda b, k, q: (b * nq + q, Z_QB // gw + k)),
            pl.BlockSpec((s_len, HEAD_DIM), lambda b, k, q: (b, (D_Z + KV_KB) // HEAD_DIM + k)),
            pl.BlockSpec((s_len, HEAD_DIM), lambda b, k, q: (b, (D_Z + KV_VB) // HEAD_DIM + k)),
        ],
        out_specs=pl.BlockSpec((SWA_TQ, gw), lambda b, k, q: (b * nq + q, k)),
        out_shape=jax.ShapeDtypeStruct((n_seq * s_len, BRANCH_WIDTH), BF16),
        scratch_shapes=[pltpu.VMEM((s_len, 2 * HEAD_DIM), BF16)],
        compiler_params=_params(("parallel", "parallel", "arbitrary")),
        name="swa_prompt",
    )(sinks, z, z, z)


def _swa_step_kernel(sink_ref, q_ref, k_ref, v_ref, ck_ref, cv_ref, o_ref, ko_ref, vo_ref, *, layer):
    n_past = ck_ref.shape[0] // SWA_KV_HEADS
    gw = SWA_GROUP * HEAD_DIM
    for kh in range(SWA_KV_HEADS):
        cs = slice(kh * HEAD_DIM, (kh + 1) * HEAD_DIM)
        kn = k_ref[:, cs]
        vn = v_ref[:, cs]
        kfull = jnp.concatenate(
            [ck_ref[pl.ds(kh, n_past, stride=SWA_KV_HEADS), :].astype(BF16), kn.astype(BF16)], axis=0)
        vfull = jnp.concatenate(
            [cv_ref[pl.ds(kh, n_past, stride=SWA_KV_HEADS), :].astype(BF16), vn.astype(BF16)], axis=0)
        sinks = [sink_ref[layer, kh * SWA_GROUP + g] for g in range(SWA_GROUP)]
        o = _swa_chunk(q_ref[:, kh * gw:(kh + 1) * gw], kfull, _with_ones(vfull), sinks, None)
        o_ref[:, kh * gw:(kh + 1) * gw] = o.astype(BF16)
        ko_ref[pl.ds(kh, CHUNK, stride=SWA_KV_HEADS), :] = kn
        vo_ref[pl.ds(kh, CHUNK, stride=SWA_KV_HEADS), :] = vn


def _swa_step(z, zkv, cache_k, cache_v, sinks, layer, n_seq):
    rows_past = cache_k.shape[2]
    rows_new = CHUNK * SWA_KV_HEADS
    cache = lambda: pl.BlockSpec((None, None, rows_past, HEAD_DIM), lambda b: (layer, b, 0, 0))
    new = lambda: pl.BlockSpec((rows_new, HEAD_DIM), lambda b: (b, 0))
    return pl.pallas_call(
        functools.partial(_swa_step_kernel, layer=layer),
        grid=(n_seq,),
        in_specs=[
            pl.BlockSpec(memory_space=pltpu.SMEM),
            pl.BlockSpec((CHUNK, BRANCH_WIDTH), lambda b: (b, Z_QB // BRANCH_WIDTH)),
            pl.BlockSpec((CHUNK, SWA_KVW), lambda b: (b, KV_KB // SWA_KVW)),
            pl.BlockSpec((CHUNK, SWA_KVW), lambda b: (b, KV_VB // SWA_KVW)),
            cache(), cache(),
        ],
        out_specs=[pl.BlockSpec((CHUNK, BRANCH_WIDTH), lambda b: (b, 0)), new(), new()],
        out_shape=[jax.ShapeDtypeStruct((n_seq * CHUNK, BRANCH_WIDTH), BF16),
                   jax.ShapeDtypeStruct((n_seq * rows_new, HEAD_DIM), F32),
                   jax.ShapeDtypeStruct((n_seq * rows_new, HEAD_DIM), F32)],
        compiler_params=_params(("parallel",)),
        name="swa_step",
    )(sinks, z, zkv, zkv, cache_k, cache_v)


def _cb_prompt_kernel(q_ref, k_ref, v_ref, r_ref, o_ref, vb_s, bias_s):
    qi = pl.program_id(2)
    s_len = k_ref.shape[0]

    @pl.when(qi == 0)
    def _():
        _cast_rows(vb_s, v_ref, s_len, 512, ones=True)
        for v in range(CB_VARIANTS):
            bias_s[v] = _cb_bias_block(r_ref[v], v, Q_BLOCK, CB_BAND)

    for sub in range(CB_QSUB):
        blk = qi * CB_QSUB + sub
        sc = jnp.maximum(blk * Q_CHUNKS - CB_PREV, 0)
        s0 = pl.multiple_of(sc * CHUNK, CHUNK)
        rows = slice(sub * Q_BLOCK, (sub + 1) * Q_BLOCK)
        o = _cb_attend(q_ref[rows, :], k_ref[pl.ds(s0, CB_BAND), :], vb_s[pl.ds(s0, CB_BAND), :],
                       bias_s[jnp.minimum(blk, CB_VARIANTS - 1)])
        o_ref[rows, :] = o.astype(BF16)


def _cb_prompt(z, rows, layer, n_seq, s_len):
    tq = Q_BLOCK * CB_QSUB
    nq = s_len // tq
    return pl.pallas_call(
        _cb_prompt_kernel,
        grid=(n_seq, CB_HEADS, nq),
        in_specs=[
            pl.BlockSpec((tq, HEAD_DIM), lambda b, h, q: (b * nq + q, Z_QC // HEAD_DIM + h)),
            pl.BlockSpec((s_len, HEAD_DIM), lambda b, h, q: (b, (D_Z + KV_KC) // HEAD_DIM + h)),
            pl.BlockSpec((s_len, HEAD_DIM), lambda b, h, q: (b, (D_Z + KV_VC) // HEAD_DIM + h)),
            pl.BlockSpec((None, CB_VARIANTS, None, 1, CB_RLEN), lambda b, h, q: (layer, 0, h, 0, 0)),
        ],
        out_specs=pl.BlockSpec((tq, HEAD_DIM), lambda b, h, q: (b * nq + q, h)),
        out_shape=jax.ShapeDtypeStruct((n_seq * s_len, BRANCH_WIDTH), BF16),
        scratch_shapes=[pltpu.VMEM((s_len, 2 * HEAD_DIM), BF16),
                        pltpu.VMEM((CB_VARIANTS, Q_BLOCK, CB_BAND), F32)],
        compiler_params=_params(("parallel", "parallel", "arbitrary")),
        name="cb_prompt",
    )(z, z, z, rows)


def _cb_step_kernel(q_ref, k_ref, v_ref, ck_ref, cv_ref, r_ref, o_ref, ko_ref, vo_ref, bias_s):
    n_past = ck_ref.shape[0] // CB_HEADS
    n_k = n_past + CHUNK

    @pl.when(pl.program_id(0) == 0)
    def _():
        for h in range(CB_HEADS):
            bias_s[h] = _cb_bias_block(r_ref[h], CB_VARIANTS - 1, CHUNK, n_k)

    for h in range(CB_HEADS):
        cs = slice(h * HEAD_DIM, (h + 1) * HEAD_DIM)
        kn = k_ref[:, cs]
        vn = v_ref[:, cs]
        kfull = jnp.concatenate(
            [ck_ref[pl.ds(h, n_past, stride=CB_HEADS), :].astype(BF16), kn.astype(BF16)], axis=0)
        vfull = jnp.concatenate(
            [cv_ref[pl.ds(h, n_past, stride=CB_HEADS), :].astype(BF16), vn.astype(BF16)], axis=0)
        o_ref[:, cs] = _cb_attend(q_ref[:, cs], kfull, _with_ones(vfull), bias_s[h]).astype(BF16)
        ko_ref[pl.ds(h, CHUNK, stride=CB_HEADS), :] = kn
        vo_ref[pl.ds(h, CHUNK, stride=CB_HEADS), :] = vn


def _cb_step(z, zkv, cache_k, cache_v, rows, layer, n_seq):
    rows_past = cache_k.shape[2]
    n_k = rows_past // CB_HEADS + CHUNK
    rows_new = CHUNK * CB_HEADS
    cache = lambda: pl.BlockSpec((None, None, rows_past, HEAD_DIM), lambda b: (layer, b, 0, 0))
    new = lambda: pl.BlockSpec((rows_new, HEAD_DIM), lambda b: (b, 0))
    return pl.pallas_call(
        _cb_step_kernel,
        grid=(n_seq,),
        in_specs=[
            pl.BlockSpec((CHUNK, BRANCH_WIDTH), lambda b: (b, Z_QC // BRANCH_WIDTH)),
            pl.BlockSpec((CHUNK, CB_KVW), lambda b: (b, KV_KC // CB_KVW)),
            pl.BlockSpec((CHUNK, CB_KVW), lambda b: (b, KV_VC // CB_KVW)),
            cache(), cache(),
            pl.BlockSpec((None, None, CB_HEADS, 1, CB_RLEN),
                         lambda b: (layer, CB_VARIANTS - 1, 0, 0, 0)),
        ],
        out_specs=[pl.BlockSpec((CHUNK, BRANCH_WIDTH), lambda b: (b, 0)), new(), new()],
        out_shape=[jax.ShapeDtypeStruct((n_seq * CHUNK, BRANCH_WIDTH), BF16),
                   jax.ShapeDtypeStruct((n_seq * rows_new, HEAD_DIM), F32),
                   jax.ShapeDtypeStruct((n_seq * rows_new, HEAD_DIM), F32)],
        scratch_shapes=[pltpu.VMEM((CB_HEADS, CHUNK, n_k), F32)],
        compiler_params=_params(("arbitrary",)),
        name="cb_step",
    )(z, zkv, zkv, cache_k, cache_v, rows)


def _merge_kernel(x_ref, a_ref, b_ref, c_ref, ga_ref, gb_ref, gc_ref, wb_ref, wo_ref, o_ref):
    mixed = None
    for r, (br, gr) in enumerate(((a_ref, ga_ref), (b_ref, gb_ref), (c_ref, gc_ref))):
        proj = jnp.dot(br[...], wb_ref[r], preferred_element_type=F32)
        gate = jax.nn.sigmoid(gr[...].astype(F32))
        mixed = gate * proj if mixed is None else mixed + gate * proj
    o_ref[...] = x_ref[...] + jnp.dot(mixed.astype(BF16), wo_ref[...], preferred_element_type=F32)


def _merge(x, out_a, out_b, out_c, z, w_branch, w_out, layer, tm):
    m = x.shape[0]
    branch = lambda: pl.BlockSpec((tm, BRANCH_WIDTH), lambda i: (i, 0))
    gate = lambda r: pl.BlockSpec((tm, D_MODEL), lambda i: (i, Z_GATES // D_MODEL + r))
    return pl.pallas_call(
        _merge_kernel,
        grid=(m // tm,),
        in_specs=[
            pl.BlockSpec((tm, D_MODEL), lambda i: (i, 0)),
            branch(), branch(), branch(),
            gate(0), gate(1), gate(2),
            pl.BlockSpec((None, N_BRANCH, BRANCH_WIDTH, D_MODEL), lambda i: (layer, 0, 0, 0),
                         pipeline_mode=pl.Buffered(1)),
            pl.BlockSpec((None, D_MODEL, D_MODEL), lambda i: (layer, 0, 0),
                         pipeline_mode=pl.Buffered(1)),
        ],
        out_specs=pl.BlockSpec((tm, D_MODEL), lambda i: (i, 0)),
        out_shape=jax.ShapeDtypeStruct((m, D_MODEL), F32),
        compiler_params=_params(("parallel",)),
        name="merge",
    )(x, out_a, out_b, out_c, z, z, z, w_branch, w_out)


def _mlp_kernel(*refs, final_norm, jobs):
    n_side = len(jobs.weights) if jobs else 0
    x_ref, g_ref, fg_ref, wu_ref, wd_ref = refs[:5]
    side_in = refs[5:5 + n_side]
    o_ref = refs[5 + n_side]
    side_out = refs[6 + n_side:6 + 2 * n_side]
    hn_ref = refs[-1]
    j = pl.program_id(1)

    @pl.when(j == 0)
    def _():
        xf = x_ref[...]
        hn_ref[...] = _rms(xf, g_ref[...]).astype(BF16)
        o_ref[...] = xf

    h = jnp.dot(hn_ref[...], wu_ref[...], preferred_element_type=F32)
    h = jnp.square(jnp.maximum(h, 0.0)).astype(BF16)
    o_ref[...] += jnp.dot(h, wd_ref[...], preferred_element_type=F32)

    if final_norm:
        @pl.when(j == pl.num_programs(1) - 1)
        def _():
            o_ref[...] = _rms(o_ref[...], fg_ref[...])

    if jobs:
        jobs.run(side_in, side_out)


def _mlp(x, g, final_g, w_up, w_down, layer, tm, tf, final_norm, cast=None):
    m = x.shape[0]
    grid = (m // tm, D_FF // tf)
    jobs = _CastJobs(cast[0], cast[1], grid, cast[2]) if cast else None
    return pl.pallas_call(
        functools.partial(_mlp_kernel, final_norm=final_norm, jobs=jobs),
        grid=grid,
        in_specs=[
            pl.BlockSpec((tm, D_MODEL), lambda i, j: (i, 0)),
            pl.BlockSpec((None, 1, D_MODEL), lambda i, j: (layer, 0, 0)),
            pl.BlockSpec((1, D_MODEL), lambda i, j: (0, 0)),
            pl.BlockSpec((None, D_MODEL, tf), lambda i, j: (0, 0, j)),
            pl.BlockSpec((None, tf, D_MODEL), lambda i, j: (0, j, 0)),
        ] + (jobs.in_specs() if jobs else []),
        out_specs=[pl.BlockSpec((tm, D_MODEL), lambda i, j: (i, 0))] + (jobs.out_specs() if jobs else []),
        out_shape=[jax.ShapeDtypeStruct((m, D_MODEL), F32)] + (jobs.out_shape() if jobs else []),
        scratch_shapes=[pltpu.VMEM((tm, D_MODEL), BF16)],
        compiler_params=_params(("arbitrary", "arbitrary")),
        name="mlp",
    )(x, g, final_g, w_up, w_down, *(cast[0] if cast else ()))


def _permute_w_in(w_in):
    seg = lambda o, w: w_in[..., o:o + w]
    return jnp.concatenate([
        seg(_O_AX, D_RNN), seg(_O_AG, D_RNN), seg(_O_QB, BRANCH_WIDTH), seg(_O_QC, BRANCH_WIDTH),
        seg(_O_GATES, N_BRANCH * D_MODEL),
        seg(_O_KC, CB_KVW), seg(_O_VC, CB_KVW), seg(_O_KB, SWA_KVW), seg(_O_VB, SWA_KVW)],
        axis=-1).astype(BF16)


def _cb_bias_rows(table):
    m = jnp.arange(CB_RLEN)
    rel = jnp.where(m < CB_BAND, m, m - CB_RLEN)
    out = []
    for v in range(CB_VARIANTS):
        c0 = v * Q_CHUNKS
        sc = max(c0 - CB_PREV, 0)
        d = (c0 - sc) * CHUNK - rel
        idx = jnp.clip(d, -REL_CLIP, REL_CLIP) + REL_CLIP
        out.append(table.astype(F32)[:, :, idx])
    return jnp.stack(out, axis=1)[:, :, :, None, :]


def _pick_tm(m, want):
    tm = min(want, m)
    while m % tm:
        tm //= 2
    return tm


def _layer(x, n_seq, t_len, conv_buf, h0, state_layer, caches, layer, p, w_in_l, w_layer):
    m = x.shape[0]
    depth = p["w_up"].shape[0]
    tm_in = _pick_tm(m, 1024)
    if w_layer is None:
        z, wu, wd, wb, wo = _in_proj(
            x, p["norm1"], w_in_l, layer, tm_in, "bf16",
            cast=((p["w_up"], p["w_down"], p["w_branch"], p["w_out"]), layer))
        w_layer = (wu[None], wd[None], wb.reshape(1, N_BRANCH, BRANCH_WIDTH, D_MODEL), wo[None])
        cast_next = ((p["w_in"],), layer + 1, _permute_w_in) if layer + 1 < depth else None
    else:
        z, zkv = _in_proj(x, p["norm1"], w_in_l, layer, tm_in, "dual")
        cast_next = None
    w_up_l, w_down_l, w_branch_l, w_out_l = w_layer
    out_a, conv_o, h_o = _lru(z, conv_buf, h0, state_layer, p["conv_w"], p["conv_b"], p["wa"], p["ba"],
                              p["wx"], p["bx"], p["lam"], layer, n_seq, t_len, _pick_tm(t_len, 512))
    if caches is None:
        out_b = _swa_prompt(z, p["sinks"], layer, n_seq, t_len)
        out_c = _cb_prompt(z, p["cb_rows"], layer, n_seq, t_len)
        n_tail = CB_REACH
        kv3 = _in_proj(x, p["norm1"], w_in_l, layer, n_tail, "kv_f32", n_tiles=n_seq,
                       row_tile=lambda i: (i + 1) * (t_len // n_tail) - 1)[0].reshape(n_seq, n_tail, D_KV)
        kv = (kv3[:, n_tail - SWA_WINDOW:, KV_KB:KV_KB + SWA_KVW],
              kv3[:, n_tail - SWA_WINDOW:, KV_VB:KV_VB + SWA_KVW],
              kv3[:, :, KV_KC:KV_KC + CB_KVW], kv3[:, :, KV_VC:KV_VC + CB_KVW])
    else:
        ck_b, cv_b, ck_c, cv_c = caches
        out_b, kb, vb = _swa_step(z, zkv, ck_b, cv_b, p["sinks"], layer, n_seq)
        out_c, kc, vc = _cb_step(z, zkv, ck_c, cv_c, p["cb_rows"], layer, n_seq)
        kv = (kb, vb, kc, vc)
    x = _merge(x, out_a, out_b, out_c, z, w_branch_l, w_out_l, 0, _pick_tm(m, 256))
    res = _mlp(x, p["norm2"], p["final_g"], w_up_l, w_down_l, layer, _pick_tm(m, 512), 1024,
               final_norm=layer == depth - 1, cast=cast_next)
    w_in_next = res[1][None] if cast_next else None
    return res[0], conv_o, h_o[:, 0], kv, w_layer, w_in_next


def kernel(x_prompt, x_sample, state_conv, state_lru, cache_swa_k, cache_swa_v, cache_cb_k, cache_cb_v, norm1_g, w_in, conv_w, conv_b, lru_wa, lru_ba, lru_wx, lru_bx, lru_lambda, attn_sinks, rel_bias_table, w_branch, w_out, norm2_g, w_up, w_down, final_g):
    depth = w_in.shape[0]
    nb, s_len, _ = x_prompt.shape
    db, d_len, _ = x_sample.shape
    assert d_len == CHUNK and s_len % (Q_BLOCK * CB_QSUB) == 0 and s_len % SWA_TQ == 0
    assert s_len % CB_REACH == 0
    assert cache_swa_k.shape[2] == SWA_WINDOW and cache_cb_k.shape[2] == CB_REACH

    row = lambda v: v.reshape(depth, 1, -1)
    p = {
        "norm1": row(norm1_g), "norm2": row(norm2_g), "final_g": final_g.reshape(1, D_MODEL),
        "w_in": w_in, "w_up": w_up, "w_down": w_down,
        "w_branch": w_branch.reshape(depth, N_BRANCH * BRANCH_WIDTH, D_MODEL), "w_out": w_out,
        "conv_w": conv_w, "conv_b": row(conv_b),
        "wa": lru_wa.astype(BF16), "ba": row(lru_ba), "wx": lru_wx.astype(BF16), "bx": row(lru_bx),
        "lam": row(lru_lambda), "sinks": attn_sinks,
        "cb_rows": _cb_bias_rows(rel_bias_table),
    }
    w_in_l = _permute_w_in(w_in[0])[None]
    caches = (cache_swa_k.reshape(depth, db, SWA_WINDOW * SWA_KV_HEADS, HEAD_DIM),
              cache_swa_v.reshape(depth, db, SWA_WINDOW * SWA_KV_HEADS, HEAD_DIM),
              cache_cb_k.reshape(depth, db, CB_REACH * CB_HEADS, HEAD_DIM),
              cache_cb_v.reshape(depth, db, CB_REACH * CB_HEADS, HEAD_DIM))

    xp = x_prompt.reshape(nb * s_len, D_MODEL)
    xs = x_sample.reshape(db * d_len, D_MODEL)
    zero_conv = jnp.zeros((1, nb, CONV_W - 1, D_RNN), F32)
    zero_h = jnp.zeros((1, nb, 1, D_RNN), F32)
    h0_s = state_lru.reshape(depth, db, 1, D_RNN)
    heads = (SWA_KV_HEADS, SWA_KV_HEADS, CB_HEADS, CB_HEADS)
    outs = [[] for _ in range(12)]
    for l in range(depth):
        xp, conv_p, h_p, kv_p, w_layer, w_in_next = _layer(xp, nb, s_len, zero_conv, zero_h, 0, None, l, p,
                                                           w_in_l, None)
        xs, conv_s, h_s, kv_s, _, _ = _layer(xs, db, d_len, state_conv, h0_s, l, caches, l, p,
                                             w_in_l, w_layer)
        w_in_l = w_in_next
        outs[0].append(conv_p)
        outs[1].append(h_p)
        outs[6].append(conv_s)
        outs[7].append(h_s)
        for n in range(4):
            outs[2 + n].append(kv_p[n].reshape(nb, -1, heads[n], HEAD_DIM))
            outs[8 + n].append(kv_s[n].reshape(db, d_len, heads[n], HEAD_DIM))

    y_prompt = xp.reshape(nb, s_len, D_MODEL)
    y_sample = xs.reshape(db, d_len, D_MODEL)
    return (y_prompt, y_sample) + tuple(jnp.stack(o) for o in outs)
```

```python
import functools

import jax
import jax.numpy as jnp
from jax import lax
from jax.experimental import pallas as pl
from jax.experimental.pallas import tpu as pltpu

F32 = jnp.float32
BF16 = jnp.bfloat16

D_MODEL = 2048
CHUNK = 64
HEAD_DIM = 128
BRANCH_WIDTH = D_MODEL // 2
N_BRANCH = 3
D_RNN = BRANCH_WIDTH
LRU_BLOCKS = 8
LRU_BLOCK = D_RNN // LRU_BLOCKS
CONV_W = 4
LRU_C = 8.0
SWA_HEADS = BRANCH_WIDTH // HEAD_DIM
SWA_KV_HEADS = 2
SWA_GROUP = SWA_HEADS // SWA_KV_HEADS
SWA_WINDOW = 128
SWA_PREV = SWA_WINDOW // CHUNK
CB_HEADS = BRANCH_WIDTH // HEAD_DIM
CB_PREV = 8
CB_REACH = CB_PREV * CHUNK
REL_CLIP = 128
D_FF = 4 * D_MODEL
EPS = 1e-6
NEG = -1e30
ATTN_SCALE = HEAD_DIM ** -0.5
LOG2E = 1.4426950408889634

_O_AX, _O_AG, _O_QB, _O_KB, _O_VB, _O_QC, _O_KC, _O_VC, _O_GATES = (
    0, 1024, 2048, 3072, 3328, 3584, 4608, 5632, 6656)
D_IN = _O_GATES + N_BRANCH * D_MODEL
SWA_KVW = SWA_KV_HEADS * HEAD_DIM
CB_KVW = CB_HEADS * HEAD_DIM
D_KV = 2 * SWA_KVW + 2 * CB_KVW
D_Z = D_IN - D_KV
KV_KC, KV_VC, KV_KB, KV_VB = 0, 1024, 2048, 2304
Z_AX, Z_AG, Z_QB, Z_QC, Z_GATES = 0, 1024, 2048, 3072, 4096

VMEM_LIMIT_BYTES = 56 * 1024 * 1024

Q_BLOCK = 4 * CHUNK
Q_CHUNKS = Q_BLOCK // CHUNK
CB_BAND = (CB_PREV + Q_CHUNKS) * CHUNK
CB_QSUB = 8
CB_RLEN = 1024
assert CB_RLEN >= Q_BLOCK + CB_BAND - 1
CB_VARIANTS = 3


def _params(semantics):
    return pltpu.CompilerParams(dimension_semantics=semantics, vmem_limit_bytes=VMEM_LIMIT_BYTES)


def _rms(xf, g):
    return xf * lax.rsqrt(jnp.mean(xf * xf, axis=-1, keepdims=True) + EPS) * g


IN_TN = 1280
IN_KV_TILES = D_KV // IN_TN
IN_Z_TILES = D_Z // IN_TN


class _CastJobs:
    def __init__(self, weights, layer, grid, transform=None, col_tiles=None):
        self.weights, self.layer, self.transform = weights, layer, transform
        self.col_tiles = col_tiles or [1] * len(weights)
        self.inner = grid[1]
        n_max = 1
        while n_max * 2 <= min(grid[0] * grid[1], MAX_CAST_STEPS):
            n_max *= 2
        self.n = []
        for w in weights:
            n = n_max
            while w.shape[1] % (n * BF16_ROWS):
                n //= 2
            self.n.append(n)

    def _slab(self, n):
        return lambda i, j: jnp.minimum(i * self.inner + j, n - 1)

    def in_specs(self):
        return [pl.BlockSpec((None, w.shape[1] // n, w.shape[2]),
                             lambda i, j, slab=self._slab(n): (self.layer, slab(i, j), 0))
                for w, n in zip(self.weights, self.n)]

    def out_specs(self):
        specs = []
        for w, n, t in zip(self.weights, self.n, self.col_tiles):
            rows, cols = w.shape[1] // n, w.shape[2]
            if t == 1:
                specs.append(pl.BlockSpec((rows, cols), lambda i, j, slab=self._slab(n): (slab(i, j), 0)))
            else:
                specs.append(pl.BlockSpec((t, rows, cols // t),
                                          lambda i, j, slab=self._slab(n): (0, slab(i, j), 0)))
        return specs

    def out_shape(self):
        return [jax.ShapeDtypeStruct(w.shape[1:] if t == 1 else (t, w.shape[1], w.shape[2] // t), BF16)
                for w, t in zip(self.weights, self.col_tiles)]

    def run(self, src_refs, dst_refs):
        step = pl.program_id(0) * self.inner + pl.program_id(1)
        for n in sorted(set(self.n)):
            @pl.when(step < n)
            def _():
                for s, d, n_w, t in zip(src_refs, dst_refs, self.n, self.col_tiles):
                    if n_w != n:
                        continue
                    v = s[...]
                    v = (self.transform(v) if self.transform else v).astype(BF16)
                    if t == 1:
                        d[...] = v
                    else:
                        ct = v.shape[1] // t
                        for c in range(t):
                            d[c] = v[:, c * ct:(c + 1) * ct]


MAX_CAST_STEPS = 128
BF16_ROWS = 16


def _in_proj_kernel(*refs, jobs, mode):
    n_side = len(jobs.weights) if jobs else 0
    n_out = 2 if mode == "dual" else 1
    x_ref, g_ref, w_ref = refs[:3]
    side_in = refs[3:3 + n_side]
    outs = refs[3 + n_side:3 + n_side + n_out]
    side_out = refs[3 + n_side + n_out:3 + 2 * n_side + n_out]
    xn_ref = refs[-1]
    j = pl.program_id(1)

    @pl.when(j == 0)
    def _():
        xn_ref[...] = _rms(x_ref[...], g_ref[...]).astype(BF16)

    dot = lambda: jnp.dot(xn_ref[...], w_ref[...], preferred_element_type=F32)
    if mode == "bf16":
        outs[0][...] = dot().astype(BF16)
    elif mode == "kv_f32":
        outs[0][...] = dot()
    else:
        z_ref, zkv_ref = outs

        @pl.when(j < IN_Z_TILES)
        def _():
            z_ref[...] = dot().astype(BF16)

        @pl.when(j >= IN_Z_TILES)
        def _():
            zkv_ref[...] = dot()

    if jobs:
        jobs.run(side_in, side_out)


def _in_proj(x, g, w, layer, tm, mode, cast=None, row_tile=None, n_tiles=None):
    m = x.shape[0]
    col0 = 0
    if mode == "kv_f32":
        grid = (n_tiles, IN_KV_TILES)
        col0 = IN_Z_TILES
        x_map = lambda i, j: (row_tile(i), 0)
        out_specs = [pl.BlockSpec((tm, IN_TN), lambda i, j: (i, j))]
        out_shape = [jax.ShapeDtypeStruct((n_tiles * tm, D_KV), F32)]
    else:
        grid = (m // tm, D_IN // IN_TN)
        x_map = lambda i, j: (i, 0)
        if mode == "bf16":
            out_specs = [pl.BlockSpec((tm, IN_TN), lambda i, j: (i, j))]
            out_shape = [jax.ShapeDtypeStruct((m, D_IN), BF16)]
        else:
            out_specs = [pl.BlockSpec((tm, IN_TN), lambda i, j: (i, jnp.minimum(j, IN_Z_TILES - 1))),
                         pl.BlockSpec((tm, IN_TN), lambda i, j: (i, jnp.maximum(j - IN_Z_TILES, 0)))]
            out_shape = [jax.ShapeDtypeStruct((m, D_Z), BF16), jax.ShapeDtypeStruct((m, D_KV), F32)]
    jobs = _CastJobs(cast[0], cast[1], grid, col_tiles=cast[2]) if cast else None
    return pl.pallas_call(
        functools.partial(_in_proj_kernel, jobs=jobs, mode=mode),
        grid=grid,
        in_specs=[
            pl.BlockSpec((tm, D_MODEL), x_map),
            pl.BlockSpec((None, 1, D_MODEL), lambda i, j: (layer, 0, 0)),
            pl.BlockSpec((None, D_MODEL, IN_TN), lambda i, j: (col0 + j, 0, 0)),
        ] + (jobs.in_specs() if jobs else []),
        out_specs=out_specs + (jobs.out_specs() if jobs else []),
        out_shape=out_shape + (jobs.out_shape() if jobs else []),
        scratch_shapes=[pltpu.VMEM((tm, D_MODEL), BF16)],
        compiler_params=_params(("arbitrary", "arbitrary")),
        name="in_proj",
    )(x, g, w, *(cast[0] if cast else ()))


_XPAD = 8
LANES = 128
_SEGS = 8
_SEG_LEN = 4


def _sigmoid(x):
    return 0.5 * (jnp.tanh(0.5 * x) + 1.0)


def _lru_kernel(ax_ref, ag_ref, cbuf_ref, h0_ref, cw_ref, cb_ref, wa_ref, ba_ref, wx_ref, bx_ref,
                lam_ref, out_ref, convo_ref, ho_ref, xbuf, a_s, b_s, h_s):
    t = pl.program_id(1)
    nt = pl.num_programs(1)
    tt = ax_ref.shape[0]

    @pl.when(t == 0)
    def _():
        xbuf[...] = jnp.zeros_like(xbuf)
        xbuf[_XPAD - (CONV_W - 1):, :] = cbuf_ref[...]
        h_s[...] = h0_ref[...]

    x = ax_ref[...].astype(F32)
    xe = jnp.concatenate([xbuf[...], x], axis=0)
    acc = xe * cw_ref[0:1, :]
    for k in range(1, CONV_W):
        acc = xe * cw_ref[k:k + 1, :] + pltpu.roll(acc, 1, 0)
    u = cb_ref[...] + acc[_XPAD:, :]
    tail = x[tt - (CONV_W - 1):, :]
    xbuf[...] = x[tt - _XPAD:, :]

    ub = u.astype(BF16)
    r_parts, i_parts = [], []
    for n in range(LRU_BLOCKS):
        un = ub[:, n * LRU_BLOCK:(n + 1) * LRU_BLOCK]
        r_parts.append(jnp.dot(un, wa_ref[n], preferred_element_type=F32))
        i_parts.append(jnp.dot(un, wx_ref[n], preferred_element_type=F32))
    r = _sigmoid(jnp.concatenate(r_parts, axis=1) + ba_ref[...])
    i = _sigmoid(jnp.concatenate(i_parts, axis=1) + bx_ref[...])
    log_a = -LRU_C * r * jax.nn.softplus(-lam_ref[...])
    a = jnp.exp(log_a)
    y = 1.0 - a * a
    b = jnp.where(y > 0.0, y * lax.rsqrt(y), 0.0) * (i * u)
    n_lg = D_RNN // LANES
    for lg in range(n_lg):
        a_s[lg] = a[:, lg * LANES:(lg + 1) * LANES]
        b_s[lg] = b[:, lg * LANES:(lg + 1) * LANES]

    row = lax.broadcasted_iota(jnp.int32, (_SEGS, LANES), 0)
    sub = _SEGS * _SEG_LEN

    def body(sb, h):
        r0 = pl.multiple_of(sb * sub, sub)
        step = lambda j: pl.ds(r0 + j, _SEGS, stride=_SEG_LEN)
        h_next = []
        for lg in range(n_lg):
            a_g, b_g = a_s.at[lg], b_s.at[lg]
            h_g = h[:, lg * LANES:(lg + 1) * LANES]
            acs, bcs = [a_g[step(0), :]], [b_g[step(0), :]]
            for j in range(1, _SEG_LEN):
                aj = a_g[step(j), :]
                bcs.append(aj * bcs[-1] + b_g[step(j), :])
                acs.append(aj * acs[-1])
            at, bt = acs[-1], bcs[-1]
            for s in (1, 2, 4):
                m = row >= s
                bt_new = jnp.where(m, at * pltpu.roll(bt, s, 0) + bt, bt)
                at = jnp.where(m, at * pltpu.roll(at, s, 0), at)
                bt = bt_new
            after = at * h_g + bt
            entry = jnp.where(row >= 1, pltpu.roll(after, 1, 0), h_g)
            for j in range(_SEG_LEN):
                b_g[step(j), :] = acs[j] * entry + bcs[j]
            h_next.append(after[_SEGS - 1:_SEGS, :])
        return jnp.concatenate(h_next, axis=1)

    h = lax.fori_loop(0, tt // sub, body, h_s[...])
    h_s[...] = h
    hs = jnp.concatenate([b_s[lg] for lg in range(n_lg)], axis=1)
    out_ref[...] = (hs * jax.nn.gelu(ag_ref[...].astype(F32))).astype(BF16)

    @pl.when(t == nt - 1)
    def _():
        convo_ref[...] = tail
        ho_ref[...] = h


def _lru(z, conv_buf, h0, state_layer, cw, cb, wa, ba, wx, bx, lam, layer, n_seq, t_len, tt):
    nt = t_len // tt
    row = lambda b, t: b * nt + t
    vec = lambda: pl.BlockSpec((None, 1, D_RNN), lambda b, t: (layer, 0, 0))
    blk = lambda: pl.BlockSpec((None, LRU_BLOCKS, LRU_BLOCK, LRU_BLOCK), lambda b, t: (layer, 0, 0, 0))
    return pl.pallas_call(
        _lru_kernel,
        grid=(n_seq, nt),
        in_specs=[
            pl.BlockSpec((tt, D_RNN), lambda b, t: (row(b, t), Z_AX // D_RNN)),
            pl.BlockSpec((tt, D_RNN), lambda b, t: (row(b, t), Z_AG // D_RNN)),
            pl.BlockSpec((None, None, CONV_W - 1, D_RNN), lambda b, t: (state_layer, b, 0, 0)),
            pl.BlockSpec((None, None, 1, D_RNN), lambda b, t: (state_layer, b, 0, 0)),
            pl.BlockSpec((None, CONV_W, D_RNN), lambda b, t: (layer, 0, 0)),
            vec(), blk(), vec(), blk(), vec(), vec(),
        ],
        out_specs=[
            pl.BlockSpec((tt, D_RNN), lambda b, t: (row(b, t), 0)),
            pl.BlockSpec((None, CONV_W - 1, D_RNN), lambda b, t: (b, 0, 0)),
            pl.BlockSpec((None, 1, D_RNN), lambda b, t: (b, 0, 0)),
        ],
        out_shape=[
            jax.ShapeDtypeStruct((n_seq * t_len, D_RNN), BF16),
            jax.ShapeDtypeStruct((n_seq, CONV_W - 1, D_RNN), F32),
            jax.ShapeDtypeStruct((n_seq, 1, D_RNN), F32),
        ],
        scratch_shapes=[
            pltpu.VMEM((_XPAD, D_RNN), F32),
            pltpu.VMEM((D_RNN // LANES, tt, LANES), F32),
            pltpu.VMEM((D_RNN // LANES, tt, LANES), F32),
            pltpu.VMEM((1, D_RNN), F32),
        ],
        compiler_params=_params(("parallel", "arbitrary")),
        name="lru",
    )(z, z, conv_buf, h0, cw, cb, wa, ba, wx, bx, lam)


def _dot_nt(a, b):
    return lax.dot_general(a, b, (((1,), (1,)), ((), ())), preferred_element_type=F32)


def _with_ones(v):
    return jnp.concatenate([v, jnp.ones_like(v)], axis=1)


def _swa_chunk(q, kband, vext, sinks, valid):
    qst = jnp.concatenate([q[:, g * HEAD_DIM:(g + 1) * HEAD_DIM] for g in range(SWA_GROUP)], axis=0)
    s = _dot_nt(qst, kband) * (ATTN_SCALE * LOG2E)
    if valid is not None:
        s = jnp.where(valid, s, NEG)
    es, sink_e = [], []
    for g in range(SWA_GROUP):
        sg = s[g * CHUNK:(g + 1) * CHUNK, :]
        sink2 = sinks[g] * LOG2E
        m = jnp.maximum(jnp.max(sg, axis=-1, keepdims=True), sink2)
        es.append(jnp.exp2(sg - m).astype(BF16))
        sink_e.append(jnp.exp2(sink2 - m))
    r = jnp.dot(jnp.concatenate(es, axis=0), vext, preferred_element_type=F32)
    outs = []
    for g in range(SWA_GROUP):
        rg = r[g * CHUNK:(g + 1) * CHUNK, :]
        outs.append(rg[:, :HEAD_DIM] * (1.0 / (rg[:, HEAD_DIM:] + sink_e[g])))
    return jnp.concatenate(outs, axis=1)


def _cb_attend(q, kband, vext, bias2):
    s = _dot_nt(q, kband) * (ATTN_SCALE * LOG2E) + bias2
    m = jnp.max(s, axis=-1, keepdims=True)
    r = jnp.dot(jnp.exp2(s - m).astype(BF16), vext, preferred_element_type=F32)
    return r[:, :HEAD_DIM] * (1.0 / r[:, HEAD_DIM:])


def _cb_bias_block(r, variant, n_rows, n_cols):
    t = pltpu.roll(jnp.broadcast_to(r, (n_rows, CB_RLEN)), 0, 1, stride=1, stride_axis=0)[:, :n_cols]
    c0 = variant * Q_CHUNKS
    sc = max(c0 - CB_PREV, 0)
    qc = c0 + lax.broadcasted_iota(jnp.int32, (n_rows, n_cols), 0) // CHUNK
    kc = sc + lax.broadcasted_iota(jnp.int32, (n_rows, n_cols), 1) // CHUNK
    return jnp.where(kc <= qc, jnp.where(kc >= qc - CB_PREV, t * LOG2E, NEG), NEG)


def _cast_rows(dst, src, n_rows, step, ones=False):
    def body(i, c):
        r0 = pl.multiple_of(i * step, step)
        v = src[pl.ds(r0, step), :].astype(BF16)
        dst[pl.ds(r0, step), :] = _with_ones(v) if ones else v
        return c
    lax.fori_loop(0, n_rows // step, body, 0)


SWA_BAND = (SWA_PREV + 1) * CHUNK
SWA_TQ = 16 * CHUNK


def _swa_prompt_kernel(sink_ref, q_ref, k_ref, v_ref, o_ref, vb_s, *, layer):
    kh = pl.program_id(1)
    qi = pl.program_id(2)
    s_len = k_ref.shape[0]

    @pl.when(qi == 0)
    def _():
        _cast_rows(vb_s, v_ref, s_len, 512, ones=True)

    sinks = [sink_ref[layer, kh * SWA_GROUP + g] for g in range(SWA_GROUP)]
    jchunk = lax.broadcasted_iota(jnp.int32, (1, SWA_BAND), 1) // CHUNK
    for c in range(SWA_TQ // CHUNK):
        cg = qi * (SWA_TQ // CHUNK) + c
        sc = jnp.maximum(cg - SWA_PREV, 0)
        s0 = pl.multiple_of(sc * CHUNK, CHUNK)
        valid = (jchunk + sc) <= cg
        o = _swa_chunk(q_ref[c * CHUNK:(c + 1) * CHUNK, :], k_ref[pl.ds(s0, SWA_BAND), :],
                       vb_s[pl.ds(s0, SWA_BAND), :], sinks, valid)
        o_ref[c * CHUNK:(c + 1) * CHUNK, :] = o.astype(BF16)


def _swa_prompt(z, sinks, layer, n_seq, s_len):
    nq = s_len // SWA_TQ
    gw = SWA_GROUP * HEAD_DIM
    return pl.pallas_call(
        functools.partial(_swa_prompt_kernel, layer=layer),
        grid=(n_seq, SWA_KV_HEADS, nq),
        in_specs=[
            pl.BlockSpec(memory_space=pltpu.SMEM),
            pl.BlockSpec((SWA_TQ, gw), lambda b, k, q: (b * nq + q, Z_QB // gw + k)),
            pl.BlockSpec((s_len, HEAD_DIM), lambda b, k, q: (b, (D_Z + KV_KB) // HEAD_DIM + k)),
            pl.BlockSpec((s_len, HEAD_DIM), lambda b, k, q: (b, (D_Z + KV_VB) // HEAD_DIM + k)),
        ],
        out_specs=pl.BlockSpec((SWA_TQ, gw), lambda b, k, q: (b * nq + q, k)),
        out_shape=jax.ShapeDtypeStruct((n_seq * s_len, BRANCH_WIDTH), BF16),
        scratch_shapes=[pltpu.VMEM((s_len, 2 * HEAD_DIM), BF16)],
        compiler_params=_params(("parallel", "parallel", "arbitrary")),
        name="swa_prompt",
    )(sinks, z, z, z)


def _swa_step_kernel(sink_ref, q_ref, k_ref, v_ref, ck_ref, cv_ref, o_ref, ko_ref, vo_ref, *, layer):
    n_past = ck_ref.shape[0] // SWA_KV_HEADS
    gw = SWA_GROUP * HEAD_DIM
    for kh in range(SWA_KV_HEADS):
        cs = slice(kh * HEAD_DIM, (kh + 1) * HEAD_DIM)
        kn = k_ref[:, cs]
        vn = v_ref[:, cs]
        kfull = jnp.concatenate(
            [ck_ref[pl.ds(kh, n_past, stride=SWA_KV_HEADS), :].astype(BF16), kn.astype(BF16)], axis=0)
        vfull = jnp.concatenate(
            [cv_ref[pl.ds(kh, n_past, stride=SWA_KV_HEADS), :].astype(BF16), vn.astype(BF16)], axis=0)
        sinks = [sink_ref[layer, kh * SWA_GROUP + g] for g in range(SWA_GROUP)]
        o = _swa_chunk(q_ref[:, kh * gw:(kh + 1) * gw], kfull, _with_ones(vfull), sinks, None)
        o_ref[:, kh * gw:(kh + 1) * gw] = o.astype(BF16)
        ko_ref[pl.ds(kh, CHUNK, stride=SWA_KV_HEADS), :] = kn
        vo_ref[pl.ds(kh, CHUNK, stride=SWA_KV_HEADS), :] = vn


def _swa_step(z, zkv, cache_k, cache_v, sinks, layer, n_seq):
    rows_past = cache_k.shape[2]
    rows_new = CHUNK * SWA_KV_HEADS
    cache = lambda: pl.BlockSpec((None, None, rows_past, HEAD_DIM), lambda b: (layer, b, 0, 0))
    new = lambda: pl.BlockSpec((rows_new, HEAD_DIM), lambda b: (b, 0))
    return pl.pallas_call(
        functools.partial(_swa_step_kernel, layer=layer),
        grid=(n_seq,),
        in_specs=[
            pl.BlockSpec(memory_space=pltpu.SMEM),
            pl.BlockSpec((CHUNK, BRANCH_WIDTH), lambda b: (b, Z_QB // BRANCH_WIDTH)),
            pl.BlockSpec((CHUNK, SWA_KVW), lambda b: (b, KV_KB // SWA_KVW)),
            pl.BlockSpec((CHUNK, SWA_KVW), lambda b: (b, KV_VB // SWA_KVW)),
            cache(), cache(),
        ],
        out_specs=[pl.BlockSpec((CHUNK, BRANCH_WIDTH), lambda b: (b, 0)), new(), new()],
        out_shape=[jax.ShapeDtypeStruct((n_seq * CHUNK, BRANCH_WIDTH), BF16),
                   jax.ShapeDtypeStruct((n_seq * rows_new, HEAD_DIM), F32),
                   jax.ShapeDtypeStruct((n_seq * rows_new, HEAD_DIM), F32)],
        compiler_params=_params(("parallel",)),
        name="swa_step",
    )(sinks, z, zkv, zkv, cache_k, cache_v)


def _cb_prompt_kernel(q_ref, k_ref, v_ref, r_ref, o_ref, vb_s, bias_s):
    qi = pl.program_id(2)
    s_len = k_ref.shape[0]

    @pl.when(qi == 0)
    def _():
        _cast_rows(vb_s, v_ref, s_len, 512, ones=True)
        for v in range(CB_VARIANTS):
            bias_s[v] = _cb_bias_block(r_ref[v], v, Q_BLOCK, CB_BAND)

    for sub in range(CB_QSUB):
        blk = qi * CB_QSUB + sub
        sc = jnp.maximum(blk * Q_CHUNKS - CB_PREV, 0)
        s0 = pl.multiple_of(sc * CHUNK, CHUNK)
        rows = slice(sub * Q_BLOCK, (sub + 1) * Q_BLOCK)
        o = _cb_attend(q_ref[rows, :], k_ref[pl.ds(s0, CB_BAND), :], vb_s[pl.ds(s0, CB_BAND), :],
                       bias_s[jnp.minimum(blk, CB_VARIANTS - 1)])
        o_ref[rows, :] = o.astype(BF16)


def _cb_prompt(z, rows, layer, n_seq, s_len):
    tq = Q_BLOCK * CB_QSUB
    nq = s_len // tq
    return pl.pallas_call(
        _cb_prompt_kernel,
        grid=(n_seq, CB_HEADS, nq),
        in_specs=[
            pl.BlockSpec((tq, HEAD_DIM), lambda b, h, q: (b * nq + q, Z_QC // HEAD_DIM + h)),
            pl.BlockSpec((s_len, HEAD_DIM), lambda b, h, q: (b, (D_Z + KV_KC) // HEAD_DIM + h)),
            pl.BlockSpec((s_len, HEAD_DIM), lambda b, h, q: (b, (D_Z + KV_VC) // HEAD_DIM + h)),
            pl.BlockSpec((None, CB_VARIANTS, None, 1, CB_RLEN), lambda b, h, q: (layer, 0, h, 0, 0)),
        ],
        out_specs=pl.BlockSpec((tq, HEAD_DIM), lambda b, h, q: (b * nq + q, h)),
        out_shape=jax.ShapeDtypeStruct((n_seq * s_len, BRANCH_WIDTH), BF16),
        scratch_shapes=[pltpu.VMEM((s_len, 2 * HEAD_DIM), BF16),
                        pltpu.VMEM((CB_VARIANTS, Q_BLOCK, CB_BAND), F32)],
        compiler_params=_params(("parallel", "parallel", "arbitrary")),
        name="cb_prompt",
    )(z, z, z, rows)


def _cb_step_kernel(q_ref, k_ref, v_ref, ck_ref, cv_ref, r_ref, o_ref, ko_ref, vo_ref, bias_s):
    n_past = ck_ref.shape[0] // CB_HEADS
    n_k = n_past + CHUNK

    @pl.when(pl.program_id(0) == 0)
    def _():
        for h in range(CB_HEADS):
            bias_s[h] = _cb_bias_block(r_ref[h], CB_VARIANTS - 1, CHUNK, n_k)

    for h in range(CB_HEADS):
        cs = slice(h * HEAD_DIM, (h + 1) * HEAD_DIM)
        kn = k_ref[:, cs]
        vn = v_ref[:, cs]
        kfull = jnp.concatenate(
            [ck_ref[pl.ds(h, n_past, stride=CB_HEADS), :].astype(BF16), kn.astype(BF16)], axis=0)
        vfull = jnp.concatenate(
            [cv_ref[pl.ds(h, n_past, stride=CB_HEADS), :].astype(BF16), vn.astype(BF16)], axis=0)
        o_ref[:, cs] = _cb_attend(q_ref[:, cs], kfull, _with_ones(vfull), bias_s[h]).astype(BF16)
        ko_ref[pl.ds(h, CHUNK, stride=CB_HEADS), :] = kn
        vo_ref[pl.ds(h, CHUNK, stride=CB_HEADS), :] = vn


def _cb_step(z, zkv, cache_k, cache_v, rows, layer, n_seq):
    rows_past = cache_k.shape[2]
    n_k = rows_past // CB_HEADS + CHUNK
    rows_new = CHUNK * CB_HEADS
    cache = lambda: pl.BlockSpec((None, None, rows_past, HEAD_DIM), lambda b: (layer, b, 0, 0))
    new = lambda: pl.BlockSpec((rows_new, HEAD_DIM), lambda b: (b, 0))
    return pl.pallas_call(
        _cb_step_kernel,
        grid=(n_seq,),
        in_specs=[
            pl.BlockSpec((CHUNK, BRANCH_WIDTH), lambda b: (b, Z_QC // BRANCH_WIDTH)),
            pl.BlockSpec((CHUNK, CB_KVW), lambda b: (b, KV_KC // CB_KVW)),
            pl.BlockSpec((CHUNK, CB_KVW), lambda b: (b, KV_VC // CB_KVW)),
            cache(), cache(),
            pl.BlockSpec((None, None, CB_HEADS, 1, CB_RLEN),
                         lambda b: (layer, CB_VARIANTS - 1, 0, 0, 0)),
        ],
        out_specs=[pl.BlockSpec((CHUNK, BRANCH_WIDTH), lambda b: (b, 0)), new(), new()],
        out_shape=[jax.ShapeDtypeStruct((n_seq * CHUNK, BRANCH_WIDTH), BF16),
                   jax.ShapeDtypeStruct((n_seq * rows_new, HEAD_DIM), F32),
                   jax.ShapeDtypeStruct((n_seq * rows_new, HEAD_DIM), F32)],
        scratch_shapes=[pltpu.VMEM((CB_HEADS, CHUNK, n_k), F32)],
        compiler_params=_params(("arbitrary",)),
        name="cb_step",
    )(z, zkv, zkv, cache_k, cache_v, rows)


def _merge_kernel(x_ref, a_ref, b_ref, c_ref, ga_ref, gb_ref, gc_ref, wb_ref, wo_ref, o_ref):
    mixed = None
    for r, (br, gr) in enumerate(((a_ref, ga_ref), (b_ref, gb_ref), (c_ref, gc_ref))):
        proj = jnp.dot(br[...], wb_ref[r], preferred_element_type=F32)
        gate = jax.nn.sigmoid(gr[...].astype(F32))
        mixed = gate * proj if mixed is None else mixed + gate * proj
    o_ref[...] = x_ref[...] + jnp.dot(mixed.astype(BF16), wo_ref[...], preferred_element_type=F32)


def _merge(x, out_a, out_b, out_c, z, w_branch, w_out, layer, tm):
    m = x.shape[0]
    branch = lambda: pl.BlockSpec((tm, BRANCH_WIDTH), lambda i: (i, 0))
    gate = lambda r: pl.BlockSpec((tm, D_MODEL), lambda i: (i, Z_GATES // D_MODEL + r))
    return pl.pallas_call(
        _merge_kernel,
        grid=(m // tm,),
        in_specs=[
            pl.BlockSpec((tm, D_MODEL), lambda i: (i, 0)),
            branch(), branch(), branch(),
            gate(0), gate(1), gate(2),
            pl.BlockSpec((None, N_BRANCH, BRANCH_WIDTH, D_MODEL), lambda i: (layer, 0, 0, 0),
                         pipeline_mode=pl.Buffered(1)),
            pl.BlockSpec((None, D_MODEL, D_MODEL), lambda i: (layer, 0, 0),
                         pipeline_mode=pl.Buffered(1)),
        ],
        out_specs=pl.BlockSpec((tm, D_MODEL), lambda i: (i, 0)),
        out_shape=jax.ShapeDtypeStruct((m, D_MODEL), F32),
        compiler_params=_params(("parallel",)),
        name="merge",
    )(x, out_a, out_b, out_c, z, z, z, w_branch, w_out)


MLP_TF = 1024


def _mlp_kernel(*refs, final_norm, jobs):
    n_side = len(jobs.weights) if jobs else 0
    x_ref, g_ref, fg_ref, wu_ref, wd_ref = refs[:5]
    side_in = refs[5:5 + n_side]
    o_ref = refs[5 + n_side]
    side_out = refs[6 + n_side:6 + 2 * n_side]
    hn_ref = refs[-1]
    j = pl.program_id(1)

    @pl.when(j == 0)
    def _():
        xf = x_ref[...]
        hn_ref[...] = _rms(xf, g_ref[...]).astype(BF16)
        o_ref[...] = xf

    h = jnp.dot(hn_ref[...], wu_ref[...], preferred_element_type=F32)
    h = jnp.square(jnp.maximum(h, 0.0)).astype(BF16)
    o_ref[...] += jnp.dot(h, wd_ref[...], preferred_element_type=F32)

    if final_norm:
        @pl.when(j == pl.num_programs(1) - 1)
        def _():
            o_ref[...] = _rms(o_ref[...], fg_ref[...])

    if jobs:
        jobs.run(side_in, side_out)


def _mlp(x, g, final_g, w_up, w_down, layer, tm, tf, final_norm, cast=None):
    m = x.shape[0]
    grid = (m // tm, D_FF // tf)
    jobs = _CastJobs(cast[0], cast[1], grid, cast[2], cast[3]) if cast else None
    return pl.pallas_call(
        functools.partial(_mlp_kernel, final_norm=final_norm, jobs=jobs),
        grid=grid,
        in_specs=[
            pl.BlockSpec((tm, D_MODEL), lambda i, j: (i, 0)),
            pl.BlockSpec((None, 1, D_MODEL), lambda i, j: (layer, 0, 0)),
            pl.BlockSpec((1, D_MODEL), lambda i, j: (0, 0)),
            pl.BlockSpec((None, D_MODEL, tf), lambda i, j: (j, 0, 0)),
            pl.BlockSpec((None, tf, D_MODEL), lambda i, j: (0, j, 0)),
        ] + (jobs.in_specs() if jobs else []),
        out_specs=[pl.BlockSpec((tm, D_MODEL), lambda i, j: (i, 0))] + (jobs.out_specs() if jobs else []),
        out_shape=[jax.ShapeDtypeStruct((m, D_MODEL), F32)] + (jobs.out_shape() if jobs else []),
        scratch_shapes=[pltpu.VMEM((tm, D_MODEL), BF16)],
        compiler_params=_params(("arbitrary", "arbitrary")),
        name="mlp",
    )(x, g, final_g, w_up, w_down, *(cast[0] if cast else ()))


def _permute_w_in(w_in):
    seg = lambda o, w: w_in[..., o:o + w]
    return jnp.concatenate([
        seg(_O_AX, D_RNN), seg(_O_AG, D_RNN), seg(_O_QB, BRANCH_WIDTH), seg(_O_QC, BRANCH_WIDTH),
        seg(_O_GATES, N_BRANCH * D_MODEL),
        seg(_O_KC, CB_KVW), seg(_O_VC, CB_KVW), seg(_O_KB, SWA_KVW), seg(_O_VB, SWA_KVW)],
        axis=-1).astype(BF16)


def _tile_cols(w, tn):
    return w.reshape(w.shape[0], w.shape[1] // tn, tn).transpose(1, 0, 2)


def _cb_bias_rows(table):
    m = jnp.arange(CB_RLEN)
    rel = jnp.where(m < CB_BAND, m, m - CB_RLEN)
    out = []
    for v in range(CB_VARIANTS):
        c0 = v * Q_CHUNKS
        sc = max(c0 - CB_PREV, 0)
        d = (c0 - sc) * CHUNK - rel
        idx = jnp.clip(d, -REL_CLIP, REL_CLIP) + REL_CLIP
        out.append(table.astype(F32)[:, :, idx])
    return jnp.stack(out, axis=1)[:, :, :, None, :]


def _pick_tm(m, want):
    tm = min(want, m)
    while m % tm:
        tm //= 2
    return tm


def _layer(x, n_seq, t_len, conv_buf, h0, state_layer, caches, layer, p, w_in_l, w_layer):
    m = x.shape[0]
    depth = p["w_up"].shape[0]
    tm_in = _pick_tm(m, 1024)
    if w_layer is None:
        z, wu, wd, wb, wo = _in_proj(
            x, p["norm1"], w_in_l, layer, tm_in, "bf16",
            cast=((p["w_up"], p["w_down"], p["w_branch"], p["w_out"]), layer, [D_FF // MLP_TF, 1, 1, 1]))
        w_layer = (wu, wd[None], wb.reshape(1, N_BRANCH, BRANCH_WIDTH, D_MODEL), wo[None])
        cast_next = (((p["w_in"],), layer + 1, _permute_w_in, [D_IN // IN_TN])
                     if layer + 1 < depth else None)
    else:
        z, zkv = _in_proj(x, p["norm1"], w_in_l, layer, tm_in, "dual")
        cast_next = None
    w_up_l, w_down_l, w_branch_l, w_out_l = w_layer
    out_a, conv_o, h_o = _lru(z, conv_buf, h0, state_layer, p["conv_w"], p["conv_b"], p["wa"], p["ba"],
                              p["wx"], p["bx"], p["lam"], layer, n_seq, t_len, _pick_tm(t_len, 512))
    if caches is None:
        out_b = _swa_prompt(z, p["sinks"], layer, n_seq, t_len)
        out_c = _cb_prompt(z, p["cb_rows"], layer, n_seq, t_len)
        n_tail = CB_REACH
        kv3 = _in_proj(x, p["norm1"], w_in_l, layer, n_tail, "kv_f32", n_tiles=n_seq,
                       row_tile=lambda i: (i + 1) * (t_len // n_tail) - 1)[0].reshape(n_seq, n_tail, D_KV)
        kv = (kv3[:, n_tail - SWA_WINDOW:, KV_KB:KV_KB + SWA_KVW],
              kv3[:, n_tail - SWA_WINDOW:, KV_VB:KV_VB + SWA_KVW],
              kv3[:, :, KV_KC:KV_KC + CB_KVW], kv3[:, :, KV_VC:KV_VC + CB_KVW])
    else:
        ck_b, cv_b, ck_c, cv_c = caches
        out_b, kb, vb = _swa_step(z, zkv, ck_b, cv_b, p["sinks"], layer, n_seq)
        out_c, kc, vc = _cb_step(z, zkv, ck_c, cv_c, p["cb_rows"], layer, n_seq)
        kv = (kb, vb, kc, vc)
    x = _merge(x, out_a, out_b, out_c, z, w_branch_l, w_out_l, 0, _pick_tm(m, 256))
    res = _mlp(x, p["norm2"], p["final_g"], w_up_l, w_down_l, layer, _pick_tm(m, 512), MLP_TF,
               final_norm=layer == depth - 1, cast=cast_next)
    w_in_next = res[1] if cast_next else None
    return res[0], conv_o, h_o[:, 0], kv, w_layer, w_in_next


def kernel(x_prompt, x_sample, state_conv, state_lru, cache_swa_k, cache_swa_v, cache_cb_k, cache_cb_v, norm1_g, w_in, conv_w, conv_b, lru_wa, lru_ba, lru_wx, lru_bx, lru_lambda, attn_sinks, rel_bias_table, w_branch, w_out, norm2_g, w_up, w_down, final_g):
    depth = w_in.shape[0]
    nb, s_len, _ = x_prompt.shape
    db, d_len, _ = x_sample.shape
    assert d_len == CHUNK and s_len % (Q_BLOCK * CB_QSUB) == 0 and s_len % SWA_TQ == 0
    assert s_len % CB_REACH == 0
    assert cache_swa_k.shape[2] == SWA_WINDOW and cache_cb_k.shape[2] == CB_REACH

    row = lambda v: v.reshape(depth, 1, -1)
    p = {
        "norm1": row(norm1_g), "norm2": row(norm2_g), "final_g": final_g.reshape(1, D_MODEL),
        "w_in": w_in, "w_up": w_up, "w_down": w_down,
        "w_branch": w_branch.reshape(depth, N_BRANCH * BRANCH_WIDTH, D_MODEL), "w_out": w_out,
        "conv_w": conv_w, "conv_b": row(conv_b),
        "wa": lru_wa.astype(BF16), "ba": row(lru_ba), "wx": lru_wx.astype(BF16), "bx": row(lru_bx),
        "lam": row(lru_lambda), "sinks": attn_sinks,
        "cb_rows": _cb_bias_rows(rel_bias_table),
    }
    w_in_l = _tile_cols(_permute_w_in(w_in[0]), IN_TN)
    caches = (cache_swa_k.reshape(depth, db, SWA_WINDOW * SWA_KV_HEADS, HEAD_DIM),
              cache_swa_v.reshape(depth, db, SWA_WINDOW * SWA_KV_HEADS, HEAD_DIM),
              cache_cb_k.reshape(depth, db, CB_REACH * CB_HEADS, HEAD_DIM),
              cache_cb_v.reshape(depth, db, CB_REACH * CB_HEADS, HEAD_DIM))

    xp = x_prompt.reshape(nb * s_len, D_MODEL)
    xs = x_sample.reshape(db * d_len, D_MODEL)
    zero_conv = jnp.zeros((1, nb, CONV_W - 1, D_RNN), F32)
    zero_h = jnp.zeros((1, nb, 1, D_RNN), F32)
    h0_s = state_lru.reshape(depth, db, 1, D_RNN)
    heads = (SWA_KV_HEADS, SWA_KV_HEADS, CB_HEADS, CB_HEADS)
    outs = [[] for _ in range(12)]
    for l in range(depth):
        xp, conv_p, h_p, kv_p, w_layer, w_in_next = _layer(xp, nb, s_len, zero_conv, zero_h, 0, None, l, p,
                                                           w_in_l, None)
        xs, conv_s, h_s, kv_s, _, _ = _layer(xs, db, d_len, state_conv, h0_s, l, caches, l, p,
                                             w_in_l, w_layer)
        w_in_l = w_in_next
        outs[0].append(conv_p)
        outs[1].append(h_p)
        outs[6].append(conv_s)
        outs[7].append(h_s)
        for n in range(4):
            outs[2 + n].append(kv_p[n].reshape(nb, -1, heads[n], HEAD_DIM))
            outs[8 + n].append(kv_s[n].reshape(db, d_len, heads[n], HEAD_DIM))

    y_prompt = xp.reshape(nb, s_len, D_MODEL)
    y_sample = xs.reshape(db, d_len, D_MODEL)
    return (y_prompt, y_sample) + tuple(jnp.stack(o) for o in outs)
```

```python
import functools

import jax
import jax.numpy as jnp
from jax import lax
from jax.experimental import pallas as pl
from jax.experimental.pallas import tpu as pltpu

F32 = jnp.float32
BF16 = jnp.bfloat16

D_MODEL = 2048
CHUNK = 64
HEAD_DIM = 128
BRANCH_WIDTH = D_MODEL // 2
N_BRANCH = 3
D_RNN = BRANCH_WIDTH
LRU_BLOCKS = 8
LRU_BLOCK = D_RNN // LRU_BLOCKS
CONV_W = 4
LRU_C = 8.0
SWA_HEADS = BRANCH_WIDTH // HEAD_DIM
SWA_KV_HEADS = 2
SWA_GROUP = SWA_HEADS // SWA_KV_HEADS
SWA_WINDOW = 128
SWA_PREV = SWA_WINDOW // CHUNK
CB_HEADS = BRANCH_WIDTH // HEAD_DIM
CB_PREV = 8
CB_REACH = CB_PREV * CHUNK
REL_CLIP = 128
D_FF = 4 * D_MODEL
EPS = 1e-6
NEG = -1e30
ATTN_SCALE = HEAD_DIM ** -0.5
LOG2E = 1.4426950408889634

_O_AX, _O_AG, _O_QB, _O_KB, _O_VB, _O_QC, _O_KC, _O_VC, _O_GATES = (
    0, 1024, 2048, 3072, 3328, 3584, 4608, 5632, 6656)
D_IN = _O_GATES + N_BRANCH * D_MODEL
SWA_KVW = SWA_KV_HEADS * HEAD_DIM
CB_KVW = CB_HEADS * HEAD_DIM
D_KV = 2 * SWA_KVW + 2 * CB_KVW
D_Z = D_IN - D_KV
KV_KC, KV_VC, KV_KB, KV_VB = 0, 1024, 2048, 2304
Z_AX, Z_AG, Z_QB, Z_QC, Z_GATES = 0, 1024, 2048, 3072, 4096

VMEM_LIMIT_BYTES = 56 * 1024 * 1024

Q_BLOCK = 4 * CHUNK
Q_CHUNKS = Q_BLOCK // CHUNK
CB_BAND = (CB_PREV + Q_CHUNKS) * CHUNK
CB_QSUB = 16
CB_RLEN = 1024
assert CB_RLEN >= Q_BLOCK + CB_BAND - 1
CB_VARIANTS = 3


def _params(semantics):
    return pltpu.CompilerParams(dimension_semantics=semantics, vmem_limit_bytes=VMEM_LIMIT_BYTES)


def _rms(xf, g):
    return xf * lax.rsqrt(jnp.mean(xf * xf, axis=-1, keepdims=True) + EPS) * g


IN_TN = 1280
IN_KV_TILES = D_KV // IN_TN
IN_Z_TILES = D_Z // IN_TN


class _CastJobs:
    def __init__(self, weights, layer, grid, transform=None):
        self.weights, self.layer, self.transform = weights, layer, transform
        self.inner = grid[1]
        n_max = 1
        while n_max * 2 <= min(grid[0] * grid[1], MAX_CAST_STEPS):
            n_max *= 2
        self.n = []
        for w in weights:
            n = n_max
            while w.shape[1] % (n * BF16_ROWS):
                n //= 2
            self.n.append(n)

    def _spec(self, w, n, lead):
        slab = lambda i, j: jnp.minimum(i * self.inner + j, n - 1)
        if lead:
            return pl.BlockSpec((None, w.shape[1] // n, w.shape[2]), lambda i, j: (self.layer, slab(i, j), 0))
        return pl.BlockSpec((w.shape[1] // n, w.shape[2]), lambda i, j: (slab(i, j), 0))

    def in_specs(self):
        return [self._spec(w, n, True) for w, n in zip(self.weights, self.n)]

    def out_specs(self):
        return [self._spec(w, n, False) for w, n in zip(self.weights, self.n)]

    def out_shape(self):
        return [jax.ShapeDtypeStruct(w.shape[1:], BF16) for w in self.weights]

    def run(self, src_refs, dst_refs):
        step = pl.program_id(0) * self.inner + pl.program_id(1)
        for n in sorted(set(self.n)):
            @pl.when(step < n)
            def _():
                for s, d, n_w in zip(src_refs, dst_refs, self.n):
                    if n_w == n:
                        v = s[...]
                        d[...] = (self.transform(v) if self.transform else v).astype(BF16)


MAX_CAST_STEPS = 128
BF16_ROWS = 16


def _in_proj_kernel(*refs, jobs, mode):
    n_side = len(jobs.weights) if jobs else 0
    n_out = 2 if mode == "dual" else 1
    x_ref, g_ref, w_ref = refs[:3]
    side_in = refs[3:3 + n_side]
    outs = refs[3 + n_side:3 + n_side + n_out]
    side_out = refs[3 + n_side + n_out:3 + 2 * n_side + n_out]
    xn_ref = refs[-1]
    j = pl.program_id(1)

    @pl.when(j == 0)
    def _():
        xn_ref[...] = _rms(x_ref[...], g_ref[...]).astype(BF16)

    dot = lambda: jnp.dot(xn_ref[...], w_ref[...], preferred_element_type=F32)
    if mode == "bf16":
        outs[0][...] = dot().astype(BF16)
    elif mode == "kv_f32":
        outs[0][...] = dot()
    else:
        z_ref, zkv_ref = outs

        @pl.when(j < IN_Z_TILES)
        def _():
            z_ref[...] = dot().astype(BF16)

        @pl.when(j >= IN_Z_TILES)
        def _():
            zkv_ref[...] = dot()

    if jobs:
        jobs.run(side_in, side_out)


def _in_proj(x, g, w, layer, tm, mode, cast=None, row_tile=None, n_tiles=None):
    m = x.shape[0]
    col0 = 0
    if mode == "kv_f32":
        grid = (n_tiles, IN_KV_TILES)
        col0 = IN_Z_TILES
        x_map = lambda i, j: (row_tile(i), 0)
        out_specs = [pl.BlockSpec((tm, IN_TN), lambda i, j: (i, j))]
        out_shape = [jax.ShapeDtypeStruct((n_tiles * tm, D_KV), F32)]
    else:
        grid = (m // tm, D_IN // IN_TN)
        x_map = lambda i, j: (i, 0)
        if mode == "bf16":
            out_specs = [pl.BlockSpec((tm, IN_TN), lambda i, j: (i, j))]
            out_shape = [jax.ShapeDtypeStruct((m, D_IN), BF16)]
        else:
            out_specs = [pl.BlockSpec((tm, IN_TN), lambda i, j: (i, jnp.minimum(j, IN_Z_TILES - 1))),
                         pl.BlockSpec((tm, IN_TN), lambda i, j: (i, jnp.maximum(j - IN_Z_TILES, 0)))]
            out_shape = [jax.ShapeDtypeStruct((m, D_Z), BF16), jax.ShapeDtypeStruct((m, D_KV), F32)]
    jobs = _CastJobs(cast[0], cast[1], grid) if cast else None
    return pl.pallas_call(
        functools.partial(_in_proj_kernel, jobs=jobs, mode=mode),
        grid=grid,
        in_specs=[
            pl.BlockSpec((tm, D_MODEL), x_map),
            pl.BlockSpec((None, 1, D_MODEL), lambda i, j: (layer, 0, 0)),
            pl.BlockSpec((None, D_MODEL, IN_TN), lambda i, j: (0, 0, col0 + j)),
        ] + (jobs.in_specs() if jobs else []),
        out_specs=out_specs + (jobs.out_specs() if jobs else []),
        out_shape=out_shape + (jobs.out_shape() if jobs else []),
        scratch_shapes=[pltpu.VMEM((tm, D_MODEL), BF16)],
        compiler_params=_params(("arbitrary", "arbitrary")),
        name="in_proj",
    )(x, g, w, *(cast[0] if cast else ()))


_XPAD = 8
LANES = 128
_SEGS = 8
_SEG_LEN = 4


def _sigmoid(x):
    return 0.5 * (jnp.tanh(0.5 * x) + 1.0)


def _lru_kernel(ax_ref, ag_ref, cbuf_ref, h0_ref, cw_ref, cb_ref, wa_ref, ba_ref, wx_ref, bx_ref,
                lam_ref, out_ref, convo_ref, ho_ref, xbuf, a_s, b_s, h_s):
    t = pl.program_id(1)
    nt = pl.num_programs(1)
    tt = ax_ref.shape[0]

    @pl.when(t == 0)
    def _():
        xbuf[...] = jnp.zeros_like(xbuf)
        xbuf[_XPAD - (CONV_W - 1):, :] = cbuf_ref[...]
        h_s[...] = h0_ref[...]

    x = ax_ref[...].astype(F32)
    xe = jnp.concatenate([xbuf[...], x], axis=0)
    acc = xe * cw_ref[0:1, :]
    for k in range(1, CONV_W):
        acc = xe * cw_ref[k:k + 1, :] + pltpu.roll(acc, 1, 0)
    u = cb_ref[...] + acc[_XPAD:, :]
    tail = x[tt - (CONV_W - 1):, :]
    xbuf[...] = x[tt - _XPAD:, :]

    ub = u.astype(BF16)
    r_parts, i_parts = [], []
    for n in range(LRU_BLOCKS):
        un = ub[:, n * LRU_BLOCK:(n + 1) * LRU_BLOCK]
        r_parts.append(jnp.dot(un, wa_ref[n], preferred_element_type=F32))
        i_parts.append(jnp.dot(un, wx_ref[n], preferred_element_type=F32))
    r = _sigmoid(jnp.concatenate(r_parts, axis=1) + ba_ref[...])
    i = _sigmoid(jnp.concatenate(i_parts, axis=1) + bx_ref[...])
    log_a = -LRU_C * r * jax.nn.softplus(-lam_ref[...])
    a = jnp.exp(log_a)
    y = 1.0 - a * a
    b = jnp.where(y > 0.0, y * lax.rsqrt(y), 0.0) * (i * u)
    n_lg = D_RNN // LANES
    for lg in range(n_lg):
        a_s[lg] = a[:, lg * LANES:(lg + 1) * LANES]
        b_s[lg] = b[:, lg * LANES:(lg + 1) * LANES]

    row = lax.broadcasted_iota(jnp.int32, (_SEGS, LANES), 0)
    sub = _SEGS * _SEG_LEN

    def body(sb, h):
        r0 = pl.multiple_of(sb * sub, sub)
        step = lambda j: pl.ds(r0 + j, _SEGS, stride=_SEG_LEN)
        h_next = []
        for lg in range(n_lg):
            a_g, b_g = a_s.at[lg], b_s.at[lg]
            h_g = h[:, lg * LANES:(lg + 1) * LANES]
            acs, bcs = [a_g[step(0), :]], [b_g[step(0), :]]
            for j in range(1, _SEG_LEN):
                aj = a_g[step(j), :]
                bcs.append(aj * bcs[-1] + b_g[step(j), :])
                acs.append(aj * acs[-1])
            at, bt = acs[-1], bcs[-1]
            for s in (1, 2, 4):
                m = row >= s
                bt_new = jnp.where(m, at * pltpu.roll(bt, s, 0) + bt, bt)
                at = jnp.where(m, at * pltpu.roll(at, s, 0), at)
                bt = bt_new
            after = at * h_g + bt
            entry = jnp.where(row >= 1, pltpu.roll(after, 1, 0), h_g)
            for j in range(_SEG_LEN):
                b_g[step(j), :] = acs[j] * entry + bcs[j]
            h_next.append(after[_SEGS - 1:_SEGS, :])
        return jnp.concatenate(h_next, axis=1)

    h = lax.fori_loop(0, tt // sub, body, h_s[...])
    h_s[...] = h
    hs = jnp.concatenate([b_s[lg] for lg in range(n_lg)], axis=1)
    out_ref[...] = (hs * jax.nn.gelu(ag_ref[...].astype(F32))).astype(BF16)

    @pl.when(t == nt - 1)
    def _():
        convo_ref[...] = tail
        ho_ref[...] = h


def _lru(z, conv_buf, h0, state_layer, cw, cb, wa, ba, wx, bx, lam, layer, n_seq, t_len, tt):
    nt = t_len // tt
    row = lambda b, t: b * nt + t
    vec = lambda: pl.BlockSpec((None, 1, D_RNN), lambda b, t: (layer, 0, 0))
    blk = lambda: pl.BlockSpec((None, LRU_BLOCKS, LRU_BLOCK, LRU_BLOCK), lambda b, t: (layer, 0, 0, 0))
    return pl.pallas_call(
        _lru_kernel,
        grid=(n_seq, nt),
        in_specs=[
            pl.BlockSpec((tt, D_RNN), lambda b, t: (row(b, t), Z_AX // D_RNN)),
            pl.BlockSpec((tt, D_RNN), lambda b, t: (row(b, t), Z_AG // D_RNN)),
            pl.BlockSpec((None, None, CONV_W - 1, D_RNN), lambda b, t: (state_layer, b, 0, 0)),
            pl.BlockSpec((None, None, 1, D_RNN), lambda b, t: (state_layer, b, 0, 0)),
            pl.BlockSpec((None, CONV_W, D_RNN), lambda b, t: (layer, 0, 0)),
            vec(), blk(), vec(), blk(), vec(), vec(),
        ],
        out_specs=[
            pl.BlockSpec((tt, D_RNN), lambda b, t: (row(b, t), 0)),
            pl.BlockSpec((None, CONV_W - 1, D_RNN), lambda b, t: (b, 0, 0)),
            pl.BlockSpec((None, 1, D_RNN), lambda b, t: (b, 0, 0)),
        ],
        out_shape=[
            jax.ShapeDtypeStruct((n_seq * t_len, D_RNN), BF16),
            jax.ShapeDtypeStruct((n_seq, CONV_W - 1, D_RNN), F32),
            jax.ShapeDtypeStruct((n_seq, 1, D_RNN), F32),
        ],
        scratch_shapes=[
            pltpu.VMEM((_XPAD, D_RNN), F32),
            pltpu.VMEM((D_RNN // LANES, tt, LANES), F32),
            pltpu.VMEM((D_RNN // LANES, tt, LANES), F32),
            pltpu.VMEM((1, D_RNN), F32),
        ],
        compiler_params=_params(("parallel", "arbitrary")),
        name="lru",
    )(z, z, conv_buf, h0, cw, cb, wa, ba, wx, bx, lam)


def _dot_nt(a, b):
    return lax.dot_general(a, b, (((1,), (1,)), ((), ())), preferred_element_type=F32)


def _with_ones(v):
    return jnp.concatenate([v, jnp.ones_like(v)], axis=1)


def _swa_chunk(q, kband, vext, sinks, valid):
    qst = jnp.concatenate([q[:, g * HEAD_DIM:(g + 1) * HEAD_DIM] for g in range(SWA_GROUP)], axis=0)
    s = _dot_nt(qst, kband) * (ATTN_SCALE * LOG2E)
    if valid is not None:
        s = jnp.where(valid, s, NEG)
    es, sink_e = [], []
    for g in range(SWA_GROUP):
        sg = s[g * CHUNK:(g + 1) * CHUNK, :]
        sink2 = sinks[g] * LOG2E
        m = jnp.maximum(jnp.max(sg, axis=-1, keepdims=True), sink2)
        es.append(jnp.exp2(sg - m).astype(BF16))
        sink_e.append(jnp.exp2(sink2 - m))
    r = jnp.dot(jnp.concatenate(es, axis=0), vext, preferred_element_type=F32)
    outs = []
    for g in range(SWA_GROUP):
        rg = r[g * CHUNK:(g + 1) * CHUNK, :]
        outs.append(rg[:, :HEAD_DIM] * (1.0 / (rg[:, HEAD_DIM:] + sink_e[g])))
    return jnp.concatenate(outs, axis=1)


def _cb_attend(q, kband, vext, bias2):
    s = _dot_nt(q, kband) * (ATTN_SCALE * LOG2E) + bias2
    m = jnp.max(s, axis=-1, keepdims=True)
    r = jnp.dot(jnp.exp2(s - m).astype(BF16), vext, preferred_element_type=F32)
    return r[:, :HEAD_DIM] * (1.0 / r[:, HEAD_DIM:])


def _cb_bias_block(r, variant, n_rows, n_cols):
    t = pltpu.roll(jnp.broadcast_to(r, (n_rows, CB_RLEN)), 0, 1, stride=1, stride_axis=0)[:, :n_cols]
    c0 = variant * Q_CHUNKS
    sc = max(c0 - CB_PREV, 0)
    qc = c0 + lax.broadcasted_iota(jnp.int32, (n_rows, n_cols), 0) // CHUNK
    kc = sc + lax.broadcasted_iota(jnp.int32, (n_rows, n_cols), 1) // CHUNK
    return jnp.where(kc <= qc, jnp.where(kc >= qc - CB_PREV, t * LOG2E, NEG), NEG)


def _cast_rows(dst, src, n_rows, step, ones=False):
    def body(i, c):
        r0 = pl.multiple_of(i * step, step)
        v = src[pl.ds(r0, step), :].astype(BF16)
        dst[pl.ds(r0, step), :] = _with_ones(v) if ones else v
        return c
    lax.fori_loop(0, n_rows // step, body, 0)


SWA_BAND = (SWA_PREV + 1) * CHUNK
SWA_TQ = 32 * CHUNK


def _swa_prompt_kernel(sink_ref, q_ref, k_ref, v_ref, o_ref, vb_s, *, layer):
    kh = pl.program_id(1)
    qi = pl.program_id(2)
    s_len = k_ref.shape[0]

    @pl.when(qi == 0)
    def _():
        _cast_rows(vb_s, v_ref, s_len, 512, ones=True)

    sinks = [sink_ref[layer, kh * SWA_GROUP + g] for g in range(SWA_GROUP)]
    jchunk = lax.broadcasted_iota(jnp.int32, (1, SWA_BAND), 1) // CHUNK
    for c in range(SWA_TQ // CHUNK):
        cg = qi * (SWA_TQ // CHUNK) + c
        sc = jnp.maximum(cg - SWA_PREV, 0)
        s0 = pl.multiple_of(sc * CHUNK, CHUNK)
        valid = (jchunk + sc) <= cg
        o = _swa_chunk(q_ref[c * CHUNK:(c + 1) * CHUNK, :], k_ref[pl.ds(s0, SWA_BAND), :],
                       vb_s[pl.ds(s0, SWA_BAND), :], sinks, valid)
        o_ref[c * CHUNK:(c + 1) * CHUNK, :] = o.astype(BF16)


def _swa_prompt(z, sinks, layer, n_seq, s_len):
    nq = s_len // SWA_TQ
    gw = SWA_GROUP * HEAD_DIM
    return pl.pallas_call(
        functools.partial(_swa_prompt_kernel, layer=layer),
        grid=(n_seq, SWA_KV_HEADS, nq),
        in_specs=[
            pl.BlockSpec(memory_space=pltpu.SMEM),
            pl.BlockSpec((SWA_TQ, gw), lambda b, k, q: (b * nq + q, Z_QB // gw + k)),
            pl.BlockSpec((s_len, HEAD_DIM), lambda b, k, q: (b, (D_Z + KV_KB) // HEAD_DIM + k)),
            pl.BlockSpec((s_len, HEAD_DIM), lambda b, k, q: (b, (D_Z + KV_VB) // HEAD_DIM + k)),
        ],
        out_specs=pl.BlockSpec((SWA_TQ, gw), lambda b, k, q: (b * nq + q, k)),
        out_shape=jax.ShapeDtypeStruct((n_seq * s_len, BRANCH_WIDTH), BF16),
        scratch_shapes=[pltpu.VMEM((s_len, 2 * HEAD_DIM), BF16)],
        compiler_params=_params(("parallel", "parallel", "arbitrary")),
        name="swa_prompt",
    )(sinks, z, z, z)


def _swa_step_kernel(sink_ref, q_ref, k_ref, v_ref, ck_ref, cv_ref, o_ref, ko_ref, vo_ref, *, layer):
    n_past = ck_ref.shape[0] // SWA_KV_HEADS
    gw = SWA_GROUP * HEAD_DIM
    for kh in range(SWA_KV_HEADS):
        cs = slice(kh * HEAD_DIM, (kh + 1) * HEAD_DIM)
        kn = k_ref[:, cs]
        vn = v_ref[:, cs]
        kfull = jnp.concatenate(
            [ck_ref[pl.ds(kh, n_past, stride=SWA_KV_HEADS), :].astype(BF16), kn.astype(BF16)], axis=0)
        vfull = jnp.concatenate(
            [cv_ref[pl.ds(kh, n_past, stride=SWA_KV_HEADS), :].astype(BF16), vn.astype(BF16)], axis=0)
        sinks = [sink_ref[layer, kh * SWA_GROUP + g] for g in range(SWA_GROUP)]
        o = _swa_chunk(q_ref[:, kh * gw:(kh + 1) * gw], kfull, _with_ones(vfull), sinks, None)
        o_ref[:, kh * gw:(kh + 1) * gw] = o.astype(BF16)
        ko_ref[pl.ds(kh, CHUNK, stride=SWA_KV_HEADS), :] = kn
        vo_ref[pl.ds(kh, CHUNK, stride=SWA_KV_HEADS), :] = vn


def _swa_step(z, zkv, cache_k, cache_v, sinks, layer, n_seq):
    rows_past = cache_k.shape[2]
    rows_new = CHUNK * SWA_KV_HEADS
    cache = lambda: pl.BlockSpec((None, None, rows_past, HEAD_DIM), lambda b: (layer, b, 0, 0))
    new = lambda: pl.BlockSpec((rows_new, HEAD_DIM), lambda b: (b, 0))
    return pl.pallas_call(
        functools.partial(_swa_step_kernel, layer=layer),
        grid=(n_seq,),
        in_specs=[
            pl.BlockSpec(memory_space=pltpu.SMEM),
            pl.BlockSpec((CHUNK, BRANCH_WIDTH), lambda b: (b, Z_QB // BRANCH_WIDTH)),
            pl.BlockSpec((CHUNK, SWA_KVW), lambda b: (b, KV_KB // SWA_KVW)),
            pl.BlockSpec((CHUNK, SWA_KVW), lambda b: (b, KV_VB // SWA_KVW)),
            cache(), cache(),
        ],
        out_specs=[pl.BlockSpec((CHUNK, BRANCH_WIDTH), lambda b: (b, 0)), new(), new()],
        out_shape=[jax.ShapeDtypeStruct((n_seq * CHUNK, BRANCH_WIDTH), BF16),
                   jax.ShapeDtypeStruct((n_seq * rows_new, HEAD_DIM), F32),
                   jax.ShapeDtypeStruct((n_seq * rows_new, HEAD_DIM), F32)],
        compiler_params=_params(("parallel",)),
        name="swa_step",
    )(sinks, z, zkv, zkv, cache_k, cache_v)


def _cb_prompt_kernel(q_ref, k_ref, v_ref, r_ref, o_ref, vb_s, bias_s):
    qi = pl.program_id(2)
    s_len = k_ref.shape[0]

    @pl.when(qi == 0)
    def _():
        _cast_rows(vb_s, v_ref, s_len, 512, ones=True)
        for v in range(CB_VARIANTS):
            bias_s[v] = _cb_bias_block(r_ref[v], v, Q_BLOCK, CB_BAND)

    for sub in range(CB_QSUB):
        blk = qi * CB_QSUB + sub
        sc = jnp.maximum(blk * Q_CHUNKS - CB_PREV, 0)
        s0 = pl.multiple_of(sc * CHUNK, CHUNK)
        rows = slice(sub * Q_BLOCK, (sub + 1) * Q_BLOCK)
        o = _cb_attend(q_ref[rows, :], k_ref[pl.ds(s0, CB_BAND), :], vb_s[pl.ds(s0, CB_BAND), :],
                       bias_s[jnp.minimum(blk, CB_VARIANTS - 1)])
        o_ref[rows, :] = o.astype(BF16)


def _cb_prompt(z, rows, layer, n_seq, s_len):
    tq = Q_BLOCK * CB_QSUB
    nq = s_len // tq
    return pl.pallas_call(
        _cb_prompt_kernel,
        grid=(n_seq, CB_HEADS, nq),
        in_specs=[
            pl.BlockSpec((tq, HEAD_DIM), lambda b, h, q: (b * nq + q, Z_QC // HEAD_DIM + h)),
            pl.BlockSpec((s_len, HEAD_DIM), lambda b, h, q: (b, (D_Z + KV_KC) // HEAD_DIM + h)),
            pl.BlockSpec((s_len, HEAD_DIM), lambda b, h, q: (b, (D_Z + KV_VC) // HEAD_DIM + h)),
            pl.BlockSpec((None, CB_VARIANTS, None, 1, CB_RLEN), lambda b, h, q: (layer, 0, h, 0, 0)),
        ],
        out_specs=pl.BlockSpec((tq, HEAD_DIM), lambda b, h, q: (b * nq + q, h)),
        out_shape=jax.ShapeDtypeStruct((n_seq * s_len, BRANCH_WIDTH), BF16),
        scratch_shapes=[pltpu.VMEM((s_len, 2 * HEAD_DIM), BF16),
                        pltpu.VMEM((CB_VARIANTS, Q_BLOCK, CB_BAND), F32)],
        compiler_params=_params(("parallel", "parallel", "arbitrary")),
        name="cb_prompt",
    )(z, z, z, rows)


def _cb_step_kernel(q_ref, k_ref, v_ref, ck_ref, cv_ref, r_ref, o_ref, ko_ref, vo_ref, bias_s):
    n_past = ck_ref.shape[0] // CB_HEADS
    n_k = n_past + CHUNK

    @pl.when(pl.program_id(0) == 0)
    def _():
        for h in range(CB_HEADS):
            bias_s[h] = _cb_bias_block(r_ref[h], CB_VARIANTS - 1, CHUNK, n_k)

    for h in range(CB_HEADS):
        cs = slice(h * HEAD_DIM, (h + 1) * HEAD_DIM)
        kn = k_ref[:, cs]
        vn = v_ref[:, cs]
        kfull = jnp.concatenate(
            [ck_ref[pl.ds(h, n_past, stride=CB_HEADS), :].astype(BF16), kn.astype(BF16)], axis=0)
        vfull = jnp.concatenate(
            [cv_ref[pl.ds(h, n_past, stride=CB_HEADS), :].astype(BF16), vn.astype(BF16)], axis=0)
        o_ref[:, cs] = _cb_attend(q_ref[:, cs], kfull, _with_ones(vfull), bias_s[h]).astype(BF16)
        ko_ref[pl.ds(h, CHUNK, stride=CB_HEADS), :] = kn
        vo_ref[pl.ds(h, CHUNK, stride=CB_HEADS), :] = vn


def _cb_step(z, zkv, cache_k, cache_v, rows, layer, n_seq):
    rows_past = cache_k.shape[2]
    n_k = rows_past // CB_HEADS + CHUNK
    rows_new = CHUNK * CB_HEADS
    cache = lambda: pl.BlockSpec((None, None, rows_past, HEAD_DIM), lambda b: (layer, b, 0, 0))
    new = lambda: pl.BlockSpec((rows_new, HEAD_DIM), lambda b: (b, 0))
    return pl.pallas_call(
        _cb_step_kernel,
        grid=(n_seq,),
        in_specs=[
            pl.BlockSpec((CHUNK, BRANCH_WIDTH), lambda b: (b, Z_QC // BRANCH_WIDTH)),
            pl.BlockSpec((CHUNK, CB_KVW), lambda b: (b, KV_KC // CB_KVW)),
            pl.BlockSpec((CHUNK, CB_KVW), lambda b: (b, KV_VC // CB_KVW)),
            cache(), cache(),
            pl.BlockSpec((None, None, CB_HEADS, 1, CB_RLEN),
                         lambda b: (layer, CB_VARIANTS - 1, 0, 0, 0)),
        ],
        out_specs=[pl.BlockSpec((CHUNK, BRANCH_WIDTH), lambda b: (b, 0)), new(), new()],
        out_shape=[jax.ShapeDtypeStruct((n_seq * CHUNK, BRANCH_WIDTH), BF16),
                   jax.ShapeDtypeStruct((n_seq * rows_new, HEAD_DIM), F32),
                   jax.ShapeDtypeStruct((n_seq * rows_new, HEAD_DIM), F32)],
        scratch_shapes=[pltpu.VMEM((CB_HEADS, CHUNK, n_k), F32)],
        compiler_params=_params(("arbitrary",)),
        name="cb_step",
    )(z, zkv, zkv, cache_k, cache_v, rows)


def _merge_kernel(x_ref, a_ref, b_ref, c_ref, ga_ref, gb_ref, gc_ref, wb_ref, wo_ref, o_ref):
    mixed = None
    for r, (br, gr) in enumerate(((a_ref, ga_ref), (b_ref, gb_ref), (c_ref, gc_ref))):
        proj = jnp.dot(br[...], wb_ref[r], preferred_element_type=F32)
        gate = jax.nn.sigmoid(gr[...].astype(F32))
        mixed = gate * proj if mixed is None else mixed + gate * proj
    o_ref[...] = x_ref[...] + jnp.dot(mixed.astype(BF16), wo_ref[...], preferred_element_type=F32)


def _merge(x, out_a, out_b, out_c, z, w_branch, w_out, layer, tm):
    m = x.shape[0]
    branch = lambda: pl.BlockSpec((tm, BRANCH_WIDTH), lambda i: (i, 0))
    gate = lambda r: pl.BlockSpec((tm, D_MODEL), lambda i: (i, Z_GATES // D_MODEL + r))
    return pl.pallas_call(
        _merge_kernel,
        grid=(m // tm,),
        in_specs=[
            pl.BlockSpec((tm, D_MODEL), lambda i: (i, 0)),
            branch(), branch(), branch(),
            gate(0), gate(1), gate(2),
            pl.BlockSpec((None, N_BRANCH, BRANCH_WIDTH, D_MODEL), lambda i: (layer, 0, 0, 0),
                         pipeline_mode=pl.Buffered(1)),
            pl.BlockSpec((None, D_MODEL, D_MODEL), lambda i: (layer, 0, 0),
                         pipeline_mode=pl.Buffered(1)),
        ],
        out_specs=pl.BlockSpec((tm, D_MODEL), lambda i: (i, 0)),
        out_shape=jax.ShapeDtypeStruct((m, D_MODEL), F32),
        compiler_params=_params(("parallel",)),
        name="merge",
    )(x, out_a, out_b, out_c, z, z, z, w_branch, w_out)


def _mlp_kernel(*refs, final_norm, jobs):
    n_side = len(jobs.weights) if jobs else 0
    x_ref, g_ref, fg_ref, wu_ref, wd_ref = refs[:5]
    side_in = refs[5:5 + n_side]
    o_ref = refs[5 + n_side]
    side_out = refs[6 + n_side:6 + 2 * n_side]
    hn_ref = refs[-1]
    j = pl.program_id(1)

    @pl.when(j == 0)
    def _():
        xf = x_ref[...]
        hn_ref[...] = _rms(xf, g_ref[...]).astype(BF16)
        o_ref[...] = xf

    h = jnp.dot(hn_ref[...], wu_ref[...], preferred_element_type=F32)
    h = jnp.square(jnp.maximum(h, 0.0)).astype(BF16)
    o_ref[...] += jnp.dot(h, wd_ref[...], preferred_element_type=F32)

    if final_norm:
        @pl.when(j == pl.num_programs(1) - 1)
        def _():
            o_ref[...] = _rms(o_ref[...], fg_ref[...])

    if jobs:
        jobs.run(side_in, side_out)


def _mlp(x, g, final_g, w_up, w_down, layer, tm, tf, final_norm, cast=None):
    m = x.shape[0]
    grid = (m // tm, D_FF // tf)
    jobs = _CastJobs(cast[0], cast[1], grid, cast[2]) if cast else None
    return pl.pallas_call(
        functools.partial(_mlp_kernel, final_norm=final_norm, jobs=jobs),
        grid=grid,
        in_specs=[
            pl.BlockSpec((tm, D_MODEL), lambda i, j: (i, 0)),
            pl.BlockSpec((None, 1, D_MODEL), lambda i, j: (layer, 0, 0)),
            pl.BlockSpec((1, D_MODEL), lambda i, j: (0, 0)),
            pl.BlockSpec((None, D_MODEL, tf), lambda i, j: (0, 0, j)),
            pl.BlockSpec((None, tf, D_MODEL), lambda i, j: (0, j, 0)),
        ] + (jobs.in_specs() if jobs else []),
        out_specs=[pl.BlockSpec((tm, D_MODEL), lambda i, j: (i, 0))] + (jobs.out_specs() if jobs else []),
        out_shape=[jax.ShapeDtypeStruct((m, D_MODEL), F32)] + (jobs.out_shape() if jobs else []),
        scratch_shapes=[pltpu.VMEM((tm, D_MODEL), BF16)],
        compiler_params=_params(("arbitrary", "arbitrary")),
        name="mlp",
    )(x, g, final_g, w_up, w_down, *(cast[0] if cast else ()))


def _permute_w_in(w_in):
    seg = lambda o, w: w_in[..., o:o + w]
    return jnp.concatenate([
        seg(_O_AX, D_RNN), seg(_O_AG, D_RNN), seg(_O_QB, BRANCH_WIDTH), seg(_O_QC, BRANCH_WIDTH),
        seg(_O_GATES, N_BRANCH * D_MODEL),
        seg(_O_KC, CB_KVW), seg(_O_VC, CB_KVW), seg(_O_KB, SWA_KVW), seg(_O_VB, SWA_KVW)],
        axis=-1).astype(BF16)


def _cb_bias_rows(table):
    m = jnp.arange(CB_RLEN)
    rel = jnp.where(m < CB_BAND, m, m - CB_RLEN)
    out = []
    for v in range(CB_VARIANTS):
        c0 = v * Q_CHUNKS
        sc = max(c0 - CB_PREV, 0)
        d = (c0 - sc) * CHUNK - rel
        idx = jnp.clip(d, -REL_CLIP, REL_CLIP) + REL_CLIP
        out.append(table.astype(F32)[:, :, idx])
    return jnp.stack(out, axis=1)[:, :, :, None, :]


def _pick_tm(m, want):
    tm = min(want, m)
    while m % tm:
        tm //= 2
    return tm


def _layer(x, n_seq, t_len, conv_buf, h0, state_layer, caches, layer, p, w_in_l, w_layer):
    m = x.shape[0]
    depth = p["w_up"].shape[0]
    tm_in = _pick_tm(m, 1024)
    if w_layer is None:
        z, wu, wd, wb, wo = _in_proj(
            x, p["norm1"], w_in_l, layer, tm_in, "bf16",
            cast=((p["w_up"], p["w_down"], p["w_branch"], p["w_out"]), layer))
        w_layer = (wu[None], wd[None], wb.reshape(1, N_BRANCH, BRANCH_WIDTH, D_MODEL), wo[None])
        cast_next = ((p["w_in"],), layer + 1, _permute_w_in) if layer + 1 < depth else None
    else:
        z, zkv = _in_proj(x, p["norm1"], w_in_l, layer, tm_in, "dual")
        cast_next = None
    w_up_l, w_down_l, w_branch_l, w_out_l = w_layer
    out_a, conv_o, h_o = _lru(z, conv_buf, h0, state_layer, p["conv_w"], p["conv_b"], p["wa"], p["ba"],
                              p["wx"], p["bx"], p["lam"], layer, n_seq, t_len, _pick_tm(t_len, 1024))
    if caches is None:
        out_b = _swa_prompt(z, p["sinks"], layer, n_seq, t_len)
        out_c = _cb_prompt(z, p["cb_rows"], layer, n_seq, t_len)
        n_tail = CB_REACH
        kv3 = _in_proj(x, p["norm1"], w_in_l, layer, n_tail, "kv_f32", n_tiles=n_seq,
                       row_tile=lambda i: (i + 1) * (t_len // n_tail) - 1)[0].reshape(n_seq, n_tail, D_KV)
        kv = (kv3[:, n_tail - SWA_WINDOW:, KV_KB:KV_KB + SWA_KVW],
              kv3[:, n_tail - SWA_WINDOW:, KV_VB:KV_VB + SWA_KVW],
              kv3[:, :, KV_KC:KV_KC + CB_KVW], kv3[:, :, KV_VC:KV_VC + CB_KVW])
    else:
        ck_b, cv_b, ck_c, cv_c = caches
        out_b, kb, vb = _swa_step(z, zkv, ck_b, cv_b, p["sinks"], layer, n_seq)
        out_c, kc, vc = _cb_step(z, zkv, ck_c, cv_c, p["cb_rows"], layer, n_seq)
        kv = (kb, vb, kc, vc)
    x = _merge(x, out_a, out_b, out_c, z, w_branch_l, w_out_l, 0, _pick_tm(m, 256))
    res = _mlp(x, p["norm2"], p["final_g"], w_up_l, w_down_l, layer, _pick_tm(m, 512), 1024,
               final_norm=layer == depth - 1, cast=cast_next)
    w_in_next = res[1][None] if cast_next else None
    return res[0], conv_o, h_o[:, 0], kv, w_layer, w_in_next


def kernel(x_prompt, x_sample, state_conv, state_lru, cache_swa_k, cache_swa_v, cache_cb_k, cache_cb_v, norm1_g, w_in, conv_w, conv_b, lru_wa, lru_ba, lru_wx, lru_bx, lru_lambda, attn_sinks, rel_bias_table, w_branch, w_out, norm2_g, w_up, w_down, final_g):
    depth = w_in.shape[0]
    nb, s_len, _ = x_prompt.shape
    db, d_len, _ = x_sample.shape
    assert d_len == CHUNK and s_len % (Q_BLOCK * CB_QSUB) == 0 and s_len % SWA_TQ == 0
    assert s_len % CB_REACH == 0
    assert cache_swa_k.shape[2] == SWA_WINDOW and cache_cb_k.shape[2] == CB_REACH

    row = lambda v: v.reshape(depth, 1, -1)
    p = {
        "norm1": row(norm1_g), "norm2": row(norm2_g), "final_g": final_g.reshape(1, D_MODEL),
        "w_in": w_in, "w_up": w_up, "w_down": w_down,
        "w_branch": w_branch.reshape(depth, N_BRANCH * BRANCH_WIDTH, D_MODEL), "w_out": w_out,
        "conv_w": conv_w, "conv_b": row(conv_b),
        "wa": lru_wa.astype(BF16), "ba": row(lru_ba), "wx": lru_wx.astype(BF16), "bx": row(lru_bx),
        "lam": row(lru_lambda), "sinks": attn_sinks,
        "cb_rows": _cb_bias_rows(rel_bias_table),
    }
    w_in_l = _permute_w_in(w_in[0])[None]
    caches = (cache_swa_k.reshape(depth, db, SWA_WINDOW * SWA_KV_HEADS, HEAD_DIM),
              cache_swa_v.reshape(depth, db, SWA_WINDOW * SWA_KV_HEADS, HEAD_DIM),
              cache_cb_k.reshape(depth, db, CB_REACH * CB_HEADS, HEAD_DIM),
              cache_cb_v.reshape(depth, db, CB_REACH * CB_HEADS, HEAD_DIM))

    xp = x_prompt.reshape(nb * s_len, D_MODEL)
    xs = x_sample.reshape(db * d_len, D_MODEL)
    zero_conv = jnp.zeros((1, nb, CONV_W - 1, D_RNN), F32)
    zero_h = jnp.zeros((1, nb, 1, D_RNN), F32)
    h0_s = state_lru.reshape(depth, db, 1, D_RNN)
    heads = (SWA_KV_HEADS, SWA_KV_HEADS, CB_HEADS, CB_HEADS)
    outs = [[] for _ in range(12)]
    for l in range(depth):
        xp, conv_p, h_p, kv_p, w_layer, w_in_next = _layer(xp, nb, s_len, zero_conv, zero_h, 0, None, l, p,
                                                           w_in_l, None)
        xs, conv_s, h_s, kv_s, _, _ = _layer(xs, db, d_len, state_conv, h0_s, l, caches, l, p,
                                             w_in_l, w_layer)
        w_in_l = w_in_next
        outs[0].append(conv_p)
        outs[1].append(h_p)
        outs[6].append(conv_s)
        outs[7].append(h_s)
        for n in range(4):
            outs[2 + n].append(kv_p[n].reshape(nb, -1, heads[n], HEAD_DIM))
            outs[8 + n].append(kv_s[n].reshape(db, d_len, heads[n], HEAD_DIM))

    y_prompt = xp.reshape(nb, s_len, D_MODEL)
    y_sample = xs.reshape(db, d_len, D_MODEL)
    return (y_prompt, y_sample) + tuple(jnp.stack(o) for o in outs)
```

```python
import functools

import jax
import jax.numpy as jnp
from jax import lax
from jax.experimental import pallas as pl
from jax.experimental.pallas import tpu as pltpu

F32 = jnp.float32
BF16 = jnp.bfloat16

D_MODEL = 2048
CHUNK = 64
HEAD_DIM = 128
BRANCH_WIDTH = D_MODEL // 2
N_BRANCH = 3
D_RNN = BRANCH_WIDTH
LRU_BLOCKS = 8
LRU_BLOCK = D_RNN // LRU_BLOCKS
CONV_W = 4
LRU_C = 8.0
SWA_HEADS = BRANCH_WIDTH // HEAD_DIM
SWA_KV_HEADS = 2
SWA_GROUP = SWA_HEADS // SWA_KV_HEADS
SWA_WINDOW = 128
SWA_PREV = SWA_WINDOW // CHUNK
CB_HEADS = BRANCH_WIDTH // HEAD_DIM
CB_PREV = 8
CB_REACH = CB_PREV * CHUNK
REL_CLIP = 128
D_FF = 4 * D_MODEL
EPS = 1e-6
NEG = -1e30
ATTN_SCALE = HEAD_DIM ** -0.5
LOG2E = 1.4426950408889634

_O_AX, _O_AG, _O_QB, _O_KB, _O_VB, _O_QC, _O_KC, _O_VC, _O_GATES = (
    0, 1024, 2048, 3072, 3328, 3584, 4608, 5632, 6656)
D_IN = _O_GATES + N_BRANCH * D_MODEL
SWA_KVW = SWA_KV_HEADS * HEAD_DIM
CB_KVW = CB_HEADS * HEAD_DIM
D_KV = 2 * SWA_KVW + 2 * CB_KVW
D_Z = D_IN - D_KV
KV_KC, KV_VC, KV_KB, KV_VB = 0, 1024, 2048, 2304
Z_AX, Z_AG, Z_QB, Z_QC, Z_GATES = 0, 1024, 2048, 3072, 4096

VMEM_LIMIT_BYTES = 56 * 1024 * 1024

Q_BLOCK = 4 * CHUNK
Q_CHUNKS = Q_BLOCK // CHUNK
CB_BAND = (CB_PREV + Q_CHUNKS) * CHUNK
CB_QSUB = 32
CB_RLEN = 1024
assert CB_RLEN >= Q_BLOCK + CB_BAND - 1
CB_VARIANTS = 3


def _params(semantics):
    return pltpu.CompilerParams(dimension_semantics=semantics, vmem_limit_bytes=VMEM_LIMIT_BYTES)


def _rms(xf, g):
    return xf * lax.rsqrt(jnp.mean(xf * xf, axis=-1, keepdims=True) + EPS) * g


IN_TN = 1280
IN_KV_TILES = D_KV // IN_TN
IN_Z_TILES = D_Z // IN_TN


class _CastJobs:
    def __init__(self, weights, layer, grid, transform=None):
        self.weights, self.layer, self.transform = weights, layer, transform
        self.inner = grid[1]
        n_max = 1
        while n_max * 2 <= min(grid[0] * grid[1], MAX_CAST_STEPS):
            n_max *= 2
        self.n = []
        for w in weights:
            n = n_max
            while w.shape[1] % (n * BF16_ROWS):
                n //= 2
            self.n.append(n)

    def _spec(self, w, n, lead):
        slab = lambda i, j: jnp.minimum(i * self.inner + j, n - 1)
        if lead:
            return pl.BlockSpec((None, w.shape[1] // n, w.shape[2]), lambda i, j: (self.layer, slab(i, j), 0))
        return pl.BlockSpec((w.shape[1] // n, w.shape[2]), lambda i, j: (slab(i, j), 0))

    def in_specs(self):
        return [self._spec(w, n, True) for w, n in zip(self.weights, self.n)]

    def out_specs(self):
        return [self._spec(w, n, False) for w, n in zip(self.weights, self.n)]

    def out_shape(self):
        return [jax.ShapeDtypeStruct(w.shape[1:], BF16) for w in self.weights]

    def run(self, src_refs, dst_refs):
        step = pl.program_id(0) * self.inner + pl.program_id(1)
        for n in sorted(set(self.n)):
            @pl.when(step < n)
            def _():
                for s, d, n_w in zip(src_refs, dst_refs, self.n):
                    if n_w == n:
                        v = s[...]
                        d[...] = (self.transform(v) if self.transform else v).astype(BF16)


MAX_CAST_STEPS = 128
BF16_ROWS = 16


def _in_proj_kernel(*refs, jobs, mode):
    n_side = len(jobs.weights) if jobs else 0
    n_out = 2 if mode == "dual" else 1
    x_ref, g_ref, w_ref = refs[:3]
    side_in = refs[3:3 + n_side]
    outs = refs[3 + n_side:3 + n_side + n_out]
    side_out = refs[3 + n_side + n_out:3 + 2 * n_side + n_out]
    xn_ref = refs[-1]
    j = pl.program_id(1)

    @pl.when(j == 0)
    def _():
        xn_ref[...] = _rms(x_ref[...], g_ref[...]).astype(BF16)

    dot = lambda: jnp.dot(xn_ref[...], w_ref[...], preferred_element_type=F32)
    if mode == "bf16":
        outs[0][...] = dot().astype(BF16)
    elif mode == "kv_f32":
        outs[0][...] = dot()
    else:
        z_ref, zkv_ref = outs

        @pl.when(j < IN_Z_TILES)
        def _():
            z_ref[...] = dot().astype(BF16)

        @pl.when(j >= IN_Z_TILES)
        def _():
            zkv_ref[...] = dot()

    if jobs:
        jobs.run(side_in, side_out)


def _in_proj(x, g, w, layer, tm, mode, cast=None, row_tile=None, n_tiles=None):
    m = x.shape[0]
    col0 = 0
    if mode == "kv_f32":
        grid = (n_tiles, IN_KV_TILES)
        col0 = IN_Z_TILES
        x_map = lambda i, j: (row_tile(i), 0)
        out_specs = [pl.BlockSpec((tm, IN_TN), lambda i, j: (i, j))]
        out_shape = [jax.ShapeDtypeStruct((n_tiles * tm, D_KV), F32)]
    else:
        grid = (m // tm, D_IN // IN_TN)
        x_map = lambda i, j: (i, 0)
        if mode == "bf16":
            out_specs = [pl.BlockSpec((tm, IN_TN), lambda i, j: (i, j))]
            out_shape = [jax.ShapeDtypeStruct((m, D_IN), BF16)]
        else:
            out_specs = [pl.BlockSpec((tm, IN_TN), lambda i, j: (i, jnp.minimum(j, IN_Z_TILES - 1))),
                         pl.BlockSpec((tm, IN_TN), lambda i, j: (i, jnp.maximum(j - IN_Z_TILES, 0)))]
            out_shape = [jax.ShapeDtypeStruct((m, D_Z), BF16), jax.ShapeDtypeStruct((m, D_KV), F32)]
    jobs = _CastJobs(cast[0], cast[1], grid) if cast else None
    return pl.pallas_call(
        functools.partial(_in_proj_kernel, jobs=jobs, mode=mode),
        grid=grid,
        in_specs=[
            pl.BlockSpec((tm, D_MODEL), x_map),
            pl.BlockSpec((None, 1, D_MODEL), lambda i, j: (layer, 0, 0)),
            pl.BlockSpec((None, D_MODEL, IN_TN), lambda i, j: (0, 0, col0 + j)),
        ] + (jobs.in_specs() if jobs else []),
        out_specs=out_specs + (jobs.out_specs() if jobs else []),
        out_shape=out_shape + (jobs.out_shape() if jobs else []),
        scratch_shapes=[pltpu.VMEM((tm, D_MODEL), BF16)],
        compiler_params=_params(("arbitrary", "arbitrary")),
        name="in_proj",
    )(x, g, w, *(cast[0] if cast else ()))


_XPAD = 8
LANES = 128
_SEGS = 8
_SEG_LEN = 4


def _sigmoid(x):
    return 0.5 * (jnp.tanh(0.5 * x) + 1.0)


def _lru_kernel(ax_ref, ag_ref, cbuf_ref, h0_ref, cw_ref, cb_ref, wa_ref, ba_ref, wx_ref, bx_ref,
                lam_ref, out_ref, convo_ref, ho_ref, xbuf, a_s, b_s, h_s):
    t = pl.program_id(1)
    nt = pl.num_programs(1)
    tt = ax_ref.shape[0]

    @pl.when(t == 0)
    def _():
        xbuf[...] = jnp.zeros_like(xbuf)
        xbuf[_XPAD - (CONV_W - 1):, :] = cbuf_ref[...]
        h_s[...] = h0_ref[...]

    x = ax_ref[...].astype(F32)
    xe = jnp.concatenate([xbuf[...], x], axis=0)
    acc = xe * cw_ref[0:1, :]
    for k in range(1, CONV_W):
        acc = xe * cw_ref[k:k + 1, :] + pltpu.roll(acc, 1, 0)
    u = cb_ref[...] + acc[_XPAD:, :]
    tail = x[tt - (CONV_W - 1):, :]
    xbuf[...] = x[tt - _XPAD:, :]

    ub = u.astype(BF16)
    r_parts, i_parts = [], []
    for n in range(LRU_BLOCKS):
        un = ub[:, n * LRU_BLOCK:(n + 1) * LRU_BLOCK]
        r_parts.append(jnp.dot(un, wa_ref[n], preferred_element_type=F32))
        i_parts.append(jnp.dot(un, wx_ref[n], preferred_element_type=F32))
    r = _sigmoid(jnp.concatenate(r_parts, axis=1) + ba_ref[...])
    i = _sigmoid(jnp.concatenate(i_parts, axis=1) + bx_ref[...])
    log_a = -LRU_C * r * jax.nn.softplus(-lam_ref[...])
    a = jnp.exp(log_a)
    y = 1.0 - a * a
    b = jnp.where(y > 0.0, y * lax.rsqrt(y), 0.0) * (i * u)
    n_lg = D_RNN // LANES
    for lg in range(n_lg):
        a_s[lg] = a[:, lg * LANES:(lg + 1) * LANES]
        b_s[lg] = b[:, lg * LANES:(lg + 1) * LANES]

    row = lax.broadcasted_iota(jnp.int32, (_SEGS, LANES), 0)
    sub = _SEGS * _SEG_LEN

    def body(sb, h):
        r0 = pl.multiple_of(sb * sub, sub)
        step = lambda j: pl.ds(r0 + j, _SEGS, stride=_SEG_LEN)
        h_next = []
        for lg in range(n_lg):
            a_g, b_g = a_s.at[lg], b_s.at[lg]
            h_g = h[:, lg * LANES:(lg + 1) * LANES]
            acs, bcs = [a_g[step(0), :]], [b_g[step(0), :]]
            for j in range(1, _SEG_LEN):
                aj = a_g[step(j), :]
                bcs.append(aj * bcs[-1] + b_g[step(j), :])
                acs.append(aj * acs[-1])
            at, bt = acs[-1], bcs[-1]
            for s in (1, 2, 4):
                m = row >= s
                bt_new = jnp.where(m, at * pltpu.roll(bt, s, 0) + bt, bt)
                at = jnp.where(m, at * pltpu.roll(at, s, 0), at)
                bt = bt_new
            after = at * h_g + bt
            entry = jnp.where(row >= 1, pltpu.roll(after, 1, 0), h_g)
            for j in range(_SEG_LEN):
                b_g[step(j), :] = acs[j] * entry + bcs[j]
            h_next.append(after[_SEGS - 1:_SEGS, :])
        return jnp.concatenate(h_next, axis=1)

    h = lax.fori_loop(0, tt // sub, body, h_s[...])
    h_s[...] = h
    hs = jnp.concatenate([b_s[lg] for lg in range(n_lg)], axis=1)
    out_ref[...] = (hs * jax.nn.gelu(ag_ref[...].astype(F32))).astype(BF16)

    @pl.when(t == nt - 1)
    def _():
        convo_ref[...] = tail
        ho_ref[...] = h


def _lru(z, conv_buf, h0, state_layer, cw, cb, wa, ba, wx, bx, lam, layer, n_seq, t_len, tt):
    nt = t_len // tt
    row = lambda b, t: b * nt + t
    vec = lambda: pl.BlockSpec((None, 1, D_RNN), lambda b, t: (layer, 0, 0))
    blk = lambda: pl.BlockSpec((None, LRU_BLOCKS, LRU_BLOCK, LRU_BLOCK), lambda b, t: (layer, 0, 0, 0))
    return pl.pallas_call(
        _lru_kernel,
        grid=(n_seq, nt),
        in_specs=[
            pl.BlockSpec((tt, D_RNN), lambda b, t: (row(b, t), Z_AX // D_RNN)),
            pl.BlockSpec((tt, D_RNN), lambda b, t: (row(b, t), Z_AG // D_RNN)),
            pl.BlockSpec((None, None, CONV_W - 1, D_RNN), lambda b, t: (state_layer, b, 0, 0)),
            pl.BlockSpec((None, None, 1, D_RNN), lambda b, t: (state_layer, b, 0, 0)),
            pl.BlockSpec((None, CONV_W, D_RNN), lambda b, t: (layer, 0, 0)),
            vec(), blk(), vec(), blk(), vec(), vec(),
        ],
        out_specs=[
            pl.BlockSpec((tt, D_RNN), lambda b, t: (row(b, t), 0)),
            pl.BlockSpec((None, CONV_W - 1, D_RNN), lambda b, t: (b, 0, 0)),
            pl.BlockSpec((None, 1, D_RNN), lambda b, t: (b, 0, 0)),
        ],
        out_shape=[
            jax.ShapeDtypeStruct((n_seq * t_len, D_RNN), BF16),
            jax.ShapeDtypeStruct((n_seq, CONV_W - 1, D_RNN), F32),
            jax.ShapeDtypeStruct((n_seq, 1, D_RNN), F32),
        ],
        scratch_shapes=[
            pltpu.VMEM((_XPAD, D_RNN), F32),
            pltpu.VMEM((D_RNN // LANES, tt, LANES), F32),
            pltpu.VMEM((D_RNN // LANES, tt, LANES), F32),
            pltpu.VMEM((1, D_RNN), F32),
        ],
        compiler_params=_params(("parallel", "arbitrary")),
        name="lru",
    )(z, z, conv_buf, h0, cw, cb, wa, ba, wx, bx, lam)


def _dot_nt(a, b):
    return lax.dot_general(a, b, (((1,), (1,)), ((), ())), preferred_element_type=F32)


def _with_ones(v):
    return jnp.concatenate([v, jnp.ones_like(v)], axis=1)


def _swa_chunk(q, kband, vext, sinks, valid):
    qst = jnp.concatenate([q[:, g * HEAD_DIM:(g + 1) * HEAD_DIM] for g in range(SWA_GROUP)], axis=0)
    s = _dot_nt(qst, kband) * (ATTN_SCALE * LOG2E)
    if valid is not None:
        s = jnp.where(valid, s, NEG)
    es, sink_e = [], []
    for g in range(SWA_GROUP):
        sg = s[g * CHUNK:(g + 1) * CHUNK, :]
        sink2 = sinks[g] * LOG2E
        m = jnp.maximum(jnp.max(sg, axis=-1, keepdims=True), sink2)
        es.append(jnp.exp2(sg - m).astype(BF16))
        sink_e.append(jnp.exp2(sink2 - m))
    r = jnp.dot(jnp.concatenate(es, axis=0), vext, preferred_element_type=F32)
    outs = []
    for g in range(SWA_GROUP):
        rg = r[g * CHUNK:(g + 1) * CHUNK, :]
        outs.append(rg[:, :HEAD_DIM] * (1.0 / (rg[:, HEAD_DIM:] + sink_e[g])))
    return jnp.concatenate(outs, axis=1)


def _cb_attend(q, kband, vext, bias2):
    s = _dot_nt(q, kband) * (ATTN_SCALE * LOG2E) + bias2
    m = jnp.max(s, axis=-1, keepdims=True)
    r = jnp.dot(jnp.exp2(s - m).astype(BF16), vext, preferred_element_type=F32)
    return r[:, :HEAD_DIM] * (1.0 / r[:, HEAD_DIM:])


def _cb_bias_block(r, variant, n_rows, n_cols):
    t = pltpu.roll(jnp.broadcast_to(r, (n_rows, CB_RLEN)), 0, 1, stride=1, stride_axis=0)[:, :n_cols]
    c0 = variant * Q_CHUNKS
    sc = max(c0 - CB_PREV, 0)
    qc = c0 + lax.broadcasted_iota(jnp.int32, (n_rows, n_cols), 0) // CHUNK
    kc = sc + lax.broadcasted_iota(jnp.int32, (n_rows, n_cols), 1) // CHUNK
    return jnp.where(kc <= qc, jnp.where(kc >= qc - CB_PREV, t * LOG2E, NEG), NEG)


def _cast_rows(dst, src, n_rows, step, ones=False):
    def body(i, c):
        r0 = pl.multiple_of(i * step, step)
        v = src[pl.ds(r0, step), :].astype(BF16)
        dst[pl.ds(r0, step), :] = _with_ones(v) if ones else v
        return c
    lax.fori_loop(0, n_rows // step, body, 0)


SWA_BAND = (SWA_PREV + 1) * CHUNK
SWA_TQ = 128 * CHUNK


def _swa_prompt_kernel(sink_ref, q_ref, k_ref, v_ref, o_ref, vb_s, *, layer):
    kh = pl.program_id(1)
    qi = pl.program_id(2)
    s_len = k_ref.shape[0]

    @pl.when(qi == 0)
    def _():
        _cast_rows(vb_s, v_ref, s_len, 512, ones=True)

    sinks = [sink_ref[layer, kh * SWA_GROUP + g] for g in range(SWA_GROUP)]
    jchunk = lax.broadcasted_iota(jnp.int32, (1, SWA_BAND), 1) // CHUNK
    n_chunks = q_ref.shape[0] // CHUNK
    for c in range(n_chunks):
        cg = qi * n_chunks + c
        sc = jnp.maximum(cg - SWA_PREV, 0)
        s0 = pl.multiple_of(sc * CHUNK, CHUNK)
        valid = (jchunk + sc) <= cg
        o = _swa_chunk(q_ref[c * CHUNK:(c + 1) * CHUNK, :], k_ref[pl.ds(s0, SWA_BAND), :],
                       vb_s[pl.ds(s0, SWA_BAND), :], sinks, valid)
        o_ref[c * CHUNK:(c + 1) * CHUNK, :] = o.astype(BF16)


def _swa_prompt(z, sinks, layer, n_seq, s_len):
    tq = min(SWA_TQ, s_len)
    nq = s_len // tq
    gw = SWA_GROUP * HEAD_DIM
    return pl.pallas_call(
        functools.partial(_swa_prompt_kernel, layer=layer),
        grid=(n_seq, SWA_KV_HEADS, nq),
        in_specs=[
            pl.BlockSpec(memory_space=pltpu.SMEM),
            pl.BlockSpec((tq, gw), lambda b, k, q: (b * nq + q, Z_QB // gw + k)),
            pl.BlockSpec((s_len, HEAD_DIM), lambda b, k, q: (b, (D_Z + KV_KB) // HEAD_DIM + k)),
            pl.BlockSpec((s_len, HEAD_DIM), lambda b, k, q: (b, (D_Z + KV_VB) // HEAD_DIM + k)),
        ],
        out_specs=pl.BlockSpec((tq, gw), lambda b, k, q: (b * nq + q, k)),
        out_shape=jax.ShapeDtypeStruct((n_seq * s_len, BRANCH_WIDTH), BF16),
        scratch_shapes=[pltpu.VMEM((s_len, 2 * HEAD_DIM), BF16)],
        compiler_params=_params(("parallel", "parallel", "arbitrary")),
        name="swa_prompt",
    )(sinks, z, z, z)


def _swa_step_kernel(sink_ref, q_ref, k_ref, v_ref, ck_ref, cv_ref, o_ref, ko_ref, vo_ref, *, layer):
    n_past = ck_ref.shape[0] // SWA_KV_HEADS
    gw = SWA_GROUP * HEAD_DIM
    for kh in range(SWA_KV_HEADS):
        cs = slice(kh * HEAD_DIM, (kh + 1) * HEAD_DIM)
        kn = k_ref[:, cs]
        vn = v_ref[:, cs]
        kfull = jnp.concatenate(
            [ck_ref[pl.ds(kh, n_past, stride=SWA_KV_HEADS), :].astype(BF16), kn.astype(BF16)], axis=0)
        vfull = jnp.concatenate(
            [cv_ref[pl.ds(kh, n_past, stride=SWA_KV_HEADS), :].astype(BF16), vn.astype(BF16)], axis=0)
        sinks = [sink_ref[layer, kh * SWA_GROUP + g] for g in range(SWA_GROUP)]
        o = _swa_chunk(q_ref[:, kh * gw:(kh + 1) * gw], kfull, _with_ones(vfull), sinks, None)
        o_ref[:, kh * gw:(kh + 1) * gw] = o.astype(BF16)
        ko_ref[pl.ds(kh, CHUNK, stride=SWA_KV_HEADS), :] = kn
        vo_ref[pl.ds(kh, CHUNK, stride=SWA_KV_HEADS), :] = vn


def _swa_step(z, zkv, cache_k, cache_v, sinks, layer, n_seq):
    rows_past = cache_k.shape[2]
    rows_new = CHUNK * SWA_KV_HEADS
    cache = lambda: pl.BlockSpec((None, None, rows_past, HEAD_DIM), lambda b: (layer, b, 0, 0))
    new = lambda: pl.BlockSpec((rows_new, HEAD_DIM), lambda b: (b, 0))
    return pl.pallas_call(
        functools.partial(_swa_step_kernel, layer=layer),
        grid=(n_seq,),
        in_specs=[
            pl.BlockSpec(memory_space=pltpu.SMEM),
            pl.BlockSpec((CHUNK, BRANCH_WIDTH), lambda b: (b, Z_QB // BRANCH_WIDTH)),
            pl.BlockSpec((CHUNK, SWA_KVW), lambda b: (b, KV_KB // SWA_KVW)),
            pl.BlockSpec((CHUNK, SWA_KVW), lambda b: (b, KV_VB // SWA_KVW)),
            cache(), cache(),
        ],
        out_specs=[pl.BlockSpec((CHUNK, BRANCH_WIDTH), lambda b: (b, 0)), new(), new()],
        out_shape=[jax.ShapeDtypeStruct((n_seq * CHUNK, BRANCH_WIDTH), BF16),
                   jax.ShapeDtypeStruct((n_seq * rows_new, HEAD_DIM), F32),
                   jax.ShapeDtypeStruct((n_seq * rows_new, HEAD_DIM), F32)],
        compiler_params=_params(("parallel",)),
        name="swa_step",
    )(sinks, z, zkv, zkv, cache_k, cache_v)


def _cb_prompt_kernel(q_ref, k_ref, v_ref, r_ref, o_ref, vb_s, bias_s):
    qi = pl.program_id(2)
    s_len = k_ref.shape[0]

    @pl.when(qi == 0)
    def _():
        _cast_rows(vb_s, v_ref, s_len, 512, ones=True)
        for v in range(CB_VARIANTS):
            bias_s[v] = _cb_bias_block(r_ref[v], v, Q_BLOCK, CB_BAND)

    n_sub = q_ref.shape[0] // Q_BLOCK
    for sub in range(n_sub):
        blk = qi * n_sub + sub
        sc = jnp.maximum(blk * Q_CHUNKS - CB_PREV, 0)
        s0 = pl.multiple_of(sc * CHUNK, CHUNK)
        rows = slice(sub * Q_BLOCK, (sub + 1) * Q_BLOCK)
        o = _cb_attend(q_ref[rows, :], k_ref[pl.ds(s0, CB_BAND), :], vb_s[pl.ds(s0, CB_BAND), :],
                       bias_s[jnp.minimum(blk, CB_VARIANTS - 1)])
        o_ref[rows, :] = o.astype(BF16)


def _cb_prompt(z, rows, layer, n_seq, s_len):
    tq = min(Q_BLOCK * CB_QSUB, s_len)
    nq = s_len // tq
    return pl.pallas_call(
        _cb_prompt_kernel,
        grid=(n_seq, CB_HEADS, nq),
        in_specs=[
            pl.BlockSpec((tq, HEAD_DIM), lambda b, h, q: (b * nq + q, Z_QC // HEAD_DIM + h)),
            pl.BlockSpec((s_len, HEAD_DIM), lambda b, h, q: (b, (D_Z + KV_KC) // HEAD_DIM + h)),
            pl.BlockSpec((s_len, HEAD_DIM), lambda b, h, q: (b, (D_Z + KV_VC) // HEAD_DIM + h)),
            pl.BlockSpec((None, CB_VARIANTS, None, 1, CB_RLEN), lambda b, h, q: (layer, 0, h, 0, 0)),
        ],
        out_specs=pl.BlockSpec((tq, HEAD_DIM), lambda b, h, q: (b * nq + q, h)),
        out_shape=jax.ShapeDtypeStruct((n_seq * s_len, BRANCH_WIDTH), BF16),
        scratch_shapes=[pltpu.VMEM((s_len, 2 * HEAD_DIM), BF16),
                        pltpu.VMEM((CB_VARIANTS, Q_BLOCK, CB_BAND), F32)],
        compiler_params=_params(("parallel", "parallel", "arbitrary")),
        name="cb_prompt",
    )(z, z, z, rows)


def _cb_step_kernel(q_ref, k_ref, v_ref, ck_ref, cv_ref, r_ref, o_ref, ko_ref, vo_ref, bias_s):
    n_past = ck_ref.shape[0] // CB_HEADS
    n_k = n_past + CHUNK

    @pl.when(pl.program_id(0) == 0)
    def _():
        for h in range(CB_HEADS):
            bias_s[h] = _cb_bias_block(r_ref[h], CB_VARIANTS - 1, CHUNK, n_k)

    for h in range(CB_HEADS):
        cs = slice(h * HEAD_DIM, (h + 1) * HEAD_DIM)
        kn = k_ref[:, cs]
        vn = v_ref[:, cs]
        kfull = jnp.concatenate(
            [ck_ref[pl.ds(h, n_past, stride=CB_HEADS), :].astype(BF16), kn.astype(BF16)], axis=0)
        vfull = jnp.concatenate(
            [cv_ref[pl.ds(h, n_past, stride=CB_HEADS), :].astype(BF16), vn.astype(BF16)], axis=0)
        o_ref[:, cs] = _cb_attend(q_ref[:, cs], kfull, _with_ones(vfull), bias_s[h]).astype(BF16)
        ko_ref[pl.ds(h, CHUNK, stride=CB_HEADS), :] = kn
        vo_ref[pl.ds(h, CHUNK, stride=CB_HEADS), :] = vn


def _cb_step(z, zkv, cache_k, cache_v, rows, layer, n_seq):
    rows_past = cache_k.shape[2]
    n_k = rows_past // CB_HEADS + CHUNK
    rows_new = CHUNK * CB_HEADS
    cache = lambda: pl.BlockSpec((None, None, rows_past, HEAD_DIM), lambda b: (layer, b, 0, 0))
    new = lambda: pl.BlockSpec((rows_new, HEAD_DIM), lambda b: (b, 0))
    return pl.pallas_call(
        _cb_step_kernel,
        grid=(n_seq,),
        in_specs=[
            pl.BlockSpec((CHUNK, BRANCH_WIDTH), lambda b: (b, Z_QC // BRANCH_WIDTH)),
            pl.BlockSpec((CHUNK, CB_KVW), lambda b: (b, KV_KC // CB_KVW)),
            pl.BlockSpec((CHUNK, CB_KVW), lambda b: (b, KV_VC // CB_KVW)),
            cache(), cache(),
            pl.BlockSpec((None, None, CB_HEADS, 1, CB_RLEN),
                         lambda b: (layer, CB_VARIANTS - 1, 0, 0, 0)),
        ],
        out_specs=[pl.BlockSpec((CHUNK, BRANCH_WIDTH), lambda b: (b, 0)), new(), new()],
        out_shape=[jax.ShapeDtypeStruct((n_seq * CHUNK, BRANCH_WIDTH), BF16),
                   jax.ShapeDtypeStruct((n_seq * rows_new, HEAD_DIM), F32),
                   jax.ShapeDtypeStruct((n_seq * rows_new, HEAD_DIM), F32)],
        scratch_shapes=[pltpu.VMEM((CB_HEADS, CHUNK, n_k), F32)],
        compiler_params=_params(("arbitrary",)),
        name="cb_step",
    )(z, zkv, zkv, cache_k, cache_v, rows)


def _merge_kernel(x_ref, a_ref, b_ref, c_ref, ga_ref, gb_ref, gc_ref, wb_ref, wo_ref, o_ref):
    mixed = None
    for r, (br, gr) in enumerate(((a_ref, ga_ref), (b_ref, gb_ref), (c_ref, gc_ref))):
        proj = jnp.dot(br[...], wb_ref[r], preferred_element_type=F32)
        gate = jax.nn.sigmoid(gr[...].astype(F32))
        mixed = gate * proj if mixed is None else mixed + gate * proj
    o_ref[...] = x_ref[...] + jnp.dot(mixed.astype(BF16), wo_ref[...], preferred_element_type=F32)


def _merge(x, out_a, out_b, out_c, z, w_branch, w_out, layer, tm):
    m = x.shape[0]
    branch = lambda: pl.BlockSpec((tm, BRANCH_WIDTH), lambda i: (i, 0))
    gate = lambda r: pl.BlockSpec((tm, D_MODEL), lambda i: (i, Z_GATES // D_MODEL + r))
    return pl.pallas_call(
        _merge_kernel,
        grid=(m // tm,),
        in_specs=[
            pl.BlockSpec((tm, D_MODEL), lambda i: (i, 0)),
            branch(), branch(), branch(),
            gate(0), gate(1), gate(2),
            pl.BlockSpec((None, N_BRANCH, BRANCH_WIDTH, D_MODEL), lambda i: (layer, 0, 0, 0),
                         pipeline_mode=pl.Buffered(1)),
            pl.BlockSpec((None, D_MODEL, D_MODEL), lambda i: (layer, 0, 0),
                         pipeline_mode=pl.Buffered(1)),
        ],
        out_specs=pl.BlockSpec((tm, D_MODEL), lambda i: (i, 0)),
        out_shape=jax.ShapeDtypeStruct((m, D_MODEL), F32),
        compiler_params=_params(("parallel",)),
        name="merge",
    )(x, out_a, out_b, out_c, z, z, z, w_branch, w_out)


def _mlp_kernel(*refs, final_norm, jobs):
    n_side = len(jobs.weights) if jobs else 0
    x_ref, g_ref, fg_ref, wu_ref, wd_ref = refs[:5]
    side_in = refs[5:5 + n_side]
    o_ref = refs[5 + n_side]
    side_out = refs[6 + n_side:6 + 2 * n_side]
    hn_ref = refs[-1]
    j = pl.program_id(1)

    @pl.when(j == 0)
    def _():
        xf = x_ref[...]
        hn_ref[...] = _rms(xf, g_ref[...]).astype(BF16)
        o_ref[...] = xf

    h = jnp.dot(hn_ref[...], wu_ref[...], preferred_element_type=F32)
    h = jnp.square(jnp.maximum(h, 0.0)).astype(BF16)
    o_ref[...] += jnp.dot(h, wd_ref[...], preferred_element_type=F32)

    if final_norm:
        @pl.when(j == pl.num_programs(1) - 1)
        def _():
            o_ref[...] = _rms(o_ref[...], fg_ref[...])

    if jobs:
        jobs.run(side_in, side_out)


def _mlp(x, g, final_g, w_up, w_down, layer, tm, tf, final_norm, cast=None):
    m = x.shape[0]
    grid = (m // tm, D_FF // tf)
    jobs = _CastJobs(cast[0], cast[1], grid, cast[2]) if cast else None
    return pl.pallas_call(
        functools.partial(_mlp_kernel, final_norm=final_norm, jobs=jobs),
        grid=grid,
        in_specs=[
            pl.BlockSpec((tm, D_MODEL), lambda i, j: (i, 0)),
            pl.BlockSpec((None, 1, D_MODEL), lambda i, j: (layer, 0, 0)),
            pl.BlockSpec((1, D_MODEL), lambda i, j: (0, 0)),
            pl.BlockSpec((None, D_MODEL, tf), lambda i, j: (0, 0, j)),
            pl.BlockSpec((None, tf, D_MODEL), lambda i, j: (0, j, 0)),
        ] + (jobs.in_specs() if jobs else []),
        out_specs=[pl.BlockSpec((tm, D_MODEL), lambda i, j: (i, 0))] + (jobs.out_specs() if jobs else []),
        out_shape=[jax.ShapeDtypeStruct((m, D_MODEL), F32)] + (jobs.out_shape() if jobs else []),
        scratch_shapes=[pltpu.VMEM((tm, D_MODEL), BF16)],
        compiler_params=_params(("arbitrary", "arbitrary")),
        name="mlp",
    )(x, g, final_g, w_up, w_down, *(cast[0] if cast else ()))


def _permute_w_in(w_in):
    seg = lambda o, w: w_in[..., o:o + w]
    return jnp.concatenate([
        seg(_O_AX, D_RNN), seg(_O_AG, D_RNN), seg(_O_QB, BRANCH_WIDTH), seg(_O_QC, BRANCH_WIDTH),
        seg(_O_GATES, N_BRANCH * D_MODEL),
        seg(_O_KC, CB_KVW), seg(_O_VC, CB_KVW), seg(_O_KB, SWA_KVW), seg(_O_VB, SWA_KVW)],
        axis=-1).astype(BF16)


def _cb_bias_rows(table):
    m = jnp.arange(CB_RLEN)
    rel = jnp.where(m < CB_BAND, m, m - CB_RLEN)
    out = []
    for v in range(CB_VARIANTS):
        c0 = v * Q_CHUNKS
        sc = max(c0 - CB_PREV, 0)
        d = (c0 - sc) * CHUNK - rel
        idx = jnp.clip(d, -REL_CLIP, REL_CLIP) + REL_CLIP
        out.append(table.astype(F32)[:, :, idx])
    return jnp.stack(out, axis=1)[:, :, :, None, :]


def _pick_tm(m, want):
    tm = min(want, m)
    while m % tm:
        tm //= 2
    return tm


def _layer(x, n_seq, t_len, conv_buf, h0, state_layer, caches, layer, p, w_in_l, w_layer):
    m = x.shape[0]
    depth = p["w_up"].shape[0]
    tm_in = _pick_tm(m, 1024)
    if w_layer is None:
        z, wu, wd, wb, wo = _in_proj(
            x, p["norm1"], w_in_l, layer, tm_in, "bf16",
            cast=((p["w_up"], p["w_down"], p["w_branch"], p["w_out"]), layer))
        w_layer = (wu[None], wd[None], wb.reshape(1, N_BRANCH, BRANCH_WIDTH, D_MODEL), wo[None])
        cast_next = ((p["w_in"],), layer + 1, _permute_w_in) if layer + 1 < depth else None
    else:
        z, zkv = _in_proj(x, p["norm1"], w_in_l, layer, tm_in, "dual")
        cast_next = None
    w_up_l, w_down_l, w_branch_l, w_out_l = w_layer
    out_a, conv_o, h_o = _lru(z, conv_buf, h0, state_layer, p["conv_w"], p["conv_b"], p["wa"], p["ba"],
                              p["wx"], p["bx"], p["lam"], layer, n_seq, t_len, _pick_tm(t_len, 1024))
    if caches is None:
        out_b = _swa_prompt(z, p["sinks"], layer, n_seq, t_len)
        out_c = _cb_prompt(z, p["cb_rows"], layer, n_seq, t_len)
        n_tail = CB_REACH
        kv3 = _in_proj(x, p["norm1"], w_in_l, layer, n_tail, "kv_f32", n_tiles=n_seq,
                       row_tile=lambda i: (i + 1) * (t_len // n_tail) - 1)[0].reshape(n_seq, n_tail, D_KV)
        kv = (kv3[:, n_tail - SWA_WINDOW:, KV_KB:KV_KB + SWA_KVW],
              kv3[:, n_tail - SWA_WINDOW:, KV_VB:KV_VB + SWA_KVW],
              kv3[:, :, KV_KC:KV_KC + CB_KVW], kv3[:, :, KV_VC:KV_VC + CB_KVW])
    else:
        ck_b, cv_b, ck_c, cv_c = caches
        out_b, kb, vb = _swa_step(z, zkv, ck_b, cv_b, p["sinks"], layer, n_seq)
        out_c, kc, vc = _cb_step(z, zkv, ck_c, cv_c, p["cb_rows"], layer, n_seq)
        kv = (kb, vb, kc, vc)
    x = _merge(x, out_a, out_b, out_c, z, w_branch_l, w_out_l, 0, _pick_tm(m, 256))
    res = _mlp(x, p["norm2"], p["final_g"], w_up_l, w_down_l, layer, _pick_tm(m, 512), 1024,
               final_norm=layer == depth - 1, cast=cast_next)
    w_in_next = res[1][None] if cast_next else None
    return res[0], conv_o, h_o[:, 0], kv, w_layer, w_in_next


def kernel(x_prompt, x_sample, state_conv, state_lru, cache_swa_k, cache_swa_v, cache_cb_k, cache_cb_v, norm1_g, w_in, conv_w, conv_b, lru_wa, lru_ba, lru_wx, lru_bx, lru_lambda, attn_sinks, rel_bias_table, w_branch, w_out, norm2_g, w_up, w_down, final_g):
    depth = w_in.shape[0]
    nb, s_len, _ = x_prompt.shape
    db, d_len, _ = x_sample.shape
    assert d_len == CHUNK and s_len % min(Q_BLOCK * CB_QSUB, s_len) == 0 and s_len % min(SWA_TQ, s_len) == 0
    assert s_len % Q_BLOCK == 0 and s_len >= CB_BAND
    assert s_len % CB_REACH == 0
    assert cache_swa_k.shape[2] == SWA_WINDOW and cache_cb_k.shape[2] == CB_REACH

    row = lambda v: v.reshape(depth, 1, -1)
    p = {
        "norm1": row(norm1_g), "norm2": row(norm2_g), "final_g": final_g.reshape(1, D_MODEL),
        "w_in": w_in, "w_up": w_up, "w_down": w_down,
        "w_branch": w_branch.reshape(depth, N_BRANCH * BRANCH_WIDTH, D_MODEL), "w_out": w_out,
        "conv_w": conv_w, "conv_b": row(conv_b),
        "wa": lru_wa.astype(BF16), "ba": row(lru_ba), "wx": lru_wx.astype(BF16), "bx": row(lru_bx),
        "lam": row(lru_lambda), "sinks": attn_sinks,
        "cb_rows": _cb_bias_rows(rel_bias_table),
    }
    w_in_l = _permute_w_in(w_in[0])[None]
    caches = (cache_swa_k.reshape(depth, db, SWA_WINDOW * SWA_KV_HEADS, HEAD_DIM),
              cache_swa_v.reshape(depth, db, SWA_WINDOW * SWA_KV_HEADS, HEAD_DIM),
              cache_cb_k.reshape(depth, db, CB_REACH * CB_HEADS, HEAD_DIM),
              cache_cb_v.reshape(depth, db, CB_REACH * CB_HEADS, HEAD_DIM))

    xp = x_prompt.reshape(nb * s_len, D_MODEL)
    xs = x_sample.reshape(db * d_len, D_MODEL)
    zero_conv = jnp.zeros((1, nb, CONV_W - 1, D_RNN), F32)
    zero_h = jnp.zeros((1, nb, 1, D_RNN), F32)
    h0_s = state_lru.reshape(depth, db, 1, D_RNN)
    heads = (SWA_KV_HEADS, SWA_KV_HEADS, CB_HEADS, CB_HEADS)
    outs = [[] for _ in range(12)]
    for l in range(depth):
        xp, conv_p, h_p, kv_p, w_layer, w_in_next = _layer(xp, nb, s_len, zero_conv, zero_h, 0, None, l, p,
                                                           w_in_l, None)
        xs, conv_s, h_s, kv_s, _, _ = _layer(xs, db, d_len, state_conv, h0_s, l, caches, l, p,
                                             w_in_l, w_layer)
        w_in_l = w_in_next
        outs[0].append(conv_p)
        outs[1].append(h_p)
        outs[6].append(conv_s)
        outs[7].append(h_s)
        for n in range(4):
            outs[2 + n].append(kv_p[n].reshape(nb, -1, heads[n], HEAD_DIM))
            outs[8 + n].append(kv_s[n].reshape(db, d_len, heads[n], HEAD_DIM))

    y_prompt = xp.reshape(nb, s_len, D_MODEL)
    y_sample = xs.reshape(db, d_len, D_MODEL)
    return (y_prompt, y_sample) + tuple(jnp.stack(o) for o in outs)
```

```python
import functools

import jax
import jax.numpy as jnp
from jax import lax
from jax.experimental import pallas as pl
from jax.experimental.pallas import tpu as pltpu

F32 = jnp.float32
BF16 = jnp.bfloat16

D_MODEL = 2048
CHUNK = 64
HEAD_DIM = 128
BRANCH_WIDTH = D_MODEL // 2
N_BRANCH = 3
D_RNN = BRANCH_WIDTH
LRU_BLOCKS = 8
LRU_BLOCK = D_RNN // LRU_BLOCKS
CONV_W = 4
LRU_C = 8.0
SWA_HEADS = BRANCH_WIDTH // HEAD_DIM
SWA_KV_HEADS = 2
SWA_GROUP = SWA_HEADS // SWA_KV_HEADS
SWA_WINDOW = 128
SWA_PREV = SWA_WINDOW // CHUNK
CB_HEADS = BRANCH_WIDTH // HEAD_DIM
CB_PREV = 8
CB_REACH = CB_PREV * CHUNK
REL_CLIP = 128
D_FF = 4 * D_MODEL
EPS = 1e-6
NEG = -1e30
ATTN_SCALE = HEAD_DIM ** -0.5
LOG2E = 1.4426950408889634

_O_AX, _O_AG, _O_QB, _O_KB, _O_VB, _O_QC, _O_KC, _O_VC, _O_GATES = (
    0, 1024, 2048, 3072, 3328, 3584, 4608, 5632, 6656)
D_IN = _O_GATES + N_BRANCH * D_MODEL
SWA_KVW = SWA_KV_HEADS * HEAD_DIM
CB_KVW = CB_HEADS * HEAD_DIM
D_KV = 2 * SWA_KVW + 2 * CB_KVW
D_Z = D_IN - D_KV
KV_KC, KV_VC, KV_KB, KV_VB = 0, 1024, 2048, 2304
Z_AX, Z_AG, Z_QB, Z_QC, Z_GATES = 0, 1024, 2048, 3072, 4096

VMEM_LIMIT_BYTES = 56 * 1024 * 1024

Q_BLOCK = 4 * CHUNK
Q_CHUNKS = Q_BLOCK // CHUNK
CB_BAND = (CB_PREV + Q_CHUNKS) * CHUNK
CB_QSUB = 32
CB_RLEN = 1024
assert CB_RLEN >= Q_BLOCK + CB_BAND - 1
CB_VARIANTS = 3


def _params(semantics):
    return pltpu.CompilerParams(dimension_semantics=semantics, vmem_limit_bytes=VMEM_LIMIT_BYTES)


def _rms(xf, g):
    return xf * lax.rsqrt(jnp.mean(xf * xf, axis=-1, keepdims=True) + EPS) * g


IN_TN = 1280
IN_KV_TILES = D_KV // IN_TN
IN_Z_TILES = D_Z // IN_TN


class _CastJobs:
    def __init__(self, weights, layer, grid, transform=None):
        self.weights, self.layer, self.transform = weights, layer, transform
        self.inner = grid[1]
        n_max = 1
        while n_max * 2 <= min(grid[0] * grid[1], MAX_CAST_STEPS):
            n_max *= 2
        self.n = []
        for w in weights:
            n = n_max
            while w.shape[1] % (n * BF16_ROWS):
                n //= 2
            self.n.append(n)

    def _spec(self, w, n, lead):
        slab = lambda i, j: jnp.minimum(i * self.inner + j, n - 1)
        if lead:
            return pl.BlockSpec((None, w.shape[1] // n, w.shape[2]), lambda i, j: (self.layer, slab(i, j), 0))
        return pl.BlockSpec((w.shape[1] // n, w.shape[2]), lambda i, j: (slab(i, j), 0))

    def in_specs(self):
        return [self._spec(w, n, True) for w, n in zip(self.weights, self.n)]

    def out_specs(self):
        return [self._spec(w, n, False) for w, n in zip(self.weights, self.n)]

    def out_shape(self):
        return [jax.ShapeDtypeStruct(w.shape[1:], BF16) for w in self.weights]

    def run(self, src_refs, dst_refs):
        step = pl.program_id(0) * self.inner + pl.program_id(1)
        for n in sorted(set(self.n)):
            @pl.when(step < n)
            def _():
                for s, d, n_w in zip(src_refs, dst_refs, self.n):
                    if n_w == n:
                        v = s[...]
                        d[...] = (self.transform(v) if self.transform else v).astype(BF16)


MAX_CAST_STEPS = 128
BF16_ROWS = 16


def _in_proj_kernel(*refs, jobs, mode):
    n_side = len(jobs.weights) if jobs else 0
    n_out = 2 if mode == "dual" else 1
    x_ref, g_ref, w_ref = refs[:3]
    side_in = refs[3:3 + n_side]
    outs = refs[3 + n_side:3 + n_side + n_out]
    side_out = refs[3 + n_side + n_out:3 + 2 * n_side + n_out]
    xn_ref = refs[-1]
    j = pl.program_id(1)

    @pl.when(j == 0)
    def _():
        xn_ref[...] = _rms(x_ref[...], g_ref[...]).astype(BF16)

    dot = lambda: jnp.dot(xn_ref[...], w_ref[...], preferred_element_type=F32)
    if mode == "bf16":
        outs[0][...] = dot().astype(BF16)
    elif mode == "kv_f32":
        outs[0][...] = dot()
    else:
        z_ref, zkv_ref = outs

        @pl.when(j < IN_Z_TILES)
        def _():
            z_ref[...] = dot().astype(BF16)

        @pl.when(j >= IN_Z_TILES)
        def _():
            zkv_ref[...] = dot()

    if jobs:
        jobs.run(side_in, side_out)


def _in_proj(x, g, w, layer, tm, mode, cast=None, row_tile=None, n_tiles=None):
    m = x.shape[0]
    col0 = 0
    if mode == "kv_f32":
        grid = (n_tiles, IN_KV_TILES)
        col0 = IN_Z_TILES
        x_map = lambda i, j: (row_tile(i), 0)
        out_specs = [pl.BlockSpec((tm, IN_TN), lambda i, j: (i, j))]
        out_shape = [jax.ShapeDtypeStruct((n_tiles * tm, D_KV), F32)]
    else:
        grid = (m // tm, D_IN // IN_TN)
        x_map = lambda i, j: (i, 0)
        if mode == "bf16":
            out_specs = [pl.BlockSpec((tm, IN_TN), lambda i, j: (i, j))]
            out_shape = [jax.ShapeDtypeStruct((m, D_IN), BF16)]
        else:
            out_specs = [pl.BlockSpec((tm, IN_TN), lambda i, j: (i, jnp.minimum(j, IN_Z_TILES - 1))),
                         pl.BlockSpec((tm, IN_TN), lambda i, j: (i, jnp.maximum(j - IN_Z_TILES, 0)))]
            out_shape = [jax.ShapeDtypeStruct((m, D_Z), BF16), jax.ShapeDtypeStruct((m, D_KV), F32)]
    jobs = _CastJobs(cast[0], cast[1], grid) if cast else None
    return pl.pallas_call(
        functools.partial(_in_proj_kernel, jobs=jobs, mode=mode),
        grid=grid,
        in_specs=[
            pl.BlockSpec((tm, D_MODEL), x_map),
            pl.BlockSpec((None, 1, D_MODEL), lambda i, j: (layer, 0, 0)),
            pl.BlockSpec((None, D_MODEL, IN_TN), lambda i, j: (0, 0, col0 + j)),
        ] + (jobs.in_specs() if jobs else []),
        out_specs=out_specs + (jobs.out_specs() if jobs else []),
        out_shape=out_shape + (jobs.out_shape() if jobs else []),
        scratch_shapes=[pltpu.VMEM((tm, D_MODEL), BF16)],
        compiler_params=_params(("arbitrary", "arbitrary")),
        name="in_proj",
    )(x, g, w, *(cast[0] if cast else ()))


_XPAD = 8
LANES = 128
_SEGS = 8
_SEG_LEN = 4


def _sigmoid(x):
    return 0.5 * (jnp.tanh(0.5 * x) + 1.0)


def _lru_kernel(ax_ref, ag_ref, cbuf_ref, h0_ref, cw_ref, cb_ref, wa_ref, ba_ref, wx_ref, bx_ref,
                lam_ref, out_ref, convo_ref, ho_ref, xbuf, a_s, b_s, h_s):
    t = pl.program_id(1)
    nt = pl.num_programs(1)
    tt = ax_ref.shape[0]

    @pl.when(t == 0)
    def _():
        xbuf[...] = jnp.zeros_like(xbuf)
        xbuf[_XPAD - (CONV_W - 1):, :] = cbuf_ref[...]
        h_s[...] = h0_ref[...]

    x = ax_ref[...].astype(F32)
    xe = jnp.concatenate([xbuf[...], x], axis=0)
    acc = xe * cw_ref[0:1, :]
    for k in range(1, CONV_W):
        acc = xe * cw_ref[k:k + 1, :] + pltpu.roll(acc, 1, 0)
    u = cb_ref[...] + acc[_XPAD:, :]
    tail = x[tt - (CONV_W - 1):, :]
    xbuf[...] = x[tt - _XPAD:, :]

    ub = u.astype(BF16)
    r_parts, i_parts = [], []
    for n in range(LRU_BLOCKS):
        un = ub[:, n * LRU_BLOCK:(n + 1) * LRU_BLOCK]
        r_parts.append(jnp.dot(un, wa_ref[n], preferred_element_type=F32))
        i_parts.append(jnp.dot(un, wx_ref[n], preferred_element_type=F32))
    r = _sigmoid(jnp.concatenate(r_parts, axis=1) + ba_ref[...])
    i = _sigmoid(jnp.concatenate(i_parts, axis=1) + bx_ref[...])
    log_a = -LRU_C * r * jax.nn.softplus(-lam_ref[...])
    a = jnp.exp(log_a)
    y = 1.0 - a * a
    b = jnp.where(y > 0.0, y * lax.rsqrt(y), 0.0) * (i * u)
    n_lg = D_RNN // LANES
    for lg in range(n_lg):
        a_s[lg] = a[:, lg * LANES:(lg + 1) * LANES]
        b_s[lg] = b[:, lg * LANES:(lg + 1) * LANES]

    row = lax.broadcasted_iota(jnp.int32, (_SEGS, LANES), 0)
    sub = _SEGS * _SEG_LEN

    def body(sb, h):
        r0 = pl.multiple_of(sb * sub, sub)
        step = lambda j: pl.ds(r0 + j, _SEGS, stride=_SEG_LEN)
        h_next = []
        for lg in range(n_lg):
            a_g, b_g = a_s.at[lg], b_s.at[lg]
            h_g = h[:, lg * LANES:(lg + 1) * LANES]
            acs, bcs = [a_g[step(0), :]], [b_g[step(0), :]]
            for j in range(1, _SEG_LEN):
                aj = a_g[step(j), :]
                bcs.append(aj * bcs[-1] + b_g[step(j), :])
                acs.append(aj * acs[-1])
            at, bt = acs[-1], bcs[-1]
            for s in (1, 2, 4):
                m = row >= s
                bt_new = jnp.where(m, at * pltpu.roll(bt, s, 0) + bt, bt)
                at = jnp.where(m, at * pltpu.roll(at, s, 0), at)
                bt = bt_new
            after = at * h_g + bt
            entry = jnp.where(row >= 1, pltpu.roll(after, 1, 0), h_g)
            for j in range(_SEG_LEN):
                b_g[step(j), :] = acs[j] * entry + bcs[j]
            h_next.append(after[_SEGS - 1:_SEGS, :])
        return jnp.concatenate(h_next, axis=1)

    h = lax.fori_loop(0, tt // sub, body, h_s[...])
    h_s[...] = h
    hs = jnp.concatenate([b_s[lg] for lg in range(n_lg)], axis=1)
    out_ref[...] = (hs * jax.nn.gelu(ag_ref[...].astype(F32))).astype(BF16)

    @pl.when(t == nt - 1)
    def _():
        convo_ref[...] = tail
        ho_ref[...] = h


def _lru(z, conv_buf, h0, state_layer, cw, cb, wa, ba, wx, bx, lam, layer, n_seq, t_len, tt):
    nt = t_len // tt
    row = lambda b, t: b * nt + t
    vec = lambda: pl.BlockSpec((None, 1, D_RNN), lambda b, t: (layer, 0, 0))
    blk = lambda: pl.BlockSpec((None, LRU_BLOCKS, LRU_BLOCK, LRU_BLOCK), lambda b, t: (layer, 0, 0, 0))
    return pl.pallas_call(
        _lru_kernel,
        grid=(n_seq, nt),
        in_specs=[
            pl.BlockSpec((tt, D_RNN), lambda b, t: (row(b, t), Z_AX // D_RNN)),
            pl.BlockSpec((tt, D_RNN), lambda b, t: (row(b, t), Z_AG // D_RNN)),
            pl.BlockSpec((None, None, CONV_W - 1, D_RNN), lambda b, t: (state_layer, b, 0, 0)),
            pl.BlockSpec((None, None, 1, D_RNN), lambda b, t: (state_layer, b, 0, 0)),
            pl.BlockSpec((None, CONV_W, D_RNN), lambda b, t: (layer, 0, 0)),
            vec(), blk(), vec(), blk(), vec(), vec(),
        ],
        out_specs=[
            pl.BlockSpec((tt, D_RNN), lambda b, t: (row(b, t), 0)),
            pl.BlockSpec((None, CONV_W - 1, D_RNN), lambda b, t: (b, 0, 0)),
            pl.BlockSpec((None, 1, D_RNN), lambda b, t: (b, 0, 0)),
        ],
        out_shape=[
            jax.ShapeDtypeStruct((n_seq * t_len, D_RNN), BF16),
            jax.ShapeDtypeStruct((n_seq, CONV_W - 1, D_RNN), F32),
            jax.ShapeDtypeStruct((n_seq, 1, D_RNN), F32),
        ],
        scratch_shapes=[
            pltpu.VMEM((_XPAD, D_RNN), F32),
            pltpu.VMEM((D_RNN // LANES, tt, LANES), F32),
            pltpu.VMEM((D_RNN // LANES, tt, LANES), F32),
            pltpu.VMEM((1, D_RNN), F32),
        ],
        compiler_params=_params(("parallel", "arbitrary")),
        name="lru",
    )(z, z, conv_buf, h0, cw, cb, wa, ba, wx, bx, lam)


def _dot_nt(a, b):
    return lax.dot_general(a, b, (((1,), (1,)), ((), ())), preferred_element_type=F32)


def _with_ones(v):
    return jnp.concatenate([v, jnp.ones_like(v)], axis=1)


def _swa_chunk(q, kband, vext, sinks, valid):
    qst = jnp.concatenate([q[:, g * HEAD_DIM:(g + 1) * HEAD_DIM] for g in range(SWA_GROUP)], axis=0)
    s = _dot_nt(qst, kband) * (ATTN_SCALE * LOG2E)
    if valid is not None:
        s = jnp.where(valid, s, NEG)
    es, sink_e = [], []
    for g in range(SWA_GROUP):
        sg = s[g * CHUNK:(g + 1) * CHUNK, :]
        sink2 = sinks[g] * LOG2E
        m = jnp.maximum(jnp.max(sg, axis=-1, keepdims=True), sink2)
        es.append(jnp.exp2(sg - m).astype(BF16))
        sink_e.append(jnp.exp2(sink2 - m))
    r = jnp.dot(jnp.concatenate(es, axis=0), vext, preferred_element_type=F32)
    outs = []
    for g in range(SWA_GROUP):
        rg = r[g * CHUNK:(g + 1) * CHUNK, :]
        outs.append(rg[:, :HEAD_DIM] * (1.0 / (rg[:, HEAD_DIM:] + sink_e[g])))
    return jnp.concatenate(outs, axis=1)


def _cb_attend(q, kband, vext, bias2):
    s = _dot_nt(q, kband) * (ATTN_SCALE * LOG2E) + bias2
    m = jnp.max(s, axis=-1, keepdims=True)
    r = jnp.dot(jnp.exp2(s - m).astype(BF16), vext, preferred_element_type=F32)
    return r[:, :HEAD_DIM] * (1.0 / r[:, HEAD_DIM:])


def _cb_bias_block(r, variant, n_rows, n_cols):
    t = pltpu.roll(jnp.broadcast_to(r, (n_rows, CB_RLEN)), 0, 1, stride=1, stride_axis=0)[:, :n_cols]
    c0 = variant * Q_CHUNKS
    sc = max(c0 - CB_PREV, 0)
    qc = c0 + lax.broadcasted_iota(jnp.int32, (n_rows, n_cols), 0) // CHUNK
    kc = sc + lax.broadcasted_iota(jnp.int32, (n_rows, n_cols), 1) // CHUNK
    return jnp.where(kc <= qc, jnp.where(kc >= qc - CB_PREV, t * LOG2E, NEG), NEG)


def _cast_rows(dst, src, n_rows, step, ones=False):
    def body(i, c):
        r0 = pl.multiple_of(i * step, step)
        v = src[pl.ds(r0, step), :].astype(BF16)
        dst[pl.ds(r0, step), :] = _with_ones(v) if ones else v
        return c
    lax.fori_loop(0, n_rows // step, body, 0)


SWA_BAND = (SWA_PREV + 1) * CHUNK
SWA_TQ = 128 * CHUNK


def _swa_prompt_kernel(sink_ref, q_ref, k_ref, v_ref, o_ref, vb_s, *, layer):
    kh = pl.program_id(1)
    qi = pl.program_id(2)
    s_len = k_ref.shape[0]

    @pl.when(qi == 0)
    def _():
        _cast_rows(vb_s, v_ref, s_len, 512, ones=True)

    sinks = [sink_ref[layer, kh * SWA_GROUP + g] for g in range(SWA_GROUP)]
    jchunk = lax.broadcasted_iota(jnp.int32, (1, SWA_BAND), 1) // CHUNK
    n_chunks = q_ref.shape[0] // CHUNK
    for c in range(n_chunks):
        cg = qi * n_chunks + c
        sc = jnp.maximum(cg - SWA_PREV, 0)
        s0 = pl.multiple_of(sc * CHUNK, CHUNK)
        valid = (jchunk + sc) <= cg
        o = _swa_chunk(q_ref[c * CHUNK:(c + 1) * CHUNK, :], k_ref[pl.ds(s0, SWA_BAND), :],
                       vb_s[pl.ds(s0, SWA_BAND), :], sinks, valid)
        o_ref[c * CHUNK:(c + 1) * CHUNK, :] = o.astype(BF16)


def _swa_prompt(z, sinks, layer, n_seq, s_len):
    tq = min(SWA_TQ, s_len)
    nq = s_len // tq
    gw = SWA_GROUP * HEAD_DIM
    return pl.pallas_call(
        functools.partial(_swa_prompt_kernel, layer=layer),
        grid=(n_seq, SWA_KV_HEADS, nq),
        in_specs=[
            pl.BlockSpec(memory_space=pltpu.SMEM),
            pl.BlockSpec((tq, gw), lambda b, k, q: (b * nq + q, Z_QB // gw + k)),
            pl.BlockSpec((s_len, HEAD_DIM), lambda b, k, q: (b, (D_Z + KV_KB) // HEAD_DIM + k)),
            pl.BlockSpec((s_len, HEAD_DIM), lambda b, k, q: (b, (D_Z + KV_VB) // HEAD_DIM + k)),
        ],
        out_specs=pl.BlockSpec((tq, gw), lambda b, k, q: (b * nq + q, k)),
        out_shape=jax.ShapeDtypeStruct((n_seq * s_len, BRANCH_WIDTH), BF16),
        scratch_shapes=[pltpu.VMEM((s_len, 2 * HEAD_DIM), BF16)],
        compiler_params=_params(("parallel", "parallel", "arbitrary")),
        name="swa_prompt",
    )(sinks, z, z, z)


STEP_SEQS = 4


def _step_seqs(n_seq):
    g = min(STEP_SEQS, n_seq)
    while n_seq % g:
        g -= 1
    return g


def _swa_step_kernel(sink_ref, q_ref, k_ref, v_ref, ck_ref, cv_ref, o_ref, ko_ref, vo_ref, *, layer):
    n_s = ck_ref.shape[0]
    n_past = ck_ref.shape[1] // SWA_KV_HEADS
    gw = SWA_GROUP * HEAD_DIM
    for s in range(n_s):
        rows = slice(s * CHUNK, (s + 1) * CHUNK)
        ck_s, cv_s = ck_ref.at[s], cv_ref.at[s]
        for kh in range(SWA_KV_HEADS):
            cs = slice(kh * HEAD_DIM, (kh + 1) * HEAD_DIM)
            kn = k_ref[rows, cs]
            vn = v_ref[rows, cs]
            kfull = jnp.concatenate(
                [ck_s[pl.ds(kh, n_past, stride=SWA_KV_HEADS), :].astype(BF16), kn.astype(BF16)], axis=0)
            vfull = jnp.concatenate(
                [cv_s[pl.ds(kh, n_past, stride=SWA_KV_HEADS), :].astype(BF16), vn.astype(BF16)], axis=0)
            sinks = [sink_ref[layer, kh * SWA_GROUP + g] for g in range(SWA_GROUP)]
            o = _swa_chunk(q_ref[rows, kh * gw:(kh + 1) * gw], kfull, _with_ones(vfull), sinks, None)
            o_ref[rows, kh * gw:(kh + 1) * gw] = o.astype(BF16)
            new = pl.ds(s * CHUNK * SWA_KV_HEADS + kh, CHUNK, stride=SWA_KV_HEADS)
            ko_ref[new, :] = kn
            vo_ref[new, :] = vn


def _swa_step(z, zkv, cache_k, cache_v, sinks, layer, n_seq):
    rows_past = cache_k.shape[2]
    rows_new = CHUNK * SWA_KV_HEADS
    g = _step_seqs(n_seq)
    cache = lambda: pl.BlockSpec((None, g, rows_past, HEAD_DIM), lambda b: (layer, b, 0, 0))
    new = lambda: pl.BlockSpec((g * rows_new, HEAD_DIM), lambda b: (b, 0))
    return pl.pallas_call(
        functools.partial(_swa_step_kernel, layer=layer),
        grid=(n_seq // g,),
        in_specs=[
            pl.BlockSpec(memory_space=pltpu.SMEM),
            pl.BlockSpec((g * CHUNK, BRANCH_WIDTH), lambda b: (b, Z_QB // BRANCH_WIDTH)),
            pl.BlockSpec((g * CHUNK, SWA_KVW), lambda b: (b, KV_KB // SWA_KVW)),
            pl.BlockSpec((g * CHUNK, SWA_KVW), lambda b: (b, KV_VB // SWA_KVW)),
            cache(), cache(),
        ],
        out_specs=[pl.BlockSpec((g * CHUNK, BRANCH_WIDTH), lambda b: (b, 0)), new(), new()],
        out_shape=[jax.ShapeDtypeStruct((n_seq * CHUNK, BRANCH_WIDTH), BF16),
                   jax.ShapeDtypeStruct((n_seq * rows_new, HEAD_DIM), F32),
                   jax.ShapeDtypeStruct((n_seq * rows_new, HEAD_DIM), F32)],
        compiler_params=_params(("parallel",)),
        name="swa_step",
    )(sinks, z, zkv, zkv, cache_k, cache_v)


def _cb_prompt_kernel(q_ref, k_ref, v_ref, r_ref, o_ref, vb_s, bias_s):
    qi = pl.program_id(2)
    s_len = k_ref.shape[0]

    @pl.when(qi == 0)
    def _():
        _cast_rows(vb_s, v_ref, s_len, 512, ones=True)
        for v in range(CB_VARIANTS):
            bias_s[v] = _cb_bias_block(r_ref[v], v, Q_BLOCK, CB_BAND)

    n_sub = q_ref.shape[0] // Q_BLOCK
    for sub in range(n_sub):
        blk = qi * n_sub + sub
        sc = jnp.maximum(blk * Q_CHUNKS - CB_PREV, 0)
        s0 = pl.multiple_of(sc * CHUNK, CHUNK)
        rows = slice(sub * Q_BLOCK, (sub + 1) * Q_BLOCK)
        o = _cb_attend(q_ref[rows, :], k_ref[pl.ds(s0, CB_BAND), :], vb_s[pl.ds(s0, CB_BAND), :],
                       bias_s[jnp.minimum(blk, CB_VARIANTS - 1)])
        o_ref[rows, :] = o.astype(BF16)


def _cb_prompt(z, rows, layer, n_seq, s_len):
    tq = min(Q_BLOCK * CB_QSUB, s_len)
    nq = s_len // tq
    return pl.pallas_call(
        _cb_prompt_kernel,
        grid=(n_seq, CB_HEADS, nq),
        in_specs=[
            pl.BlockSpec((tq, HEAD_DIM), lambda b, h, q: (b * nq + q, Z_QC // HEAD_DIM + h)),
            pl.BlockSpec((s_len, HEAD_DIM), lambda b, h, q: (b, (D_Z + KV_KC) // HEAD_DIM + h)),
            pl.BlockSpec((s_len, HEAD_DIM), lambda b, h, q: (b, (D_Z + KV_VC) // HEAD_DIM + h)),
            pl.BlockSpec((None, CB_VARIANTS, None, 1, CB_RLEN), lambda b, h, q: (layer, 0, h, 0, 0)),
        ],
        out_specs=pl.BlockSpec((tq, HEAD_DIM), lambda b, h, q: (b * nq + q, h)),
        out_shape=jax.ShapeDtypeStruct((n_seq * s_len, BRANCH_WIDTH), BF16),
        scratch_shapes=[pltpu.VMEM((s_len, 2 * HEAD_DIM), BF16),
                        pltpu.VMEM((CB_VARIANTS, Q_BLOCK, CB_BAND), F32)],
        compiler_params=_params(("parallel", "parallel", "arbitrary")),
        name="cb_prompt",
    )(z, z, z, rows)


def _cb_step_kernel(q_ref, k_ref, v_ref, ck_ref, cv_ref, r_ref, o_ref, ko_ref, vo_ref, bias_s):
    n_s = ck_ref.shape[0]
    n_past = ck_ref.shape[1] // CB_HEADS
    n_k = n_past + CHUNK

    @pl.when(pl.program_id(0) == 0)
    def _():
        for h in range(CB_HEADS):
            bias_s[h] = _cb_bias_block(r_ref[h], CB_VARIANTS - 1, CHUNK, n_k)

    for s in range(n_s):
        rows = slice(s * CHUNK, (s + 1) * CHUNK)
        ck_s, cv_s = ck_ref.at[s], cv_ref.at[s]
        for h in range(CB_HEADS):
            cs = slice(h * HEAD_DIM, (h + 1) * HEAD_DIM)
            kn = k_ref[rows, cs]
            vn = v_ref[rows, cs]
            kfull = jnp.concatenate(
                [ck_s[pl.ds(h, n_past, stride=CB_HEADS), :].astype(BF16), kn.astype(BF16)], axis=0)
            vfull = jnp.concatenate(
                [cv_s[pl.ds(h, n_past, stride=CB_HEADS), :].astype(BF16), vn.astype(BF16)], axis=0)
            o_ref[rows, cs] = _cb_attend(q_ref[rows, cs], kfull, _with_ones(vfull), bias_s[h]).astype(BF16)
            new = pl.ds(s * CHUNK * CB_HEADS + h, CHUNK, stride=CB_HEADS)
            ko_ref[new, :] = kn
            vo_ref[new, :] = vn


def _cb_step(z, zkv, cache_k, cache_v, rows, layer, n_seq):
    rows_past = cache_k.shape[2]
    n_k = rows_past // CB_HEADS + CHUNK
    rows_new = CHUNK * CB_HEADS
    g = _step_seqs(n_seq)
    cache = lambda: pl.BlockSpec((None, g, rows_past, HEAD_DIM), lambda b: (layer, b, 0, 0))
    new = lambda: pl.BlockSpec((g * rows_new, HEAD_DIM), lambda b: (b, 0))
    return pl.pallas_call(
        _cb_step_kernel,
        grid=(n_seq // g,),
        in_specs=[
            pl.BlockSpec((g * CHUNK, BRANCH_WIDTH), lambda b: (b, Z_QC // BRANCH_WIDTH)),
            pl.BlockSpec((g * CHUNK, CB_KVW), lambda b: (b, KV_KC // CB_KVW)),
            pl.BlockSpec((g * CHUNK, CB_KVW), lambda b: (b, KV_VC // CB_KVW)),
            cache(), cache(),
            pl.BlockSpec((None, None, CB_HEADS, 1, CB_RLEN),
                         lambda b: (layer, CB_VARIANTS - 1, 0, 0, 0)),
        ],
        out_specs=[pl.BlockSpec((g * CHUNK, BRANCH_WIDTH), lambda b: (b, 0)), new(), new()],
        out_shape=[jax.ShapeDtypeStruct((n_seq * CHUNK, BRANCH_WIDTH), BF16),
                   jax.ShapeDtypeStruct((n_seq * rows_new, HEAD_DIM), F32),
                   jax.ShapeDtypeStruct((n_seq * rows_new, HEAD_DIM), F32)],
        scratch_shapes=[pltpu.VMEM((CB_HEADS, CHUNK, n_k), F32)],
        compiler_params=_params(("arbitrary",)),
        name="cb_step",
    )(z, zkv, zkv, cache_k, cache_v, rows)


def _merge_kernel(x_ref, a_ref, b_ref, c_ref, ga_ref, gb_ref, gc_ref, wb_ref, wo_ref, o_ref):
    mixed = None
    for r, (br, gr) in enumerate(((a_ref, ga_ref), (b_ref, gb_ref), (c_ref, gc_ref))):
        proj = jnp.dot(br[...], wb_ref[r], preferred_element_type=F32)
        gate = jax.nn.sigmoid(gr[...].astype(F32))
        mixed = gate * proj if mixed is None else mixed + gate * proj
    o_ref[...] = x_ref[...] + jnp.dot(mixed.astype(BF16), wo_ref[...], preferred_element_type=F32)


def _merge(x, out_a, out_b, out_c, z, w_branch, w_out, layer, tm):
    m = x.shape[0]
    branch = lambda: pl.BlockSpec((tm, BRANCH_WIDTH), lambda i: (i, 0))
    gate = lambda r: pl.BlockSpec((tm, D_MODEL), lambda i: (i, Z_GATES // D_MODEL + r))
    return pl.pallas_call(
        _merge_kernel,
        grid=(m // tm,),
        in_specs=[
            pl.BlockSpec((tm, D_MODEL), lambda i: (i, 0)),
            branch(), branch(), branch(),
            gate(0), gate(1), gate(2),
            pl.BlockSpec((None, N_BRANCH, BRANCH_WIDTH, D_MODEL), lambda i: (layer, 0, 0, 0),
                         pipeline_mode=pl.Buffered(1)),
            pl.BlockSpec((None, D_MODEL, D_MODEL), lambda i: (layer, 0, 0),
                         pipeline_mode=pl.Buffered(1)),
        ],
        out_specs=pl.BlockSpec((tm, D_MODEL), lambda i: (i, 0)),
        out_shape=jax.ShapeDtypeStruct((m, D_MODEL), F32),
        compiler_params=_params(("parallel",)),
        name="merge",
    )(x, out_a, out_b, out_c, z, z, z, w_branch, w_out)


def _mlp_kernel(*refs, final_norm, jobs):
    n_side = len(jobs.weights) if jobs else 0
    x_ref, g_ref, fg_ref, wu_ref, wd_ref = refs[:5]
    side_in = refs[5:5 + n_side]
    o_ref = refs[5 + n_side]
    side_out = refs[6 + n_side:6 + 2 * n_side]
    hn_ref = refs[-1]
    j = pl.program_id(1)

    @pl.when(j == 0)
    def _():
        xf = x_ref[...]
        hn_ref[...] = _rms(xf, g_ref[...]).astype(BF16)
        o_ref[...] = xf

    h = jnp.dot(hn_ref[...], wu_ref[...], preferred_element_type=F32)
    h = jnp.square(jnp.maximum(h, 0.0)).astype(BF16)
    o_ref[...] += jnp.dot(h, wd_ref[...], preferred_element_type=F32)

    if final_norm:
        @pl.when(j == pl.num_programs(1) - 1)
        def _():
            o_ref[...] = _rms(o_ref[...], fg_ref[...])

    if jobs:
        jobs.run(side_in, side_out)


def _mlp(x, g, final_g, w_up, w_down, layer, tm, tf, final_norm, cast=None):
    m = x.shape[0]
    grid = (m // tm, D_FF // tf)
    jobs = _CastJobs(cast[0], cast[1], grid, cast[2]) if cast else None
    return pl.pallas_call(
        functools.partial(_mlp_kernel, final_norm=final_norm, jobs=jobs),
        grid=grid,
        in_specs=[
            pl.BlockSpec((tm, D_MODEL), lambda i, j: (i, 0)),
            pl.BlockSpec((None, 1, D_MODEL), lambda i, j: (layer, 0, 0)),
            pl.BlockSpec((1, D_MODEL), lambda i, j: (0, 0)),
            pl.BlockSpec((None, D_MODEL, tf), lambda i, j: (0, 0, j)),
            pl.BlockSpec((None, tf, D_MODEL), lambda i, j: (0, j, 0)),
        ] + (jobs.in_specs() if jobs else []),
        out_specs=[pl.BlockSpec((tm, D_MODEL), lambda i, j: (i, 0))] + (jobs.out_specs() if jobs else []),
        out_shape=[jax.ShapeDtypeStruct((m, D_MODEL), F32)] + (jobs.out_shape() if jobs else []),
        scratch_shapes=[pltpu.VMEM((tm, D_MODEL), BF16)],
        compiler_params=_params(("arbitrary", "arbitrary")),
        name="mlp",
    )(x, g, final_g, w_up, w_down, *(cast[0] if cast else ()))


def _permute_w_in(w_in):
    seg = lambda o, w: w_in[..., o:o + w]
    return jnp.concatenate([
        seg(_O_AX, D_RNN), seg(_O_AG, D_RNN), seg(_O_QB, BRANCH_WIDTH), seg(_O_QC, BRANCH_WIDTH),
        seg(_O_GATES, N_BRANCH * D_MODEL),
        seg(_O_KC, CB_KVW), seg(_O_VC, CB_KVW), seg(_O_KB, SWA_KVW), seg(_O_VB, SWA_KVW)],
        axis=-1).astype(BF16)


def _cb_bias_rows(table):
    m = jnp.arange(CB_RLEN)
    rel = jnp.where(m < CB_BAND, m, m - CB_RLEN)
    out = []
    for v in range(CB_VARIANTS):
        c0 = v * Q_CHUNKS
        sc = max(c0 - CB_PREV, 0)
        d = (c0 - sc) * CHUNK - rel
        idx = jnp.clip(d, -REL_CLIP, REL_CLIP) + REL_CLIP
        out.append(table.astype(F32)[:, :, idx])
    return jnp.stack(out, axis=1)[:, :, :, None, :]


def _pick_tm(m, want):
    tm = min(want, m)
    while m % tm:
        tm //= 2
    return tm


def _layer(x, n_seq, t_len, conv_buf, h0, state_layer, caches, layer, p, w_in_l, w_layer):
    m = x.shape[0]
    depth = p["w_up"].shape[0]
    tm_in = _pick_tm(m, 1024)
    if w_layer is None:
        z, wu, wd, wb, wo = _in_proj(
            x, p["norm1"], w_in_l, layer, tm_in, "bf16",
            cast=((p["w_up"], p["w_down"], p["w_branch"], p["w_out"]), layer))
        w_layer = (wu[None], wd[None], wb.reshape(1, N_BRANCH, BRANCH_WIDTH, D_MODEL), wo[None])
        cast_next = ((p["w_in"],), layer + 1, _permute_w_in) if layer + 1 < depth else None
    else:
        z, zkv = _in_proj(x, p["norm1"], w_in_l, layer, tm_in, "dual")
        cast_next = None
    w_up_l, w_down_l, w_branch_l, w_out_l = w_layer
    out_a, conv_o, h_o = _lru(z, conv_buf, h0, state_layer, p["conv_w"], p["conv_b"], p["wa"], p["ba"],
                              p["wx"], p["bx"], p["lam"], layer, n_seq, t_len, _pick_tm(t_len, 1024))
    if caches is None:
        out_b = _swa_prompt(z, p["sinks"], layer, n_seq, t_len)
        out_c = _cb_prompt(z, p["cb_rows"], layer, n_seq, t_len)
        n_tail = CB_REACH
        kv3 = _in_proj(x, p["norm1"], w_in_l, layer, n_tail, "kv_f32", n_tiles=n_seq,
                       row_tile=lambda i: (i + 1) * (t_len // n_tail) - 1)[0].reshape(n_seq, n_tail, D_KV)
        kv = (kv3[:, n_tail - SWA_WINDOW:, KV_KB:KV_KB + SWA_KVW],
              kv3[:, n_tail - SWA_WINDOW:, KV_VB:KV_VB + SWA_KVW],
              kv3[:, :, KV_KC:KV_KC + CB_KVW], kv3[:, :, KV_VC:KV_VC + CB_KVW])
    else:
        ck_b, cv_b, ck_c, cv_c = caches
        out_b, kb, vb = _swa_step(z, zkv, ck_b, cv_b, p["sinks"], layer, n_seq)
        out_c, kc, vc = _cb_step(z, zkv, ck_c, cv_c, p["cb_rows"], layer, n_seq)
        kv = (kb, vb, kc, vc)
    x = _merge(x, out_a, out_b, out_c, z, w_branch_l, w_out_l, 0, _pick_tm(m, 256))
    res = _mlp(x, p["norm2"], p["final_g"], w_up_l, w_down_l, layer, _pick_tm(m, 512), 1024,
               final_norm=layer == depth - 1, cast=cast_next)
    w_in_next = res[1][None] if cast_next else None
    return res[0], conv_o, h_o[:, 0], kv, w_layer, w_in_next


def kernel(x_prompt, x_sample, state_conv, state_lru, cache_swa_k, cache_swa_v, cache_cb_k, cache_cb_v, norm1_g, w_in, conv_w, conv_b, lru_wa, lru_ba, lru_wx, lru_bx, lru_lambda, attn_sinks, rel_bias_table, w_branch, w_out, norm2_g, w_up, w_down, final_g):
    depth = w_in.shape[0]
    nb, s_len, _ = x_prompt.shape
    db, d_len, _ = x_sample.shape
    assert d_len == CHUNK and s_len % min(Q_BLOCK * CB_QSUB, s_len) == 0 and s_len % min(SWA_TQ, s_len) == 0
    assert s_len % Q_BLOCK == 0 and s_len >= CB_BAND
    assert s_len % CB_REACH == 0
    assert cache_swa_k.shape[2] == SWA_WINDOW and cache_cb_k.shape[2] == CB_REACH

    row = lambda v: v.reshape(depth, 1, -1)
    p = {
        "norm1": row(norm1_g), "norm2": row(norm2_g), "final_g": final_g.reshape(1, D_MODEL),
        "w_in": w_in, "w_up": w_up, "w_down": w_down,
        "w_branch": w_branch.reshape(depth, N_BRANCH * BRANCH_WIDTH, D_MODEL), "w_out": w_out,
        "conv_w": conv_w, "conv_b": row(conv_b),
        "wa": lru_wa.astype(BF16), "ba": row(lru_ba), "wx": lru_wx.astype(BF16), "bx": row(lru_bx),
        "lam": row(lru_lambda), "sinks": attn_sinks,
        "cb_rows": _cb_bias_rows(rel_bias_table),
    }
    w_in_l = _permute_w_in(w_in[0])[None]
    caches = (cache_swa_k.reshape(depth, db, SWA_WINDOW * SWA_KV_HEADS, HEAD_DIM),
              cache_swa_v.reshape(depth, db, SWA_WINDOW * SWA_KV_HEADS, HEAD_DIM),
              cache_cb_k.reshape(depth, db, CB_REACH * CB_HEADS, HEAD_DIM),
              cache_cb_v.reshape(depth, db, CB_REACH * CB_HEADS, HEAD_DIM))

    xp = x_prompt.reshape(nb * s_len, D_MODEL)
    xs = x_sample.reshape(db * d_len, D_MODEL)
    zero_conv = jnp.zeros((1, nb, CONV_W - 1, D_RNN), F32)
    zero_h = jnp.zeros((1, nb, 1, D_RNN), F32)
    h0_s = state_lru.reshape(depth, db, 1, D_RNN)
    heads = (SWA_KV_HEADS, SWA_KV_HEADS, CB_HEADS, CB_HEADS)
    outs = [[] for _ in range(12)]
    for l in range(depth):
        xp, conv_p, h_p, kv_p, w_layer, w_in_next = _layer(xp, nb, s_len, zero_conv, zero_h, 0, None, l, p,
                                                           w_in_l, None)
        xs, conv_s, h_s, kv_s, _, _ = _layer(xs, db, d_len, state_conv, h0_s, l, caches, l, p,
                                             w_in_l, w_layer)
        w_in_l = w_in_next
        outs[0].append(conv_p)
        outs[1].append(h_p)
        outs[6].append(conv_s)
        outs[7].append(h_s)
        for n in range(4):
            outs[2 + n].append(kv_p[n].reshape(nb, -1, heads[n], HEAD_DIM))
            outs[8 + n].append(kv_s[n].reshape(db, d_len, heads[n], HEAD_DIM))

    y_prompt = xp.reshape(nb, s_len, D_MODEL)
    y_sample = xs.reshape(db, d_len, D_MODEL)
    return (y_prompt, y_sample) + tuple(jnp.stack(o) for o in outs)
```

```python
import functools

import jax
import jax.numpy as jnp
from jax import lax
from jax.experimental import pallas as pl
from jax.experimental.pallas import tpu as pltpu

F32 = jnp.float32
BF16 = jnp.bfloat16

D_MODEL = 2048
CHUNK = 64
HEAD_DIM = 128
BRANCH_WIDTH = D_MODEL // 2
N_BRANCH = 3
D_RNN = BRANCH_WIDTH
LRU_BLOCKS = 8
LRU_BLOCK = D_RNN // LRU_BLOCKS
CONV_W = 4
LRU_C = 8.0
SWA_HEADS = BRANCH_WIDTH // HEAD_DIM
SWA_KV_HEADS = 2
SWA_GROUP = SWA_HEADS // SWA_KV_HEADS
SWA_WINDOW = 128
SWA_PREV = SWA_WINDOW // CHUNK
CB_HEADS = BRANCH_WIDTH // HEAD_DIM
CB_PREV = 8
CB_REACH = CB_PREV * CHUNK
REL_CLIP = 128
D_FF = 4 * D_MODEL
EPS = 1e-6
NEG = -1e30
ATTN_SCALE = HEAD_DIM ** -0.5
LOG2E = 1.4426950408889634

_O_AX, _O_AG, _O_QB, _O_KB, _O_VB, _O_QC, _O_KC, _O_VC, _O_GATES = (
    0, 1024, 2048, 3072, 3328, 3584, 4608, 5632, 6656)
D_IN = _O_GATES + N_BRANCH * D_MODEL
SWA_KVW = SWA_KV_HEADS * HEAD_DIM
CB_KVW = CB_HEADS * HEAD_DIM
D_KV = 2 * SWA_KVW + 2 * CB_KVW
D_Z = D_IN - D_KV
KV_KC, KV_VC, KV_KB, KV_VB = 0, 1024, 2048, 2304
Z_AX, Z_AG, Z_QB, Z_QC, Z_GATES = 0, 1024, 2048, 3072, 4096

VMEM_LIMIT_BYTES = 56 * 1024 * 1024

Q_BLOCK = 4 * CHUNK
Q_CHUNKS = Q_BLOCK // CHUNK
CB_BAND = (CB_PREV + Q_CHUNKS) * CHUNK
CB_QSUB = 32
CB_RLEN = 1024
assert CB_RLEN >= Q_BLOCK + CB_BAND - 1
CB_VARIANTS = 3


def _params(semantics):
    return pltpu.CompilerParams(dimension_semantics=semantics, vmem_limit_bytes=VMEM_LIMIT_BYTES)


def _rms(xf, g):
    return xf * lax.rsqrt(jnp.mean(xf * xf, axis=-1, keepdims=True) + EPS) * g


IN_TN = 1280
IN_KV_TILES = D_KV // IN_TN
IN_Z_TILES = D_Z // IN_TN


class _CastJobs:
    def __init__(self, weights, layer, grid, transform=None):
        self.weights, self.layer, self.transform = weights, layer, transform
        self.inner = grid[1]
        n_max = 1
        while n_max * 2 <= min(grid[0] * grid[1], MAX_CAST_STEPS):
            n_max *= 2
        self.n = []
        for w in weights:
            n = n_max
            while w.shape[1] % (n * BF16_ROWS):
                n //= 2
            self.n.append(n)

    def _spec(self, w, n, lead):
        slab = lambda i, j: jnp.minimum(i * self.inner + j, n - 1)
        if lead:
            return pl.BlockSpec((None, w.shape[1] // n, w.shape[2]), lambda i, j: (self.layer, slab(i, j), 0))
        return pl.BlockSpec((w.shape[1] // n, w.shape[2]), lambda i, j: (slab(i, j), 0))

    def in_specs(self):
        return [self._spec(w, n, True) for w, n in zip(self.weights, self.n)]

    def out_specs(self):
        return [self._spec(w, n, False) for w, n in zip(self.weights, self.n)]

    def out_shape(self):
        return [jax.ShapeDtypeStruct(w.shape[1:], BF16) for w in self.weights]

    def run(self, src_refs, dst_refs):
        step = pl.program_id(0) * self.inner + pl.program_id(1)
        for n in sorted(set(self.n)):
            @pl.when(step < n)
            def _():
                for s, d, n_w in zip(src_refs, dst_refs, self.n):
                    if n_w == n:
                        v = s[...]
                        d[...] = (self.transform(v) if self.transform else v).astype(BF16)


MAX_CAST_STEPS = 128
BF16_ROWS = 16


W_RING = 3


def _w_tile_copy(w_hbm, wbuf, sem, step):
    col = pl.multiple_of(lax.rem(step, D_IN // IN_TN) * IN_TN, IN_TN)
    slot = lax.rem(step, W_RING)
    return pltpu.make_async_copy(w_hbm.at[0, :, pl.ds(col, IN_TN)], wbuf.at[slot], sem.at[slot])


def _in_proj_kernel(*refs, jobs, mode):
    n_side = len(jobs.weights) if jobs else 0
    n_out = 2 if mode == "dual" else 1
    x_ref, g_ref, w_ref = refs[:3]
    side_in = refs[3:3 + n_side]
    outs = refs[3 + n_side:3 + n_side + n_out]
    side_out = refs[3 + n_side + n_out:3 + 2 * n_side + n_out]
    xn_ref = refs[3 + 2 * n_side + n_out]
    j = pl.program_id(1)

    @pl.when(j == 0)
    def _():
        xn_ref[...] = _rms(x_ref[...], g_ref[...]).astype(BF16)

    dot = lambda: jnp.dot(xn_ref[...], w_ref[...], preferred_element_type=F32)
    if mode == "bf16":
        wbuf, sem = refs[-2:]
        n_steps = pl.num_programs(0) * pl.num_programs(1)
        t = pl.program_id(0) * pl.num_programs(1) + j

        @pl.when(t == 0)
        def _():
            _w_tile_copy(w_ref, wbuf, sem, 0).start()
            _w_tile_copy(w_ref, wbuf, sem, 1).start()

        @pl.when(t + 2 < n_steps)
        def _():
            _w_tile_copy(w_ref, wbuf, sem, t + 2).start()

        _w_tile_copy(w_ref, wbuf, sem, t).wait()
        w_t = wbuf[lax.rem(t, W_RING)]
        outs[0][...] = jnp.dot(xn_ref[...], w_t, preferred_element_type=F32).astype(BF16)
    elif mode == "kv_f32":
        outs[0][...] = dot()
    else:
        z_ref, zkv_ref = outs

        @pl.when(j < IN_Z_TILES)
        def _():
            z_ref[...] = dot().astype(BF16)

        @pl.when(j >= IN_Z_TILES)
        def _():
            zkv_ref[...] = dot()

    if jobs:
        jobs.run(side_in, side_out)


def _in_proj(x, g, w, layer, tm, mode, cast=None, row_tile=None, n_tiles=None):
    m = x.shape[0]
    col0 = 0
    if mode == "kv_f32":
        grid = (n_tiles, IN_KV_TILES)
        col0 = IN_Z_TILES
        x_map = lambda i, j: (row_tile(i), 0)
        out_specs = [pl.BlockSpec((tm, IN_TN), lambda i, j: (i, j))]
        out_shape = [jax.ShapeDtypeStruct((n_tiles * tm, D_KV), F32)]
    else:
        grid = (m // tm, D_IN // IN_TN)
        x_map = lambda i, j: (i, 0)
        if mode == "bf16":
            out_specs = [pl.BlockSpec((tm, IN_TN), lambda i, j: (i, j))]
            out_shape = [jax.ShapeDtypeStruct((m, D_IN), BF16)]
        else:
            out_specs = [pl.BlockSpec((tm, IN_TN), lambda i, j: (i, jnp.minimum(j, IN_Z_TILES - 1))),
                         pl.BlockSpec((tm, IN_TN), lambda i, j: (i, jnp.maximum(j - IN_Z_TILES, 0)))]
            out_shape = [jax.ShapeDtypeStruct((m, D_Z), BF16), jax.ShapeDtypeStruct((m, D_KV), F32)]
    jobs = _CastJobs(cast[0], cast[1], grid) if cast else None
    scratch = [pltpu.VMEM((tm, D_MODEL), BF16)]
    if mode == "bf16":
        assert grid[0] * grid[1] >= 2
        w_spec = pl.BlockSpec(memory_space=pl.ANY)
        scratch += [pltpu.VMEM((W_RING, D_MODEL, IN_TN), BF16), pltpu.SemaphoreType.DMA((W_RING,))]
    else:
        w_spec = pl.BlockSpec((None, D_MODEL, IN_TN), lambda i, j: (0, 0, col0 + j))
    return pl.pallas_call(
        functools.partial(_in_proj_kernel, jobs=jobs, mode=mode),
        grid=grid,
        in_specs=[
            pl.BlockSpec((tm, D_MODEL), x_map),
            pl.BlockSpec((None, 1, D_MODEL), lambda i, j: (layer, 0, 0)),
            w_spec,
        ] + (jobs.in_specs() if jobs else []),
        out_specs=out_specs + (jobs.out_specs() if jobs else []),
        out_shape=out_shape + (jobs.out_shape() if jobs else []),
        scratch_shapes=scratch,
        compiler_params=_params(("arbitrary", "arbitrary")),
        name="in_proj",
    )(x, g, w, *(cast[0] if cast else ()))


_XPAD = 8
LANES = 128
_SEGS = 8
_SEG_LEN = 4


def _sigmoid(x):
    return 0.5 * (jnp.tanh(0.5 * x) + 1.0)


def _lru_kernel(ax_ref, ag_ref, cbuf_ref, h0_ref, cw_ref, cb_ref, wa_ref, ba_ref, wx_ref, bx_ref,
                lam_ref, out_ref, convo_ref, ho_ref, xbuf, a_s, b_s, h_s):
    t = pl.program_id(1)
    nt = pl.num_programs(1)
    tt = ax_ref.shape[0]

    @pl.when(t == 0)
    def _():
        xbuf[...] = jnp.zeros_like(xbuf)
        xbuf[_XPAD - (CONV_W - 1):, :] = cbuf_ref[...]
        h_s[...] = h0_ref[...]

    x = ax_ref[...].astype(F32)
    xe = jnp.concatenate([xbuf[...], x], axis=0)
    acc = xe * cw_ref[0:1, :]
    for k in range(1, CONV_W):
        acc = xe * cw_ref[k:k + 1, :] + pltpu.roll(acc, 1, 0)
    u = cb_ref[...] + acc[_XPAD:, :]
    tail = x[tt - (CONV_W - 1):, :]
    xbuf[...] = x[tt - _XPAD:, :]

    ub = u.astype(BF16)
    r_parts, i_parts = [], []
    for n in range(LRU_BLOCKS):
        un = ub[:, n * LRU_BLOCK:(n + 1) * LRU_BLOCK]
        r_parts.append(jnp.dot(un, wa_ref[n], preferred_element_type=F32))
        i_parts.append(jnp.dot(un, wx_ref[n], preferred_element_type=F32))
    r = _sigmoid(jnp.concatenate(r_parts, axis=1) + ba_ref[...])
    i = _sigmoid(jnp.concatenate(i_parts, axis=1) + bx_ref[...])
    log_a = -LRU_C * r * jax.nn.softplus(-lam_ref[...])
    a = jnp.exp(log_a)
    y = 1.0 - a * a
    b = jnp.where(y > 0.0, y * lax.rsqrt(y), 0.0) * (i * u)
    n_lg = D_RNN // LANES
    for lg in range(n_lg):
        a_s[lg] = a[:, lg * LANES:(lg + 1) * LANES]
        b_s[lg] = b[:, lg * LANES:(lg + 1) * LANES]

    row = lax.broadcasted_iota(jnp.int32, (_SEGS, LANES), 0)
    sub = _SEGS * _SEG_LEN

    def body(sb, h):
        r0 = pl.multiple_of(sb * sub, sub)
        step = lambda j: pl.ds(r0 + j, _SEGS, stride=_SEG_LEN)
        h_next = []
        for lg in range(n_lg):
            a_g, b_g = a_s.at[lg], b_s.at[lg]
            h_g = h[:, lg * LANES:(lg + 1) * LANES]
            acs, bcs = [a_g[step(0), :]], [b_g[step(0), :]]
            for j in range(1, _SEG_LEN):
                aj = a_g[step(j), :]
                bcs.append(aj * bcs[-1] + b_g[step(j), :])
                acs.append(aj * acs[-1])
            at, bt = acs[-1], bcs[-1]
            for s in (1, 2, 4):
                m = row >= s
                bt_new = jnp.where(m, at * pltpu.roll(bt, s, 0) + bt, bt)
                at = jnp.where(m, at * pltpu.roll(at, s, 0), at)
                bt = bt_new
            after = at * h_g + bt
            entry = jnp.where(row >= 1, pltpu.roll(after, 1, 0), h_g)
            for j in range(_SEG_LEN):
                b_g[step(j), :] = acs[j] * entry + bcs[j]
            h_next.append(after[_SEGS - 1:_SEGS, :])
        return jnp.concatenate(h_next, axis=1)

    h = lax.fori_loop(0, tt // sub, body, h_s[...])
    h_s[...] = h
    hs = jnp.concatenate([b_s[lg] for lg in range(n_lg)], axis=1)
    out_ref[...] = (hs * jax.nn.gelu(ag_ref[...].astype(F32))).astype(BF16)

    @pl.when(t == nt - 1)
    def _():
        convo_ref[...] = tail
        ho_ref[...] = h


def _lru(z, conv_buf, h0, state_layer, cw, cb, wa, ba, wx, bx, lam, layer, n_seq, t_len, tt):
    nt = t_len // tt
    row = lambda b, t: b * nt + t
    vec = lambda: pl.BlockSpec((None, 1, D_RNN), lambda b, t: (layer, 0, 0))
    blk = lambda: pl.BlockSpec((None, LRU_BLOCKS, LRU_BLOCK, LRU_BLOCK), lambda b, t: (layer, 0, 0, 0))
    return pl.pallas_call(
        _lru_kernel,
        grid=(n_seq, nt),
        in_specs=[
            pl.BlockSpec((tt, D_RNN), lambda b, t: (row(b, t), Z_AX // D_RNN)),
            pl.BlockSpec((tt, D_RNN), lambda b, t: (row(b, t), Z_AG // D_RNN)),
            pl.BlockSpec((None, None, CONV_W - 1, D_RNN), lambda b, t: (state_layer, b, 0, 0)),
            pl.BlockSpec((None, None, 1, D_RNN), lambda b, t: (state_layer, b, 0, 0)),
            pl.BlockSpec((None, CONV_W, D_RNN), lambda b, t: (layer, 0, 0)),
            vec(), blk(), vec(), blk(), vec(), vec(),
        ],
        out_specs=[
            pl.BlockSpec((tt, D_RNN), lambda b, t: (row(b, t), 0)),
            pl.BlockSpec((None, CONV_W - 1, D_RNN), lambda b, t: (b, 0, 0)),
            pl.BlockSpec((None, 1, D_RNN), lambda b, t: (b, 0, 0)),
        ],
        out_shape=[
            jax.ShapeDtypeStruct((n_seq * t_len, D_RNN), BF16),
            jax.ShapeDtypeStruct((n_seq, CONV_W - 1, D_RNN), F32),
            jax.ShapeDtypeStruct((n_seq, 1, D_RNN), F32),
        ],
        scratch_shapes=[
            pltpu.VMEM((_XPAD, D_RNN), F32),
            pltpu.VMEM((D_RNN // LANES, tt, LANES), F32),
            pltpu.VMEM((D_RNN // LANES, tt, LANES), F32),
            pltpu.VMEM((1, D_RNN), F32),
        ],
        compiler_params=_params(("parallel", "arbitrary")),
        name="lru",
    )(z, z, conv_buf, h0, cw, cb, wa, ba, wx, bx, lam)


def _dot_nt(a, b):
    return lax.dot_general(a, b, (((1,), (1,)), ((), ())), preferred_element_type=F32)


def _with_ones(v):
    return jnp.concatenate([v, jnp.ones_like(v)], axis=1)


def _swa_chunk(q, kband, vext, sinks, valid):
    qst = jnp.concatenate([q[:, g * HEAD_DIM:(g + 1) * HEAD_DIM] for g in range(SWA_GROUP)], axis=0)
    s = _dot_nt(qst, kband) * (ATTN_SCALE * LOG2E)
    if valid is not None:
        s = jnp.where(valid, s, NEG)
    es, sink_e = [], []
    for g in range(SWA_GROUP):
        sg = s[g * CHUNK:(g + 1) * CHUNK, :]
        sink2 = sinks[g] * LOG2E
        m = jnp.maximum(jnp.max(sg, axis=-1, keepdims=True), sink2)
        es.append(jnp.exp2(sg - m).astype(BF16))
        sink_e.append(jnp.exp2(sink2 - m))
    r = jnp.dot(jnp.concatenate(es, axis=0), vext, preferred_element_type=F32)
    outs = []
    for g in range(SWA_GROUP):
        rg = r[g * CHUNK:(g + 1) * CHUNK, :]
        outs.append(rg[:, :HEAD_DIM] * (1.0 / (rg[:, HEAD_DIM:] + sink_e[g])))
    return jnp.concatenate(outs, axis=1)


def _cb_attend(q, kband, vext, bias2):
    s = _dot_nt(q, kband) * (ATTN_SCALE * LOG2E) + bias2
    m = jnp.max(s, axis=-1, keepdims=True)
    r = jnp.dot(jnp.exp2(s - m).astype(BF16), vext, preferred_element_type=F32)
    return r[:, :HEAD_DIM] * (1.0 / r[:, HEAD_DIM:])


def _cb_bias_block(r, variant, n_rows, n_cols):
    t = pltpu.roll(jnp.broadcast_to(r, (n_rows, CB_RLEN)), 0, 1, stride=1, stride_axis=0)[:, :n_cols]
    c0 = variant * Q_CHUNKS
    sc = max(c0 - CB_PREV, 0)
    qc = c0 + lax.broadcasted_iota(jnp.int32, (n_rows, n_cols), 0) // CHUNK
    kc = sc + lax.broadcasted_iota(jnp.int32, (n_rows, n_cols), 1) // CHUNK
    return jnp.where(kc <= qc, jnp.where(kc >= qc - CB_PREV, t * LOG2E, NEG), NEG)


def _cast_rows(dst, src, n_rows, step, ones=False):
    def body(i, c):
        r0 = pl.multiple_of(i * step, step)
        v = src[pl.ds(r0, step), :].astype(BF16)
        dst[pl.ds(r0, step), :] = _with_ones(v) if ones else v
        return c
    lax.fori_loop(0, n_rows // step, body, 0)


SWA_BAND = (SWA_PREV + 1) * CHUNK
SWA_TQ = 128 * CHUNK


def _swa_prompt_kernel(sink_ref, q_ref, k_ref, v_ref, o_ref, vb_s, *, layer):
    kh = pl.program_id(1)
    qi = pl.program_id(2)
    s_len = k_ref.shape[0]

    @pl.when(qi == 0)
    def _():
        _cast_rows(vb_s, v_ref, s_len, 512, ones=True)

    sinks = [sink_ref[layer, kh * SWA_GROUP + g] for g in range(SWA_GROUP)]
    jchunk = lax.broadcasted_iota(jnp.int32, (1, SWA_BAND), 1) // CHUNK
    n_chunks = q_ref.shape[0] // CHUNK
    for c in range(n_chunks):
        cg = qi * n_chunks + c
        sc = jnp.maximum(cg - SWA_PREV, 0)
        s0 = pl.multiple_of(sc * CHUNK, CHUNK)
        valid = (jchunk + sc) <= cg
        o = _swa_chunk(q_ref[c * CHUNK:(c + 1) * CHUNK, :], k_ref[pl.ds(s0, SWA_BAND), :],
                       vb_s[pl.ds(s0, SWA_BAND), :], sinks, valid)
        o_ref[c * CHUNK:(c + 1) * CHUNK, :] = o.astype(BF16)


def _swa_prompt(z, sinks, layer, n_seq, s_len):
    tq = min(SWA_TQ, s_len)
    nq = s_len // tq
    gw = SWA_GROUP * HEAD_DIM
    return pl.pallas_call(
        functools.partial(_swa_prompt_kernel, layer=layer),
        grid=(n_seq, SWA_KV_HEADS, nq),
        in_specs=[
            pl.BlockSpec(memory_space=pltpu.SMEM),
            pl.BlockSpec((tq, gw), lambda b, k, q: (b * nq + q, Z_QB // gw + k)),
            pl.BlockSpec((s_len, HEAD_DIM), lambda b, k, q: (b, (D_Z + KV_KB) // HEAD_DIM + k)),
            pl.BlockSpec((s_len, HEAD_DIM), lambda b, k, q: (b, (D_Z + KV_VB) // HEAD_DIM + k)),
        ],
        out_specs=pl.BlockSpec((tq, gw), lambda b, k, q: (b * nq + q, k)),
        out_shape=jax.ShapeDtypeStruct((n_seq * s_len, BRANCH_WIDTH), BF16),
        scratch_shapes=[pltpu.VMEM((s_len, 2 * HEAD_DIM), BF16)],
        compiler_params=_params(("parallel", "parallel", "arbitrary")),
        name="swa_prompt",
    )(sinks, z, z, z)


STEP_SEQS = 4


def _step_seqs(n_seq):
    g = min(STEP_SEQS, n_seq)
    while n_seq % g:
        g -= 1
    return g


def _swa_step_kernel(sink_ref, q_ref, k_ref, v_ref, ck_ref, cv_ref, o_ref, ko_ref, vo_ref, *, layer):
    n_s = ck_ref.shape[0]
    n_past = ck_ref.shape[1] // SWA_KV_HEADS
    gw = SWA_GROUP * HEAD_DIM
    for s in range(n_s):
        rows = slice(s * CHUNK, (s + 1) * CHUNK)
        ck_s, cv_s = ck_ref.at[s], cv_ref.at[s]
        for kh in range(SWA_KV_HEADS):
            cs = slice(kh * HEAD_DIM, (kh + 1) * HEAD_DIM)
            kn = k_ref[rows, cs]
            vn = v_ref[rows, cs]
            kfull = jnp.concatenate(
                [ck_s[pl.ds(kh, n_past, stride=SWA_KV_HEADS), :].astype(BF16), kn.astype(BF16)], axis=0)
            vfull = jnp.concatenate(
                [cv_s[pl.ds(kh, n_past, stride=SWA_KV_HEADS), :].astype(BF16), vn.astype(BF16)], axis=0)
            sinks = [sink_ref[layer, kh * SWA_GROUP + g] for g in range(SWA_GROUP)]
            o = _swa_chunk(q_ref[rows, kh * gw:(kh + 1) * gw], kfull, _with_ones(vfull), sinks, None)
            o_ref[rows, kh * gw:(kh + 1) * gw] = o.astype(BF16)
            new = pl.ds(s * CHUNK * SWA_KV_HEADS + kh, CHUNK, stride=SWA_KV_HEADS)
            ko_ref[new, :] = kn
            vo_ref[new, :] = vn


def _swa_step(z, zkv, cache_k, cache_v, sinks, layer, n_seq):
    rows_past = cache_k.shape[2]
    rows_new = CHUNK * SWA_KV_HEADS
    g = _step_seqs(n_seq)
    cache = lambda: pl.BlockSpec((None, g, rows_past, HEAD_DIM), lambda b: (layer, b, 0, 0))
    new = lambda: pl.BlockSpec((g * rows_new, HEAD_DIM), lambda b: (b, 0))
    return pl.pallas_call(
        functools.partial(_swa_step_kernel, layer=layer),
        grid=(n_seq // g,),
        in_specs=[
            pl.BlockSpec(memory_space=pltpu.SMEM),
            pl.BlockSpec((g * CHUNK, BRANCH_WIDTH), lambda b: (b, Z_QB // BRANCH_WIDTH)),
            pl.BlockSpec((g * CHUNK, SWA_KVW), lambda b: (b, KV_KB // SWA_KVW)),
            pl.BlockSpec((g * CHUNK, SWA_KVW), lambda b: (b, KV_VB // SWA_KVW)),
            cache(), cache(),
        ],
        out_specs=[pl.BlockSpec((g * CHUNK, BRANCH_WIDTH), lambda b: (b, 0)), new(), new()],
        out_shape=[jax.ShapeDtypeStruct((n_seq * CHUNK, BRANCH_WIDTH), BF16),
                   jax.ShapeDtypeStruct((n_seq * rows_new, HEAD_DIM), F32),
                   jax.ShapeDtypeStruct((n_seq * rows_new, HEAD_DIM), F32)],
        compiler_params=_params(("parallel",)),
        name="swa_step",
    )(sinks, z, zkv, zkv, cache_k, cache_v)


def _cb_prompt_kernel(q_ref, k_ref, v_ref, r_ref, o_ref, vb_s, bias_s):
    qi = pl.program_id(2)
    s_len = k_ref.shape[0]

    @pl.when(qi == 0)
    def _():
        _cast_rows(vb_s, v_ref, s_len, 512, ones=True)
        for v in range(CB_VARIANTS):
            bias_s[v] = _cb_bias_block(r_ref[v], v, Q_BLOCK, CB_BAND)

    n_sub = q_ref.shape[0] // Q_BLOCK
    for sub in range(n_sub):
        blk = qi * n_sub + sub
        sc = jnp.maximum(blk * Q_CHUNKS - CB_PREV, 0)
        s0 = pl.multiple_of(sc * CHUNK, CHUNK)
        rows = slice(sub * Q_BLOCK, (sub + 1) * Q_BLOCK)
        o = _cb_attend(q_ref[rows, :], k_ref[pl.ds(s0, CB_BAND), :], vb_s[pl.ds(s0, CB_BAND), :],
                       bias_s[jnp.minimum(blk, CB_VARIANTS - 1)])
        o_ref[rows, :] = o.astype(BF16)


def _cb_prompt(z, rows, layer, n_seq, s_len):
    tq = min(Q_BLOCK * CB_QSUB, s_len)
    nq = s_len // tq
    return pl.pallas_call(
        _cb_prompt_kernel,
        grid=(n_seq, CB_HEADS, nq),
        in_specs=[
            pl.BlockSpec((tq, HEAD_DIM), lambda b, h, q: (b * nq + q, Z_QC // HEAD_DIM + h)),
            pl.BlockSpec((s_len, HEAD_DIM), lambda b, h, q: (b, (D_Z + KV_KC) // HEAD_DIM + h)),
            pl.BlockSpec((s_len, HEAD_DIM), lambda b, h, q: (b, (D_Z + KV_VC) // HEAD_DIM + h)),
            pl.BlockSpec((None, CB_VARIANTS, None, 1, CB_RLEN), lambda b, h, q: (layer, 0, h, 0, 0)),
        ],
        out_specs=pl.BlockSpec((tq, HEAD_DIM), lambda b, h, q: (b * nq + q, h)),
        out_shape=jax.ShapeDtypeStruct((n_seq * s_len, BRANCH_WIDTH), BF16),
        scratch_shapes=[pltpu.VMEM((s_len, 2 * HEAD_DIM), BF16),
                        pltpu.VMEM((CB_VARIANTS, Q_BLOCK, CB_BAND), F32)],
        compiler_params=_params(("parallel", "parallel", "arbitrary")),
        name="cb_prompt",
    )(z, z, z, rows)


def _cb_step_kernel(q_ref, k_ref, v_ref, ck_ref, cv_ref, r_ref, o_ref, ko_ref, vo_ref, bias_s):
    n_s = ck_ref.shape[0]
    n_past = ck_ref.shape[1] // CB_HEADS
    n_k = n_past + CHUNK

    @pl.when(pl.program_id(0) == 0)
    def _():
        for h in range(CB_HEADS):
            bias_s[h] = _cb_bias_block(r_ref[h], CB_VARIANTS - 1, CHUNK, n_k)

    for s in range(n_s):
        rows = slice(s * CHUNK, (s + 1) * CHUNK)
        ck_s, cv_s = ck_ref.at[s], cv_ref.at[s]
        for h in range(CB_HEADS):
            cs = slice(h * HEAD_DIM, (h + 1) * HEAD_DIM)
            kn = k_ref[rows, cs]
            vn = v_ref[rows, cs]
            kfull = jnp.concatenate(
                [ck_s[pl.ds(h, n_past, stride=CB_HEADS), :].astype(BF16), kn.astype(BF16)], axis=0)
            vfull = jnp.concatenate(
                [cv_s[pl.ds(h, n_past, stride=CB_HEADS), :].astype(BF16), vn.astype(BF16)], axis=0)
            o_ref[rows, cs] = _cb_attend(q_ref[rows, cs], kfull, _with_ones(vfull), bias_s[h]).astype(BF16)
            new = pl.ds(s * CHUNK * CB_HEADS + h, CHUNK, stride=CB_HEADS)
            ko_ref[new, :] = kn
            vo_ref[new, :] = vn


def _cb_step(z, zkv, cache_k, cache_v, rows, layer, n_seq):
    rows_past = cache_k.shape[2]
    n_k = rows_past // CB_HEADS + CHUNK
    rows_new = CHUNK * CB_HEADS
    g = _step_seqs(n_seq)
    cache = lambda: pl.BlockSpec((None, g, rows_past, HEAD_DIM), lambda b: (layer, b, 0, 0))
    new = lambda: pl.BlockSpec((g * rows_new, HEAD_DIM), lambda b: (b, 0))
    return pl.pallas_call(
        _cb_step_kernel,
        grid=(n_seq // g,),
        in_specs=[
            pl.BlockSpec((g * CHUNK, BRANCH_WIDTH), lambda b: (b, Z_QC // BRANCH_WIDTH)),
            pl.BlockSpec((g * CHUNK, CB_KVW), lambda b: (b, KV_KC // CB_KVW)),
            pl.BlockSpec((g * CHUNK, CB_KVW), lambda b: (b, KV_VC // CB_KVW)),
            cache(), cache(),
            pl.BlockSpec((None, None, CB_HEADS, 1, CB_RLEN),
                         lambda b: (layer, CB_VARIANTS - 1, 0, 0, 0)),
        ],
        out_specs=[pl.BlockSpec((g * CHUNK, BRANCH_WIDTH), lambda b: (b, 0)), new(), new()],
        out_shape=[jax.ShapeDtypeStruct((n_seq * CHUNK, BRANCH_WIDTH), BF16),
                   jax.ShapeDtypeStruct((n_seq * rows_new, HEAD_DIM), F32),
                   jax.ShapeDtypeStruct((n_seq * rows_new, HEAD_DIM), F32)],
        scratch_shapes=[pltpu.VMEM((CB_HEADS, CHUNK, n_k), F32)],
        compiler_params=_params(("arbitrary",)),
        name="cb_step",
    )(z, zkv, zkv, cache_k, cache_v, rows)


def _merge_kernel(x_ref, a_ref, b_ref, c_ref, ga_ref, gb_ref, gc_ref, wb_ref, wo_ref, o_ref):
    mixed = None
    for r, (br, gr) in enumerate(((a_ref, ga_ref), (b_ref, gb_ref), (c_ref, gc_ref))):
        proj = jnp.dot(br[...], wb_ref[r], preferred_element_type=F32)
        gate = jax.nn.sigmoid(gr[...].astype(F32))
        mixed = gate * proj if mixed is None else mixed + gate * proj
    o_ref[...] = x_ref[...] + jnp.dot(mixed.astype(BF16), wo_ref[...], preferred_element_type=F32)


def _merge(x, out_a, out_b, out_c, z, w_branch, w_out, layer, tm):
    m = x.shape[0]
    branch = lambda: pl.BlockSpec((tm, BRANCH_WIDTH), lambda i: (i, 0))
    gate = lambda r: pl.BlockSpec((tm, D_MODEL), lambda i: (i, Z_GATES // D_MODEL + r))
    return pl.pallas_call(
        _merge_kernel,
        grid=(m // tm,),
        in_specs=[
            pl.BlockSpec((tm, D_MODEL), lambda i: (i, 0)),
            branch(), branch(), branch(),
            gate(0), gate(1), gate(2),
            pl.BlockSpec((None, N_BRANCH, BRANCH_WIDTH, D_MODEL), lambda i: (layer, 0, 0, 0),
                         pipeline_mode=pl.Buffered(1)),
            pl.BlockSpec((None, D_MODEL, D_MODEL), lambda i: (layer, 0, 0),
                         pipeline_mode=pl.Buffered(1)),
        ],
        out_specs=pl.BlockSpec((tm, D_MODEL), lambda i: (i, 0)),
        out_shape=jax.ShapeDtypeStruct((m, D_MODEL), F32),
        compiler_params=_params(("parallel",)),
        name="merge",
    )(x, out_a, out_b, out_c, z, z, z, w_branch, w_out)


def _mlp_kernel(*refs, final_norm, jobs):
    n_side = len(jobs.weights) if jobs else 0
    x_ref, g_ref, fg_ref, wu_ref, wd_ref = refs[:5]
    side_in = refs[5:5 + n_side]
    o_ref = refs[5 + n_side]
    side_out = refs[6 + n_side:6 + 2 * n_side]
    hn_ref = refs[-1]
    j = pl.program_id(1)

    @pl.when(j == 0)
    def _():
        xf = x_ref[...]
        hn_ref[...] = _rms(xf, g_ref[...]).astype(BF16)
        o_ref[...] = xf

    h = jnp.dot(hn_ref[...], wu_ref[...], preferred_element_type=F32)
    h = jnp.square(jnp.maximum(h, 0.0)).astype(BF16)
    o_ref[...] += jnp.dot(h, wd_ref[...], preferred_element_type=F32)

    if final_norm:
        @pl.when(j == pl.num_programs(1) - 1)
        def _():
            o_ref[...] = _rms(o_ref[...], fg_ref[...])

    if jobs:
        jobs.run(side_in, side_out)


def _mlp(x, g, final_g, w_up, w_down, layer, tm, tf, final_norm, cast=None):
    m = x.shape[0]
    grid = (m // tm, D_FF // tf)
    jobs = _CastJobs(cast[0], cast[1], grid, cast[2]) if cast else None
    return pl.pallas_call(
        functools.partial(_mlp_kernel, final_norm=final_norm, jobs=jobs),
        grid=grid,
        in_specs=[
            pl.BlockSpec((tm, D_MODEL), lambda i, j: (i, 0)),
            pl.BlockSpec((None, 1, D_MODEL), lambda i, j: (layer, 0, 0)),
            pl.BlockSpec((1, D_MODEL), lambda i, j: (0, 0)),
            pl.BlockSpec((None, D_MODEL, tf), lambda i, j: (0, 0, j)),
            pl.BlockSpec((None, tf, D_MODEL), lambda i, j: (0, j, 0)),
        ] + (jobs.in_specs() if jobs else []),
        out_specs=[pl.BlockSpec((tm, D_MODEL), lambda i, j: (i, 0))] + (jobs.out_specs() if jobs else []),
        out_shape=[jax.ShapeDtypeStruct((m, D_MODEL), F32)] + (jobs.out_shape() if jobs else []),
        scratch_shapes=[pltpu.VMEM((tm, D_MODEL), BF16)],
        compiler_params=_params(("arbitrary", "arbitrary")),
        name="mlp",
    )(x, g, final_g, w_up, w_down, *(cast[0] if cast else ()))


def _permute_w_in(w_in):
    seg = lambda o, w: w_in[..., o:o + w]
    return jnp.concatenate([
        seg(_O_AX, D_RNN), seg(_O_AG, D_RNN), seg(_O_QB, BRANCH_WIDTH), seg(_O_QC, BRANCH_WIDTH),
        seg(_O_GATES, N_BRANCH * D_MODEL),
        seg(_O_KC, CB_KVW), seg(_O_VC, CB_KVW), seg(_O_KB, SWA_KVW), seg(_O_VB, SWA_KVW)],
        axis=-1).astype(BF16)


def _cb_bias_rows(table):
    m = jnp.arange(CB_RLEN)
    rel = jnp.where(m < CB_BAND, m, m - CB_RLEN)
    out = []
    for v in range(CB_VARIANTS):
        c0 = v * Q_CHUNKS
        sc = max(c0 - CB_PREV, 0)
        d = (c0 - sc) * CHUNK - rel
        idx = jnp.clip(d, -REL_CLIP, REL_CLIP) + REL_CLIP
        out.append(table.astype(F32)[:, :, idx])
    return jnp.stack(out, axis=1)[:, :, :, None, :]


def _pick_tm(m, want):
    tm = min(want, m)
    while m % tm:
        tm //= 2
    return tm


def _layer(x, n_seq, t_len, conv_buf, h0, state_layer, caches, layer, p, w_in_l, w_layer):
    m = x.shape[0]
    depth = p["w_up"].shape[0]
    tm_in = _pick_tm(m, 1024)
    if w_layer is None:
        z, wu, wd, wb, wo = _in_proj(
            x, p["norm1"], w_in_l, layer, tm_in, "bf16",
            cast=((p["w_up"], p["w_down"], p["w_branch"], p["w_out"]), layer))
        w_layer = (wu[None], wd[None], wb.reshape(1, N_BRANCH, BRANCH_WIDTH, D_MODEL), wo[None])
        cast_next = ((p["w_in"],), layer + 1, _permute_w_in) if layer + 1 < depth else None
    else:
        z, zkv = _in_proj(x, p["norm1"], w_in_l, layer, tm_in, "dual")
        cast_next = None
    w_up_l, w_down_l, w_branch_l, w_out_l = w_layer
    out_a, conv_o, h_o = _lru(z, conv_buf, h0, state_layer, p["conv_w"], p["conv_b"], p["wa"], p["ba"],
                              p["wx"], p["bx"], p["lam"], layer, n_seq, t_len, _pick_tm(t_len, 1024))
    if caches is None:
        out_b = _swa_prompt(z, p["sinks"], layer, n_seq, t_len)
        out_c = _cb_prompt(z, p["cb_rows"], layer, n_seq, t_len)
        n_tail = CB_REACH
        kv3 = _in_proj(x, p["norm1"], w_in_l, layer, n_tail, "kv_f32", n_tiles=n_seq,
                       row_tile=lambda i: (i + 1) * (t_len // n_tail) - 1)[0].reshape(n_seq, n_tail, D_KV)
        kv = (kv3[:, n_tail - SWA_WINDOW:, KV_KB:KV_KB + SWA_KVW],
              kv3[:, n_tail - SWA_WINDOW:, KV_VB:KV_VB + SWA_KVW],
              kv3[:, :, KV_KC:KV_KC + CB_KVW], kv3[:, :, KV_VC:KV_VC + CB_KVW])
    else:
        ck_b, cv_b, ck_c, cv_c = caches
        out_b, kb, vb = _swa_step(z, zkv, ck_b, cv_b, p["sinks"], layer, n_seq)
        out_c, kc, vc = _cb_step(z, zkv, ck_c, cv_c, p["cb_rows"], layer, n_seq)
        kv = (kb, vb, kc, vc)
    x = _merge(x, out_a, out_b, out_c, z, w_branch_l, w_out_l, 0, _pick_tm(m, 256))
    res = _mlp(x, p["norm2"], p["final_g"], w_up_l, w_down_l, layer, _pick_tm(m, 512), 1024,
               final_norm=layer == depth - 1, cast=cast_next)
    w_in_next = res[1][None] if cast_next else None
    return res[0], conv_o, h_o[:, 0], kv, w_layer, w_in_next


def kernel(x_prompt, x_sample, state_conv, state_lru, cache_swa_k, cache_swa_v, cache_cb_k, cache_cb_v, norm1_g, w_in, conv_w, conv_b, lru_wa, lru_ba, lru_wx, lru_bx, lru_lambda, attn_sinks, rel_bias_table, w_branch, w_out, norm2_g, w_up, w_down, final_g):
    depth = w_in.shape[0]
    nb, s_len, _ = x_prompt.shape
    db, d_len, _ = x_sample.shape
    assert d_len == CHUNK and s_len % min(Q_BLOCK * CB_QSUB, s_len) == 0 and s_len % min(SWA_TQ, s_len) == 0
    assert s_len % Q_BLOCK == 0 and s_len >= CB_BAND
    assert s_len % CB_REACH == 0
    assert cache_swa_k.shape[2] == SWA_WINDOW and cache_cb_k.shape[2] == CB_REACH

    row = lambda v: v.reshape(depth, 1, -1)
    p = {
        "norm1": row(norm1_g), "norm2": row(norm2_g), "final_g": final_g.reshape(1, D_MODEL),
        "w_in": w_in, "w_up": w_up, "w_down": w_down,
        "w_branch": w_branch.reshape(depth, N_BRANCH * BRANCH_WIDTH, D_MODEL), "w_out": w_out,
        "conv_w": conv_w, "conv_b": row(conv_b),
        "wa": lru_wa.astype(BF16), "ba": row(lru_ba), "wx": lru_wx.astype(BF16), "bx": row(lru_bx),
        "lam": row(lru_lambda), "sinks": attn_sinks,
        "cb_rows": _cb_bias_rows(rel_bias_table),
    }
    w_in_l = _permute_w_in(w_in[0])[None]
    caches = (cache_swa_k.reshape(depth, db, SWA_WINDOW * SWA_KV_HEADS, HEAD_DIM),
              cache_swa_v.reshape(depth, db, SWA_WINDOW * SWA_KV_HEADS, HEAD_DIM),
              cache_cb_k.reshape(depth, db, CB_REACH * CB_HEADS, HEAD_DIM),
              cache_cb_v.reshape(depth, db, CB_REACH * CB_HEADS, HEAD_DIM))

    xp = x_prompt.reshape(nb * s_len, D_MODEL)
    xs = x_sample.reshape(db * d_len, D_MODEL)
    zero_conv = jnp.zeros((1, nb, CONV_W - 1, D_RNN), F32)
    zero_h = jnp.zeros((1, nb, 1, D_RNN), F32)
    h0_s = state_lru.reshape(depth, db, 1, D_RNN)
    heads = (SWA_KV_HEADS, SWA_KV_HEADS, CB_HEADS, CB_HEADS)
    outs = [[] for _ in range(12)]
    for l in range(depth):
        xp, conv_p, h_p, kv_p, w_layer, w_in_next = _layer(xp, nb, s_len, zero_conv, zero_h, 0, None, l, p,
                                                           w_in_l, None)
        xs, conv_s, h_s, kv_s, _, _ = _layer(xs, db, d_len, state_conv, h0_s, l, caches, l, p,
                                             w_in_l, w_layer)
        w_in_l = w_in_next
        outs[0].append(conv_p)
        outs[1].append(h_p)
        outs[6].append(conv_s)
        outs[7].append(h_s)
        for n in range(4):
            outs[2 + n].append(kv_p[n].reshape(nb, -1, heads[n], HEAD_DIM))
            outs[8 + n].append(kv_s[n].reshape(db, d_len, heads[n], HEAD_DIM))

    y_prompt = xp.reshape(nb, s_len, D_MODEL)
    y_sample = xs.reshape(db, d_len, D_MODEL)
    return (y_prompt, y_sample) + tuple(jnp.stack(o) for o in outs)
```

```python
import functools

import jax
import jax.numpy as jnp
from jax import lax
from jax.experimental import pallas as pl
from jax.experimental.pallas import tpu as pltpu

F32 = jnp.float32
BF16 = jnp.bfloat16

D_MODEL = 2048
CHUNK = 64
HEAD_DIM = 128
BRANCH_WIDTH = D_MODEL // 2
N_BRANCH = 3
D_RNN = BRANCH_WIDTH
LRU_BLOCKS = 8
LRU_BLOCK = D_RNN // LRU_BLOCKS
CONV_W = 4
LRU_C = 8.0
SWA_HEADS = BRANCH_WIDTH // HEAD_DIM
SWA_KV_HEADS = 2
SWA_GROUP = SWA_HEADS // SWA_KV_HEADS
SWA_WINDOW = 128
SWA_PREV = SWA_WINDOW // CHUNK
CB_HEADS = BRANCH_WIDTH // HEAD_DIM
CB_PREV = 8
CB_REACH = CB_PREV * CHUNK
REL_CLIP = 128
D_FF = 4 * D_MODEL
EPS = 1e-6
NEG = -1e30
ATTN_SCALE = HEAD_DIM ** -0.5
LOG2E = 1.4426950408889634

_O_AX, _O_AG, _O_QB, _O_KB, _O_VB, _O_QC, _O_KC, _O_VC, _O_GATES = (
    0, 1024, 2048, 3072, 3328, 3584, 4608, 5632, 6656)
D_IN = _O_GATES + N_BRANCH * D_MODEL
SWA_KVW = SWA_KV_HEADS * HEAD_DIM
CB_KVW = CB_HEADS * HEAD_DIM
D_KV = 2 * SWA_KVW + 2 * CB_KVW
D_Z = D_IN - D_KV
KV_KC, KV_VC, KV_KB, KV_VB = 0, 1024, 2048, 2304
Z_AX, Z_AG, Z_QB, Z_QC, Z_GATES = 0, 1024, 2048, 3072, 4096

VMEM_LIMIT_BYTES = 56 * 1024 * 1024

Q_BLOCK = 4 * CHUNK
Q_CHUNKS = Q_BLOCK // CHUNK
CB_BAND = (CB_PREV + Q_CHUNKS) * CHUNK
CB_QSUB = 32
CB_RLEN = 1024
assert CB_RLEN >= Q_BLOCK + CB_BAND - 1
CB_VARIANTS = 3


def _params(semantics):
    return pltpu.CompilerParams(dimension_semantics=semantics, vmem_limit_bytes=VMEM_LIMIT_BYTES)


def _rms(xf, g):
    return xf * lax.rsqrt(jnp.mean(xf * xf, axis=-1, keepdims=True) + EPS) * g


IN_TN = 1280
IN_KV_TILES = D_KV // IN_TN
IN_Z_TILES = D_Z // IN_TN


class _CastJobs:
    def __init__(self, weights, layer, grid, transform=None):
        self.weights, self.layer, self.transform = weights, layer, transform
        self.inner = grid[1]
        n_max = 1
        while n_max * 2 <= min(grid[0] * grid[1], MAX_CAST_STEPS):
            n_max *= 2
        self.n = []
        for w in weights:
            n = n_max
            while w.shape[1] % (n * BF16_ROWS):
                n //= 2
            self.n.append(n)

    def _spec(self, w, n, lead):
        slab = lambda i, j: jnp.minimum(i * self.inner + j, n - 1)
        if lead:
            return pl.BlockSpec((None, w.shape[1] // n, w.shape[2]), lambda i, j: (self.layer, slab(i, j), 0))
        return pl.BlockSpec((w.shape[1] // n, w.shape[2]), lambda i, j: (slab(i, j), 0))

    def in_specs(self):
        return [self._spec(w, n, True) for w, n in zip(self.weights, self.n)]

    def out_specs(self):
        return [self._spec(w, n, False) for w, n in zip(self.weights, self.n)]

    def out_shape(self):
        return [jax.ShapeDtypeStruct(w.shape[1:], BF16) for w in self.weights]

    def run(self, src_refs, dst_refs):
        step = pl.program_id(0) * self.inner + pl.program_id(1)
        for n in sorted(set(self.n)):
            @pl.when(step < n)
            def _():
                for s, d, n_w in zip(src_refs, dst_refs, self.n):
                    if n_w == n:
                        v = s[...]
                        d[...] = (self.transform(v) if self.transform else v).astype(BF16)


MAX_CAST_STEPS = 128
BF16_ROWS = 16


W_RING = 3


def _w_tile_copy(w_hbm, wbuf, sem, step):
    col = pl.multiple_of(lax.rem(step, D_IN // IN_TN) * IN_TN, IN_TN)
    slot = lax.rem(step, W_RING)
    return pltpu.make_async_copy(w_hbm.at[0, :, pl.ds(col, IN_TN)], wbuf.at[slot], sem.at[slot])


def _in_proj_kernel(*refs, jobs, mode):
    n_side = len(jobs.weights) if jobs else 0
    n_out = 2 if mode == "dual" else 1
    x_ref, g_ref, w_ref = refs[:3]
    side_in = refs[3:3 + n_side]
    outs = refs[3 + n_side:3 + n_side + n_out]
    side_out = refs[3 + n_side + n_out:3 + 2 * n_side + n_out]
    xn_ref = refs[3 + 2 * n_side + n_out]
    j = pl.program_id(1)

    @pl.when(j == 0)
    def _():
        xn_ref[...] = _rms(x_ref[...], g_ref[...]).astype(BF16)

    dot = lambda: jnp.dot(xn_ref[...], w_ref[...], preferred_element_type=F32)
    if mode == "bf16":
        wbuf, sem = refs[-2:]
        n_steps = pl.num_programs(0) * pl.num_programs(1)
        t = pl.program_id(0) * pl.num_programs(1) + j

        @pl.when(t == 0)
        def _():
            _w_tile_copy(w_ref, wbuf, sem, 0).start()
            _w_tile_copy(w_ref, wbuf, sem, 1).start()

        @pl.when(t + 2 < n_steps)
        def _():
            _w_tile_copy(w_ref, wbuf, sem, t + 2).start()

        _w_tile_copy(w_ref, wbuf, sem, t).wait()
        w_t = wbuf[lax.rem(t, W_RING)]
        outs[0][...] = jnp.dot(xn_ref[...], w_t, preferred_element_type=F32).astype(BF16)
    elif mode == "kv_f32":
        outs[0][...] = dot()
    else:
        z_ref, zkv_ref = outs

        @pl.when(j < IN_Z_TILES)
        def _():
            z_ref[...] = dot().astype(BF16)

        @pl.when(j >= IN_Z_TILES)
        def _():
            zkv_ref[...] = dot()

    if jobs:
        jobs.run(side_in, side_out)


def _in_proj(x, g, w, layer, tm, mode, cast=None, row_tile=None, n_tiles=None):
    m = x.shape[0]
    col0 = 0
    if mode == "kv_f32":
        grid = (n_tiles, IN_KV_TILES)
        col0 = IN_Z_TILES
        x_map = lambda i, j: (row_tile(i), 0)
        out_specs = [pl.BlockSpec((tm, IN_TN), lambda i, j: (i, j))]
        out_shape = [jax.ShapeDtypeStruct((n_tiles * tm, D_KV), F32)]
    else:
        grid = (m // tm, D_IN // IN_TN)
        x_map = lambda i, j: (i, 0)
        if mode == "bf16":
            out_specs = [pl.BlockSpec((tm, IN_TN), lambda i, j: (i, j))]
            out_shape = [jax.ShapeDtypeStruct((m, D_IN), BF16)]
        else:
            out_specs = [pl.BlockSpec((tm, IN_TN), lambda i, j: (i, jnp.minimum(j, IN_Z_TILES - 1))),
                         pl.BlockSpec((tm, IN_TN), lambda i, j: (i, jnp.maximum(j - IN_Z_TILES, 0)))]
            out_shape = [jax.ShapeDtypeStruct((m, D_Z), BF16), jax.ShapeDtypeStruct((m, D_KV), F32)]
    jobs = _CastJobs(cast[0], cast[1], grid) if cast else None
    scratch = [pltpu.VMEM((tm, D_MODEL), BF16)]
    if mode == "bf16":
        assert grid[0] * grid[1] >= 2
        w_spec = pl.BlockSpec(memory_space=pl.ANY)
        scratch += [pltpu.VMEM((W_RING, D_MODEL, IN_TN), BF16), pltpu.SemaphoreType.DMA((W_RING,))]
    else:
        w_spec = pl.BlockSpec((None, D_MODEL, IN_TN), lambda i, j: (0, 0, col0 + j))
    return pl.pallas_call(
        functools.partial(_in_proj_kernel, jobs=jobs, mode=mode),
        grid=grid,
        in_specs=[
            pl.BlockSpec((tm, D_MODEL), x_map),
            pl.BlockSpec((None, 1, D_MODEL), lambda i, j: (layer, 0, 0)),
            w_spec,
        ] + (jobs.in_specs() if jobs else []),
        out_specs=out_specs + (jobs.out_specs() if jobs else []),
        out_shape=out_shape + (jobs.out_shape() if jobs else []),
        scratch_shapes=scratch,
        compiler_params=_params(("arbitrary", "arbitrary")),
        name="in_proj",
    )(x, g, w, *(cast[0] if cast else ()))


_XPAD = 8
LANES = 128
_SEGS = 8
_SEG_LEN = 4


def _sigmoid(x):
    return 0.5 * (jnp.tanh(0.5 * x) + 1.0)


def _lru_kernel(ax_ref, ag_ref, cbuf_ref, h0_ref, cw_ref, cb_ref, wa_ref, ba_ref, wx_ref, bx_ref,
                lam_ref, out_ref, convo_ref, ho_ref, xbuf, a_s, b_s, h_s):
    t = pl.program_id(1)
    nt = pl.num_programs(1)
    tt = ax_ref.shape[0]

    @pl.when(t == 0)
    def _():
        xbuf[...] = jnp.zeros_like(xbuf)
        xbuf[_XPAD - (CONV_W - 1):, :] = cbuf_ref[...]
        h_s[...] = h0_ref[...]

    x = ax_ref[...].astype(F32)
    xe = jnp.concatenate([xbuf[...], x], axis=0)
    acc = xe * cw_ref[0:1, :]
    for k in range(1, CONV_W):
        acc = xe * cw_ref[k:k + 1, :] + pltpu.roll(acc, 1, 0)
    u = cb_ref[...] + acc[_XPAD:, :]
    tail = x[tt - (CONV_W - 1):, :]
    xbuf[...] = x[tt - _XPAD:, :]

    ub = u.astype(BF16)
    r_parts, i_parts = [], []
    for n in range(LRU_BLOCKS):
        un = ub[:, n * LRU_BLOCK:(n + 1) * LRU_BLOCK]
        r_parts.append(jnp.dot(un, wa_ref[n], preferred_element_type=F32))
        i_parts.append(jnp.dot(un, wx_ref[n], preferred_element_type=F32))
    r = _sigmoid(jnp.concatenate(r_parts, axis=1) + ba_ref[...])
    i = _sigmoid(jnp.concatenate(i_parts, axis=1) + bx_ref[...])
    log_a = -LRU_C * r * jax.nn.softplus(-lam_ref[...])
    a = jnp.exp(log_a)
    y = 1.0 - a * a
    b = jnp.where(y > 0.0, y * lax.rsqrt(y), 0.0) * (i * u)
    n_lg = D_RNN // LANES
    for lg in range(n_lg):
        a_s[lg] = a[:, lg * LANES:(lg + 1) * LANES]
        b_s[lg] = b[:, lg * LANES:(lg + 1) * LANES]

    row = lax.broadcasted_iota(jnp.int32, (_SEGS, LANES), 0)
    sub = _SEGS * _SEG_LEN

    def body(sb, h):
        r0 = pl.multiple_of(sb * sub, sub)
        step = lambda j: pl.ds(r0 + j, _SEGS, stride=_SEG_LEN)
        h_next = []
        for lg in range(n_lg):
            a_g, b_g = a_s.at[lg], b_s.at[lg]
            h_g = h[:, lg * LANES:(lg + 1) * LANES]
            acs, bcs = [a_g[step(0), :]], [b_g[step(0), :]]
            for j in range(1, _SEG_LEN):
                aj = a_g[step(j), :]
                bcs.append(aj * bcs[-1] + b_g[step(j), :])
                acs.append(aj * acs[-1])
            at, bt = acs[-1], bcs[-1]
            for s in (1, 2, 4):
                m = row >= s
                bt_new = jnp.where(m, at * pltpu.roll(bt, s, 0) + bt, bt)
                at = jnp.where(m, at * pltpu.roll(at, s, 0), at)
                bt = bt_new
            after = at * h_g + bt
            entry = jnp.where(row >= 1, pltpu.roll(after, 1, 0), h_g)
            for j in range(_SEG_LEN):
                b_g[step(j), :] = acs[j] * entry + bcs[j]
            h_next.append(after[_SEGS - 1:_SEGS, :])
        return jnp.concatenate(h_next, axis=1)

    h = lax.fori_loop(0, tt // sub, body, h_s[...])
    h_s[...] = h
    hs = jnp.concatenate([b_s[lg] for lg in range(n_lg)], axis=1)
    out_ref[...] = (hs * jax.nn.gelu(ag_ref[...].astype(F32))).astype(BF16)

    @pl.when(t == nt - 1)
    def _():
        convo_ref[...] = tail
        ho_ref[...] = h


def _lru(z, conv_buf, h0, state_layer, cw, cb, wa, ba, wx, bx, lam, layer, n_seq, t_len, tt):
    nt = t_len // tt
    row = lambda b, t: b * nt + t
    vec = lambda: pl.BlockSpec((None, 1, D_RNN), lambda b, t: (layer, 0, 0))
    blk = lambda: pl.BlockSpec((None, LRU_BLOCKS, LRU_BLOCK, LRU_BLOCK), lambda b, t: (layer, 0, 0, 0))
    return pl.pallas_call(
        _lru_kernel,
        grid=(n_seq, nt),
        in_specs=[
            pl.BlockSpec((tt, D_RNN), lambda b, t: (row(b, t), Z_AX // D_RNN)),
            pl.BlockSpec((tt, D_RNN), lambda b, t: (row(b, t), Z_AG // D_RNN)),
            pl.BlockSpec((None, None, CONV_W - 1, D_RNN), lambda b, t: (state_layer, b, 0, 0)),
            pl.BlockSpec((None, None, 1, D_RNN), lambda b, t: (state_layer, b, 0, 0)),
            pl.BlockSpec((None, CONV_W, D_RNN), lambda b, t: (layer, 0, 0)),
            vec(), blk(), vec(), blk(), vec(), vec(),
        ],
        out_specs=[
            pl.BlockSpec((tt, D_RNN), lambda b, t: (row(b, t), 0)),
            pl.BlockSpec((None, CONV_W - 1, D_RNN), lambda b, t: (b, 0, 0)),
            pl.BlockSpec((None, 1, D_RNN), lambda b, t: (b, 0, 0)),
        ],
        out_shape=[
            jax.ShapeDtypeStruct((n_seq * t_len, D_RNN), BF16),
            jax.ShapeDtypeStruct((n_seq, CONV_W - 1, D_RNN), F32),
            jax.ShapeDtypeStruct((n_seq, 1, D_RNN), F32),
        ],
        scratch_shapes=[
            pltpu.VMEM((_XPAD, D_RNN), F32),
            pltpu.VMEM((D_RNN // LANES, tt, LANES), F32),
            pltpu.VMEM((D_RNN // LANES, tt, LANES), F32),
            pltpu.VMEM((1, D_RNN), F32),
        ],
        compiler_params=_params(("parallel", "arbitrary")),
        name="lru",
    )(z, z, conv_buf, h0, cw, cb, wa, ba, wx, bx, lam)


def _dot_nt(a, b):
    return lax.dot_general(a, b, (((1,), (1,)), ((), ())), preferred_element_type=F32)


def _with_ones(v):
    return jnp.concatenate([v, jnp.ones_like(v)], axis=1)


def _swa_chunk(q, kband, vext, sinks, valid):
    qst = jnp.concatenate([q[:, g * HEAD_DIM:(g + 1) * HEAD_DIM] for g in range(SWA_GROUP)], axis=0)
    s = _dot_nt(qst, kband) * (ATTN_SCALE * LOG2E)
    if valid is not None:
        s = jnp.where(valid, s, NEG)
    es, sink_e = [], []
    for g in range(SWA_GROUP):
        sg = s[g * CHUNK:(g + 1) * CHUNK, :]
        sink2 = sinks[g] * LOG2E
        m = jnp.maximum(jnp.max(sg, axis=-1, keepdims=True), sink2)
        es.append(jnp.exp2(sg - m).astype(BF16))
        sink_e.append(jnp.exp2(sink2 - m))
    r = jnp.dot(jnp.concatenate(es, axis=0), vext, preferred_element_type=F32)
    outs = []
    for g in range(SWA_GROUP):
        rg = r[g * CHUNK:(g + 1) * CHUNK, :]
        outs.append(rg[:, :HEAD_DIM] * (1.0 / (rg[:, HEAD_DIM:] + sink_e[g])))
    return jnp.concatenate(outs, axis=1)


def _cb_attend(q, kband, vext, bias2):
    s = _dot_nt(q, kband) * (ATTN_SCALE * LOG2E) + bias2
    m = jnp.max(s, axis=-1, keepdims=True)
    r = jnp.dot(jnp.exp2(s - m).astype(BF16), vext, preferred_element_type=F32)
    return r[:, :HEAD_DIM] * (1.0 / r[:, HEAD_DIM:])


def _cb_bias_block(r, variant, n_rows, n_cols):
    t = pltpu.roll(jnp.broadcast_to(r, (n_rows, CB_RLEN)), 0, 1, stride=1, stride_axis=0)[:, :n_cols]
    c0 = variant * Q_CHUNKS
    sc = max(c0 - CB_PREV, 0)
    qc = c0 + lax.broadcasted_iota(jnp.int32, (n_rows, n_cols), 0) // CHUNK
    kc = sc + lax.broadcasted_iota(jnp.int32, (n_rows, n_cols), 1) // CHUNK
    return jnp.where(kc <= qc, jnp.where(kc >= qc - CB_PREV, t * LOG2E, NEG), NEG)


def _cast_rows(dst, src, n_rows, step, ones=False):
    def body(i, c):
        r0 = pl.multiple_of(i * step, step)
        v = src[pl.ds(r0, step), :].astype(BF16)
        dst[pl.ds(r0, step), :] = _with_ones(v) if ones else v
        return c
    lax.fori_loop(0, n_rows // step, body, 0)


SWA_BAND = (SWA_PREV + 1) * CHUNK
SWA_TQ = 128 * CHUNK


def _swa_prompt_kernel(sink_ref, q_ref, k_ref, v_ref, o_ref, vb_s, *, layer):
    kh = pl.program_id(1)
    qi = pl.program_id(2)
    s_len = k_ref.shape[0]

    @pl.when(qi == 0)
    def _():
        _cast_rows(vb_s, v_ref, s_len, 512, ones=True)

    sinks = [sink_ref[layer, kh * SWA_GROUP + g] for g in range(SWA_GROUP)]
    jchunk = lax.broadcasted_iota(jnp.int32, (1, SWA_BAND), 1) // CHUNK
    n_chunks = q_ref.shape[0] // CHUNK
    for c in range(n_chunks):
        cg = qi * n_chunks + c
        sc = jnp.maximum(cg - SWA_PREV, 0)
        s0 = pl.multiple_of(sc * CHUNK, CHUNK)
        valid = (jchunk + sc) <= cg
        o = _swa_chunk(q_ref[c * CHUNK:(c + 1) * CHUNK, :], k_ref[pl.ds(s0, SWA_BAND), :],
                       vb_s[pl.ds(s0, SWA_BAND), :], sinks, valid)
        o_ref[c * CHUNK:(c + 1) * CHUNK, :] = o.astype(BF16)


def _swa_prompt(z, sinks, layer, n_seq, s_len):
    tq = min(SWA_TQ, s_len)
    nq = s_len // tq
    gw = SWA_GROUP * HEAD_DIM
    return pl.pallas_call(
        functools.partial(_swa_prompt_kernel, layer=layer),
        grid=(n_seq, SWA_KV_HEADS, nq),
        in_specs=[
            pl.BlockSpec(memory_space=pltpu.SMEM),
            pl.BlockSpec((tq, gw), lambda b, k, q: (b * nq + q, Z_QB // gw + k)),
            pl.BlockSpec((s_len, HEAD_DIM), lambda b, k, q: (b, (D_Z + KV_KB) // HEAD_DIM + k)),
            pl.BlockSpec((s_len, HEAD_DIM), lambda b, k, q: (b, (D_Z + KV_VB) // HEAD_DIM + k)),
        ],
        out_specs=pl.BlockSpec((tq, gw), lambda b, k, q: (b * nq + q, k)),
        out_shape=jax.ShapeDtypeStruct((n_seq * s_len, BRANCH_WIDTH), BF16),
        scratch_shapes=[pltpu.VMEM((s_len, 2 * HEAD_DIM), BF16)],
        compiler_params=_params(("parallel", "parallel", "arbitrary")),
        name="swa_prompt",
    )(sinks, z, z, z)


STEP_SEQS = 4


def _step_seqs(n_seq):
    g = min(STEP_SEQS, n_seq)
    while n_seq % g:
        g -= 1
    return g


def _swa_step_kernel(sink_ref, q_ref, k_ref, v_ref, ck_ref, cv_ref, o_ref, ko_ref, vo_ref, *, layer):
    n_s = ck_ref.shape[0]
    n_past = ck_ref.shape[1] // SWA_KV_HEADS
    gw = SWA_GROUP * HEAD_DIM
    for s in range(n_s):
        rows = slice(s * CHUNK, (s + 1) * CHUNK)
        ck_s, cv_s = ck_ref.at[s], cv_ref.at[s]
        for kh in range(SWA_KV_HEADS):
            cs = slice(kh * HEAD_DIM, (kh + 1) * HEAD_DIM)
            kn = k_ref[rows, cs]
            vn = v_ref[rows, cs]
            kfull = jnp.concatenate(
                [ck_s[pl.ds(kh, n_past, stride=SWA_KV_HEADS), :].astype(BF16), kn.astype(BF16)], axis=0)
            vfull = jnp.concatenate(
                [cv_s[pl.ds(kh, n_past, stride=SWA_KV_HEADS), :].astype(BF16), vn.astype(BF16)], axis=0)
            sinks = [sink_ref[layer, kh * SWA_GROUP + g] for g in range(SWA_GROUP)]
            o = _swa_chunk(q_ref[rows, kh * gw:(kh + 1) * gw], kfull, _with_ones(vfull), sinks, None)
            o_ref[rows, kh * gw:(kh + 1) * gw] = o.astype(BF16)
            new = pl.ds(s * CHUNK * SWA_KV_HEADS + kh, CHUNK, stride=SWA_KV_HEADS)
            ko_ref[new, :] = kn
            vo_ref[new, :] = vn


def _swa_step(z, zkv, cache_k, cache_v, sinks, layer, n_seq):
    rows_past = cache_k.shape[2]
    rows_new = CHUNK * SWA_KV_HEADS
    g = _step_seqs(n_seq)
    cache = lambda: pl.BlockSpec((None, g, rows_past, HEAD_DIM), lambda b: (layer, b, 0, 0))
    new = lambda: pl.BlockSpec((g * rows_new, HEAD_DIM), lambda b: (b, 0))
    return pl.pallas_call(
        functools.partial(_swa_step_kernel, layer=layer),
        grid=(n_seq // g,),
        in_specs=[
            pl.BlockSpec(memory_space=pltpu.SMEM),
            pl.BlockSpec((g * CHUNK, BRANCH_WIDTH), lambda b: (b, Z_QB // BRANCH_WIDTH)),
            pl.BlockSpec((g * CHUNK, SWA_KVW), lambda b: (b, KV_KB // SWA_KVW)),
            pl.BlockSpec((g * CHUNK, SWA_KVW), lambda b: (b, KV_VB // SWA_KVW)),
            cache(), cache(),
        ],
        out_specs=[pl.BlockSpec((g * CHUNK, BRANCH_WIDTH), lambda b: (b, 0)), new(), new()],
        out_shape=[jax.ShapeDtypeStruct((n_seq * CHUNK, BRANCH_WIDTH), BF16),
                   jax.ShapeDtypeStruct((n_seq * rows_new, HEAD_DIM), F32),
                   jax.ShapeDtypeStruct((n_seq * rows_new, HEAD_DIM), F32)],
        compiler_params=_params(("parallel",)),
        name="swa_step",
    )(sinks, z, zkv, zkv, cache_k, cache_v)


def _cb_prompt_kernel(q_ref, k_ref, v_ref, r_ref, o_ref, vb_s, bias_s):
    qi = pl.program_id(2)
    s_len = k_ref.shape[0]

    @pl.when(qi == 0)
    def _():
        _cast_rows(vb_s, v_ref, s_len, 512, ones=True)
        for v in range(CB_VARIANTS):
            bias_s[v] = _cb_bias_block(r_ref[v], v, Q_BLOCK, CB_BAND)

    n_sub = q_ref.shape[0] // Q_BLOCK
    for sub in range(n_sub):
        blk = qi * n_sub + sub
        sc = jnp.maximum(blk * Q_CHUNKS - CB_PREV, 0)
        s0 = pl.multiple_of(sc * CHUNK, CHUNK)
        rows = slice(sub * Q_BLOCK, (sub + 1) * Q_BLOCK)
        o = _cb_attend(q_ref[rows, :], k_ref[pl.ds(s0, CB_BAND), :], vb_s[pl.ds(s0, CB_BAND), :],
                       bias_s[jnp.minimum(blk, CB_VARIANTS - 1)])
        o_ref[rows, :] = o.astype(BF16)


def _cb_prompt(z, rows, layer, n_seq, s_len):
    tq = min(Q_BLOCK * CB_QSUB, s_len)
    nq = s_len // tq
    return pl.pallas_call(
        _cb_prompt_kernel,
        grid=(n_seq, CB_HEADS, nq),
        in_specs=[
            pl.BlockSpec((tq, HEAD_DIM), lambda b, h, q: (b * nq + q, Z_QC // HEAD_DIM + h)),
            pl.BlockSpec((s_len, HEAD_DIM), lambda b, h, q: (b, (D_Z + KV_KC) // HEAD_DIM + h)),
            pl.BlockSpec((s_len, HEAD_DIM), lambda b, h, q: (b, (D_Z + KV_VC) // HEAD_DIM + h)),
            pl.BlockSpec((None, CB_VARIANTS, None, 1, CB_RLEN), lambda b, h, q: (layer, 0, h, 0, 0)),
        ],
        out_specs=pl.BlockSpec((tq, HEAD_DIM), lambda b, h, q: (b * nq + q, h)),
        out_shape=jax.ShapeDtypeStruct((n_seq * s_len, BRANCH_WIDTH), BF16),
        scratch_shapes=[pltpu.VMEM((s_len, 2 * HEAD_DIM), BF16),
                        pltpu.VMEM((CB_VARIANTS, Q_BLOCK, CB_BAND), F32)],
        compiler_params=_params(("parallel", "parallel", "arbitrary")),
        name="cb_prompt",
    )(z, z, z, rows)


def _cb_step_kernel(q_ref, k_ref, v_ref, ck_ref, cv_ref, r_ref, o_ref, ko_ref, vo_ref, bias_s):
    n_s = ck_ref.shape[0]
    n_past = ck_ref.shape[1] // CB_HEADS
    n_k = n_past + CHUNK

    @pl.when(pl.program_id(0) == 0)
    def _():
        for h in range(CB_HEADS):
            bias_s[h] = _cb_bias_block(r_ref[h], CB_VARIANTS - 1, CHUNK, n_k)

    for s in range(n_s):
        rows = slice(s * CHUNK, (s + 1) * CHUNK)
        ck_s, cv_s = ck_ref.at[s], cv_ref.at[s]
        for h in range(CB_HEADS):
            cs = slice(h * HEAD_DIM, (h + 1) * HEAD_DIM)
            kn = k_ref[rows, cs]
            vn = v_ref[rows, cs]
            kfull = jnp.concatenate(
                [ck_s[pl.ds(h, n_past, stride=CB_HEADS), :].astype(BF16), kn.astype(BF16)], axis=0)
            vfull = jnp.concatenate(
                [cv_s[pl.ds(h, n_past, stride=CB_HEADS), :].astype(BF16), vn.astype(BF16)], axis=0)
            o_ref[rows, cs] = _cb_attend(q_ref[rows, cs], kfull, _with_ones(vfull), bias_s[h]).astype(BF16)
            new = pl.ds(s * CHUNK * CB_HEADS + h, CHUNK, stride=CB_HEADS)
            ko_ref[new, :] = kn
            vo_ref[new, :] = vn


def _cb_step(z, zkv, cache_k, cache_v, rows, layer, n_seq):
    rows_past = cache_k.shape[2]
    n_k = rows_past // CB_HEADS + CHUNK
    rows_new = CHUNK * CB_HEADS
    g = _step_seqs(n_seq)
    cache = lambda: pl.BlockSpec((None, g, rows_past, HEAD_DIM), lambda b: (layer, b, 0, 0))
    new = lambda: pl.BlockSpec((g * rows_new, HEAD_DIM), lambda b: (b, 0))
    return pl.pallas_call(
        _cb_step_kernel,
        grid=(n_seq // g,),
        in_specs=[
            pl.BlockSpec((g * CHUNK, BRANCH_WIDTH), lambda b: (b, Z_QC // BRANCH_WIDTH)),
            pl.BlockSpec((g * CHUNK, CB_KVW), lambda b: (b, KV_KC // CB_KVW)),
            pl.BlockSpec((g * CHUNK, CB_KVW), lambda b: (b, KV_VC // CB_KVW)),
            cache(), cache(),
            pl.BlockSpec((None, None, CB_HEADS, 1, CB_RLEN),
                         lambda b: (layer, CB_VARIANTS - 1, 0, 0, 0)),
        ],
        out_specs=[pl.BlockSpec((g * CHUNK, BRANCH_WIDTH), lambda b: (b, 0)), new(), new()],
        out_shape=[jax.ShapeDtypeStruct((n_seq * CHUNK, BRANCH_WIDTH), BF16),
                   jax.ShapeDtypeStruct((n_seq * rows_new, HEAD_DIM), F32),
                   jax.ShapeDtypeStruct((n_seq * rows_new, HEAD_DIM), F32)],
        scratch_shapes=[pltpu.VMEM((CB_HEADS, CHUNK, n_k), F32)],
        compiler_params=_params(("arbitrary",)),
        name="cb_step",
    )(z, zkv, zkv, cache_k, cache_v, rows)


def _merge_kernel(x_ref, a_ref, b_ref, c_ref, ga_ref, gb_ref, gc_ref, wb_ref, wo_ref, o_ref):
    mixed = None
    for r, (br, gr) in enumerate(((a_ref, ga_ref), (b_ref, gb_ref), (c_ref, gc_ref))):
        proj = jnp.dot(br[...], wb_ref[r], preferred_element_type=F32)
        gate = jax.nn.sigmoid(gr[...].astype(F32))
        mixed = gate * proj if mixed is None else mixed + gate * proj
    o_ref[...] = x_ref[...] + jnp.dot(mixed.astype(BF16), wo_ref[...], preferred_element_type=F32)


def _merge(x, out_a, out_b, out_c, z, w_branch, w_out, layer, tm):
    m = x.shape[0]
    branch = lambda: pl.BlockSpec((tm, BRANCH_WIDTH), lambda i: (i, 0))
    gate = lambda r: pl.BlockSpec((tm, D_MODEL), lambda i: (i, Z_GATES // D_MODEL + r))
    return pl.pallas_call(
        _merge_kernel,
        grid=(m // tm,),
        in_specs=[
            pl.BlockSpec((tm, D_MODEL), lambda i: (i, 0)),
            branch(), branch(), branch(),
            gate(0), gate(1), gate(2),
            pl.BlockSpec((None, N_BRANCH, BRANCH_WIDTH, D_MODEL), lambda i: (layer, 0, 0, 0),
                         pipeline_mode=pl.Buffered(1)),
            pl.BlockSpec((None, D_MODEL, D_MODEL), lambda i: (layer, 0, 0),
                         pipeline_mode=pl.Buffered(1)),
        ],
        out_specs=pl.BlockSpec((tm, D_MODEL), lambda i: (i, 0)),
        out_shape=jax.ShapeDtypeStruct((m, D_MODEL), F32),
        compiler_params=_params(("parallel",)),
        name="merge",
    )(x, out_a, out_b, out_c, z, z, z, w_branch, w_out)


def _mlp_kernel(*refs, final_norm, jobs):
    n_side = len(jobs.weights) if jobs else 0
    x_ref, g_ref, fg_ref, wu_ref, wd_ref = refs[:5]
    side_in = refs[5:5 + n_side]
    o_ref = refs[5 + n_side]
    side_out = refs[6 + n_side:6 + 2 * n_side]
    hn_ref, wubuf, wdbuf, sem_u, sem_d = refs[-5:]
    j = pl.program_id(1)
    n_cols = pl.num_programs(1)
    tf = wubuf.shape[2]

    def copies(step):
        col = pl.multiple_of(lax.rem(step, n_cols) * tf, tf)
        slot = lax.rem(step, W_RING)
        return (pltpu.make_async_copy(wu_ref.at[0, :, pl.ds(col, tf)], wubuf.at[slot], sem_u.at[slot]),
                pltpu.make_async_copy(wd_ref.at[0, pl.ds(col, tf), :], wdbuf.at[slot], sem_d.at[slot]))

    t = pl.program_id(0) * n_cols + j

    @pl.when(t == 0)
    def _():
        for step in (0, 1):
            for c in copies(step):
                c.start()

    @pl.when(t + 2 < pl.num_programs(0) * n_cols)
    def _():
        for c in copies(t + 2):
            c.start()

    @pl.when(j == 0)
    def _():
        xf = x_ref[...]
        hn_ref[...] = _rms(xf, g_ref[...]).astype(BF16)
        o_ref[...] = xf

    for c in copies(t):
        c.wait()
    slot = lax.rem(t, W_RING)
    h = jnp.dot(hn_ref[...], wubuf[slot], preferred_element_type=F32)
    h = jnp.square(jnp.maximum(h, 0.0)).astype(BF16)
    o_ref[...] += jnp.dot(h, wdbuf[slot], preferred_element_type=F32)

    if final_norm:
        @pl.when(j == pl.num_programs(1) - 1)
        def _():
            o_ref[...] = _rms(o_ref[...], fg_ref[...])

    if jobs:
        jobs.run(side_in, side_out)


def _mlp(x, g, final_g, w_up, w_down, layer, tm, tf, final_norm, cast=None):
    m = x.shape[0]
    grid = (m // tm, D_FF // tf)
    jobs = _CastJobs(cast[0], cast[1], grid, cast[2]) if cast else None
    return pl.pallas_call(
        functools.partial(_mlp_kernel, final_norm=final_norm, jobs=jobs),
        grid=grid,
        in_specs=[
            pl.BlockSpec((tm, D_MODEL), lambda i, j: (i, 0)),
            pl.BlockSpec((None, 1, D_MODEL), lambda i, j: (layer, 0, 0)),
            pl.BlockSpec((1, D_MODEL), lambda i, j: (0, 0)),
            pl.BlockSpec(memory_space=pl.ANY),
            pl.BlockSpec(memory_space=pl.ANY),
        ] + (jobs.in_specs() if jobs else []),
        out_specs=[pl.BlockSpec((tm, D_MODEL), lambda i, j: (i, 0))] + (jobs.out_specs() if jobs else []),
        out_shape=[jax.ShapeDtypeStruct((m, D_MODEL), F32)] + (jobs.out_shape() if jobs else []),
        scratch_shapes=[pltpu.VMEM((tm, D_MODEL), BF16),
                        pltpu.VMEM((W_RING, D_MODEL, tf), BF16), pltpu.VMEM((W_RING, tf, D_MODEL), BF16),
                        pltpu.SemaphoreType.DMA((W_RING,)), pltpu.SemaphoreType.DMA((W_RING,))],
        compiler_params=_params(("arbitrary", "arbitrary")),
        name="mlp",
    )(x, g, final_g, w_up, w_down, *(cast[0] if cast else ()))


def _permute_w_in(w_in):
    seg = lambda o, w: w_in[..., o:o + w]
    return jnp.concatenate([
        seg(_O_AX, D_RNN), seg(_O_AG, D_RNN), seg(_O_QB, BRANCH_WIDTH), seg(_O_QC, BRANCH_WIDTH),
        seg(_O_GATES, N_BRANCH * D_MODEL),
        seg(_O_KC, CB_KVW), seg(_O_VC, CB_KVW), seg(_O_KB, SWA_KVW), seg(_O_VB, SWA_KVW)],
        axis=-1).astype(BF16)


def _cb_bias_rows(table):
    m = jnp.arange(CB_RLEN)
    rel = jnp.where(m < CB_BAND, m, m - CB_RLEN)
    out = []
    for v in range(CB_VARIANTS):
        c0 = v * Q_CHUNKS
        sc = max(c0 - CB_PREV, 0)
        d = (c0 - sc) * CHUNK - rel
        idx = jnp.clip(d, -REL_CLIP, REL_CLIP) + REL_CLIP
        out.append(table.astype(F32)[:, :, idx])
    return jnp.stack(out, axis=1)[:, :, :, None, :]


def _pick_tm(m, want):
    tm = min(want, m)
    while m % tm:
        tm //= 2
    return tm


def _layer(x, n_seq, t_len, conv_buf, h0, state_layer, caches, layer, p, w_in_l, w_layer):
    m = x.shape[0]
    depth = p["w_up"].shape[0]
    tm_in = _pick_tm(m, 1024)
    if w_layer is None:
        z, wu, wd, wb, wo = _in_proj(
            x, p["norm1"], w_in_l, layer, tm_in, "bf16",
            cast=((p["w_up"], p["w_down"], p["w_branch"], p["w_out"]), layer))
        w_layer = (wu[None], wd[None], wb.reshape(1, N_BRANCH, BRANCH_WIDTH, D_MODEL), wo[None])
        cast_next = ((p["w_in"],), layer + 1, _permute_w_in) if layer + 1 < depth else None
    else:
        z, zkv = _in_proj(x, p["norm1"], w_in_l, layer, tm_in, "dual")
        cast_next = None
    w_up_l, w_down_l, w_branch_l, w_out_l = w_layer
    out_a, conv_o, h_o = _lru(z, conv_buf, h0, state_layer, p["conv_w"], p["conv_b"], p["wa"], p["ba"],
                              p["wx"], p["bx"], p["lam"], layer, n_seq, t_len, _pick_tm(t_len, 1024))
    if caches is None:
        out_b = _swa_prompt(z, p["sinks"], layer, n_seq, t_len)
        out_c = _cb_prompt(z, p["cb_rows"], layer, n_seq, t_len)
        n_tail = CB_REACH
        kv3 = _in_proj(x, p["norm1"], w_in_l, layer, n_tail, "kv_f32", n_tiles=n_seq,
                       row_tile=lambda i: (i + 1) * (t_len // n_tail) - 1)[0].reshape(n_seq, n_tail, D_KV)
        kv = (kv3[:, n_tail - SWA_WINDOW:, KV_KB:KV_KB + SWA_KVW],
              kv3[:, n_tail - SWA_WINDOW:, KV_VB:KV_VB + SWA_KVW],
              kv3[:, :, KV_KC:KV_KC + CB_KVW], kv3[:, :, KV_VC:KV_VC + CB_KVW])
    else:
        ck_b, cv_b, ck_c, cv_c = caches
        out_b, kb, vb = _swa_step(z, zkv, ck_b, cv_b, p["sinks"], layer, n_seq)
        out_c, kc, vc = _cb_step(z, zkv, ck_c, cv_c, p["cb_rows"], layer, n_seq)
        kv = (kb, vb, kc, vc)
    x = _merge(x, out_a, out_b, out_c, z, w_branch_l, w_out_l, 0, _pick_tm(m, 256))
    res = _mlp(x, p["norm2"], p["final_g"], w_up_l, w_down_l, layer, _pick_tm(m, 512), 1024,
               final_norm=layer == depth - 1, cast=cast_next)
    w_in_next = res[1][None] if cast_next else None
    return res[0], conv_o, h_o[:, 0], kv, w_layer, w_in_next


def kernel(x_prompt, x_sample, state_conv, state_lru, cache_swa_k, cache_swa_v, cache_cb_k, cache_cb_v, norm1_g, w_in, conv_w, conv_b, lru_wa, lru_ba, lru_wx, lru_bx, lru_lambda, attn_sinks, rel_bias_table, w_branch, w_out, norm2_g, w_up, w_down, final_g):
    depth = w_in.shape[0]
    nb, s_len, _ = x_prompt.shape
    db, d_len, _ = x_sample.shape
    assert d_len == CHUNK and s_len % min(Q_BLOCK * CB_QSUB, s_len) == 0 and s_len % min(SWA_TQ, s_len) == 0
    assert s_len % Q_BLOCK == 0 and s_len >= CB_BAND
    assert s_len % CB_REACH == 0
    assert cache_swa_k.shape[2] == SWA_WINDOW and cache_cb_k.shape[2] == CB_REACH

    row = lambda v: v.reshape(depth, 1, -1)
    p = {
        "norm1": row(norm1_g), "norm2": row(norm2_g), "final_g": final_g.reshape(1, D_MODEL),
        "w_in": w_in, "w_up": w_up, "w_down": w_down,
        "w_branch": w_branch.reshape(depth, N_BRANCH * BRANCH_WIDTH, D_MODEL), "w_out": w_out,
        "conv_w": conv_w, "conv_b": row(conv_b),
        "wa": lru_wa.astype(BF16), "ba": row(lru_ba), "wx": lru_wx.astype(BF16), "bx": row(lru_bx),
        "lam": row(lru_lambda), "sinks": attn_sinks,
        "cb_rows": _cb_bias_rows(rel_bias_table),
    }
    w_in_l = _permute_w_in(w_in[0])[None]
    caches = (cache_swa_k.reshape(depth, db, SWA_WINDOW * SWA_KV_HEADS, HEAD_DIM),
              cache_swa_v.reshape(depth, db, SWA_WINDOW * SWA_KV_HEADS, HEAD_DIM),
              cache_cb_k.reshape(depth, db, CB_REACH * CB_HEADS, HEAD_DIM),
              cache_cb_v.reshape(depth, db, CB_REACH * CB_HEADS, HEAD_DIM))

    xp = x_prompt.reshape(nb * s_len, D_MODEL)
    xs = x_sample.reshape(db * d_len, D_MODEL)
    zero_conv = jnp.zeros((1, nb, CONV_W - 1, D_RNN), F32)
    zero_h = jnp.zeros((1, nb, 1, D_RNN), F32)
    h0_s = state_lru.reshape(depth, db, 1, D_RNN)
    heads = (SWA_KV_HEADS, SWA_KV_HEADS, CB_HEADS, CB_HEADS)
    outs = [[] for _ in range(12)]
    for l in range(depth):
        xp, conv_p, h_p, kv_p, w_layer, w_in_next = _layer(xp, nb, s_len, zero_conv, zero_h, 0, None, l, p,
                                                           w_in_l, None)
        xs, conv_s, h_s, kv_s, _, _ = _layer(xs, db, d_len, state_conv, h0_s, l, caches, l, p,
                                             w_in_l, w_layer)
        w_in_l = w_in_next
        outs[0].append(conv_p)
        outs[1].append(h_p)
        outs[6].append(conv_s)
        outs[7].append(h_s)
        for n in range(4):
            outs[2 + n].append(kv_p[n].reshape(nb, -1, heads[n], HEAD_DIM))
            outs[8 + n].append(kv_s[n].reshape(db, d_len, heads[n], HEAD_DIM))

    y_prompt = xp.reshape(nb, s_len, D_MODEL)
    y_sample = xs.reshape(db, d_len, D_MODEL)
    return (y_prompt, y_sample) + tuple(jnp.stack(o) for o in outs)
```
